```python
import jax, jax.numpy as jnp
from jax import lax
import numpy as np

D_MODEL = 1024
BATCH = 8
SEQ = 2048
DEPTH = 1
DEC_BATCH = 128
DEC_SEQ = 1
PAST_LEN = 2048
PAGE_SIZE = 128

NSA_HEADS = 8
NSA_KV_HEADS = 2
NSA_GROUP = NSA_HEADS // NSA_KV_HEADS
HEAD_DIM = 64
ROT_DIM = HEAD_DIM // 4
ROPE_THETA = 500000.0
CMP_LEN = 32
CMP_STRIDE = 16
CMP_HIDDEN = 256
SEL_BLOCK = 64
SEL_TOP_N = 16
WINDOW = 512
BAND_Q_BLOCK = 128
SEL_Q_BLOCK = 64
FORCE_SCORE = 1.0e4
GLA_HEADS = 4
GLA_DK = 64
GLA_DV = 128
GLA_GATE_RANK = 16
GLA_TAU = 16.0
GLA_CHUNK = 64
N_EXPERTS = 256
TOP_K = 8
N_GROUPS = 8
TOPK_GROUPS = 4
EXPERT_DIM = 256
SHARED_DIM = 256
ROUTED_SCALE = 2.5
EXPERT_BLOCK = 128
DN_ALPHA = (2 * DEPTH) ** 0.25
DN_BETA = (8 * DEPTH) ** -0.25
LN_EPS = 1e-5

NSA_Q_DIM = NSA_HEADS * HEAD_DIM
NSA_KV_DIM = NSA_KV_HEADS * HEAD_DIM
GLA_K_DIM = GLA_HEADS * GLA_DK
GLA_V_DIM = GLA_HEADS * GLA_DV
IN_SIZES = (NSA_Q_DIM, NSA_KV_DIM, NSA_KV_DIM, NSA_KV_DIM, NSA_KV_DIM, NSA_KV_DIM, NSA_KV_DIM, NSA_HEADS * 3, GLA_K_DIM, GLA_K_DIM, GLA_V_DIM, GLA_V_DIM, GLA_GATE_RANK, 2 * D_MODEL)
IN_VALUE_SEGMENTS = (2, 4, 6, 10)
IN_TOTAL = sum(IN_SIZES)

kernel_name = 'nsa_gla_gated_hybrid_moe_decode_step'

F32 = jnp.float32


def _layer_norm(x, g, b):
    xf = x.astype(F32)
    mu = xf.mean(-1, keepdims=True)
    var = jnp.mean(jnp.square(xf - mu), -1, keepdims=True)
    return ((xf - mu) * lax.rsqrt(var + LN_EPS) * g.astype(F32) + b.astype(F32)).astype(x.dtype)


def _rope(x, pos):
    half = ROT_DIM // 2
    inv = jnp.power(ROPE_THETA, -jnp.arange(half, dtype=F32) * 2.0 / ROT_DIM)
    ang = pos.astype(F32)[:, None] * inv[None, :]
    cos, sin = jnp.cos(ang)[:, None, :], jnp.sin(ang)[:, None, :]
    xr = x[..., :ROT_DIM].astype(F32)
    x1, x2 = xr[..., :half], xr[..., half:]
    rot = jnp.concatenate([x1 * cos - x2 * sin, x2 * cos + x1 * sin], axis=-1)
    return jnp.concatenate([rot.astype(x.dtype), x[..., ROT_DIM:]], axis=-1)


def _masked_softmax(s, mask):
    s = jnp.where(mask, s, -jnp.inf)
    m = jnp.max(s, axis=-1, keepdims=True)
    m = jnp.where(jnp.isfinite(m), m, 0.0)
    p = jnp.where(mask, jnp.exp(s - m), 0.0)
    return p / jnp.maximum(p.sum(-1, keepdims=True), 1e-30)


def _compress(rows, pos_emb, w1, w2):
    n_cmp = (rows.shape[1] - CMP_LEN) // CMP_STRIDE + 1
    idx = np.arange(n_cmp)[:, None] * CMP_STRIDE + np.arange(CMP_LEN)[None, :]
    blk = rows[:, idx] + pos_emb[:, None, :]
    hid = jax.nn.gelu(jnp.einsum('bclhd,ldf->bchf', blk, w1))
    return jnp.einsum('bchf,fd->bchd', hid, w2)


def _nsa_global(q, pos, kv_cmp, kv_sel, cmp_w, q_block):
    cmp_k_pos, cmp_k_w1, cmp_k_w2, cmp_v_pos, cmp_v_w1, cmp_v_w2 = cmp_w
    B, Tq = q.shape[:2]
    scale = HEAD_DIM ** -0.5
    kc = _compress(kv_cmp[:, :, 0], cmp_k_pos, cmp_k_w1, cmp_k_w2)
    vc = _compress(kv_cmp[:, :, 1], cmp_v_pos, cmp_v_w1, cmp_v_w2)
    n_cmp = kc.shape[1]
    c_start = jnp.arange(n_cmp) * CMP_STRIDE
    s = jnp.einsum('bqhgd,bchd->bhgqc', q, kc).astype(F32) * scale
    p_cmp = _masked_softmax(s, (c_start + CMP_LEN - 1)[None, :] <= pos[:, None])
    o_cmp = jnp.einsum('bhgqc,bchd->bqhgd', p_cmp.astype(vc.dtype), vc)
    L = kv_sel.shape[1]
    n_sel_blocks = -(-L // SEL_BLOCK)
    s_start = jnp.arange(n_sel_blocks) * SEL_BLOCK
    cover = ((c_start[:, None] < s_start[None, :] + SEL_BLOCK) & (c_start[:, None] + CMP_LEN > s_start[None, :])).astype(F32)
    imp = jnp.einsum('bhgqc,cs->bhqs', p_cmp, cover)
    cur = (pos // SEL_BLOCK)[:, None]
    blk = jnp.arange(n_sel_blocks)[None, :]
    forced = (blk == 0) | (blk == cur) | (blk == cur - 1)
    imp = jnp.where(forced, FORCE_SCORE, jnp.where(s_start[None, :] <= pos[:, None], imp, -FORCE_SCORE))
    _, sel_idx = lax.top_k(imp, min(SEL_TOP_N, n_sel_blocks))
    kv_pad = jnp.pad(kv_sel, ((0, 0), (0, n_sel_blocks * SEL_BLOCK - L), (0, 0), (0, 0), (0, 0)))
    kv_pad = kv_pad.reshape(B, n_sel_blocks, SEL_BLOCK, 2, NSA_KV_HEADS, HEAD_DIM)
    ks = kv_pad[:, :, :, 0].transpose(0, 3, 1, 2, 4)
    vs = kv_pad[:, :, :, 1].transpose(0, 3, 1, 2, 4)
    bi = jnp.arange(B)[:, None, None, None]
    hi = jnp.arange(NSA_KV_HEADS)[None, :, None, None]

    def sel_block(args):
        qb, ib, pb = args
        gk = ks[bi, hi, ib]
        gv = vs[bi, hi, ib]
        sb = jnp.einsum('bqhgd,bhqnld->bhgqnl', qb, gk).astype(F32) * scale
        kpos = ib[..., None] * SEL_BLOCK + jnp.arange(SEL_BLOCK)
        mask = (kpos <= pb[None, None, :, None, None])[:, :, None]
        shp = sb.shape
        p = _masked_softmax(sb.reshape(*shp[:4], -1), mask.reshape(*mask.shape[:4], -1)).reshape(shp)
        return jnp.einsum('bhgqnl,bhqnld->bqhgd', p.astype(gv.dtype), gv)

    nb = Tq // q_block
    q_blocks = jnp.moveaxis(q.reshape(B, nb, q_block, *q.shape[2:]), 1, 0)
    i_blocks = jnp.moveaxis(sel_idx.reshape(B, NSA_KV_HEADS, nb, q_block, -1), 2, 0)
    o_sel = lax.map(sel_block, (q_blocks, i_blocks, pos.reshape(nb, q_block)))
    o_sel = jnp.moveaxis(o_sel, 0, 1).reshape(q.shape)
    return o_cmp, o_sel


def _band_attn(qb, kvb, qpos, kpos):
    s = jnp.einsum('bnqhgd,bnkhd->bnhgqk', qb, kvb[:, :, :, 0]).astype(F32) * HEAD_DIM ** -0.5
    rel = qpos[:, :, None] - kpos[:, None, :]
    mask = (kpos[:, None, :] >= 0) & (rel >= 0) & (rel <= WINDOW)
    p = _masked_softmax(s, mask[None, :, None, None])
    return jnp.einsum('bnhgqk,bnkhd->bnqhgd', p.astype(qb.dtype), kvb[:, :, :, 1])


def _window_prompt(q, pos, kv):
    B, T = q.shape[:2]
    nb = T // BAND_Q_BLOCK
    span = WINDOW + BAND_Q_BLOCK
    idx = np.arange(nb)[:, None] * BAND_Q_BLOCK + np.arange(span)[None, :]
    kvb = jnp.pad(kv, ((0, 0), (WINDOW, 0), (0, 0), (0, 0), (0, 0)))[:, idx]
    qb = q.reshape(B, nb, BAND_Q_BLOCK, *q.shape[2:])
    o = _band_attn(qb, kvb, pos.reshape(nb, BAND_Q_BLOCK), jnp.asarray(idx - WINDOW))
    return o.reshape(q.shape)


def _window_sample(q, pos, kv_all, first_pos):
    kpos = first_pos + jnp.arange(kv_all.shape[1])
    return _band_attn(q[:, None], kv_all[:, None], pos[None], kpos[None])[:, 0]


def _gla(q, k, v, log_a, s0, chunk):
    B, T, H, _ = q.shape
    n = T // chunk

    def split(t):
        return jnp.moveaxis(t.reshape(B, n, chunk, H, t.shape[-1]), 1, 0)

    causal = jnp.tril(jnp.ones((chunk, chunk), bool))[None, :, :, None, None]

    def step(S, inp):
        qc, kc, vc, ac = inp
        b = jnp.cumsum(ac, axis=1)
        dec = jnp.exp(jnp.where(causal, b[:, :, None] - b[:, None, :], -jnp.inf))
        att = jnp.einsum('bihd,bjhd,bijhd->bhij', qc, kc, dec)
        o = jnp.einsum('bhij,bjhe->bihe', att, vc) + jnp.einsum('bihd,bhde->bihe', qc * jnp.exp(b), S)
        b_last = b[:, -1]
        S = jnp.exp(b_last)[..., None] * S + jnp.einsum('bjhd,bjhe->bhde', kc * jnp.exp(b_last[:, None] - b), vc)
        return S, o

    S, o = lax.scan(step, s0, (split(q.astype(F32)), split(k.astype(F32)), split(v.astype(F32)), split(log_a)))
    return jnp.moveaxis(o, 0, 1).reshape(B, T, H, v.shape[-1]), S


def _token_mixer(h, pos, mix_w, past_len, past_cmp, past_sel, win_buf, gla_state):
    (w_in, b_in, cmp_k_pos, cmp_k_w1, cmp_k_w2, cmp_v_pos, cmp_v_w1, cmp_v_w2,
     gla_w_a2, gla_b_a, gla_norm_g, w_br_a, w_br_b, w_out) = mix_w
    cmp_w = (cmp_k_pos, cmp_k_w1, cmp_k_w2, cmp_v_pos, cmp_v_w1, cmp_v_w2)
    B, T, _ = h.shape
    z = h @ w_in + b_in
    (q_n, k_c, v_c, k_s, v_s, k_w, v_w, g_n, q_g, k_g, v_g, r_g, a_g, g_m) = jnp.split(z, np.cumsum(IN_SIZES)[:-1].tolist(), axis=-1)

    def kv_rows(k, v):
        return jnp.stack([_rope(k.reshape(B, T, NSA_KV_HEADS, HEAD_DIM), pos), v.reshape(B, T, NSA_KV_HEADS, HEAD_DIM)], axis=2)

    q = _rope(q_n.reshape(B, T, NSA_HEADS, HEAD_DIM), pos).reshape(B, T, NSA_KV_HEADS, NSA_GROUP, HEAD_DIM)
    kv_c, kv_s, kv_w = kv_rows(k_c, v_c), kv_rows(k_s, v_s), kv_rows(k_w, v_w)
    if past_cmp is None:
        o_cmp, o_sel = _nsa_global(q, pos, kv_c, kv_s, cmp_w, SEL_Q_BLOCK)
        o_win = _window_prompt(q, pos, kv_w)
        win_new = kv_w[:, -min(WINDOW, T):]
        s0 = jnp.zeros((B, GLA_HEADS, GLA_DK, GLA_DV), F32)
        chunk = GLA_CHUNK
    else:
        o_cmp, o_sel = _nsa_global(q, pos, jnp.concatenate([past_cmp, kv_c], axis=1), jnp.concatenate([past_sel, kv_s], axis=1), cmp_w, T)
        kv_all = jnp.concatenate([win_buf, kv_w], axis=1)
        o_win = _window_sample(q, pos, kv_all, past_len - win_buf.shape[1])
        win_new = kv_all[:, T:]
        s0 = gla_state.astype(F32)
        chunk = T
    g = jax.nn.sigmoid(g_n.astype(F32)).reshape(B, T, NSA_KV_HEADS, NSA_GROUP, 3).astype(h.dtype)
    o_nsa = (g[..., 0:1] * o_cmp + g[..., 1:2] * o_sel + g[..., 2:3] * o_win).reshape(B, T, NSA_Q_DIM)
    qg = q_g.reshape(B, T, GLA_HEADS, GLA_DK) * GLA_DK ** -0.5
    kg = k_g.reshape(B, T, GLA_HEADS, GLA_DK)
    vg = v_g.reshape(B, T, GLA_HEADS, GLA_DV)
    log_a = jax.nn.log_sigmoid((a_g @ gla_w_a2 + gla_b_a).astype(F32)).reshape(B, T, GLA_HEADS, GLA_DK) / GLA_TAU
    o_g, s_new = _gla(qg, kg, vg, log_a, s0, chunk)
    mu = o_g.mean(-1, keepdims=True)
    var = jnp.mean(jnp.square(o_g - mu), -1, keepdims=True)
    o_g = ((o_g - mu) * lax.rsqrt(var + LN_EPS)).reshape(B, T, GLA_V_DIM) * gla_norm_g.astype(F32)
    o_gla = o_g.astype(h.dtype) * jax.nn.silu(r_g)
    gm = jax.nn.sigmoid(g_m.astype(F32)).astype(h.dtype).reshape(B, T, 2, D_MODEL)
    y = (gm[:, :, 0] * (o_nsa @ w_br_a) + gm[:, :, 1] * (o_gla @ w_br_b)) @ w_out
    return y, (kv_c, kv_s, win_new, s_new.astype(h.dtype))


def _swiglu(x, wg, wu, wd):
    return (jax.nn.silu(x @ wg) * (x @ wu)) @ wd


def _routed_experts(x, eidx, wts, w_gate, w_up, w_down):
    n, d = x.shape
    k = eidx.shape[1]
    n_exp = w_gate.shape[0]
    flat_e = eidx.reshape(-1)
    order = jnp.argsort(flat_e)
    e_sorted = flat_e[order]
    tok_sorted = (order // k).astype(jnp.int32)
    w_sorted = wts.reshape(-1)[order]
    counts = jnp.bincount(flat_e, length=n_exp)
    seg_start = jnp.cumsum(counts) - counts
    padded = (counts + EXPERT_BLOCK - 1) // EXPERT_BLOCK * EXPERT_BLOCK
    pad_end = jnp.cumsum(padded)
    pad_start = pad_end - padded
    dest = pad_start[e_sorted] + jnp.arange(n * k) - seg_start[e_sorted]
    n_blocks = -(-(n * k) // EXPERT_BLOCK) + n_exp
    n_rows = n_blocks * EXPERT_BLOCK
    src_tok = jnp.full((n_rows,), n, jnp.int32).at[dest].set(tok_sorted)
    src_w = jnp.zeros((n_rows,), x.dtype).at[dest].set(w_sorted.astype(x.dtype))
    block_e = jnp.minimum(jnp.searchsorted(pad_end, jnp.arange(n_blocks) * EXPERT_BLOCK, side='right'), n_exp - 1)
    x_ext = jnp.concatenate([x, jnp.zeros((1, d), x.dtype)], axis=0)

    def expert_block(args):
        rows, e = args
        return _swiglu(x_ext[rows], w_gate[e], w_up[e], w_down[e])

    y = lax.map(expert_block, (src_tok.reshape(n_blocks, EXPERT_BLOCK), block_e)).reshape(n_rows, d)
    out = jnp.zeros((n + 1, d), x.dtype).at[src_tok].add(y * src_w[:, None])
    return out[:n]


def _moe(h, w_router, router_bias, w_e_gate, w_e_up, w_e_down, w_s_gate, w_s_up, w_s_down):
    shp = h.shape
    x = h.reshape(-1, shp[-1])
    n = x.shape[0]
    per = N_EXPERTS // N_GROUPS
    s = jax.nn.sigmoid((x @ w_router).astype(F32))
    sb = s + router_bias.astype(F32)
    gscore = lax.top_k(sb.reshape(n, N_GROUPS, per), 2)[0].sum(-1)
    _, gidx = lax.top_k(gscore, TOPK_GROUPS)
    gmask = jax.nn.one_hot(gidx, N_GROUPS, dtype=F32).sum(1) > 0
    sb = jnp.where(jnp.repeat(gmask, per, axis=1), sb, -jnp.inf)
    _, eidx = lax.top_k(sb, TOP_K)
    sel = jnp.take_along_axis(s, eidx, axis=1)
    wts = sel / sel.sum(-1, keepdims=True) * ROUTED_SCALE
    y = _routed_experts(x, eidx, wts, w_e_gate, w_e_up, w_e_down) + _swiglu(x, w_s_gate, w_s_up, w_s_down)
    return y.reshape(shp)


def _layer(x, c, pos, mix_w, moe_w, ada_w, norm_w, past_len, past_cmp, past_sel, win_buf, gla_state):
    w_ada, b_ada = ada_w
    ln1_g, ln1_b, ln2_g, ln2_b = norm_w
    mod = (jax.nn.silu(c) @ w_ada + b_ada)[:, None, :]
    shift_m, scale_m, gate_m, shift_f, scale_f, gate_f = jnp.split(mod, 6, axis=-1)
    mix, new_state = _token_mixer(x * (1 + scale_m) + shift_m, pos, mix_w, past_len, past_cmp, past_sel, win_buf, gla_state)
    x = _layer_norm(DN_ALPHA * x + gate_m * mix, ln1_g, ln1_b)
    x = _layer_norm(DN_ALPHA * x + gate_f * _moe(x * (1 + scale_f) + shift_f, *moe_w), ln2_g, ln2_b)
    return x, new_state


def setup_inputs(seed: int = 0) -> dict:
    key = jax.random.key(seed)
    ks = iter(jax.random.split(key, 48))

    def nrm(shape, scale=1.0):
        return jax.random.normal(next(ks), shape, F32) * scale

    n_pages = PAST_LEN // PAGE_SIZE
    n_pool = (DEC_BATCH * n_pages * 5) // 4
    win_buf = min(WINDOW, PAST_LEN)
    col_scale = jnp.asarray(np.concatenate([np.full((sz,), DN_BETA if i in IN_VALUE_SEGMENTS else 1.0, dtype=np.float32) for i, sz in enumerate(IN_SIZES)]))
    kv_shape = (n_pool, PAGE_SIZE, 2, NSA_KV_HEADS, HEAD_DIM)
    return {
        'x_prompt': nrm((BATCH, SEQ, D_MODEL)),
        'x_sample': nrm((DEC_BATCH, DEC_SEQ, D_MODEL)),
        'cache_kv_cmp': nrm(kv_shape),
        'cache_kv_sel': nrm(kv_shape),
        'state_kv_win': nrm((DEC_BATCH, win_buf, 2, NSA_KV_HEADS, HEAD_DIM)),
        'state_gla': nrm((DEC_BATCH, GLA_HEADS, GLA_DK, GLA_DV), 0.5),
        'page_table': jax.random.permutation(next(ks), n_pool)[: DEC_BATCH * n_pages].reshape(DEC_BATCH, n_pages).astype(jnp.int32),
        'c_prompt': nrm((BATCH, D_MODEL)),
        'c_sample': nrm((DEC_BATCH, D_MODEL)),
        'w_in': nrm((D_MODEL, IN_TOTAL), D_MODEL ** -0.5) * col_scale,
        'b_in': nrm((IN_TOTAL,), 0.02),
        'cmp_k_pos': nrm((CMP_LEN, HEAD_DIM), 0.1),
        'cmp_k_w1': nrm((CMP_LEN, HEAD_DIM, CMP_HIDDEN), (CMP_LEN * HEAD_DIM) ** -0.5),
        'cmp_k_w2': nrm((CMP_HIDDEN, HEAD_DIM), CMP_HIDDEN ** -0.5),
        'cmp_v_pos': nrm((CMP_LEN, HEAD_DIM), 0.1),
        'cmp_v_w1': nrm((CMP_LEN, HEAD_DIM, CMP_HIDDEN), (CMP_LEN * HEAD_DIM) ** -0.5),
        'cmp_v_w2': nrm((CMP_HIDDEN, HEAD_DIM), CMP_HIDDEN ** -0.5),
        'gla_w_a2': nrm((GLA_GATE_RANK, GLA_K_DIM), GLA_GATE_RANK ** -0.5),
        'gla_b_a': nrm((GLA_K_DIM,), 0.1),
        'gla_norm_g': 1.0 + nrm((GLA_V_DIM,), 0.02),
        'w_br_a': nrm((NSA_Q_DIM, D_MODEL), NSA_Q_DIM ** -0.5 * DN_BETA),
        'w_br_b': nrm((GLA_V_DIM, D_MODEL), GLA_V_DIM ** -0.5 * DN_BETA),
        'w_out': nrm((D_MODEL, D_MODEL), D_MODEL ** -0.5 * DN_BETA),
        'ln1_g': 1.0 + nrm((D_MODEL,), 0.02),
        'ln1_b': nrm((D_MODEL,), 0.02),
        'w_ada': nrm((D_MODEL, 6 * D_MODEL), 0.5 * D_MODEL ** -0.5),
        'b_ada': nrm((6 * D_MODEL,), 0.02),
        'w_router': nrm((D_MODEL, N_EXPERTS), D_MODEL ** -0.5),
        'router_bias': nrm((N_EXPERTS,), 0.01),
        'w_e_gate': nrm((N_EXPERTS, D_MODEL, EXPERT_DIM), D_MODEL ** -0.5),
        'w_e_up': nrm((N_EXPERTS, D_MODEL, EXPERT_DIM), D_MODEL ** -0.5),
        'w_e_down': nrm((N_EXPERTS, EXPERT_DIM, D_MODEL), EXPERT_DIM ** -0.5 * DN_BETA),
        'w_s_gate': nrm((D_MODEL, SHARED_DIM), D_MODEL ** -0.5),
        'w_s_up': nrm((D_MODEL, SHARED_DIM), D_MODEL ** -0.5),
        'w_s_down': nrm((SHARED_DIM, D_MODEL), SHARED_DIM ** -0.5 * DN_BETA),
        'ln2_g': 1.0 + nrm((D_MODEL,), 0.02),
        'ln2_b': nrm((D_MODEL,), 0.02),
    }


def reference(x_prompt, x_sample, cache_kv_cmp, cache_kv_sel, state_kv_win, state_gla, page_table, c_prompt, c_sample,
              w_in, b_in, cmp_k_pos, cmp_k_w1, cmp_k_w2, cmp_v_pos, cmp_v_w1, cmp_v_w2, gla_w_a2, gla_b_a, gla_norm_g,
              w_br_a, w_br_b, w_out, ln1_g, ln1_b, w_ada, b_ada, w_router, router_bias, w_e_gate, w_e_up, w_e_down,
              w_s_gate, w_s_up, w_s_down, ln2_g, ln2_b):
    mix_w = (w_in, b_in, cmp_k_pos, cmp_k_w1, cmp_k_w2, cmp_v_pos, cmp_v_w1, cmp_v_w2,
             gla_w_a2, gla_b_a, gla_norm_g, w_br_a, w_br_b, w_out)
    moe_w = (w_router, router_bias, w_e_gate, w_e_up, w_e_down, w_s_gate, w_s_up, w_s_down)
    ada_w = (w_ada, b_ada)
    norm_w = (ln1_g, ln1_b, ln2_g, ln2_b)
    n_dec, t_dec = x_sample.shape[:2]
    past_len = page_table.shape[1] * cache_kv_cmp.shape[1]
    pos_p = jnp.arange(x_prompt.shape[1], dtype=jnp.int32)
    pos_s = past_len + jnp.arange(t_dec, dtype=jnp.int32)
    y_p, y_s = x_prompt, x_sample
    for _ in range(DEPTH):
        past_cmp = cache_kv_cmp[page_table].reshape(n_dec, past_len, *cache_kv_cmp.shape[2:])
        past_sel = cache_kv_sel[page_table].reshape(n_dec, past_len, *cache_kv_sel.shape[2:])
        y_p, (kvc_p, kvs_p, win_p, gla_p) = _layer(y_p, c_prompt, pos_p, mix_w, moe_w, ada_w, norm_w, 0, None, None, None, None)
        y_s, (kvc_s, kvs_s, win_s, gla_s) = _layer(y_s, c_sample, pos_s, mix_w, moe_w, ada_w, norm_w, past_len, past_cmp, past_sel, state_kv_win, state_gla)
    return (y_p, y_s, kvc_p, kvs_p, win_p, gla_p, kvc_s, kvs_s, win_s, gla_s)
```

```python
import functools

import numpy as np
import jax
import jax.numpy as jnp
from jax import lax
from jax.experimental import pallas as pl
from jax.experimental.pallas import tpu as pltpu

F32 = jnp.float32
BF16 = jnp.bfloat16
I32 = jnp.int32

D_MODEL = 1024
NSA_HEADS = 8
NSA_KV_HEADS = 2
NSA_GROUP = NSA_HEADS // NSA_KV_HEADS
HEAD_DIM = 64
ROT_DIM = HEAD_DIM // 4
ROPE_THETA = 500000.0
CMP_LEN = 32
CMP_STRIDE = 16
CMP_HIDDEN = 256
SEL_BLOCK = 64
SEL_TOP_N = 16
WINDOW = 512
FORCE_SCORE = 1.0e4
GLA_HEADS = 4
GLA_DK = 64
GLA_DV = 128
GLA_GATE_RANK = 16
GLA_TAU = 16.0
GLA_SUB = 16
N_EXPERTS = 256
TOP_K = 8
N_GROUPS = 8
TOPK_GROUPS = 4
EXPERT_DIM = 256
SHARED_DIM = 256
ROUTED_SCALE = 2.5
EXPERT_BLOCK = 128
DN_ALPHA = 2.0 ** 0.25
LN_EPS = 1e-5
LANE = 128
NEG = -1.0e30
VMEM_LIMIT = 56 * 1024 * 1024

SEG_GM = (0, 2 * D_MODEL)
SEG_QN = (SEG_GM[0] + SEG_GM[1], NSA_HEADS * LANE)
SEG_VG = (SEG_QN[0] + SEG_QN[1], GLA_HEADS * GLA_DV)
SEG_RG = (SEG_VG[0] + SEG_VG[1], GLA_HEADS * GLA_DV)
SEG_KVC = (SEG_RG[0] + SEG_RG[1], 2 * LANE)
SEG_KVS = (SEG_KVC[0] + SEG_KVC[1], 2 * LANE)
SEG_KVW = (SEG_KVS[0] + SEG_KVS[1], 2 * LANE)
SEG_QG = (SEG_KVW[0] + SEG_KVW[1], GLA_HEADS * GLA_DK)
SEG_KG = (SEG_QG[0] + SEG_QG[1], GLA_HEADS * GLA_DK)
SEG_MISC = (SEG_KG[0] + SEG_KG[1], LANE)
IN_PACKED = SEG_MISC[0] + SEG_MISC[1]
MISC_GN = 0
MISC_AG = NSA_HEADS * 3


def _cparams(*sem):
    return pltpu.CompilerParams(dimension_semantics=sem, vmem_limit_bytes=VMEM_LIMIT)


def _bdot(a, b):
    return jnp.dot(a.astype(BF16), b.astype(BF16), preferred_element_type=F32)


def _dot_nt(a, b, precision=None):
    return lax.dot_general(a, b, (((1,), (1,)), ((), ())), preferred_element_type=F32, precision=precision)


def _adaln_kernel(c_ref, w_ref, b_ref, o_ref):
    c = c_ref[...]
    o_ref[...] = _bdot(c * jax.nn.sigmoid(c), w_ref[...]) + b_ref[...]


def _adaln(c, w_ada, b_ada):
    n, d = c.shape
    m = w_ada.shape[1]
    tn = 512
    return pl.pallas_call(
        _adaln_kernel,
        grid=(m // tn,),
        in_specs=[pl.BlockSpec((n, d), lambda j: (0, 0)),
                  pl.BlockSpec((d, tn), lambda j: (0, j)),
                  pl.BlockSpec((1, tn), lambda j: (0, j))],
        out_specs=pl.BlockSpec((n, tn), lambda j: (0, j)),
        out_shape=jax.ShapeDtypeStruct((n, m), F32),
        compiler_params=_cparams("parallel"),
        name="adaln",
    )(c, w_ada, b_ada.reshape(1, m))


def _rope_tables(pos):
    half = ROT_DIM // 2
    inv = jnp.power(ROPE_THETA, -jnp.arange(half, dtype=F32) * 2.0 / ROT_DIM)
    ang = pos.astype(F32)[:, None] * inv[None, :]
    cos, sin = jnp.cos(ang), jnp.sin(ang)
    t = pos.shape[0]
    one = jnp.ones((t, HEAD_DIM - ROT_DIM), F32)
    z8 = jnp.zeros((t, half), F32)
    z48 = jnp.zeros((t, HEAD_DIM - ROT_DIM), F32)
    c = jnp.concatenate([cos, cos, one, cos, cos, one], axis=1)
    s1 = jnp.concatenate([-sin, z8, z48, -sin, z8, z48], axis=1)
    s2 = jnp.concatenate([z8, sin, z48, z8, sin, z48], axis=1)
    return c, s1, s2


def _pack_in_weights(w_in, b_in):
    sizes = (512, 128, 128, 128, 128, 128, 128, 24, 256, 256, 512, 512, 16, 2048)
    offs = np.concatenate([[0], np.cumsum(sizes)])

    def pack(w):
        seg = [w[..., offs[i]:offs[i + 1]] for i in range(len(sizes))]
        q_n, k_c, v_c, k_s, v_s, k_w, v_w, g_n, q_g, k_g, v_g, r_g, a_g, g_m = seg
        zero = jnp.zeros_like(q_n[..., :HEAD_DIM])
        q_slots = []
        for hh in range(NSA_HEADS):
            qh = q_n[..., hh * HEAD_DIM:(hh + 1) * HEAD_DIM] * (HEAD_DIM ** -0.5)
            q_slots += [qh, zero] if hh // NSA_GROUP == 0 else [zero, qh]
        misc_pad = jnp.zeros_like(w[..., :LANE - g_n.shape[-1] - a_g.shape[-1]])
        return jnp.concatenate([g_m] + q_slots + [v_g, r_g, k_c, v_c, k_s, v_s, k_w, v_w,
                                                   q_g * (GLA_DK ** -0.5), k_g, g_n, a_g, misc_pad], axis=-1)

    return pack(w_in).astype(BF16), pack(b_in.reshape(1, -1))


def _inproj_kernel(x_ref, sh_ref, sc_ref, w_ref, b_ref, rc_ref, rs1_ref, rs2_ref,
                   gm_ref, qn_ref, vg_ref, rg_ref, kvc_ref, kvs_ref, kvw_ref, qg_ref, kg_ref, misc_ref):
    h = (x_ref[0] * (1.0 + sc_ref[0]) + sh_ref[0]).astype(BF16)
    rc, rs1, rs2 = rc_ref[...], rs1_ref[...], rs2_ref[...]

    def proj(off, width):
        return jnp.dot(h, w_ref[:, off:off + width], preferred_element_type=F32) + b_ref[:, off:off + width]

    def rope(z):
        return z * rc + pltpu.roll(z, LANE - ROT_DIM // 2, 1) * rs1 + pltpu.roll(z, ROT_DIM // 2, 1) * rs2

    def plain(ref, seg):
        off, width = seg
        step = min(width, 512)
        for c in range(0, width, step):
            ref[0, :, c:c + step] = proj(off + c, step)

    plain(gm_ref, SEG_GM)
    for c in range(0, SEG_QN[1], 512):
        z = proj(SEG_QN[0] + c, 512)
        for s in range(0, 512, LANE):
            qn_ref[0, :, c + s:c + s + LANE] = rope(z[:, s:s + LANE])
    plain(vg_ref, SEG_VG)
    plain(rg_ref, SEG_RG)
    for ref, seg in ((kvc_ref, SEG_KVC), (kvs_ref, SEG_KVS), (kvw_ref, SEG_KVW)):
        z = proj(seg[0], seg[1])
        ref[0, :, 0:LANE] = rope(z[:, 0:LANE])
        ref[0, :, LANE:2 * LANE] = z[:, LANE:2 * LANE]
    plain(qg_ref, SEG_QG)
    plain(kg_ref, SEG_KG)
    plain(misc_ref, SEG_MISC)


def _inproj(x, shift, scale, w_pack, b_pack, tables, tm):
    b, t, d = x.shape
    per_tok = shift.shape[1] != 1
    mod_spec = (pl.BlockSpec((1, tm, d), lambda i, j: (i, j, 0)) if per_tok
                else pl.BlockSpec((1, 1, d), lambda i, j: (i, 0, 0)))
    segs = (SEG_GM, SEG_QN, SEG_VG, SEG_RG, SEG_KVC, SEG_KVS, SEG_KVW, SEG_QG, SEG_KG, SEG_MISC)
    tab_spec = pl.BlockSpec((tm, LANE), lambda i, j: (j, 0))
    return pl.pallas_call(
        _inproj_kernel,
        grid=(b, t // tm),
        in_specs=[pl.BlockSpec((1, tm, d), lambda i, j: (i, j, 0)), mod_spec, mod_spec,
                  pl.BlockSpec((d, IN_PACKED), lambda i, j: (0, 0)),
                  pl.BlockSpec((1, IN_PACKED), lambda i, j: (0, 0)),
                  tab_spec, tab_spec, tab_spec],
        out_specs=[pl.BlockSpec((1, tm, w), lambda i, j: (i, j, 0)) for _, w in segs],
        out_shape=[jax.ShapeDtypeStruct((b, t, w), F32) for _, w in segs],
        compiler_params=_cparams("parallel", "parallel"),
        name="inproj",
    )(x, shift, scale, w_pack, b_pack, *tables)


CHUNKS = 128
PAGE_ROWS = 128
PAGES = 16


def _pack_cmp_weights(pos, w1, w2):
    pos2 = jnp.concatenate([pos, pos], axis=1)
    z1 = jnp.zeros_like(w1)
    bd1 = jnp.concatenate([jnp.concatenate([w1, z1], axis=2), jnp.concatenate([z1, w1], axis=2)], axis=1)
    w1p = jnp.concatenate([bd1[:CMP_STRIDE], bd1[CMP_STRIDE:]], axis=2).astype(BF16)
    z2 = jnp.zeros_like(w2)
    w2p = jnp.concatenate([jnp.concatenate([w2, z2], axis=1), jnp.concatenate([z2, w2], axis=1)], axis=0).astype(BF16)
    return pos2, w1p, w2p


def _fill_chunks(page_refs, xs_ref):
    for p, pr in enumerate(page_refs):
        for l in range(CMP_STRIDE):
            xs_ref[l, 8 * p:8 * p + 8, :] = pr[0, pl.ds(l, PAGE_ROWS // CMP_STRIDE, stride=CMP_STRIDE), :]


def _compress_chunks(xs_ref, pos_ref, w1_ref, w2_ref):
    hid2 = 2 * CMP_HIDDEN
    acc_a = jnp.zeros((CHUNKS, hid2), F32)
    acc_b = jnp.zeros((CHUNKS, hid2), F32)
    for l in range(CMP_STRIDE):
        x = xs_ref[l]
        acc_a = acc_a + jnp.dot((x + pos_ref[l:l + 1, :]).astype(BF16), w1_ref[l, :, 0:hid2], preferred_element_type=F32)
        acc_b = acc_b + jnp.dot((x + pos_ref[CMP_STRIDE + l:CMP_STRIDE + l + 1, :]).astype(BF16), w1_ref[l, :, hid2:2 * hid2],
                                preferred_element_type=F32)
    hid = acc_a + pltpu.roll(acc_b, CHUNKS - 1, 0)
    out = jnp.dot(jax.nn.gelu(hid).astype(BF16), w2_ref[...], preferred_element_type=F32)
    row = lax.broadcasted_iota(I32, out.shape, 0)
    return jnp.where(row < CHUNKS - 1, out, 0.0)


def _compress_kernel(pt_ref, *refs):
    k_pages, v_pages = refs[:PAGES], refs[PAGES:2 * PAGES]
    posk_ref, w1k_ref, w2k_ref, posv_ref, w1v_ref, w2v_ref, kc_ref, vc_ref, xk_ref, xv_ref = refs[2 * PAGES:]
    _fill_chunks(k_pages, xk_ref)
    _fill_chunks(v_pages, xv_ref)
    kc_ref[0] = _compress_chunks(xk_ref, posk_ref, w1k_ref, w2k_ref)
    vc_ref[0] = _compress_chunks(xv_ref, posv_ref, w1v_ref, w2v_ref)


def _page_spec(p, half):
    return pl.BlockSpec((1, PAGE_ROWS, LANE), lambda i, pt: (pt[i * PAGES + p], 0, half))


def _const_spec(shape):
    nd = len(shape)
    return pl.BlockSpec(shape, lambda i, pt: (0,) * nd)


def _compress(pages, page_ids, cmp_wk, cmp_wv):
    n_b = page_ids.shape[0] // PAGES
    consts = list(cmp_wk) + list(cmp_wv)
    grid_spec = pltpu.PrefetchScalarGridSpec(
        num_scalar_prefetch=1,
        grid=(n_b,),
        in_specs=[_page_spec(p, h) for h in range(2) for p in range(PAGES)] + [_const_spec(c.shape) for c in consts],
        out_specs=[pl.BlockSpec((1, CHUNKS, LANE), lambda i, pt: (i, 0, 0))] * 2,
        scratch_shapes=[pltpu.VMEM((CMP_STRIDE, CHUNKS, LANE), F32)] * 2,
    )
    return pl.pallas_call(
        _compress_kernel,
        grid_spec=grid_spec,
        out_shape=[jax.ShapeDtypeStruct((n_b, CHUNKS, LANE), F32)] * 2,
        compiler_params=_cparams("parallel"),
        name="compress",
    )(page_ids, *([pages] * (2 * PAGES)), *consts)


def _cover_tables(n_sel):
    c_start = np.arange(CHUNKS) * CMP_STRIDE
    s_start = np.arange(n_sel) * SEL_BLOCK
    cover = ((c_start[:, None] < s_start[None, :] + SEL_BLOCK) & (c_start[:, None] + CMP_LEN > s_start[None, :])).astype(np.float32)
    cover[CHUNKS - 1] = 0.0
    out = np.zeros((NSA_KV_HEADS, LANE, CHUNKS), np.float32)
    for h in range(NSA_KV_HEADS):
        out[h, h * 64:h * 64 + n_sel] = cover.T
    return jnp.asarray(out)


def _softmax_rows(s, valid):
    s = jnp.where(valid, s, NEG)
    m = jnp.max(s, axis=-1, keepdims=True)
    m = jnp.where(m > 0.5 * NEG, m, 0.0)
    p = jnp.where(valid, jnp.exp(s - m), 0.0)
    return p / jnp.maximum(jnp.sum(p, axis=-1, keepdims=True), 1e-30)


def _select_blocks(imp, n_sel, top_n):
    ridx = lax.broadcasted_iota(I32, imp.shape, 0)
    cnt = jnp.zeros(imp.shape, F32)
    for i in range(n_sel):
        vi = imp[i:i + 1, :]
        ahead = (vi > imp) | ((vi == imp) & (ridx > i))
        cnt = cnt + jnp.where(ahead, 1.0, 0.0)
    return jnp.where((cnt < top_n) & (ridx < n_sel), 1.0, 0.0)


def _cmp_attn_kernel(qn_ref, kc_ref, vc_ref, cov_ref, o_ref, sel_ref, *, tq, n_sel):
    qi = pl.program_id(1)
    kc = kc_ref[0].astype(BF16)
    vc = vc_ref[0].astype(BF16)
    qpos = qi * tq + lax.broadcasted_iota(I32, (tq, CHUNKS), 0)
    cidx = lax.broadcasted_iota(I32, (tq, CHUNKS), 1)
    valid = (cidx * CMP_STRIDE + CMP_LEN - 1 <= qpos) & (cidx < CHUNKS - 1)
    psum = [jnp.zeros((tq, CHUNKS), F32) for _ in range(NSA_KV_HEADS)]
    for hh in range(NSA_HEADS):
        q = qn_ref[0, :, hh * LANE:(hh + 1) * LANE].astype(BF16)
        p = _softmax_rows(_dot_nt(q, kc), valid)
        o_ref[0, :, hh * LANE:(hh + 1) * LANE] = jnp.dot(p.astype(BF16), vc, preferred_element_type=F32)
        psum[hh // NSA_GROUP] = psum[hh // NSA_GROUP] + p
    imp = (_dot_nt(cov_ref[0], psum[0], lax.Precision.HIGHEST) + _dot_nt(cov_ref[1], psum[1], lax.Precision.HIGHEST))
    blk = lax.broadcasted_iota(I32, (LANE, tq), 0) & 63
    qpos_t = qi * tq + lax.broadcasted_iota(I32, (LANE, tq), 1)
    cur = qpos_t // SEL_BLOCK
    forced = (blk == 0) | (blk == cur) | (blk == cur - 1)
    imp = jnp.where(forced, FORCE_SCORE, jnp.where(blk * SEL_BLOCK <= qpos_t, imp, -FORCE_SCORE))
    sel_t = jnp.concatenate([_select_blocks(imp[0:64], n_sel, SEL_TOP_N), _select_blocks(imp[64:128], n_sel, SEL_TOP_N)], axis=0)
    sel_ref[0] = sel_t.T


def _cmp_attn(qn, kc, vc, tq):
    b, t, _ = qn.shape
    n_sel = -(-t // SEL_BLOCK)
    cov = _cover_tables(n_sel)
    return pl.pallas_call(
        functools.partial(_cmp_attn_kernel, tq=tq, n_sel=n_sel),
        grid=(b, t // tq),
        in_specs=[pl.BlockSpec((1, tq, NSA_HEADS * LANE), lambda i, j: (i, j, 0)),
                  pl.BlockSpec((1, CHUNKS, LANE), lambda i, j: (i, 0, 0)),
                  pl.BlockSpec((1, CHUNKS, LANE), lambda i, j: (i, 0, 0)),
                  pl.BlockSpec((NSA_KV_HEADS, LANE, CHUNKS), lambda i, j: (0, 0, 0))],
        out_specs=[pl.BlockSpec((1, tq, NSA_HEADS * LANE), lambda i, j: (i, j, 0)),
                   pl.BlockSpec((1, tq, LANE), lambda i, j: (i, j, 0))],
        out_shape=[jax.ShapeDtypeStruct((b, t, NSA_HEADS * LANE), F32), jax.ShapeDtypeStruct((b, t, LANE), F32)],
        compiler_params=_cparams("parallel", "parallel"),
        name="cmp_attn",
    )(qn, kc, vc, cov)


def _expand_tables(t, tk):
    n_kt = t // tk
    key_blk = (np.arange(t) // SEL_BLOCK).reshape(n_kt, 1, tk)
    rows = np.arange(LANE).reshape(1, LANE, 1)
    out = np.stack([(key_blk == rows - 64 * h) for h in range(NSA_KV_HEADS)]).astype(np.float32)
    return jnp.asarray(out, dtype=BF16)


def _flash_kernel(*refs, tq, tk, banded):
    if banded:
        qn_ref, kv_ref, o_ref, q_scr, m_scr, l_scr, acc_scr = refs
    else:
        qn_ref, kv_ref, sel_ref, e_ref, o_ref, q_scr, m_scr, l_scr, acc_scr = refs
    qi = pl.program_id(1)
    rows = NSA_HEADS * tq
    for hh in range(NSA_HEADS):
        q_scr[hh * tq:(hh + 1) * tq, :] = qn_ref[0, :, hh * LANE:(hh + 1) * LANE].astype(BF16)
    m_scr[...] = jnp.full((rows, 1), NEG, F32)
    l_scr[...] = jnp.zeros((rows, 1), F32)
    acc_scr[...] = jnp.zeros((rows, LANE), F32)
    qpos = qi * tq + lax.broadcasted_iota(I32, (tq, tk), 0)

    def body(kt, carry):
        k0 = pl.multiple_of(kt * tk, tk)
        k = kv_ref[0, pl.ds(k0, tk), 0:LANE].astype(BF16)
        v = kv_ref[0, pl.ds(k0, tk), LANE:2 * LANE].astype(BF16)
        s = _dot_nt(q_scr[...], k)
        kpos = k0 + lax.broadcasted_iota(I32, (tq, tk), 1)
        rel = qpos - kpos
        if banded:
            bias1 = jnp.where((rel >= 0) & (rel <= WINDOW), 0.0, NEG)
            bias = jnp.concatenate([bias1] * NSA_HEADS, axis=0)
        else:
            sel = sel_ref[0].astype(BF16)
            per_head = []
            for h in range(NSA_KV_HEADS):
                picked = jnp.dot(sel, e_ref[h, kt], preferred_element_type=F32) > 0.5
                per_head += [jnp.where(picked & (rel >= 0), 0.0, NEG)] * NSA_GROUP
            bias = jnp.concatenate(per_head, axis=0)
        s = s + bias
        m_old = m_scr[...]
        m_new = jnp.maximum(m_old, jnp.max(s, axis=-1, keepdims=True))
        alpha = jnp.exp(m_old - m_new)
        p = jnp.exp(s - m_new)
        l_scr[...] = alpha * l_scr[...] + jnp.sum(p, axis=-1, keepdims=True)
        acc_scr[...] = alpha * acc_scr[...] + jnp.dot(p.astype(BF16), v, preferred_element_type=F32)
        m_scr[...] = m_new
        return carry

    hi = (qi * tq + tq + tk - 1) // tk
    lo = jnp.maximum(qi * tq - WINDOW, 0) // tk if banded else 0
    lax.fori_loop(lo, hi, body, 0)
    out = acc_scr[...] / l_scr[...]
    for hh in range(NSA_HEADS):
        o_ref[0, :, hh * LANE:(hh + 1) * LANE] = out[hh * tq:(hh + 1) * tq]


def _flash(qn, kv, sel, tq, tk):
    b, t, _ = qn.shape
    banded = sel is None
    rows = NSA_HEADS * tq
    in_specs = [pl.BlockSpec((1, tq, NSA_HEADS * LANE), lambda i, j: (i, j, 0)),
                pl.BlockSpec((1, t, 2 * LANE), lambda i, j: (i, 0, 0))]
    args = [qn, kv]
    if not banded:
        in_specs += [pl.BlockSpec((1, tq, LANE), lambda i, j: (i, j, 0)),
                     pl.BlockSpec((NSA_KV_HEADS, t // tk, LANE, tk), lambda i, j: (0, 0, 0, 0))]
        args += [sel, _expand_tables(t, tk)]
    return pl.pallas_call(
        functools.partial(_flash_kernel, tq=tq, tk=tk, banded=banded),
        grid=(b, t // tq),
        in_specs=in_specs,
        out_specs=pl.BlockSpec((1, tq, NSA_HEADS * LANE), lambda i, j: (i, j, 0)),
        out_shape=jax.ShapeDtypeStruct((b, t, NSA_HEADS * LANE), F32),
        scratch_shapes=[pltpu.VMEM((rows, LANE), BF16), pltpu.VMEM((rows, 1), F32), pltpu.VMEM((rows, 1), F32),
                        pltpu.VMEM((rows, LANE), F32)],
        compiler_params=_cparams("parallel", "parallel"),
        name="win_attn" if banded else "sel_attn",
    )(*args)


def _dec_softmax(scores, vals):
    m = scores[0].max(axis=-1, keepdims=True)
    for s in scores[1:]:
        m = jnp.maximum(m, s.max(axis=-1, keepdims=True))
    den = jnp.zeros_like(m)
    out = jnp.zeros((m.shape[0], LANE), F32)
    for s, v in zip(scores, vals):
        p = jnp.exp(s - m)
        den = den + p.sum(axis=-1, keepdims=True)
        out = out + (p * v if s.shape[1] == 1 else jnp.dot(p.astype(BF16), v, preferred_element_type=F32))
    return out / den


def _nsa_decode_kernel(pt_ref, *refs, past_len, n_sel):
    pages = refs[:PAGES]
    (qn_ref, kc_ref, vc_ref, kvs_ref, kvw_ref, win_ref, cov_ref,
     ocmp_ref, osel_ref, owin_ref, wnew_ref) = refs[PAGES:]
    nh = NSA_HEADS
    q = jnp.concatenate([qn_ref[0, :, hh * LANE:(hh + 1) * LANE] for hh in range(nh)], axis=0)
    qb = q.astype(BF16)
    cidx = lax.broadcasted_iota(I32, (nh, CHUNKS), 1)
    valid = (cidx * CMP_STRIDE + CMP_LEN - 1 <= past_len) & (cidx < CHUNKS - 1)
    p = _softmax_rows(_dot_nt(qb, kc_ref[0].astype(BF16)), valid)
    o_cmp = jnp.dot(p.astype(BF16), vc_ref[0].astype(BF16), preferred_element_type=F32)
    imp = None
    for h in range(NSA_KV_HEADS):
        ps = jnp.sum(p[h * NSA_GROUP:(h + 1) * NSA_GROUP], axis=0, keepdims=True)
        term = _dot_nt(cov_ref[h], jnp.broadcast_to(ps, (LANE, CHUNKS)), lax.Precision.HIGHEST)
        imp = term if imp is None else imp + term
    blk = lax.broadcasted_iota(I32, (LANE, LANE), 0) & 63
    cur = past_len // SEL_BLOCK
    forced = (blk == 0) | (blk == cur) | (blk == cur - 1)
    imp = jnp.where(forced, FORCE_SCORE, jnp.where(blk * SEL_BLOCK <= past_len, imp, -FORCE_SCORE))
    sel_t = jnp.concatenate([_select_blocks(imp[0:64], n_sel, min(SEL_TOP_N, n_sel)),
                             _select_blocks(imp[64:128], n_sel, min(SEL_TOP_N, n_sel))], axis=0)
    sel = sel_t.T[0:1, :]
    head_of_row = lax.broadcasted_iota(I32, (nh, 1), 0) // NSA_GROUP

    def picked(s):
        return jnp.where(head_of_row == 0, sel[:, s:s + 1], sel[:, 64 + s:64 + s + 1])

    first_half = lax.broadcasted_iota(I32, (nh, PAGE_ROWS), 1) < SEL_BLOCK
    scores, vals = [], []
    for pg, pr in enumerate(pages):
        s = _dot_nt(qb, pr[0, :, 0:LANE].astype(BF16))
        ok = jnp.where(first_half, picked(2 * pg), picked(2 * pg + 1)) > 0.5
        scores.append(jnp.where(ok, s, NEG))
        vals.append(pr[0, :, LANE:2 * LANE].astype(BF16))
    s_new = jnp.sum(q * kvs_ref[0, :, 0:LANE], axis=-1, keepdims=True)
    scores.append(jnp.where(picked(past_len // SEL_BLOCK) > 0.5, s_new, NEG))
    vals.append(kvs_ref[0, :, LANE:2 * LANE])
    o_sel = _dec_softmax(scores, vals)
    n_win = win_ref.shape[1]
    kpos = past_len - n_win + lax.broadcasted_iota(I32, (nh, n_win), 1)
    rel = past_len - kpos
    s_win = jnp.where((kpos >= 0) & (rel >= 0) & (rel <= WINDOW), _dot_nt(qb, win_ref[0, :, 0:LANE].astype(BF16)), NEG)
    s_new = jnp.sum(q * kvw_ref[0, :, 0:LANE], axis=-1, keepdims=True)
    o_win = _dec_softmax([s_win, s_new], [win_ref[0, :, LANE:2 * LANE].astype(BF16), kvw_ref[0, :, LANE:2 * LANE]])
    for hh in range(nh):
        ocmp_ref[0, :, hh * LANE:(hh + 1) * LANE] = o_cmp[hh:hh + 1]
        osel_ref[0, :, hh * LANE:(hh + 1) * LANE] = o_sel[hh:hh + 1]
        owin_ref[0, :, hh * LANE:(hh + 1) * LANE] = o_win[hh:hh + 1]
    wnew_ref[0, 0:n_win - 1, :] = win_ref[0, 1:n_win, :]
    wnew_ref[0, n_win - 1:n_win, :] = kvw_ref[0]


def _nsa_decode(qn, kc, vc, sel_pages, page_ids, kvs_new, kvw_new, win_state):
    b = qn.shape[0]
    n_win = win_state.shape[1]
    past_len = PAGES * PAGE_ROWS
    n_sel = -(-(past_len + 1) // SEL_BLOCK)
    cov = _cover_tables(n_sel)

    def per_b(shape):
        nd = len(shape)
        return pl.BlockSpec((1,) + shape[1:], lambda i, pt: (i,) + (0,) * (nd - 1))

    slots = NSA_HEADS * LANE
    grid_spec = pltpu.PrefetchScalarGridSpec(
        num_scalar_prefetch=1,
        grid=(b,),
        in_specs=[pl.BlockSpec((1, PAGE_ROWS, 2 * LANE), (lambda i, pt, p=p: (pt[i * PAGES + p], 0, 0))) for p in range(PAGES)]
        + [per_b(qn.shape), per_b(kc.shape), per_b(vc.shape), per_b(kvs_new.shape), per_b(kvw_new.shape), per_b(win_state.shape),
           _const_spec(cov.shape)],
        out_specs=[per_b((b, 1, slots))] * 3 + [per_b(win_state.shape)],
    )
    return pl.pallas_call(
        functools.partial(_nsa_decode_kernel, past_len=past_len, n_sel=n_sel),
        grid_spec=grid_spec,
        out_shape=[jax.ShapeDtypeStruct((b, 1, slots), F32)] * 3 + [jax.ShapeDtypeStruct(win_state.shape, F32)],
        compiler_params=_cparams("parallel"),
        name="nsa_decode",
    )(page_ids, *([sel_pages] * PAGES), qn, kc, vc, kvs_new, kvw_new, win_state, cov)


def _dot_tn(a, b):
    return lax.dot_general(a, b, (((0,), (0,)), ((), ())), preferred_element_type=F32)


def _cumsum_table():
    r = np.arange(LANE)
    return jnp.asarray(((r[:, None] // GLA_SUB == r[None, :] // GLA_SUB) & (r[None, :] <= r[:, None])).astype(np.float32))


def _gla_kernel(*refs, t, t_valid, has_state):
    if has_state:
        qg_ref, kg_ref, vg_ref, misc_ref, wa_ref, ba_ref, lt_ref, s0_ref, o_ref, s_ref, b_scr, st_scr = refs
    else:
        qg_ref, kg_ref, vg_ref, misc_ref, wa_ref, ba_ref, lt_ref, o_ref, s_ref, b_scr, st_scr = refs
    z = jnp.dot(misc_ref[0], wa_ref[...], preferred_element_type=F32, precision=lax.Precision.HIGHEST) + ba_ref[...]
    la = (jnp.minimum(z, 0.0) - jnp.log1p(jnp.exp(-jnp.abs(z)))) * (1.0 / GLA_TAU)
    if t_valid < t:
        la = jnp.where(lax.broadcasted_iota(I32, la.shape, 0) < t_valid, la, 0.0)
    tile = min(t, LANE)
    for r in range(0, t, tile):
        b_scr[r:r + tile, :] = jnp.dot(lt_ref[0:tile, 0:tile], la[r:r + tile, :], preferred_element_type=F32,
                                       precision=lax.Precision.HIGHEST)
    if has_state:
        st_scr[...] = s0_ref[0].reshape(2 * GLA_DK, GLA_DV).T
    else:
        st_scr[...] = jnp.zeros((GLA_DV, LANE), F32)
    head_a = lax.broadcasted_iota(I32, (GLA_SUB, LANE), 1) < GLA_DK
    row = lax.broadcasted_iota(I32, (GLA_SUB, LANE), 0)

    def chunk(c, carry):
        r0 = pl.multiple_of(c * GLA_SUB, GLA_SUB)
        q = qg_ref[0, pl.ds(r0, GLA_SUB), :]
        k = kg_ref[0, pl.ds(r0, GLA_SUB), :]
        v = vg_ref[0, pl.ds(r0, GLA_SUB), :]
        b = b_scr[pl.ds(r0, GLA_SUB), :]
        b_last = b[GLA_SUB - 1:GLA_SUB, :]
        st = st_scr[...]
        st_b = st.astype(BF16)
        qe = q * jnp.exp(b)
        o_a = _dot_nt(jnp.where(head_a, qe, 0.0).astype(BF16), st_b)
        o_b = _dot_nt(jnp.where(head_a, 0.0, qe).astype(BF16), st_b)
        for j in range(GLA_SUB):
            w = q * k[j:j + 1, :] * jnp.exp(jnp.minimum(b - b[j:j + 1, :], 0.0))
            w = jnp.where(row >= j, w, 0.0)
            a_a = jnp.sum(jnp.where(head_a, w, 0.0), axis=-1, keepdims=True)
            a_b = jnp.sum(jnp.where(head_a, 0.0, w), axis=-1, keepdims=True)
            o_a = o_a + a_a * v[j:j + 1, 0:GLA_DV]
            o_b = o_b + a_b * v[j:j + 1, GLA_DV:2 * GLA_DV]
        o_ref[0, pl.ds(r0, GLA_SUB), 0:GLA_DV] = o_a
        o_ref[0, pl.ds(r0, GLA_SUB), GLA_DV:2 * GLA_DV] = o_b
        kd = k * jnp.exp(b_last - b)
        upd = (_dot_tn(v[:, 0:GLA_DV].astype(BF16), jnp.where(head_a, kd, 0.0).astype(BF16))
               + _dot_tn(v[:, GLA_DV:2 * GLA_DV].astype(BF16), jnp.where(head_a, 0.0, kd).astype(BF16)))
        st_scr[...] = jnp.exp(b_last) * st + upd
        return carry

    lax.fori_loop(0, t // GLA_SUB, chunk, 0)
    s_ref[0] = st_scr[...].T.reshape(2, GLA_DK, GLA_DV)


def _gla(qg, kg, vg, misc, wa_pad, ba, s0, t_valid):
    b, t, _ = qg.shape
    pairs = GLA_HEADS // 2
    has_state = s0 is not None
    in_specs = [pl.BlockSpec((1, t, LANE), lambda i, p: (i, 0, p)),
                pl.BlockSpec((1, t, LANE), lambda i, p: (i, 0, p)),
                pl.BlockSpec((1, t, 2 * GLA_DV), lambda i, p: (i, 0, p)),
                pl.BlockSpec((1, t, LANE), lambda i, p: (i, 0, 0)),
                pl.BlockSpec((LANE, LANE), lambda i, p: (0, p)),
                pl.BlockSpec((1, LANE), lambda i, p: (0, p)),
                pl.BlockSpec((LANE, LANE), lambda i, p: (0, 0))]
    args = [qg, kg, vg, misc, wa_pad, ba.reshape(1, -1), _cumsum_table()]
    if has_state:
        in_specs.append(pl.BlockSpec((1, 2, GLA_DK, GLA_DV), lambda i, p: (i, p, 0, 0)))
        args.append(s0)
    return pl.pallas_call(
        functools.partial(_gla_kernel, t=t, t_valid=t_valid, has_state=has_state),
        grid=(b, pairs),
        in_specs=in_specs,
        out_specs=[pl.BlockSpec((1, t, 2 * GLA_DV), lambda i, p: (i, 0, p)),
                   pl.BlockSpec((1, 2, GLA_DK, GLA_DV), lambda i, p: (i, p, 0, 0))],
        out_shape=[jax.ShapeDtypeStruct((b, t, GLA_HEADS * GLA_DV), F32),
                   jax.ShapeDtypeStruct((b, GLA_HEADS, GLA_DK, GLA_DV), F32)],
        scratch_shapes=[pltpu.VMEM((t, LANE), F32), pltpu.VMEM((GLA_DV, LANE), F32)],
        compiler_params=_cparams("parallel", "parallel"),
        name="gla",
    )(*args)


def _gate_expand_table():
    out = np.zeros((3, LANE, NSA_HEADS * LANE), np.float32)
    for hh in range(NSA_HEADS):
        for j in range(3):
            out[j, MISC_GN + 3 * hh + j, hh * LANE:(hh + 1) * LANE] = 1.0
    return jnp.asarray(out, dtype=BF16)


def _pad_br_a(w_br_a):
    zero = jnp.zeros((HEAD_DIM, w_br_a.shape[1]), w_br_a.dtype)
    parts = []
    for hh in range(NSA_HEADS):
        wh = w_br_a[hh * HEAD_DIM:(hh + 1) * HEAD_DIM]
        parts += [wh, zero] if hh // NSA_GROUP == 0 else [zero, wh]
    return jnp.concatenate(parts, axis=0).astype(BF16)


def _layer_norm(v, g, b):
    mu = jnp.mean(v, axis=-1, keepdims=True)
    var = jnp.mean(jnp.square(v - mu), axis=-1, keepdims=True)
    return (v - mu) * lax.rsqrt(var + LN_EPS) * g + b


def _mixer_tail_kernel(ocmp_ref, osel_ref, owin_ref, misc_ref, ogla_ref, rg_ref, gm_ref, x_ref, gate_ref, scf_ref, shf_ref,
                       ex_ref, ng_ref, wa_ref, wb_ref, wo_ref, lg_ref, lb_ref, x1_ref, xm_ref):
    sig = jax.nn.sigmoid(misc_ref[0])
    sig_hi = sig.astype(BF16)
    sig_lo = (sig - sig_hi.astype(F32)).astype(BF16)
    o_nsa = None
    for j, ref in enumerate((ocmp_ref, osel_ref, owin_ref)):
        g = (jnp.dot(sig_hi, ex_ref[j], preferred_element_type=F32) + jnp.dot(sig_lo, ex_ref[j], preferred_element_type=F32))
        o_nsa = g * ref[0] if o_nsa is None else o_nsa + g * ref[0]
    br_a = _bdot(o_nsa, wa_ref[...])
    heads = []
    for h in range(GLA_HEADS):
        seg = ogla_ref[0, :, h * GLA_DV:(h + 1) * GLA_DV]
        mu = jnp.mean(seg, axis=-1, keepdims=True)
        var = jnp.mean(jnp.square(seg - mu), axis=-1, keepdims=True)
        r = rg_ref[0, :, h * GLA_DV:(h + 1) * GLA_DV]
        heads.append((seg - mu) * lax.rsqrt(var + LN_EPS) * ng_ref[:, h * GLA_DV:(h + 1) * GLA_DV] * (r * jax.nn.sigmoid(r)))
    br_b = _bdot(jnp.concatenate(heads, axis=1), wb_ref[...])
    gm_a = jax.nn.sigmoid(gm_ref[0, :, 0:D_MODEL])
    gm_b = jax.nn.sigmoid(gm_ref[0, :, D_MODEL:2 * D_MODEL])
    y = _bdot(gm_a * br_a + gm_b * br_b, wo_ref[...])
    x1 = _layer_norm(DN_ALPHA * x_ref[0] + gate_ref[0] * y, lg_ref[...], lb_ref[...])
    x1_ref[0] = x1
    xm_ref[0] = x1 * (1.0 + scf_ref[0]) + shf_ref[0]


def _mixer_tail(ocmp, osel, owin, misc, ogla, rg, gm, x, gate_m, scale_f, shift_f, consts, tm):
    b, t, d = x.shape
    per_tok = gate_m.shape[1] != 1

    def tok(w):
        return pl.BlockSpec((1, tm, w), lambda i, j: (i, j, 0))

    mod_spec = tok(d) if per_tok else pl.BlockSpec((1, 1, d), lambda i, j: (i, 0, 0))

    def const(a):
        nd = a.ndim
        return pl.BlockSpec(a.shape, lambda i, j: (0,) * nd)

    return pl.pallas_call(
        _mixer_tail_kernel,
        grid=(b, t // tm),
        in_specs=[tok(NSA_HEADS * LANE)] * 3 + [tok(LANE), tok(GLA_HEADS * GLA_DV), tok(GLA_HEADS * GLA_DV), tok(2 * d), tok(d),
                                               mod_spec, mod_spec, mod_spec] + [const(c) for c in consts],
        out_specs=[tok(d), tok(d)],
        out_shape=[jax.ShapeDtypeStruct((b, t, d), F32)] * 2,
        compiler_params=_cparams("parallel", "parallel"),
        name="mixer_tail",
    )(ocmp, osel, owin, misc, ogla, rg, gm, x, gate_m, scale_f, shift_f, *consts)


ROUTE_TILE = LANE


def _first_index(hit, iota, size, axis):
    return jnp.min(jnp.where(hit, iota, size), axis=axis, keepdims=True)


def _router_kernel(xm_ref, wr_ref, bias_ref, tri_ref, eidx_ref, rank_ref, wrow_ref, cnt_ref, carry_ref):
    i = pl.program_id(0)
    tm = ROUTE_TILE
    per = N_EXPERTS // N_GROUPS

    @pl.when(i == 0)
    def _():
        carry_ref[...] = jnp.zeros_like(carry_ref)

    logits = _dot_nt(wr_ref[...], xm_ref[...], lax.Precision.HIGHEST)
    s = jax.nn.sigmoid(logits)
    sb = s + bias_ref[...]
    sb3 = sb.reshape(N_GROUPS, per, tm)
    in_grp = lax.broadcasted_iota(I32, sb3.shape, 1)
    m1 = jnp.max(sb3, axis=1, keepdims=True)
    first = _first_index(sb3 == m1, in_grp, per, 1)
    m2 = jnp.max(jnp.where(in_grp == first, NEG, sb3), axis=1, keepdims=True)
    gs = (m1 + m2).reshape(N_GROUPS, tm)
    g_iota = lax.broadcasted_iota(I32, gs.shape, 0)
    g_keep = jnp.zeros(gs.shape, jnp.bool_)
    for _ in range(TOPK_GROUPS):
        pick = g_iota == _first_index(gs == jnp.max(gs, axis=0, keepdims=True), g_iota, N_GROUPS, 0)
        g_keep = g_keep | pick
        gs = jnp.where(pick, NEG, gs)
    sbm = jnp.where(g_keep.reshape(N_GROUPS, 1, tm), sb3, NEG).reshape(N_EXPERTS, tm)
    e_iota = lax.broadcasted_iota(I32, sbm.shape, 0)
    idxs, sels = [], []
    onehot = jnp.zeros(sbm.shape, F32)
    for _ in range(TOP_K):
        idx = _first_index(sbm == jnp.max(sbm, axis=0, keepdims=True), e_iota, N_EXPERTS, 0)
        pick = e_iota == idx
        idxs.append(idx)
        sels.append(jnp.sum(jnp.where(pick, s, 0.0), axis=0, keepdims=True))
        sbm = jnp.where(pick, NEG, sbm)
        onehot = onehot + jnp.where(pick, 1.0, 0.0)
    sel = jnp.concatenate(sels, axis=0)
    wts = sel / jnp.sum(sel, axis=0, keepdims=True) * ROUTED_SCALE
    carry = carry_ref[...]
    before = carry + jnp.dot(onehot.astype(BF16), tri_ref[...], preferred_element_type=F32)
    ranks = [jnp.sum(jnp.where(e_iota == idx, before, 0.0), axis=0, keepdims=True) for idx in idxs]
    eidx_ref[...] = jnp.concatenate(idxs, axis=0)
    rank_ref[...] = jnp.concatenate(ranks, axis=0).astype(I32)
    wrow_ref[...] = jnp.concatenate([wts, jnp.zeros((LANE - TOP_K, tm), F32)], axis=0).T
    carry = carry + jnp.sum(onehot, axis=1, keepdims=True)
    carry_ref[...] = carry
    cnt_ref[...] = carry


def _router(xm, w_router, router_bias):
    n, d = xm.shape
    tm = ROUTE_TILE
    r = np.arange(tm)
    tri = jnp.asarray((r[:, None] < r[None, :]).astype(np.float32), dtype=BF16)
    return pl.pallas_call(
        _router_kernel,
        grid=(n // tm,),
        in_specs=[pl.BlockSpec((tm, d), lambda i: (i, 0)),
                  pl.BlockSpec((N_EXPERTS, d), lambda i: (0, 0)),
                  pl.BlockSpec((N_EXPERTS, 1), lambda i: (0, 0)),
                  pl.BlockSpec((tm, tm), lambda i: (0, 0))],
        out_specs=[pl.BlockSpec((TOP_K, tm), lambda i: (0, i)),
                   pl.BlockSpec((TOP_K, tm), lambda i: (0, i)),
                   pl.BlockSpec((tm, LANE), lambda i: (i, 0)),
                   pl.BlockSpec((N_EXPERTS, LANE), lambda i: (0, 0))],
        out_shape=[jax.ShapeDtypeStruct((TOP_K, n), I32), jax.ShapeDtypeStruct((TOP_K, n), I32),
                   jax.ShapeDtypeStruct((n, LANE), F32), jax.ShapeDtypeStruct((N_EXPERTS, LANE), F32)],
        scratch_shapes=[pltpu.VMEM((N_EXPERTS, LANE), F32)],
        compiler_params=_cparams("arbitrary"),
        name="router",
    )(xm, w_router.T, router_bias.reshape(N_EXPERTS, 1), tri)


def _row_copy(src_ref, src_row, dst_ref, dst_row, sem):
    return pltpu.make_async_copy(src_ref.at[pl.ds(src_row, 1)], dst_ref.at[pl.ds(dst_row, 1)], sem)


def _dispatch_kernel(dest_ref, pad_ref, nb_ref, xm_ref, xs_ref, zero_ref, sem):
    i = pl.program_id(0)
    tm = ROUTE_TILE
    bm = EXPERT_BLOCK

    def start_row(r, c):
        for k in range(TOP_K):
            _row_copy(xm_ref, r, xs_ref, dest_ref[0, k, r], sem).start()
        return c

    def wait_row(r, c):
        for k in range(TOP_K):
            _row_copy(xm_ref, r, xs_ref, dest_ref[0, k, r], sem).wait()
        return c

    lax.fori_loop(0, tm, start_row, 0)
    lax.fori_loop(0, tm, wait_row, 0)

    @pl.when(i == pl.num_programs(0) - 1)
    def _():
        zero_ref[...] = jnp.zeros_like(zero_ref)

        def per_expert(e, c):
            lo, hi = pad_ref[0, e], pad_ref[1, e]
            lax.fori_loop(lo, hi, lambda r, cc: (_row_copy(zero_ref, 0, xs_ref, r, sem).start(), cc)[1], 0)
            lax.fori_loop(lo, hi, lambda r, cc: (_row_copy(zero_ref, 0, xs_ref, r, sem).wait(), cc)[1], 0)
            return c

        lax.fori_loop(0, N_EXPERTS, per_expert, 0)

        def tail_copy(blk):
            return pltpu.make_async_copy(zero_ref, xs_ref.at[pl.ds(blk * bm, bm)], sem)

        n_blocks = xs_ref.shape[0] // bm
        lax.fori_loop(nb_ref[0], n_blocks, lambda blk, c: (tail_copy(blk).start(), c)[1], 0)
        lax.fori_loop(nb_ref[0], n_blocks, lambda blk, c: (tail_copy(blk).wait(), c)[1], 0)


def _dispatch(xm, dest_tiles, pad_range, n_used, n_rows):
    n, d = xm.shape
    tm = ROUTE_TILE
    return pl.pallas_call(
        _dispatch_kernel,
        grid=(n // tm,),
        in_specs=[pl.BlockSpec((1, TOP_K, tm), lambda i: (i, 0, 0), memory_space=pltpu.SMEM),
                  pl.BlockSpec(memory_space=pltpu.SMEM),
                  pl.BlockSpec(memory_space=pltpu.SMEM),
                  pl.BlockSpec((tm, d), lambda i: (i, 0))],
        out_specs=pl.BlockSpec(memory_space=pl.ANY),
        out_shape=jax.ShapeDtypeStruct((n_rows, d), F32),
        scratch_shapes=[pltpu.VMEM((EXPERT_BLOCK, d), F32), pltpu.SemaphoreType.DMA(())],
        compiler_params=_cparams("arbitrary"),
        name="dispatch",
    )(dest_tiles, pad_range, n_used, xm)


def _experts_kernel(be_ref, nb_ref, xs_ref, wg_ref, wu_ref, wd_ref, ys_ref, wg_s, wu_s, wd_s):
    j = pl.program_id(0)
    used = j < nb_ref[0]
    changed = (j == 0) | (be_ref[j] != be_ref[jnp.maximum(j - 1, 0)])

    @pl.when(used & changed)
    def _():
        wg_s[...] = wg_ref[0].astype(BF16)
        wu_s[...] = wu_ref[0].astype(BF16)
        wd_s[...] = wd_ref[0].astype(BF16)

    @pl.when(used)
    def _():
        x = xs_ref[...].astype(BF16)
        g = jnp.dot(x, wg_s[...], preferred_element_type=F32)
        u = jnp.dot(x, wu_s[...], preferred_element_type=F32)
        ys_ref[...] = jnp.dot((g * jax.nn.sigmoid(g) * u).astype(BF16), wd_s[...], preferred_element_type=F32)

    @pl.when(jnp.logical_not(used))
    def _():
        ys_ref[...] = jnp.zeros_like(ys_ref)


def _experts(xs, block_e, n_used, w_gate, w_up, w_down):
    n_rows, d = xs.shape
    bm = EXPERT_BLOCK
    n_blocks = n_rows // bm
    f = w_gate.shape[2]

    def blk(j, be, nb):
        return jnp.minimum(j, jnp.maximum(nb[0] - 1, 0))

    grid_spec = pltpu.PrefetchScalarGridSpec(
        num_scalar_prefetch=2,
        grid=(n_blocks,),
        in_specs=[pl.BlockSpec((bm, d), lambda j, be, nb: (blk(j, be, nb), 0)),
                  pl.BlockSpec((1, d, f), lambda j, be, nb: (be[blk(j, be, nb)], 0, 0)),
                  pl.BlockSpec((1, d, f), lambda j, be, nb: (be[blk(j, be, nb)], 0, 0)),
                  pl.BlockSpec((1, f, d), lambda j, be, nb: (be[blk(j, be, nb)], 0, 0))],
        out_specs=pl.BlockSpec((bm, d), lambda j, be, nb: (j, 0)),
        scratch_shapes=[pltpu.VMEM((d, f), BF16), pltpu.VMEM((d, f), BF16), pltpu.VMEM((f, d), BF16)],
    )
    return pl.pallas_call(
        _experts_kernel,
        grid_spec=grid_spec,
        out_shape=jax.ShapeDtypeStruct((n_rows, d), F32),
        compiler_params=_cparams("arbitrary"),
        name="experts",
    )(block_e, n_used, xs, w_gate, w_up, w_down)


def _combine_kernel(dest_ref, ys_ref, wrow_ref, xm_ref, sg_ref, su_ref, sd_ref, out_ref, buf_ref, sem):
    tm = ROUTE_TILE

    def start_row(r, c):
        for k in range(TOP_K):
            _row_copy(ys_ref, dest_ref[0, k, r], buf_ref.at[k], r, sem).start()
        return c

    def wait_row(r, c):
        for k in range(TOP_K):
            _row_copy(ys_ref, dest_ref[0, k, r], buf_ref.at[k], r, sem).wait()
        return c

    lax.fori_loop(0, tm, start_row, 0)
    x = xm_ref[...].astype(BF16)
    g = jnp.dot(x, sg_ref[...], preferred_element_type=F32)
    u = jnp.dot(x, su_ref[...], preferred_element_type=F32)
    shared = jnp.dot((g * jax.nn.sigmoid(g) * u).astype(BF16), sd_ref[...], preferred_element_type=F32)
    lax.fori_loop(0, tm, wait_row, 0)
    w = wrow_ref[...]
    routed = w[:, 0:1] * buf_ref[0]
    for k in range(1, TOP_K):
        routed = routed + w[:, k:k + 1] * buf_ref[k]
    out_ref[...] = routed + shared


def _combine(ys, dest_tiles, wrow, xm, ws_gate, ws_up, ws_down):
    n, d = xm.shape
    tm = ROUTE_TILE
    f = ws_gate.shape[1]
    return pl.pallas_call(
        _combine_kernel,
        grid=(n // tm,),
        in_specs=[pl.BlockSpec((1, TOP_K, tm), lambda i: (i, 0, 0), memory_space=pltpu.SMEM),
                  pl.BlockSpec(memory_space=pl.ANY),
                  pl.BlockSpec((tm, LANE), lambda i: (i, 0)),
                  pl.BlockSpec((tm, d), lambda i: (i, 0)),
                  pl.BlockSpec((d, f), lambda i: (0, 0)),
                  pl.BlockSpec((d, f), lambda i: (0, 0)),
                  pl.BlockSpec((f, d), lambda i: (0, 0))],
        out_specs=pl.BlockSpec((tm, d), lambda i: (i, 0)),
        out_shape=jax.ShapeDtypeStruct((n, d), F32),
        scratch_shapes=[pltpu.VMEM((TOP_K, tm, d), F32), pltpu.SemaphoreType.DMA(())],
        compiler_params=_cparams("arbitrary"),
        name="combine",
    )(dest_tiles, ys, wrow, xm, ws_gate.astype(BF16), ws_up.astype(BF16), ws_down.astype(BF16))


def _final_ln_kernel(x1_ref, moe_ref, gate_ref, g_ref, b_ref, y_ref):
    y_ref[0] = _layer_norm(DN_ALPHA * x1_ref[0] + gate_ref[0] * moe_ref[0], g_ref[...], b_ref[...])


def _final_ln(x1, moe, gate_f, ln_g, ln_b, tm):
    b, t, d = x1.shape
    per_tok = gate_f.shape[1] != 1
    tok = pl.BlockSpec((1, tm, d), lambda i, j: (i, j, 0))
    mod_spec = tok if per_tok else pl.BlockSpec((1, 1, d), lambda i, j: (i, 0, 0))
    vec = pl.BlockSpec((1, d), lambda i, j: (0, 0))
    return pl.pallas_call(
        _final_ln_kernel,
        grid=(b, t // tm),
        in_specs=[tok, tok, mod_spec, vec, vec],
        out_specs=tok,
        out_shape=jax.ShapeDtypeStruct((b, t, d), F32),
        compiler_params=_cparams("parallel", "parallel"),
        name="final_ln",
    )(x1, moe, gate_f, ln_g.reshape(1, d), ln_b.reshape(1, d))


def _moe(xm, w_router, router_bias, w_e_gate, w_e_up, w_e_down, w_s_gate, w_s_up, w_s_down):
    n = xm.shape[0]
    eidx, rank, wrow, cnt = _router(xm, w_router, router_bias)
    counts = cnt[:, 0].astype(I32)
    padded = (counts + EXPERT_BLOCK - 1) // EXPERT_BLOCK * EXPERT_BLOCK
    pad_end = jnp.cumsum(padded)
    pad_start = pad_end - padded
    dest = pad_start[eidx] + rank
    n_blocks = -(-(n * TOP_K) // EXPERT_BLOCK) + N_EXPERTS
    block_e = jnp.minimum(jnp.searchsorted(pad_end, jnp.arange(n_blocks, dtype=I32) * EXPERT_BLOCK, side="right"),
                          N_EXPERTS - 1).astype(I32)
    n_used = (pad_end[-1:] // EXPERT_BLOCK).astype(I32)
    dest_tiles = dest.reshape(TOP_K, n // ROUTE_TILE, ROUTE_TILE).transpose(1, 0, 2)
    pad_range = jnp.stack([pad_start + counts, pad_end]).astype(I32)
    xs = _dispatch(xm, dest_tiles, pad_range, n_used, n_blocks * EXPERT_BLOCK)
    ys = _experts(xs, block_e, n_used, w_e_gate, w_e_up, w_e_down)
    return _combine(ys, dest_tiles, wrow, xm, w_s_gate, w_s_up, w_s_down)


def kernel(x_prompt, x_sample, cache_kv_cmp, cache_kv_sel, state_kv_win, state_gla, page_table, c_prompt, c_sample, w_in, b_in, cmp_k_pos, cmp_k_w1, cmp_k_w2, cmp_v_pos, cmp_v_w1, cmp_v_w2, gla_w_a2, gla_b_a, gla_norm_g, w_br_a, w_br_b, w_out, ln1_g, ln1_b, w_ada, b_ada, w_router, router_bias, w_e_gate, w_e_up, w_e_down, w_s_gate, w_s_up, w_s_down, ln2_g, ln2_b):
    bp, tp, d = x_prompt.shape
    nd, td = x_sample.shape[:2]
    n_pool, page_rows = cache_kv_cmp.shape[:2]
    past_len = page_table.shape[1] * page_rows
    assert td == 1 and d == D_MODEL and page_rows == PAGE_ROWS and page_table.shape[1] == PAGES and tp == PAGES * PAGE_ROWS
    kv_w = 2 * NSA_KV_HEADS * HEAD_DIM

    mod = _adaln(jnp.concatenate([c_prompt, c_sample], axis=0), w_ada, b_ada)
    mod_p = [m.reshape(bp, 1, d) for m in jnp.split(mod[:bp], 6, axis=-1)]
    mod_s = [m.reshape(1, nd, d) for m in jnp.split(mod[bp:], 6, axis=-1)]

    w_pack, b_pack = _pack_in_weights(w_in, b_in)
    cmp_wk = _pack_cmp_weights(cmp_k_pos, cmp_k_w1, cmp_k_w2)
    cmp_wv = _pack_cmp_weights(cmp_v_pos, cmp_v_w1, cmp_v_w2)
    wa_pad = jnp.zeros((LANE, GLA_HEADS * GLA_DK), F32).at[MISC_AG:MISC_AG + GLA_GATE_RANK].set(gla_w_a2)
    tail_consts = (_gate_expand_table(), gla_norm_g.reshape(1, -1), _pad_br_a(w_br_a), w_br_b.astype(BF16), w_out.astype(BF16),
                   ln1_g.reshape(1, d), ln1_b.reshape(1, d))

    gm, qn, vg, rg, kvc, kvs, kvw, qg, kg, misc = _inproj(
        x_prompt, mod_p[0], mod_p[1], w_pack, b_pack, _rope_tables(jnp.arange(tp, dtype=I32)), 256)
    kc, vc = _compress(kvc.reshape(bp * PAGES, PAGE_ROWS, kv_w), jnp.arange(bp * PAGES, dtype=I32), cmp_wk, cmp_wv)
    ocmp, sel = _cmp_attn(qn, kc, vc, 256)
    osel = _flash(qn, kvs, sel, 128, 256)
    owin = _flash(qn, kvw, None, 128, 128)
    ogla, gla_p = _gla(qg, kg, vg, misc, wa_pad, gla_b_a, None, tp)
    x1_p, xm_p = _mixer_tail(ocmp, osel, owin, misc, ogla, rg, gm, x_prompt, mod_p[2], mod_p[4], mod_p[3], tail_consts, 256)
    n_win = min(WINDOW, tp)
    outs_p = (kvc.reshape(bp, tp, 2, NSA_KV_HEADS, HEAD_DIM), kvs.reshape(bp, tp, 2, NSA_KV_HEADS, HEAD_DIM),
              kvw[:, tp - n_win:].reshape(bp, n_win, 2, NSA_KV_HEADS, HEAD_DIM), gla_p)

    gm, qn, vg, rg, kvc, kvs, kvw, qg, kg, misc = _inproj(
        x_sample.reshape(1, nd, d), mod_s[0], mod_s[1], w_pack, b_pack, _rope_tables(jnp.full((nd,), past_len, I32)), nd)
    page_ids = page_table.reshape(-1).astype(I32)
    kc, vc = _compress(cache_kv_cmp.reshape(n_pool, PAGE_ROWS, kv_w), page_ids, cmp_wk, cmp_wv)
    ocmp, osel, owin, win_new = _nsa_decode(
        qn.reshape(nd, 1, -1), kc, vc, cache_kv_sel.reshape(n_pool, PAGE_ROWS, kv_w), page_ids,
        kvs.reshape(nd, 1, kv_w), kvw.reshape(nd, 1, kv_w), state_kv_win.reshape(nd, -1, kv_w))

    def pad_rows(a):
        return jnp.pad(a.reshape(nd, 1, -1), ((0, 0), (0, GLA_SUB - 1), (0, 0)))

    ogla, gla_s = _gla(pad_rows(qg), pad_rows(kg), pad_rows(vg), pad_rows(misc), wa_pad, gla_b_a, state_gla, 1)
    x1_s, xm_s = _mixer_tail(ocmp.reshape(1, nd, -1), osel.reshape(1, nd, -1), owin.reshape(1, nd, -1), misc,
                             ogla[:, 0].reshape(1, nd, -1), rg, gm, x_sample.reshape(1, nd, d),
                             mod_s[2], mod_s[4], mod_s[3], tail_consts, nd)
    outs_s = (kvc.reshape(nd, 1, 2, NSA_KV_HEADS, HEAD_DIM), kvs.reshape(nd, 1, 2, NSA_KV_HEADS, HEAD_DIM),
              win_new.reshape(state_kv_win.shape), gla_s)

    n_p = bp * tp
    moe = _moe(jnp.concatenate([xm_p.reshape(n_p, d), xm_s.reshape(nd, d)], axis=0),
               w_router, router_bias, w_e_gate, w_e_up, w_e_down, w_s_gate, w_s_up, w_s_down)
    y_p = _final_ln(x1_p, moe[:n_p].reshape(bp, tp, d), mod_p[5], ln2_g, ln2_b, 256)
    y_s = _final_ln(x1_s, moe[n_p:].reshape(1, nd, d), mod_s[5], ln2_g, ln2_b, nd).reshape(nd, 1, d)
    return (y_p, y_s) + outs_p + outs_s
```

```python
import functools

import numpy as np
import jax
import jax.numpy as jnp
from jax import lax
from jax.experimental import pallas as pl
from jax.experimental.pallas import tpu as pltpu

F32 = jnp.float32
BF16 = jnp.bfloat16
I32 = jnp.int32

D_MODEL = 1024
NSA_HEADS = 8
NSA_KV_HEADS = 2
NSA_GROUP = NSA_HEADS // NSA_KV_HEADS
HEAD_DIM = 64
ROT_DIM = HEAD_DIM // 4
ROPE_THETA = 500000.0
CMP_LEN = 32
CMP_STRIDE = 16
CMP_HIDDEN = 256
SEL_BLOCK = 64
SEL_TOP_N = 16
WINDOW = 512
FORCE_SCORE = 1.0e4
GLA_HEADS = 4
GLA_DK = 64
GLA_DV = 128
GLA_GATE_RANK = 16
GLA_TAU = 16.0
GLA_SUB = 16
N_EXPERTS = 256
TOP_K = 8
N_GROUPS = 8
TOPK_GROUPS = 4
EXPERT_DIM = 256
SHARED_DIM = 256
ROUTED_SCALE = 2.5
EXPERT_BLOCK = 128
DN_ALPHA = 2.0 ** 0.25
LN_EPS = 1e-5
LANE = 128
NEG = -1.0e30
VMEM_LIMIT = 56 * 1024 * 1024

SEG_GM = (0, 2 * D_MODEL)
SEG_QN = (SEG_GM[0] + SEG_GM[1], NSA_HEADS * LANE)
SEG_VG = (SEG_QN[0] + SEG_QN[1], GLA_HEADS * GLA_DV)
SEG_RG = (SEG_VG[0] + SEG_VG[1], GLA_HEADS * GLA_DV)
SEG_KVC = (SEG_RG[0] + SEG_RG[1], 2 * LANE)
SEG_KVS = (SEG_KVC[0] + SEG_KVC[1], 2 * LANE)
SEG_KVW = (SEG_KVS[0] + SEG_KVS[1], 2 * LANE)
SEG_QG = (SEG_KVW[0] + SEG_KVW[1], GLA_HEADS * GLA_DK)
SEG_KG = (SEG_QG[0] + SEG_QG[1], GLA_HEADS * GLA_DK)
SEG_MISC = (SEG_KG[0] + SEG_KG[1], LANE)
IN_PACKED = SEG_MISC[0] + SEG_MISC[1]
MISC_GN = 0
MISC_AG = NSA_HEADS * 3


def _cparams(*sem):
    return pltpu.CompilerParams(dimension_semantics=sem, vmem_limit_bytes=VMEM_LIMIT)


def _bdot(a, b):
    return jnp.dot(a.astype(BF16), b.astype(BF16), preferred_element_type=F32)


def _dot_nt(a, b, precision=None):
    return lax.dot_general(a, b, (((1,), (1,)), ((), ())), preferred_element_type=F32, precision=precision)


def _adaln_kernel(c_ref, w_ref, b_ref, o_ref):
    c = c_ref[...]
    o_ref[...] = _bdot(c * jax.nn.sigmoid(c), w_ref[...]) + b_ref[...]


def _adaln(c, w_ada, b_ada):
    n, d = c.shape
    m = w_ada.shape[1]
    tn = 512
    return pl.pallas_call(
        _adaln_kernel,
        grid=(m // tn,),
        in_specs=[pl.BlockSpec((n, d), lambda j: (0, 0)),
                  pl.BlockSpec((d, tn), lambda j: (0, j)),
                  pl.BlockSpec((1, tn), lambda j: (0, j))],
        out_specs=pl.BlockSpec((n, tn), lambda j: (0, j)),
        out_shape=jax.ShapeDtypeStruct((n, m), F32),
        compiler_params=_cparams("parallel"),
        name="adaln",
    )(c, w_ada, b_ada.reshape(1, m))


def _rope_tables(pos):
    half = ROT_DIM // 2
    inv = jnp.power(ROPE_THETA, -jnp.arange(half, dtype=F32) * 2.0 / ROT_DIM)
    ang = pos.astype(F32)[:, None] * inv[None, :]
    cos, sin = jnp.cos(ang), jnp.sin(ang)
    t = pos.shape[0]
    one = jnp.ones((t, HEAD_DIM - ROT_DIM), F32)
    z8 = jnp.zeros((t, half), F32)
    z48 = jnp.zeros((t, HEAD_DIM - ROT_DIM), F32)
    c = jnp.concatenate([cos, cos, one, cos, cos, one], axis=1)
    s1 = jnp.concatenate([-sin, z8, z48, -sin, z8, z48], axis=1)
    s2 = jnp.concatenate([z8, sin, z48, z8, sin, z48], axis=1)
    return c, s1, s2


def _pack_in_weights(w_in, b_in):
    sizes = (512, 128, 128, 128, 128, 128, 128, 24, 256, 256, 512, 512, 16, 2048)
    offs = np.concatenate([[0], np.cumsum(sizes)])

    def pack(w):
        seg = [w[..., offs[i]:offs[i + 1]] for i in range(len(sizes))]
        q_n, k_c, v_c, k_s, v_s, k_w, v_w, g_n, q_g, k_g, v_g, r_g, a_g, g_m = seg
        zero = jnp.zeros_like(q_n[..., :HEAD_DIM])
        q_slots = []
        for hh in range(NSA_HEADS):
            qh = q_n[..., hh * HEAD_DIM:(hh + 1) * HEAD_DIM] * (HEAD_DIM ** -0.5)
            q_slots += [qh, zero] if hh // NSA_GROUP == 0 else [zero, qh]
        misc_pad = jnp.zeros_like(w[..., :LANE - g_n.shape[-1] - a_g.shape[-1]])
        return jnp.concatenate([g_m] + q_slots + [v_g, r_g, k_c, v_c, k_s, v_s, k_w, v_w,
                                                   q_g * (GLA_DK ** -0.5), k_g, g_n, a_g, misc_pad], axis=-1)

    return pack(w_in).astype(BF16), pack(b_in.reshape(1, -1))


def _inproj_kernel(x_ref, sh_ref, sc_ref, w_ref, b_ref, rc_ref, rs1_ref, rs2_ref,
                   gm_ref, qn_ref, vg_ref, rg_ref, kvc_ref, kvs_ref, kvw_ref, qg_ref, kg_ref, misc_ref):
    h = (x_ref[0] * (1.0 + sc_ref[0]) + sh_ref[0]).astype(BF16)
    rc, rs1, rs2 = rc_ref[...], rs1_ref[...], rs2_ref[...]

    def proj(off, width):
        return jnp.dot(h, w_ref[:, off:off + width], preferred_element_type=F32) + b_ref[:, off:off + width]

    def rope(z):
        return z * rc + pltpu.roll(z, LANE - ROT_DIM // 2, 1) * rs1 + pltpu.roll(z, ROT_DIM // 2, 1) * rs2

    def plain(ref, seg):
        off, width = seg
        step = min(width, 512)
        for c in range(0, width, step):
            ref[0, :, c:c + step] = proj(off + c, step)

    plain(gm_ref, SEG_GM)
    for c in range(0, SEG_QN[1], 512):
        z = proj(SEG_QN[0] + c, 512)
        for s in range(0, 512, LANE):
            qn_ref[0, :, c + s:c + s + LANE] = rope(z[:, s:s + LANE])
    plain(vg_ref, SEG_VG)
    plain(rg_ref, SEG_RG)
    for ref, seg in ((kvc_ref, SEG_KVC), (kvs_ref, SEG_KVS), (kvw_ref, SEG_KVW)):
        z = proj(seg[0], seg[1])
        ref[0, :, 0:LANE] = rope(z[:, 0:LANE])
        ref[0, :, LANE:2 * LANE] = z[:, LANE:2 * LANE]
    plain(qg_ref, SEG_QG)
    plain(kg_ref, SEG_KG)
    plain(misc_ref, SEG_MISC)


def _inproj(x, shift, scale, w_pack, b_pack, tables, tm):
    b, t, d = x.shape
    per_tok = shift.shape[1] != 1
    mod_spec = (pl.BlockSpec((1, tm, d), lambda i, j: (i, j, 0)) if per_tok
                else pl.BlockSpec((1, 1, d), lambda i, j: (i, 0, 0)))
    segs = (SEG_GM, SEG_QN, SEG_VG, SEG_RG, SEG_KVC, SEG_KVS, SEG_KVW, SEG_QG, SEG_KG, SEG_MISC)
    tab_spec = pl.BlockSpec((tm, LANE), lambda i, j: (j, 0))
    return pl.pallas_call(
        _inproj_kernel,
        grid=(b, t // tm),
        in_specs=[pl.BlockSpec((1, tm, d), lambda i, j: (i, j, 0)), mod_spec, mod_spec,
                  pl.BlockSpec((d, IN_PACKED), lambda i, j: (0, 0)),
                  pl.BlockSpec((1, IN_PACKED), lambda i, j: (0, 0)),
                  tab_spec, tab_spec, tab_spec],
        out_specs=[pl.BlockSpec((1, tm, w), lambda i, j: (i, j, 0)) for _, w in segs],
        out_shape=[jax.ShapeDtypeStruct((b, t, w), F32) for _, w in segs],
        compiler_params=_cparams("parallel", "parallel"),
        name="inproj",
    )(x, shift, scale, w_pack, b_pack, *tables)


CHUNKS = 128
PAGE_ROWS = 128
PAGES = 16


def _pack_cmp_weights(pos, w1, w2):
    pos2 = jnp.concatenate([pos, pos], axis=1)
    z1 = jnp.zeros_like(w1)
    bd1 = jnp.concatenate([jnp.concatenate([w1, z1], axis=2), jnp.concatenate([z1, w1], axis=2)], axis=1)
    w1p = jnp.concatenate([bd1[:CMP_STRIDE], bd1[CMP_STRIDE:]], axis=2).astype(BF16)
    z2 = jnp.zeros_like(w2)
    w2p = jnp.concatenate([jnp.concatenate([w2, z2], axis=1), jnp.concatenate([z2, w2], axis=1)], axis=0).astype(BF16)
    return pos2, w1p, w2p


def _fill_chunks(page_refs, xs_ref):
    for p, pr in enumerate(page_refs):
        for l in range(CMP_STRIDE):
            xs_ref[l, 8 * p:8 * p + 8, :] = pr[0, pl.ds(l, PAGE_ROWS // CMP_STRIDE, stride=CMP_STRIDE), :]


def _compress_chunks(xs_ref, pos_ref, w1_ref, w2_ref):
    hid2 = 2 * CMP_HIDDEN
    acc_a = jnp.zeros((CHUNKS, hid2), F32)
    acc_b = jnp.zeros((CHUNKS, hid2), F32)
    for l in range(CMP_STRIDE):
        x = xs_ref[l]
        acc_a = acc_a + jnp.dot((x + pos_ref[l:l + 1, :]).astype(BF16), w1_ref[l, :, 0:hid2], preferred_element_type=F32)
        acc_b = acc_b + jnp.dot((x + pos_ref[CMP_STRIDE + l:CMP_STRIDE + l + 1, :]).astype(BF16), w1_ref[l, :, hid2:2 * hid2],
                                preferred_element_type=F32)
    hid = acc_a + pltpu.roll(acc_b, CHUNKS - 1, 0)
    out = jnp.dot(jax.nn.gelu(hid).astype(BF16), w2_ref[...], preferred_element_type=F32)
    row = lax.broadcasted_iota(I32, out.shape, 0)
    return jnp.where(row < CHUNKS - 1, out, 0.0)


def _compress_kernel(pt_ref, *refs):
    k_pages, v_pages = refs[:PAGES], refs[PAGES:2 * PAGES]
    posk_ref, w1k_ref, w2k_ref, posv_ref, w1v_ref, w2v_ref, kc_ref, vc_ref, xk_ref, xv_ref = refs[2 * PAGES:]
    _fill_chunks(k_pages, xk_ref)
    _fill_chunks(v_pages, xv_ref)
    kc_ref[0] = _compress_chunks(xk_ref, posk_ref, w1k_ref, w2k_ref)
    vc_ref[0] = _compress_chunks(xv_ref, posv_ref, w1v_ref, w2v_ref)


def _page_spec(p, half):
    return pl.BlockSpec((1, PAGE_ROWS, LANE), lambda i, pt: (pt[i * PAGES + p], 0, half))


def _const_spec(shape):
    nd = len(shape)
    return pl.BlockSpec(shape, lambda i, pt: (0,) * nd)


def _compress(pages, page_ids, cmp_wk, cmp_wv):
    n_b = page_ids.shape[0] // PAGES
    consts = list(cmp_wk) + list(cmp_wv)
    grid_spec = pltpu.PrefetchScalarGridSpec(
        num_scalar_prefetch=1,
        grid=(n_b,),
        in_specs=[_page_spec(p, h) for h in range(2) for p in range(PAGES)] + [_const_spec(c.shape) for c in consts],
        out_specs=[pl.BlockSpec((1, CHUNKS, LANE), lambda i, pt: (i, 0, 0))] * 2,
        scratch_shapes=[pltpu.VMEM((CMP_STRIDE, CHUNKS, LANE), F32)] * 2,
    )
    return pl.pallas_call(
        _compress_kernel,
        grid_spec=grid_spec,
        out_shape=[jax.ShapeDtypeStruct((n_b, CHUNKS, LANE), F32)] * 2,
        compiler_params=_cparams("parallel"),
        name="compress",
    )(page_ids, *([pages] * (2 * PAGES)), *consts)


def _cover_tables(n_sel):
    c_start = np.arange(CHUNKS) * CMP_STRIDE
    s_start = np.arange(n_sel) * SEL_BLOCK
    cover = ((c_start[:, None] < s_start[None, :] + SEL_BLOCK) & (c_start[:, None] + CMP_LEN > s_start[None, :])).astype(np.float32)
    cover[CHUNKS - 1] = 0.0
    out = np.zeros((NSA_KV_HEADS, LANE, CHUNKS), np.float32)
    for h in range(NSA_KV_HEADS):
        out[h, h * 64:h * 64 + n_sel] = cover.T
    return jnp.asarray(out)


def _softmax_rows(s, valid):
    s = jnp.where(valid, s, NEG)
    m = jnp.max(s, axis=-1, keepdims=True)
    m = jnp.where(m > 0.5 * NEG, m, 0.0)
    p = jnp.where(valid, jnp.exp(s - m), 0.0)
    return p / jnp.maximum(jnp.sum(p, axis=-1, keepdims=True), 1e-30)


def _select_blocks(imp, n_sel, top_n):
    ridx = lax.broadcasted_iota(I32, imp.shape, 0)
    cnt = jnp.zeros(imp.shape, F32)
    for i in range(n_sel):
        vi = imp[i:i + 1, :]
        ahead = (vi > imp) | ((vi == imp) & (ridx > i))
        cnt = cnt + jnp.where(ahead, 1.0, 0.0)
    return jnp.where((cnt < top_n) & (ridx < n_sel), 1.0, 0.0)


def _cmp_attn_kernel(qn_ref, kc_ref, vc_ref, cov_ref, o_ref, sel_ref, *, tq, n_sel):
    qi = pl.program_id(1)
    kc = kc_ref[0].astype(BF16)
    vc = vc_ref[0].astype(BF16)
    qpos = qi * tq + lax.broadcasted_iota(I32, (tq, CHUNKS), 0)
    cidx = lax.broadcasted_iota(I32, (tq, CHUNKS), 1)
    valid = (cidx * CMP_STRIDE + CMP_LEN - 1 <= qpos) & (cidx < CHUNKS - 1)
    psum = [jnp.zeros((tq, CHUNKS), F32) for _ in range(NSA_KV_HEADS)]
    for hh in range(NSA_HEADS):
        q = qn_ref[0, :, hh * LANE:(hh + 1) * LANE].astype(BF16)
        p = _softmax_rows(_dot_nt(q, kc), valid)
        o_ref[0, :, hh * LANE:(hh + 1) * LANE] = jnp.dot(p.astype(BF16), vc, preferred_element_type=F32)
        psum[hh // NSA_GROUP] = psum[hh // NSA_GROUP] + p
    imp = (_dot_nt(cov_ref[0], psum[0], lax.Precision.HIGHEST) + _dot_nt(cov_ref[1], psum[1], lax.Precision.HIGHEST))
    blk = lax.broadcasted_iota(I32, (LANE, tq), 0) & 63
    qpos_t = qi * tq + lax.broadcasted_iota(I32, (LANE, tq), 1)
    cur = qpos_t // SEL_BLOCK
    forced = (blk == 0) | (blk == cur) | (blk == cur - 1)
    imp = jnp.where(forced, FORCE_SCORE, jnp.where(blk * SEL_BLOCK <= qpos_t, imp, -FORCE_SCORE))
    sel_t = jnp.concatenate([_select_blocks(imp[0:64], n_sel, SEL_TOP_N), _select_blocks(imp[64:128], n_sel, SEL_TOP_N)], axis=0)
    sel_ref[0] = sel_t.T


def _cmp_attn(qn, kc, vc, tq):
    b, t, _ = qn.shape
    n_sel = -(-t // SEL_BLOCK)
    cov = _cover_tables(n_sel)
    return pl.pallas_call(
        functools.partial(_cmp_attn_kernel, tq=tq, n_sel=n_sel),
        grid=(b, t // tq),
        in_specs=[pl.BlockSpec((1, tq, NSA_HEADS * LANE), lambda i, j: (i, j, 0)),
                  pl.BlockSpec((1, CHUNKS, LANE), lambda i, j: (i, 0, 0)),
                  pl.BlockSpec((1, CHUNKS, LANE), lambda i, j: (i, 0, 0)),
                  pl.BlockSpec((NSA_KV_HEADS, LANE, CHUNKS), lambda i, j: (0, 0, 0))],
        out_specs=[pl.BlockSpec((1, tq, NSA_HEADS * LANE), lambda i, j: (i, j, 0)),
                   pl.BlockSpec((1, tq, LANE), lambda i, j: (i, j, 0))],
        out_shape=[jax.ShapeDtypeStruct((b, t, NSA_HEADS * LANE), F32), jax.ShapeDtypeStruct((b, t, LANE), F32)],
        compiler_params=_cparams("parallel", "parallel"),
        name="cmp_attn",
    )(qn, kc, vc, cov)


def _key_block_table(t):
    blk = np.arange(t)[:, None] // SEL_BLOCK
    lanes = np.arange(LANE)[None, :] & 63
    return jnp.asarray((blk == lanes).astype(np.float32), dtype=BF16)


def _lane_fold(x, op):
    out = x[:, 0:LANE]
    for c in range(LANE, x.shape[1], LANE):
        out = op(out, x[:, c:c + LANE])
    return out


def _sel_attn_kernel(qn_ref, kv_ref, sel_ref, kb_ref, o_ref, q2_scr, k2_scr, v_scr, s_scr, *, tq, tk):
    qi = pl.program_id(1)
    rows = NSA_HEADS * tq
    t = kv_ref.shape[1]

    @pl.when(qi == 0)
    def _():
        k2_scr[:, 0:LANE] = kv_ref[0, :, 0:LANE].astype(BF16)
        k2_scr[:, LANE:2 * LANE] = kb_ref[...]
        v_scr[...] = kv_ref[0, :, LANE:2 * LANE].astype(BF16)

    not_sel = (1.0 - sel_ref[0]) * NEG
    lane_head = lax.broadcasted_iota(I32, (tq, LANE), 1) // 64
    for hh in range(NSA_HEADS):
        q2_scr[hh * tq:(hh + 1) * tq, 0:LANE] = qn_ref[0, :, hh * LANE:(hh + 1) * LANE].astype(BF16)
        q2_scr[hh * tq:(hh + 1) * tq, LANE:2 * LANE] = jnp.where(lane_head == hh // NSA_GROUP, not_sel, 0.0).astype(BF16)
    q2 = q2_scr[...]
    last = (qi * tq + tq - 1) // tk

    def scores(kt):
        k0 = pl.multiple_of(kt * tk, tk)
        return _dot_nt(q2, k2_scr[pl.ds(k0, tk), :])

    def pass1(kt, m_acc):
        s = scores(kt)
        s_scr[kt] = s
        return jnp.maximum(m_acc, _lane_fold(s, jnp.maximum))

    m_acc = lax.fori_loop(0, last, pass1, jnp.full((rows, LANE), NEG, F32))
    rel = (qi * tq + lax.broadcasted_iota(I32, (tq, tk), 0)) - (last * tk + lax.broadcasted_iota(I32, (tq, tk), 1))
    causal = jnp.where(rel >= 0, 0.0, NEG)
    s_last = scores(last) + jnp.concatenate([causal] * NSA_HEADS, axis=0)
    m = jnp.max(jnp.maximum(m_acc, _lane_fold(s_last, jnp.maximum)), axis=-1, keepdims=True)

    def accumulate(s, v, l_acc, acc):
        p = jnp.exp(s - m)
        return l_acc + _lane_fold(p, jnp.add), acc + jnp.dot(p.astype(BF16), v, preferred_element_type=F32)

    def pass2(kt, carry):
        k0 = pl.multiple_of(kt * tk, tk)
        return accumulate(s_scr[kt], v_scr[pl.ds(k0, tk), :], *carry)

    l_acc, acc = lax.fori_loop(0, last, pass2, (jnp.zeros((rows, LANE), F32), jnp.zeros((rows, LANE), F32)))
    l_acc, acc = accumulate(s_last, v_scr[pl.ds(pl.multiple_of(last * tk, tk), tk), :], l_acc, acc)
    out = acc / jnp.sum(l_acc, axis=-1, keepdims=True)
    for hh in range(NSA_HEADS):
        o_ref[0, :, hh * LANE:(hh + 1) * LANE] = out[hh * tq:(hh + 1) * tq]


def _sel_attn(qn, kv, sel, tq, tk):
    b, t, _ = qn.shape
    rows = NSA_HEADS * tq
    return pl.pallas_call(
        functools.partial(_sel_attn_kernel, tq=tq, tk=tk),
        grid=(b, t // tq),
        in_specs=[pl.BlockSpec((1, tq, NSA_HEADS * LANE), lambda i, j: (i, j, 0)),
                  pl.BlockSpec((1, t, 2 * LANE), lambda i, j: (i, 0, 0)),
                  pl.BlockSpec((1, tq, LANE), lambda i, j: (i, j, 0)),
                  pl.BlockSpec((t, LANE), lambda i, j: (0, 0))],
        out_specs=pl.BlockSpec((1, tq, NSA_HEADS * LANE), lambda i, j: (i, j, 0)),
        out_shape=jax.ShapeDtypeStruct((b, t, NSA_HEADS * LANE), F32),
        scratch_shapes=[pltpu.VMEM((rows, 2 * LANE), BF16), pltpu.VMEM((t, 2 * LANE), BF16), pltpu.VMEM((t, LANE), BF16),
                        pltpu.VMEM((t // tk, rows, tk), F32)],
        compiler_params=_cparams("parallel", "arbitrary"),
        name="sel_attn",
    )(qn, kv, sel, _key_block_table(t))


def _win_attn_kernel(qn_ref, kv_ref, o_ref, q_scr, k_scr, v_scr, *, tq):
    qi = pl.program_id(1)
    span = WINDOW + tq

    @pl.when(qi == 0)
    def _():
        k_scr[...] = kv_ref[0, :, 0:LANE].astype(BF16)
        v_scr[...] = kv_ref[0, :, LANE:2 * LANE].astype(BF16)

    for hh in range(NSA_HEADS):
        q_scr[hh * tq:(hh + 1) * tq, :] = qn_ref[0, :, hh * LANE:(hh + 1) * LANE].astype(BF16)
    k0 = pl.multiple_of(jnp.maximum(qi * tq - WINDOW, 0), tq)
    rel = (qi * tq + lax.broadcasted_iota(I32, (tq, span), 0)) - (k0 + lax.broadcasted_iota(I32, (tq, span), 1))
    bias = jnp.where((rel >= 0) & (rel <= WINDOW), 0.0, NEG)
    s = _dot_nt(q_scr[...], k_scr[pl.ds(k0, span), :]) + jnp.concatenate([bias] * NSA_HEADS, axis=0)
    p = jnp.exp(s - jnp.max(s, axis=-1, keepdims=True))
    out = jnp.dot(p.astype(BF16), v_scr[pl.ds(k0, span), :], preferred_element_type=F32) / jnp.sum(p, axis=-1, keepdims=True)
    for hh in range(NSA_HEADS):
        o_ref[0, :, hh * LANE:(hh + 1) * LANE] = out[hh * tq:(hh + 1) * tq]


def _win_attn(qn, kv, tq):
    b, t, _ = qn.shape
    assert t >= WINDOW + tq
    rows = NSA_HEADS * tq
    return pl.pallas_call(
        functools.partial(_win_attn_kernel, tq=tq),
        grid=(b, t // tq),
        in_specs=[pl.BlockSpec((1, tq, NSA_HEADS * LANE), lambda i, j: (i, j, 0)),
                  pl.BlockSpec((1, t, 2 * LANE), lambda i, j: (i, 0, 0))],
        out_specs=pl.BlockSpec((1, tq, NSA_HEADS * LANE), lambda i, j: (i, j, 0)),
        out_shape=jax.ShapeDtypeStruct((b, t, NSA_HEADS * LANE), F32),
        scratch_shapes=[pltpu.VMEM((rows, LANE), BF16), pltpu.VMEM((t, LANE), BF16), pltpu.VMEM((t, LANE), BF16)],
        compiler_params=_cparams("parallel", "arbitrary"),
        name="win_attn",
    )(qn, kv)


def _dec_softmax(scores, vals):
    m = scores[0].max(axis=-1, keepdims=True)
    for s in scores[1:]:
        m = jnp.maximum(m, s.max(axis=-1, keepdims=True))
    den = jnp.zeros_like(m)
    out = jnp.zeros((m.shape[0], LANE), F32)
    for s, v in zip(scores, vals):
        p = jnp.exp(s - m)
        den = den + p.sum(axis=-1, keepdims=True)
        out = out + (p * v if s.shape[1] == 1 else _dot_nt(p.astype(BF16), v))
    return out / den


def _nsa_decode_kernel(pt_ref, *refs, past_len, n_sel):
    pages = refs[:PAGES]
    (qn_ref, kc_ref, vc_ref, kvs_ref, kvw_ref, win_ref, cov_ref,
     ocmp_ref, osel_ref, owin_ref, wnew_ref) = refs[PAGES:]
    nh = NSA_HEADS
    q = jnp.concatenate([qn_ref[0, :, hh * LANE:(hh + 1) * LANE] for hh in range(nh)], axis=0)
    qb = q.astype(BF16)
    cidx = lax.broadcasted_iota(I32, (nh, CHUNKS), 1)
    valid = (cidx * CMP_STRIDE + CMP_LEN - 1 <= past_len) & (cidx < CHUNKS - 1)
    p = _softmax_rows(_dot_nt(qb, kc_ref[0].astype(BF16)), valid)
    o_cmp = jnp.dot(p.astype(BF16), vc_ref[0].astype(BF16), preferred_element_type=F32)
    imp = None
    for h in range(NSA_KV_HEADS):
        ps = jnp.sum(p[h * NSA_GROUP:(h + 1) * NSA_GROUP], axis=0, keepdims=True)
        term = _dot_nt(cov_ref[h], jnp.broadcast_to(ps, (LANE, CHUNKS)), lax.Precision.HIGHEST)
        imp = term if imp is None else imp + term
    blk = lax.broadcasted_iota(I32, (LANE, LANE), 0) & 63
    cur = past_len // SEL_BLOCK
    forced = (blk == 0) | (blk == cur) | (blk == cur - 1)
    imp = jnp.where(forced, FORCE_SCORE, jnp.where(blk * SEL_BLOCK <= past_len, imp, -FORCE_SCORE))
    sel_t = jnp.concatenate([_select_blocks(imp[0:64], n_sel, min(SEL_TOP_N, n_sel)),
                             _select_blocks(imp[64:128], n_sel, min(SEL_TOP_N, n_sel))], axis=0)
    sel = sel_t.T[0:1, :]
    head_of_row = lax.broadcasted_iota(I32, (nh, 1), 0) // NSA_GROUP

    def picked(s):
        return jnp.where(head_of_row == 0, sel[:, s:s + 1], sel[:, 64 + s:64 + s + 1])

    first_half = lax.broadcasted_iota(I32, (nh, PAGE_ROWS), 1) < SEL_BLOCK
    scores, vals = [], []
    for pg, pr in enumerate(pages):
        s = jnp.dot(qb, pr[0, 0:LANE, :].astype(BF16), preferred_element_type=F32)
        ok = jnp.where(first_half, picked(2 * pg), picked(2 * pg + 1)) > 0.5
        scores.append(jnp.where(ok, s, NEG))
        vals.append(pr[0, LANE:2 * LANE, :].astype(BF16))
    s_new = jnp.sum(q * kvs_ref[0, :, 0:LANE], axis=-1, keepdims=True)
    scores.append(jnp.where(picked(past_len // SEL_BLOCK) > 0.5, s_new, NEG))
    vals.append(kvs_ref[0, :, LANE:2 * LANE])
    o_sel = _dec_softmax(scores, vals)
    n_win = win_ref.shape[2]
    kpos = past_len - n_win + lax.broadcasted_iota(I32, (nh, n_win), 1)
    rel = past_len - kpos
    s_win = jnp.dot(qb, win_ref[0, 0:LANE, :].astype(BF16), preferred_element_type=F32)
    s_win = jnp.where((kpos >= 0) & (rel >= 0) & (rel <= WINDOW), s_win, NEG)
    s_new = jnp.sum(q * kvw_ref[0, :, 0:LANE], axis=-1, keepdims=True)
    o_win = _dec_softmax([s_win, s_new], [win_ref[0, LANE:2 * LANE, :].astype(BF16), kvw_ref[0, :, LANE:2 * LANE]])
    for hh in range(nh):
        ocmp_ref[0, :, hh * LANE:(hh + 1) * LANE] = o_cmp[hh:hh + 1]
        osel_ref[0, :, hh * LANE:(hh + 1) * LANE] = o_sel[hh:hh + 1]
        owin_ref[0, :, hh * LANE:(hh + 1) * LANE] = o_win[hh:hh + 1]
    for c in range(0, 2 * LANE, LANE):
        new_col = jnp.broadcast_to(kvw_ref[0, :, c:c + LANE], (LANE, LANE)).T
        shifted = pltpu.roll(win_ref[0, c:c + LANE, :], n_win - 1, 1)
        lane = lax.broadcasted_iota(I32, shifted.shape, 1)
        wnew_ref[0, c:c + LANE, :] = jnp.where(lane == n_win - 1, jnp.concatenate([new_col] * (n_win // LANE), axis=1), shifted)


def _nsa_decode(qn, kc, vc, sel_pages, page_ids, kvs_new, kvw_new, win_state):
    b = qn.shape[0]
    n_win = win_state.shape[2]
    past_len = PAGES * PAGE_ROWS
    n_sel = -(-(past_len + 1) // SEL_BLOCK)
    cov = _cover_tables(n_sel)

    def per_b(shape):
        nd = len(shape)
        return pl.BlockSpec((1,) + shape[1:], lambda i, pt: (i,) + (0,) * (nd - 1))

    slots = NSA_HEADS * LANE
    grid_spec = pltpu.PrefetchScalarGridSpec(
        num_scalar_prefetch=1,
        grid=(b,),
        in_specs=[pl.BlockSpec((1, 2 * LANE, PAGE_ROWS), (lambda i, pt, p=p: (pt[i * PAGES + p], 0, 0))) for p in range(PAGES)]
        + [per_b(qn.shape), per_b(kc.shape), per_b(vc.shape), per_b(kvs_new.shape), per_b(kvw_new.shape), per_b(win_state.shape),
           _const_spec(cov.shape)],
        out_specs=[per_b((b, 1, slots))] * 3 + [per_b(win_state.shape)],
    )
    return pl.pallas_call(
        functools.partial(_nsa_decode_kernel, past_len=past_len, n_sel=n_sel),
        grid_spec=grid_spec,
        out_shape=[jax.ShapeDtypeStruct((b, 1, slots), F32)] * 3 + [jax.ShapeDtypeStruct(win_state.shape, F32)],
        compiler_params=_cparams("parallel"),
        name="nsa_decode",
    )(page_ids, *([sel_pages] * PAGES), qn, kc, vc, kvs_new, kvw_new, win_state, cov)


def _dot_tn(a, b):
    return lax.dot_general(a, b, (((0,), (0,)), ((), ())), preferred_element_type=F32)


def _cumsum_table():
    r = np.arange(LANE)
    return jnp.asarray(((r[:, None] // GLA_SUB == r[None, :] // GLA_SUB) & (r[None, :] <= r[:, None])).astype(np.float32))


def _gla_kernel(*refs, t, t_valid, has_state):
    if has_state:
        qg_ref, kg_ref, vg_ref, misc_ref, wa_ref, ba_ref, lt_ref, s0_ref, o_ref, s_ref, b_scr, st_scr = refs
    else:
        qg_ref, kg_ref, vg_ref, misc_ref, wa_ref, ba_ref, lt_ref, o_ref, s_ref, b_scr, st_scr = refs
    z = jnp.dot(misc_ref[0], wa_ref[...], preferred_element_type=F32, precision=lax.Precision.HIGHEST) + ba_ref[...]
    la = (jnp.minimum(z, 0.0) - jnp.log1p(jnp.exp(-jnp.abs(z)))) * (1.0 / GLA_TAU)
    if t_valid < t:
        la = jnp.where(lax.broadcasted_iota(I32, la.shape, 0) < t_valid, la, 0.0)
    tile = min(t, LANE)
    for r in range(0, t, tile):
        b_scr[r:r + tile, :] = jnp.dot(lt_ref[0:tile, 0:tile], la[r:r + tile, :], preferred_element_type=F32,
                                       precision=lax.Precision.HIGHEST)
    if has_state:
        st_scr[...] = s0_ref[0].reshape(2 * GLA_DK, GLA_DV).T
    else:
        st_scr[...] = jnp.zeros((GLA_DV, LANE), F32)
    head_a = lax.broadcasted_iota(I32, (GLA_SUB, LANE), 1) < GLA_DK
    row = lax.broadcasted_iota(I32, (GLA_SUB, LANE), 0)

    def chunk(c, carry):
        r0 = pl.multiple_of(c * GLA_SUB, GLA_SUB)
        q = qg_ref[0, pl.ds(r0, GLA_SUB), :]
        k = kg_ref[0, pl.ds(r0, GLA_SUB), :]
        v = vg_ref[0, pl.ds(r0, GLA_SUB), :]
        b = b_scr[pl.ds(r0, GLA_SUB), :]
        b_last = b[GLA_SUB - 1:GLA_SUB, :]
        st = st_scr[...]
        st_b = st.astype(BF16)
        qe = q * jnp.exp(b)
        o_a = _dot_nt(jnp.where(head_a, qe, 0.0).astype(BF16), st_b)
        o_b = _dot_nt(jnp.where(head_a, 0.0, qe).astype(BF16), st_b)
        for j in range(GLA_SUB):
            w = q * k[j:j + 1, :] * jnp.exp(jnp.minimum(b - b[j:j + 1, :], 0.0))
            w = jnp.where(row >= j, w, 0.0)
            a_a = jnp.sum(jnp.where(head_a, w, 0.0), axis=-1, keepdims=True)
            a_b = jnp.sum(jnp.where(head_a, 0.0, w), axis=-1, keepdims=True)
            o_a = o_a + a_a * v[j:j + 1, 0:GLA_DV]
            o_b = o_b + a_b * v[j:j + 1, GLA_DV:2 * GLA_DV]
        o_ref[0, pl.ds(r0, GLA_SUB), 0:GLA_DV] = o_a
        o_ref[0, pl.ds(r0, GLA_SUB), GLA_DV:2 * GLA_DV] = o_b
        kd = k * jnp.exp(b_last - b)
        upd = (_dot_tn(v[:, 0:GLA_DV].astype(BF16), jnp.where(head_a, kd, 0.0).astype(BF16))
               + _dot_tn(v[:, GLA_DV:2 * GLA_DV].astype(BF16), jnp.where(head_a, 0.0, kd).astype(BF16)))
        st_scr[...] = jnp.exp(b_last) * st + upd
        return carry

    lax.fori_loop(0, t // GLA_SUB, chunk, 0)
    s_ref[0] = st_scr[...].T.reshape(2, GLA_DK, GLA_DV)


def _gla(qg, kg, vg, misc, wa_pad, ba, s0, t_valid):
    b, t, _ = qg.shape
    pairs = GLA_HEADS // 2
    has_state = s0 is not None
    in_specs = [pl.BlockSpec((1, t, LANE), lambda i, p: (i, 0, p)),
                pl.BlockSpec((1, t, LANE), lambda i, p: (i, 0, p)),
                pl.BlockSpec((1, t, 2 * GLA_DV), lambda i, p: (i, 0, p)),
                pl.BlockSpec((1, t, LANE), lambda i, p: (i, 0, 0)),
                pl.BlockSpec((LANE, LANE), lambda i, p: (0, p)),
                pl.BlockSpec((1, LANE), lambda i, p: (0, p)),
                pl.BlockSpec((LANE, LANE), lambda i, p: (0, 0))]
    args = [qg, kg, vg, misc, wa_pad, ba.reshape(1, -1), _cumsum_table()]
    if has_state:
        in_specs.append(pl.BlockSpec((1, 2, GLA_DK, GLA_DV), lambda i, p: (i, p, 0, 0)))
        args.append(s0)
    return pl.pallas_call(
        functools.partial(_gla_kernel, t=t, t_valid=t_valid, has_state=has_state),
        grid=(b, pairs),
        in_specs=in_specs,
        out_specs=[pl.BlockSpec((1, t, 2 * GLA_DV), lambda i, p: (i, 0, p)),
                   pl.BlockSpec((1, 2, GLA_DK, GLA_DV), lambda i, p: (i, p, 0, 0))],
        out_shape=[jax.ShapeDtypeStruct((b, t, GLA_HEADS * GLA_DV), F32),
                   jax.ShapeDtypeStruct((b, GLA_HEADS, GLA_DK, GLA_DV), F32)],
        scratch_shapes=[pltpu.VMEM((t, LANE), F32), pltpu.VMEM((GLA_DV, LANE), F32)],
        compiler_params=_cparams("parallel", "parallel"),
        name="gla",
    )(*args)


def _gate_expand_table():
    out = np.zeros((3, LANE, NSA_HEADS * LANE), np.float32)
    for hh in range(NSA_HEADS):
        for j in range(3):
            out[j, MISC_GN + 3 * hh + j, hh * LANE:(hh + 1) * LANE] = 1.0
    return jnp.asarray(out, dtype=BF16)


def _pad_br_a(w_br_a):
    zero = jnp.zeros((HEAD_DIM, w_br_a.shape[1]), w_br_a.dtype)
    parts = []
    for hh in range(NSA_HEADS):
        wh = w_br_a[hh * HEAD_DIM:(hh + 1) * HEAD_DIM]
        parts += [wh, zero] if hh // NSA_GROUP == 0 else [zero, wh]
    return jnp.concatenate(parts, axis=0).astype(BF16)


def _layer_norm(v, g, b):
    mu = jnp.mean(v, axis=-1, keepdims=True)
    var = jnp.mean(jnp.square(v - mu), axis=-1, keepdims=True)
    return (v - mu) * lax.rsqrt(var + LN_EPS) * g + b


def _mixer_tail_kernel(ocmp_ref, osel_ref, owin_ref, misc_ref, ogla_ref, rg_ref, gm_ref, x_ref, gate_ref, scf_ref, shf_ref,
                       ex_ref, ng_ref, wa_ref, wb_ref, wo_ref, lg_ref, lb_ref, x1_ref, xm_ref):
    sig = jax.nn.sigmoid(misc_ref[0])
    sig_hi = sig.astype(BF16)
    sig_lo = (sig - sig_hi.astype(F32)).astype(BF16)
    o_nsa = None
    for j, ref in enumerate((ocmp_ref, osel_ref, owin_ref)):
        g = (jnp.dot(sig_hi, ex_ref[j], preferred_element_type=F32) + jnp.dot(sig_lo, ex_ref[j], preferred_element_type=F32))
        o_nsa = g * ref[0] if o_nsa is None else o_nsa + g * ref[0]
    br_a = _bdot(o_nsa, wa_ref[...])
    heads = []
    for h in range(GLA_HEADS):
        seg = ogla_ref[0, :, h * GLA_DV:(h + 1) * GLA_DV]
        mu = jnp.mean(seg, axis=-1, keepdims=True)
        var = jnp.mean(jnp.square(seg - mu), axis=-1, keepdims=True)
        r = rg_ref[0, :, h * GLA_DV:(h + 1) * GLA_DV]
        heads.append((seg - mu) * lax.rsqrt(var + LN_EPS) * ng_ref[:, h * GLA_DV:(h + 1) * GLA_DV] * (r * jax.nn.sigmoid(r)))
    br_b = _bdot(jnp.concatenate(heads, axis=1), wb_ref[...])
    gm_a = jax.nn.sigmoid(gm_ref[0, :, 0:D_MODEL])
    gm_b = jax.nn.sigmoid(gm_ref[0, :, D_MODEL:2 * D_MODEL])
    y = _bdot(gm_a * br_a + gm_b * br_b, wo_ref[...])
    x1 = _layer_norm(DN_ALPHA * x_ref[0] + gate_ref[0] * y, lg_ref[...], lb_ref[...])
    x1_ref[0] = x1
    xm_ref[0] = x1 * (1.0 + scf_ref[0]) + shf_ref[0]


def _mixer_tail(ocmp, osel, owin, misc, ogla, rg, gm, x, gate_m, scale_f, shift_f, consts, tm):
    b, t, d = x.shape
    per_tok = gate_m.shape[1] != 1

    def tok(w):
        return pl.BlockSpec((1, tm, w), lambda i, j: (i, j, 0))

    mod_spec = tok(d) if per_tok else pl.BlockSpec((1, 1, d), lambda i, j: (i, 0, 0))

    def const(a):
        nd = a.ndim
        return pl.BlockSpec(a.shape, lambda i, j: (0,) * nd)

    return pl.pallas_call(
        _mixer_tail_kernel,
        grid=(b, t // tm),
        in_specs=[tok(NSA_HEADS * LANE)] * 3 + [tok(LANE), tok(GLA_HEADS * GLA_DV), tok(GLA_HEADS * GLA_DV), tok(2 * d), tok(d),
                                               mod_spec, mod_spec, mod_spec] + [const(c) for c in consts],
        out_specs=[tok(d), tok(d)],
        out_shape=[jax.ShapeDtypeStruct((b, t, d), F32)] * 2,
        compiler_params=_cparams("parallel", "parallel"),
        name="mixer_tail",
    )(ocmp, osel, owin, misc, ogla, rg, gm, x, gate_m, scale_f, shift_f, *consts)


ROUTE_TILE = LANE


def _first_index(hit, iota, size, axis):
    return jnp.min(jnp.where(hit, iota, size), axis=axis, keepdims=True)


def _router_kernel(xm_ref, wr_ref, bias_ref, tri_ref, eidx_ref, rank_ref, wrow_ref, cnt_ref, carry_ref):
    i = pl.program_id(0)
    tm = ROUTE_TILE
    per = N_EXPERTS // N_GROUPS

    @pl.when(i == 0)
    def _():
        carry_ref[...] = jnp.zeros_like(carry_ref)

    logits = _dot_nt(wr_ref[...], xm_ref[...], lax.Precision.HIGHEST)
    s = jax.nn.sigmoid(logits)
    sb = s + bias_ref[...]
    sb3 = sb.reshape(N_GROUPS, per, tm)
    in_grp = lax.broadcasted_iota(I32, sb3.shape, 1)
    m1 = jnp.max(sb3, axis=1, keepdims=True)
    first = _first_index(sb3 == m1, in_grp, per, 1)
    m2 = jnp.max(jnp.where(in_grp == first, NEG, sb3), axis=1, keepdims=True)
    gs = (m1 + m2).reshape(N_GROUPS, tm)
    g_iota = lax.broadcasted_iota(I32, gs.shape, 0)
    g_keep = jnp.zeros(gs.shape, jnp.bool_)
    for _ in range(TOPK_GROUPS):
        pick = g_iota == _first_index(gs == jnp.max(gs, axis=0, keepdims=True), g_iota, N_GROUPS, 0)
        g_keep = g_keep | pick
        gs = jnp.where(pick, NEG, gs)
    sbm = jnp.where(g_keep.reshape(N_GROUPS, 1, tm), sb3, NEG).reshape(N_EXPERTS, tm)
    e_iota = lax.broadcasted_iota(I32, sbm.shape, 0)
    idxs, sels = [], []
    onehot = jnp.zeros(sbm.shape, F32)
    for _ in range(TOP_K):
        idx = _first_index(sbm == jnp.max(sbm, axis=0, keepdims=True), e_iota, N_EXPERTS, 0)
        pick = e_iota == idx
        idxs.append(idx)
        sels.append(jnp.sum(jnp.where(pick, s, 0.0), axis=0, keepdims=True))
        sbm = jnp.where(pick, NEG, sbm)
        onehot = onehot + jnp.where(pick, 1.0, 0.0)
    sel = jnp.concatenate(sels, axis=0)
    wts = sel / jnp.sum(sel, axis=0, keepdims=True) * ROUTED_SCALE
    carry = carry_ref[...]
    before = carry + jnp.dot(onehot.astype(BF16), tri_ref[...], preferred_element_type=F32)
    ranks = [jnp.sum(jnp.where(e_iota == idx, before, 0.0), axis=0, keepdims=True) for idx in idxs]
    eidx_ref[...] = jnp.concatenate(idxs, axis=0)
    rank_ref[...] = jnp.concatenate(ranks, axis=0).astype(I32)
    wrow_ref[...] = jnp.concatenate([wts, jnp.zeros((LANE - TOP_K, tm), F32)], axis=0).T
    carry = carry + jnp.sum(onehot, axis=1, keepdims=True)
    carry_ref[...] = carry
    cnt_ref[...] = carry


def _router(xm, w_router, router_bias):
    n, d = xm.shape
    tm = ROUTE_TILE
    r = np.arange(tm)
    tri = jnp.asarray((r[:, None] < r[None, :]).astype(np.float32), dtype=BF16)
    return pl.pallas_call(
        _router_kernel,
        grid=(n // tm,),
        in_specs=[pl.BlockSpec((tm, d), lambda i: (i, 0)),
                  pl.BlockSpec((N_EXPERTS, d), lambda i: (0, 0)),
                  pl.BlockSpec((N_EXPERTS, 1), lambda i: (0, 0)),
                  pl.BlockSpec((tm, tm), lambda i: (0, 0))],
        out_specs=[pl.BlockSpec((TOP_K, tm), lambda i: (0, i)),
                   pl.BlockSpec((TOP_K, tm), lambda i: (0, i)),
                   pl.BlockSpec((tm, LANE), lambda i: (i, 0)),
                   pl.BlockSpec((N_EXPERTS, LANE), lambda i: (0, 0))],
        out_shape=[jax.ShapeDtypeStruct((TOP_K, n), I32), jax.ShapeDtypeStruct((TOP_K, n), I32),
                   jax.ShapeDtypeStruct((n, LANE), F32), jax.ShapeDtypeStruct((N_EXPERTS, LANE), F32)],
        scratch_shapes=[pltpu.VMEM((N_EXPERTS, LANE), F32)],
        compiler_params=_cparams("arbitrary"),
        name="router",
    )(xm, w_router.T, router_bias.reshape(N_EXPERTS, 1), tri)


def _dest_kernel(eidx_ref, rank_ref, start_ref, dest_ref):
    e_iota = lax.broadcasted_iota(I32, (N_EXPERTS, ROUTE_TILE), 0)
    start = start_ref[...]
    rows = [jnp.sum(jnp.where(e_iota == eidx_ref[k:k + 1, :], start, 0.0), axis=0, keepdims=True) for k in range(TOP_K)]
    dest_ref[0] = jnp.concatenate(rows, axis=0).astype(I32) + rank_ref[...]


def _dest(eidx, rank, pad_start):
    n = eidx.shape[1]
    tm = ROUTE_TILE
    return pl.pallas_call(
        _dest_kernel,
        grid=(n // tm,),
        in_specs=[pl.BlockSpec((TOP_K, tm), lambda i: (0, i)),
                  pl.BlockSpec((TOP_K, tm), lambda i: (0, i)),
                  pl.BlockSpec((N_EXPERTS, 1), lambda i: (0, 0))],
        out_specs=pl.BlockSpec((1, TOP_K, tm), lambda i: (i, 0, 0)),
        out_shape=jax.ShapeDtypeStruct((n // tm, TOP_K, tm), I32),
        compiler_params=_cparams("parallel"),
        name="dest",
    )(eidx, rank, pad_start.astype(F32).reshape(N_EXPERTS, 1))


def _row_copy(src_ref, src_row, dst_ref, dst_row, sem):
    return pltpu.make_async_copy(src_ref.at[pl.ds(src_row, 1)], dst_ref.at[pl.ds(dst_row, 1)], sem)


def _dispatch_kernel(dest_ref, pad_ref, nb_ref, xm_ref, xs_ref, zero_ref, sem):
    i = pl.program_id(0)
    tm = ROUTE_TILE
    bm = EXPERT_BLOCK

    def start_row(r, c):
        for k in range(TOP_K):
            _row_copy(xm_ref, r, xs_ref, dest_ref[0, k, r], sem).start()
        return c

    def wait_row(r, c):
        for k in range(TOP_K):
            _row_copy(xm_ref, r, xs_ref, dest_ref[0, k, r], sem).wait()
        return c

    lax.fori_loop(0, tm, start_row, 0)
    lax.fori_loop(0, tm, wait_row, 0)

    @pl.when(i == pl.num_programs(0) - 1)
    def _():
        zero_ref[...] = jnp.zeros_like(zero_ref)

        def per_expert(e, c):
            lo, hi = pad_ref[0, e], pad_ref[1, e]
            lax.fori_loop(lo, hi, lambda r, cc: (_row_copy(zero_ref, 0, xs_ref, r, sem).start(), cc)[1], 0)
            lax.fori_loop(lo, hi, lambda r, cc: (_row_copy(zero_ref, 0, xs_ref, r, sem).wait(), cc)[1], 0)
            return c

        lax.fori_loop(0, N_EXPERTS, per_expert, 0)

        def tail_copy(blk):
            return pltpu.make_async_copy(zero_ref, xs_ref.at[pl.ds(blk * bm, bm)], sem)

        n_blocks = xs_ref.shape[0] // bm
        lax.fori_loop(nb_ref[0], n_blocks, lambda blk, c: (tail_copy(blk).start(), c)[1], 0)
        lax.fori_loop(nb_ref[0], n_blocks, lambda blk, c: (tail_copy(blk).wait(), c)[1], 0)


def _dispatch(xm, dest_tiles, pad_range, n_used, n_rows):
    n, d = xm.shape
    tm = ROUTE_TILE
    return pl.pallas_call(
        _dispatch_kernel,
        grid=(n // tm,),
        in_specs=[pl.BlockSpec((1, TOP_K, tm), lambda i: (i, 0, 0), memory_space=pltpu.SMEM),
                  pl.BlockSpec(memory_space=pltpu.SMEM),
                  pl.BlockSpec(memory_space=pltpu.SMEM),
                  pl.BlockSpec((tm, d), lambda i: (i, 0))],
        out_specs=pl.BlockSpec(memory_space=pl.ANY),
        out_shape=jax.ShapeDtypeStruct((n_rows, d), F32),
        scratch_shapes=[pltpu.VMEM((EXPERT_BLOCK, d), F32), pltpu.SemaphoreType.DMA(())],
        compiler_params=_cparams("arbitrary"),
        name="dispatch",
    )(dest_tiles, pad_range, n_used, xm)


def _experts_kernel(be_ref, nb_ref, xs_ref, wg_ref, wu_ref, wd_ref, ys_ref, wg_s, wu_s, wd_s):
    j = pl.program_id(0)
    used = j < nb_ref[0]
    changed = (j == 0) | (be_ref[j] != be_ref[jnp.maximum(j - 1, 0)])

    @pl.when(used & changed)
    def _():
        wg_s[...] = wg_ref[0].astype(BF16)
        wu_s[...] = wu_ref[0].astype(BF16)
        wd_s[...] = wd_ref[0].astype(BF16)

    @pl.when(used)
    def _():
        x = xs_ref[...].astype(BF16)
        g = jnp.dot(x, wg_s[...], preferred_element_type=F32)
        u = jnp.dot(x, wu_s[...], preferred_element_type=F32)
        ys_ref[...] = jnp.dot((g * jax.nn.sigmoid(g) * u).astype(BF16), wd_s[...], preferred_element_type=F32)

    @pl.when(jnp.logical_not(used))
    def _():
        ys_ref[...] = jnp.zeros_like(ys_ref)


def _experts(xs, block_e, n_used, w_gate, w_up, w_down):
    n_rows, d = xs.shape
    bm = EXPERT_BLOCK
    n_blocks = n_rows // bm
    f = w_gate.shape[2]

    def blk(j, be, nb):
        return jnp.minimum(j, jnp.maximum(nb[0] - 1, 0))

    grid_spec = pltpu.PrefetchScalarGridSpec(
        num_scalar_prefetch=2,
        grid=(n_blocks,),
        in_specs=[pl.BlockSpec((bm, d), lambda j, be, nb: (blk(j, be, nb), 0)),
                  pl.BlockSpec((1, d, f), lambda j, be, nb: (be[blk(j, be, nb)], 0, 0)),
                  pl.BlockSpec((1, d, f), lambda j, be, nb: (be[blk(j, be, nb)], 0, 0)),
                  pl.BlockSpec((1, f, d), lambda j, be, nb: (be[blk(j, be, nb)], 0, 0))],
        out_specs=pl.BlockSpec((bm, d), lambda j, be, nb: (j, 0)),
        scratch_shapes=[pltpu.VMEM((d, f), BF16), pltpu.VMEM((d, f), BF16), pltpu.VMEM((f, d), BF16)],
    )
    return pl.pallas_call(
        _experts_kernel,
        grid_spec=grid_spec,
        out_shape=jax.ShapeDtypeStruct((n_rows, d), F32),
        compiler_params=_cparams("arbitrary"),
        name="experts",
    )(block_e, n_used, xs, w_gate, w_up, w_down)


def _combine_kernel(dest_ref, ys_ref, wrow_ref, xm_ref, sg_ref, su_ref, sd_ref, out_ref, buf_ref, sem):
    tm = ROUTE_TILE

    def start_row(r, c):
        for k in range(TOP_K):
            _row_copy(ys_ref, dest_ref[0, k, r], buf_ref.at[k], r, sem).start()
        return c

    def wait_row(r, c):
        for k in range(TOP_K):
            _row_copy(ys_ref, dest_ref[0, k, r], buf_ref.at[k], r, sem).wait()
        return c

    lax.fori_loop(0, tm, start_row, 0)
    x = xm_ref[...].astype(BF16)
    g = jnp.dot(x, sg_ref[...], preferred_element_type=F32)
    u = jnp.dot(x, su_ref[...], preferred_element_type=F32)
    shared = jnp.dot((g * jax.nn.sigmoid(g) * u).astype(BF16), sd_ref[...], preferred_element_type=F32)
    lax.fori_loop(0, tm, wait_row, 0)
    w = wrow_ref[...]
    routed = w[:, 0:1] * buf_ref[0]
    for k in range(1, TOP_K):
        routed = routed + w[:, k:k + 1] * buf_ref[k]
    out_ref[...] = routed + shared


def _combine(ys, dest_tiles, wrow, xm, ws_gate, ws_up, ws_down):
    n, d = xm.shape
    tm = ROUTE_TILE
    f = ws_gate.shape[1]
    return pl.pallas_call(
        _combine_kernel,
        grid=(n // tm,),
        in_specs=[pl.BlockSpec((1, TOP_K, tm), lambda i: (i, 0, 0), memory_space=pltpu.SMEM),
                  pl.BlockSpec(memory_space=pl.ANY),
                  pl.BlockSpec((tm, LANE), lambda i: (i, 0)),
                  pl.BlockSpec((tm, d), lambda i: (i, 0)),
                  pl.BlockSpec((d, f), lambda i: (0, 0)),
                  pl.BlockSpec((d, f), lambda i: (0, 0)),
                  pl.BlockSpec((f, d), lambda i: (0, 0))],
        out_specs=pl.BlockSpec((tm, d), lambda i: (i, 0)),
        out_shape=jax.ShapeDtypeStruct((n, d), F32),
        scratch_shapes=[pltpu.VMEM((TOP_K, tm, d), F32), pltpu.SemaphoreType.DMA(())],
        compiler_params=_cparams("arbitrary"),
        name="combine",
    )(dest_tiles, ys, wrow, xm, ws_gate.astype(BF16), ws_up.astype(BF16), ws_down.astype(BF16))


def _final_ln_kernel(x1_ref, moe_ref, gate_ref, g_ref, b_ref, y_ref):
    y_ref[0] = _layer_norm(DN_ALPHA * x1_ref[0] + gate_ref[0] * moe_ref[0], g_ref[...], b_ref[...])


def _final_ln(x1, moe, gate_f, ln_g, ln_b, tm):
    b, t, d = x1.shape
    per_tok = gate_f.shape[1] != 1
    tok = pl.BlockSpec((1, tm, d), lambda i, j: (i, j, 0))
    mod_spec = tok if per_tok else pl.BlockSpec((1, 1, d), lambda i, j: (i, 0, 0))
    vec = pl.BlockSpec((1, d), lambda i, j: (0, 0))
    return pl.pallas_call(
        _final_ln_kernel,
        grid=(b, t // tm),
        in_specs=[tok, tok, mod_spec, vec, vec],
        out_specs=tok,
        out_shape=jax.ShapeDtypeStruct((b, t, d), F32),
        compiler_params=_cparams("parallel", "parallel"),
        name="final_ln",
    )(x1, moe, gate_f, ln_g.reshape(1, d), ln_b.reshape(1, d))


def _moe(xm, w_router, router_bias, w_e_gate, w_e_up, w_e_down, w_s_gate, w_s_up, w_s_down):
    n = xm.shape[0]
    eidx, rank, wrow, cnt = _router(xm, w_router, router_bias)
    counts = cnt[:, 0].astype(I32)
    padded = (counts + EXPERT_BLOCK - 1) // EXPERT_BLOCK * EXPERT_BLOCK
    pad_end = jnp.cumsum(padded)
    pad_start = pad_end - padded
    dest_tiles = _dest(eidx, rank, pad_start)
    n_blocks = -(-(n * TOP_K) // EXPERT_BLOCK) + N_EXPERTS
    block_row0 = jnp.arange(n_blocks, dtype=I32) * EXPERT_BLOCK
    block_e = jnp.minimum(jnp.sum((pad_end[None, :] <= block_row0[:, None]).astype(I32), axis=1), N_EXPERTS - 1)
    n_used = (pad_end[-1:] // EXPERT_BLOCK).astype(I32)
    pad_range = jnp.stack([pad_start + counts, pad_end]).astype(I32)
    xs = _dispatch(xm, dest_tiles, pad_range, n_used, n_blocks * EXPERT_BLOCK)
    ys = _experts(xs, block_e, n_used, w_e_gate, w_e_up, w_e_down)
    return _combine(ys, dest_tiles, wrow, xm, w_s_gate, w_s_up, w_s_down)


def kernel(x_prompt, x_sample, cache_kv_cmp, cache_kv_sel, state_kv_win, state_gla, page_table, c_prompt, c_sample, w_in, b_in, cmp_k_pos, cmp_k_w1, cmp_k_w2, cmp_v_pos, cmp_v_w1, cmp_v_w2, gla_w_a2, gla_b_a, gla_norm_g, w_br_a, w_br_b, w_out, ln1_g, ln1_b, w_ada, b_ada, w_router, router_bias, w_e_gate, w_e_up, w_e_down, w_s_gate, w_s_up, w_s_down, ln2_g, ln2_b):
    bp, tp, d = x_prompt.shape
    nd, td = x_sample.shape[:2]
    n_pool, page_rows = cache_kv_cmp.shape[:2]
    past_len = page_table.shape[1] * page_rows
    assert td == 1 and d == D_MODEL and page_rows == PAGE_ROWS and page_table.shape[1] == PAGES and tp == PAGES * PAGE_ROWS
    kv_w = 2 * NSA_KV_HEADS * HEAD_DIM

    mod = _adaln(jnp.concatenate([c_prompt, c_sample], axis=0), w_ada, b_ada)
    mod_p = [m.reshape(bp, 1, d) for m in jnp.split(mod[:bp], 6, axis=-1)]
    mod_s = [m.reshape(1, nd, d) for m in jnp.split(mod[bp:], 6, axis=-1)]

    w_pack, b_pack = _pack_in_weights(w_in, b_in)
    cmp_wk = _pack_cmp_weights(cmp_k_pos, cmp_k_w1, cmp_k_w2)
    cmp_wv = _pack_cmp_weights(cmp_v_pos, cmp_v_w1, cmp_v_w2)
    wa_pad = jnp.zeros((LANE, GLA_HEADS * GLA_DK), F32).at[MISC_AG:MISC_AG + GLA_GATE_RANK].set(gla_w_a2)
    tail_consts = (_gate_expand_table(), gla_norm_g.reshape(1, -1), _pad_br_a(w_br_a), w_br_b.astype(BF16), w_out.astype(BF16),
                   ln1_g.reshape(1, d), ln1_b.reshape(1, d))

    gm, qn, vg, rg, kvc, kvs, kvw, qg, kg, misc = _inproj(
        x_prompt, mod_p[0], mod_p[1], w_pack, b_pack, _rope_tables(jnp.arange(tp, dtype=I32)), 256)
    kc, vc = _compress(kvc.reshape(bp * PAGES, PAGE_ROWS, kv_w), jnp.arange(bp * PAGES, dtype=I32), cmp_wk, cmp_wv)
    ocmp, sel = _cmp_attn(qn, kc, vc, 256)
    osel = _sel_attn(qn, kvs, sel, 128, 256)
    owin = _win_attn(qn, kvw, 128)
    ogla, gla_p = _gla(qg, kg, vg, misc, wa_pad, gla_b_a, None, tp)
    x1_p, xm_p = _mixer_tail(ocmp, osel, owin, misc, ogla, rg, gm, x_prompt, mod_p[2], mod_p[4], mod_p[3], tail_consts, 256)
    n_win = min(WINDOW, tp)
    outs_p = (kvc.reshape(bp, tp, 2, NSA_KV_HEADS, HEAD_DIM), kvs.reshape(bp, tp, 2, NSA_KV_HEADS, HEAD_DIM),
              kvw[:, tp - n_win:].reshape(bp, n_win, 2, NSA_KV_HEADS, HEAD_DIM), gla_p)

    gm, qn, vg, rg, kvc, kvs, kvw, qg, kg, misc = _inproj(
        x_sample.reshape(1, nd, d), mod_s[0], mod_s[1], w_pack, b_pack, _rope_tables(jnp.full((nd,), past_len, I32)), nd)
    page_ids = page_table.reshape(-1).astype(I32)
    kc, vc = _compress(cache_kv_cmp.reshape(n_pool, PAGE_ROWS, kv_w), page_ids, cmp_wk, cmp_wv)
    ocmp, osel, owin, win_new = _nsa_decode(
        qn.reshape(nd, 1, -1), kc, vc, cache_kv_sel.reshape(n_pool, PAGE_ROWS, kv_w).transpose(0, 2, 1), page_ids,
        kvs.reshape(nd, 1, kv_w), kvw.reshape(nd, 1, kv_w), state_kv_win.reshape(nd, -1, kv_w).transpose(0, 2, 1))
    win_new = win_new.transpose(0, 2, 1)

    def pad_rows(a):
        return jnp.pad(a.reshape(nd, 1, -1), ((0, 0), (0, GLA_SUB - 1), (0, 0)))

    ogla, gla_s = _gla(pad_rows(qg), pad_rows(kg), pad_rows(vg), pad_rows(misc), wa_pad, gla_b_a, state_gla, 1)
    x1_s, xm_s = _mixer_tail(ocmp.reshape(1, nd, -1), osel.reshape(1, nd, -1), owin.reshape(1, nd, -1), misc,
                             ogla[:, 0].reshape(1, nd, -1), rg, gm, x_sample.reshape(1, nd, d),
                             mod_s[2], mod_s[4], mod_s[3], tail_consts, nd)
    outs_s = (kvc.reshape(nd, 1, 2, NSA_KV_HEADS, HEAD_DIM), kvs.reshape(nd, 1, 2, NSA_KV_HEADS, HEAD_DIM),
              win_new.reshape(state_kv_win.shape), gla_s)

    n_p = bp * tp
    moe = _moe(jnp.concatenate([xm_p.reshape(n_p, d), xm_s.reshape(nd, d)], axis=0),
               w_router, router_bias, w_e_gate, w_e_up, w_e_down, w_s_gate, w_s_up, w_s_down)
    y_p = _final_ln(x1_p, moe[:n_p].reshape(bp, tp, d), mod_p[5], ln2_g, ln2_b, 256)
    y_s = _final_ln(x1_s, moe[n_p:].reshape(1, nd, d), mod_s[5], ln2_g, ln2_b, nd).reshape(nd, 1, d)
    return (y_p, y_s) + outs_p + outs_s
```

```python
import functools

import numpy as np
import jax
import jax.numpy as jnp
from jax import lax
from jax.experimental import pallas as pl
from jax.experimental.pallas import tpu as pltpu

F32 = jnp.float32
BF16 = jnp.bfloat16
I32 = jnp.int32

D_MODEL = 1024
NSA_HEADS = 8
NSA_KV_HEADS = 2
NSA_GROUP = NSA_HEADS // NSA_KV_HEADS
HEAD_DIM = 64
ROT_DIM = HEAD_DIM // 4
ROPE_THETA = 500000.0
CMP_LEN = 32
CMP_STRIDE = 16
CMP_HIDDEN = 256
SEL_BLOCK = 64
SEL_TOP_N = 16
WINDOW = 512
FORCE_SCORE = 1.0e4
GLA_HEADS = 4
GLA_DK = 64
GLA_DV = 128
GLA_GATE_RANK = 16
GLA_TAU = 16.0
GLA_SUB = 16
N_EXPERTS = 256
TOP_K = 8
N_GROUPS = 8
TOPK_GROUPS = 4
EXPERT_DIM = 256
SHARED_DIM = 256
ROUTED_SCALE = 2.5
EXPERT_BLOCK = 128
DN_ALPHA = 2.0 ** 0.25
LN_EPS = 1e-5
LANE = 128
NEG = -1.0e30
VMEM_LIMIT = 56 * 1024 * 1024

SEG_GM = (0, 2 * D_MODEL)
SEG_QN = (SEG_GM[0] + SEG_GM[1], NSA_HEADS * LANE)
SEG_VG = (SEG_QN[0] + SEG_QN[1], GLA_HEADS * GLA_DV)
SEG_RG = (SEG_VG[0] + SEG_VG[1], GLA_HEADS * GLA_DV)
SEG_KVC = (SEG_RG[0] + SEG_RG[1], 2 * LANE)
SEG_KVS = (SEG_KVC[0] + SEG_KVC[1], 2 * LANE)
SEG_KVW = (SEG_KVS[0] + SEG_KVS[1], 2 * LANE)
SEG_QG = (SEG_KVW[0] + SEG_KVW[1], GLA_HEADS * GLA_DK)
SEG_KG = (SEG_QG[0] + SEG_QG[1], GLA_HEADS * GLA_DK)
SEG_MISC = (SEG_KG[0] + SEG_KG[1], LANE)
IN_PACKED = SEG_MISC[0] + SEG_MISC[1]
MISC_GN = 0
MISC_AG = NSA_HEADS * 3


def _cparams(*sem):
    return pltpu.CompilerParams(dimension_semantics=sem, vmem_limit_bytes=VMEM_LIMIT)


def _bdot(a, b):
    return jnp.dot(a.astype(BF16), b.astype(BF16), preferred_element_type=F32)


def _dot_nt(a, b, precision=None):
    return lax.dot_general(a, b, (((1,), (1,)), ((), ())), preferred_element_type=F32, precision=precision)


def _adaln_kernel(c_ref, w_ref, b_ref, o_ref):
    c = c_ref[...]
    o_ref[...] = _bdot(c * jax.nn.sigmoid(c), w_ref[...]) + b_ref[...]


def _adaln(c, w_ada, b_ada):
    n, d = c.shape
    m = w_ada.shape[1]
    tn = 512
    return pl.pallas_call(
        _adaln_kernel,
        grid=(m // tn,),
        in_specs=[pl.BlockSpec((n, d), lambda j: (0, 0)),
                  pl.BlockSpec((d, tn), lambda j: (0, j)),
                  pl.BlockSpec((1, tn), lambda j: (0, j))],
        out_specs=pl.BlockSpec((n, tn), lambda j: (0, j)),
        out_shape=jax.ShapeDtypeStruct((n, m), F32),
        compiler_params=_cparams("parallel"),
        name="adaln",
    )(c, w_ada, b_ada.reshape(1, m))


def _rope_tables(pos):
    half = ROT_DIM // 2
    inv = jnp.power(ROPE_THETA, -jnp.arange(half, dtype=F32) * 2.0 / ROT_DIM)
    ang = pos.astype(F32)[:, None] * inv[None, :]
    cos, sin = jnp.cos(ang), jnp.sin(ang)
    t = pos.shape[0]
    one = jnp.ones((t, HEAD_DIM - ROT_DIM), F32)
    z8 = jnp.zeros((t, half), F32)
    z48 = jnp.zeros((t, HEAD_DIM - ROT_DIM), F32)
    c = jnp.concatenate([cos, cos, one, cos, cos, one], axis=1)
    s1 = jnp.concatenate([-sin, z8, z48, -sin, z8, z48], axis=1)
    s2 = jnp.concatenate([z8, sin, z48, z8, sin, z48], axis=1)
    return c, s1, s2


def _pack_in_weights(w_in, b_in):
    sizes = (512, 128, 128, 128, 128, 128, 128, 24, 256, 256, 512, 512, 16, 2048)
    offs = np.concatenate([[0], np.cumsum(sizes)])

    def pack(w):
        seg = [w[..., offs[i]:offs[i + 1]] for i in range(len(sizes))]
        q_n, k_c, v_c, k_s, v_s, k_w, v_w, g_n, q_g, k_g, v_g, r_g, a_g, g_m = seg
        zero = jnp.zeros_like(q_n[..., :HEAD_DIM])
        q_slots = []
        for hh in range(NSA_HEADS):
            qh = q_n[..., hh * HEAD_DIM:(hh + 1) * HEAD_DIM] * (HEAD_DIM ** -0.5)
            q_slots += [qh, zero] if hh // NSA_GROUP == 0 else [zero, qh]
        misc_pad = jnp.zeros_like(w[..., :LANE - g_n.shape[-1] - a_g.shape[-1]])
        return jnp.concatenate([g_m] + q_slots + [v_g, r_g, k_c, v_c, k_s, v_s, k_w, v_w,
                                                   q_g * (GLA_DK ** -0.5), k_g, g_n, a_g, misc_pad], axis=-1)

    return pack(w_in).astype(BF16), pack(b_in.reshape(1, -1))


def _inproj_kernel(x_ref, sh_ref, sc_ref, w_ref, b_ref, rc_ref, rs1_ref, rs2_ref,
                   gm_ref, qn_ref, vg_ref, rg_ref, kvc_ref, kvs_ref, kvw_ref, qg_ref, kg_ref, misc_ref):
    h = (x_ref[0] * (1.0 + sc_ref[0]) + sh_ref[0]).astype(BF16)
    rc, rs1, rs2 = rc_ref[...], rs1_ref[...], rs2_ref[...]

    def proj(off, width):
        return jnp.dot(h, w_ref[:, off:off + width], preferred_element_type=F32) + b_ref[:, off:off + width]

    def rope(z):
        return z * rc + pltpu.roll(z, LANE - ROT_DIM // 2, 1) * rs1 + pltpu.roll(z, ROT_DIM // 2, 1) * rs2

    def plain(ref, seg):
        off, width = seg
        step = min(width, 512)
        for c in range(0, width, step):
            ref[0, :, c:c + step] = proj(off + c, step)

    plain(gm_ref, SEG_GM)
    for c in range(0, SEG_QN[1], 512):
        z = proj(SEG_QN[0] + c, 512)
        for s in range(0, 512, LANE):
            qn_ref[0, :, c + s:c + s + LANE] = rope(z[:, s:s + LANE])
    plain(vg_ref, SEG_VG)
    plain(rg_ref, SEG_RG)
    for ref, seg in ((kvc_ref, SEG_KVC), (kvs_ref, SEG_KVS), (kvw_ref, SEG_KVW)):
        z = proj(seg[0], seg[1])
        ref[0, :, 0:LANE] = rope(z[:, 0:LANE])
        ref[0, :, LANE:2 * LANE] = z[:, LANE:2 * LANE]
    plain(qg_ref, SEG_QG)
    plain(kg_ref, SEG_KG)
    plain(misc_ref, SEG_MISC)


def _inproj(x, shift, scale, w_pack, b_pack, tables, tm):
    b, t, d = x.shape
    per_tok = shift.shape[1] != 1
    mod_spec = (pl.BlockSpec((1, tm, d), lambda i, j: (i, j, 0)) if per_tok
                else pl.BlockSpec((1, 1, d), lambda i, j: (i, 0, 0)))
    segs = (SEG_GM, SEG_QN, SEG_VG, SEG_RG, SEG_KVC, SEG_KVS, SEG_KVW, SEG_QG, SEG_KG, SEG_MISC)
    tab_spec = pl.BlockSpec((tm, LANE), lambda i, j: (j, 0))
    return pl.pallas_call(
        _inproj_kernel,
        grid=(b, t // tm),
        in_specs=[pl.BlockSpec((1, tm, d), lambda i, j: (i, j, 0)), mod_spec, mod_spec,
                  pl.BlockSpec((d, IN_PACKED), lambda i, j: (0, 0)),
                  pl.BlockSpec((1, IN_PACKED), lambda i, j: (0, 0)),
                  tab_spec, tab_spec, tab_spec],
        out_specs=[pl.BlockSpec((1, tm, w), lambda i, j: (i, j, 0)) for _, w in segs],
        out_shape=[jax.ShapeDtypeStruct((b, t, w), F32) for _, w in segs],
        compiler_params=_cparams("parallel", "parallel"),
        name="inproj",
    )(x, shift, scale, w_pack, b_pack, *tables)


CHUNKS = 128
PAGE_ROWS = 128
PAGES = 16


def _pack_cmp_weights(pos, w1, w2):
    pos2 = jnp.concatenate([pos, pos], axis=1)
    z1 = jnp.zeros_like(w1)
    bd1 = jnp.concatenate([jnp.concatenate([w1, z1], axis=2), jnp.concatenate([z1, w1], axis=2)], axis=1)
    w1p = jnp.concatenate([bd1[:CMP_STRIDE], bd1[CMP_STRIDE:]], axis=2).astype(BF16)
    z2 = jnp.zeros_like(w2)
    w2p = jnp.concatenate([jnp.concatenate([w2, z2], axis=1), jnp.concatenate([z2, w2], axis=1)], axis=0).astype(BF16)
    return pos2, w1p, w2p


def _fill_chunks(page_refs, xs_ref):
    for p, pr in enumerate(page_refs):
        for l in range(CMP_STRIDE):
            xs_ref[l, 8 * p:8 * p + 8, :] = pr[0, pl.ds(l, PAGE_ROWS // CMP_STRIDE, stride=CMP_STRIDE), :]


def _compress_chunks(xs_ref, pos_ref, w1_ref, w2_ref):
    hid2 = 2 * CMP_HIDDEN
    acc_a = jnp.zeros((CHUNKS, hid2), F32)
    acc_b = jnp.zeros((CHUNKS, hid2), F32)
    for l in range(CMP_STRIDE):
        x = xs_ref[l]
        acc_a = acc_a + jnp.dot((x + pos_ref[l:l + 1, :]).astype(BF16), w1_ref[l, :, 0:hid2], preferred_element_type=F32)
        acc_b = acc_b + jnp.dot((x + pos_ref[CMP_STRIDE + l:CMP_STRIDE + l + 1, :]).astype(BF16), w1_ref[l, :, hid2:2 * hid2],
                                preferred_element_type=F32)
    hid = acc_a + pltpu.roll(acc_b, CHUNKS - 1, 0)
    out = jnp.dot(jax.nn.gelu(hid).astype(BF16), w2_ref[...], preferred_element_type=F32)
    row = lax.broadcasted_iota(I32, out.shape, 0)
    return jnp.where(row < CHUNKS - 1, out, 0.0)


def _compress_kernel(pt_ref, *refs):
    k_pages, v_pages = refs[:PAGES], refs[PAGES:2 * PAGES]
    posk_ref, w1k_ref, w2k_ref, posv_ref, w1v_ref, w2v_ref, kc_ref, vc_ref, xk_ref, xv_ref = refs[2 * PAGES:]
    _fill_chunks(k_pages, xk_ref)
    _fill_chunks(v_pages, xv_ref)
    kc_ref[0] = _compress_chunks(xk_ref, posk_ref, w1k_ref, w2k_ref)
    vc_ref[0] = _compress_chunks(xv_ref, posv_ref, w1v_ref, w2v_ref)


def _page_spec(p, half):
    return pl.BlockSpec((1, PAGE_ROWS, LANE), lambda i, pt: (pt[i * PAGES + p], 0, half))


def _const_spec(shape):
    nd = len(shape)
    return pl.BlockSpec(shape, lambda i, pt: (0,) * nd)


def _compress(pages, page_ids, cmp_wk, cmp_wv):
    n_b = page_ids.shape[0] // PAGES
    consts = list(cmp_wk) + list(cmp_wv)
    grid_spec = pltpu.PrefetchScalarGridSpec(
        num_scalar_prefetch=1,
        grid=(n_b,),
        in_specs=[_page_spec(p, h) for h in range(2) for p in range(PAGES)] + [_const_spec(c.shape) for c in consts],
        out_specs=[pl.BlockSpec((1, CHUNKS, LANE), lambda i, pt: (i, 0, 0))] * 2,
        scratch_shapes=[pltpu.VMEM((CMP_STRIDE, CHUNKS, LANE), F32)] * 2,
    )
    return pl.pallas_call(
        _compress_kernel,
        grid_spec=grid_spec,
        out_shape=[jax.ShapeDtypeStruct((n_b, CHUNKS, LANE), F32)] * 2,
        compiler_params=_cparams("parallel"),
        name="compress",
    )(page_ids, *([pages] * (2 * PAGES)), *consts)


def _cover_tables(n_sel):
    c_start = np.arange(CHUNKS) * CMP_STRIDE
    s_start = np.arange(n_sel) * SEL_BLOCK
    cover = ((c_start[:, None] < s_start[None, :] + SEL_BLOCK) & (c_start[:, None] + CMP_LEN > s_start[None, :])).astype(np.float32)
    cover[CHUNKS - 1] = 0.0
    out = np.zeros((NSA_KV_HEADS, LANE, CHUNKS), np.float32)
    for h in range(NSA_KV_HEADS):
        out[h, h * 64:h * 64 + n_sel] = cover.T
    return jnp.asarray(out)


def _softmax_rows(s, valid):
    s = jnp.where(valid, s, NEG)
    m = jnp.max(s, axis=-1, keepdims=True)
    m = jnp.where(m > 0.5 * NEG, m, 0.0)
    p = jnp.where(valid, jnp.exp(s - m), 0.0)
    return p / jnp.maximum(jnp.sum(p, axis=-1, keepdims=True), 1e-30)


def _select_blocks(imp, n_sel, top_n):
    ridx = lax.broadcasted_iota(I32, imp.shape, 0)
    cnt = jnp.zeros(imp.shape, F32)
    for i in range(n_sel):
        vi = imp[i:i + 1, :]
        ahead = (vi > imp) | ((vi == imp) & (ridx > i))
        cnt = cnt + jnp.where(ahead, 1.0, 0.0)
    return jnp.where((cnt < top_n) & (ridx < n_sel), 1.0, 0.0)


def _cmp_attn_kernel(qn_ref, kc_ref, vc_ref, cov_ref, o_ref, sel_ref, *, tq, n_sel):
    qi = pl.program_id(1)
    kc = kc_ref[0].astype(BF16)
    vc = vc_ref[0].astype(BF16)
    qpos = qi * tq + lax.broadcasted_iota(I32, (tq, CHUNKS), 0)
    cidx = lax.broadcasted_iota(I32, (tq, CHUNKS), 1)
    valid = (cidx * CMP_STRIDE + CMP_LEN - 1 <= qpos) & (cidx < CHUNKS - 1)
    psum = [jnp.zeros((tq, CHUNKS), F32) for _ in range(NSA_KV_HEADS)]
    for hh in range(NSA_HEADS):
        q = qn_ref[0, :, hh * LANE:(hh + 1) * LANE].astype(BF16)
        p = _softmax_rows(_dot_nt(q, kc), valid)
        o_ref[0, :, hh * LANE:(hh + 1) * LANE] = jnp.dot(p.astype(BF16), vc, preferred_element_type=F32)
        psum[hh // NSA_GROUP] = psum[hh // NSA_GROUP] + p
    imp = (_dot_nt(cov_ref[0], psum[0], lax.Precision.HIGHEST) + _dot_nt(cov_ref[1], psum[1], lax.Precision.HIGHEST))
    blk = lax.broadcasted_iota(I32, (LANE, tq), 0) & 63
    qpos_t = qi * tq + lax.broadcasted_iota(I32, (LANE, tq), 1)
    cur = qpos_t // SEL_BLOCK
    forced = (blk == 0) | (blk == cur) | (blk == cur - 1)
    imp = jnp.where(forced, FORCE_SCORE, jnp.where(blk * SEL_BLOCK <= qpos_t, imp, -FORCE_SCORE))
    sel_t = jnp.concatenate([_select_blocks(imp[0:64], n_sel, SEL_TOP_N), _select_blocks(imp[64:128], n_sel, SEL_TOP_N)], axis=0)
    sel_ref[0] = sel_t.T


def _cmp_attn(qn, kc, vc, tq):
    b, t, _ = qn.shape
    n_sel = -(-t // SEL_BLOCK)
    cov = _cover_tables(n_sel)
    return pl.pallas_call(
        functools.partial(_cmp_attn_kernel, tq=tq, n_sel=n_sel),
        grid=(b, t // tq),
        in_specs=[pl.BlockSpec((1, tq, NSA_HEADS * LANE), lambda i, j: (i, j, 0)),
                  pl.BlockSpec((1, CHUNKS, LANE), lambda i, j: (i, 0, 0)),
                  pl.BlockSpec((1, CHUNKS, LANE), lambda i, j: (i, 0, 0)),
                  pl.BlockSpec((NSA_KV_HEADS, LANE, CHUNKS), lambda i, j: (0, 0, 0))],
        out_specs=[pl.BlockSpec((1, tq, NSA_HEADS * LANE), lambda i, j: (i, j, 0)),
                   pl.BlockSpec((1, tq, LANE), lambda i, j: (i, j, 0))],
        out_shape=[jax.ShapeDtypeStruct((b, t, NSA_HEADS * LANE), F32), jax.ShapeDtypeStruct((b, t, LANE), F32)],
        compiler_params=_cparams("parallel", "parallel"),
        name="cmp_attn",
    )(qn, kc, vc, cov)


def _key_block_table(t):
    blk = np.arange(t)[:, None] // SEL_BLOCK
    lanes = np.arange(LANE)[None, :] & 63
    return jnp.asarray((blk == lanes).astype(np.float32), dtype=BF16)


def _lane_fold(x, op):
    out = x[:, 0:LANE]
    for c in range(LANE, x.shape[1], LANE):
        out = op(out, x[:, c:c + LANE])
    return out


def _sel_attn_kernel(qn_ref, kv_ref, sel_ref, kb_ref, o_ref, q2_scr, k2_scr, v_scr, s_scr, *, tq, tk):
    qi = pl.program_id(1)
    rows = NSA_HEADS * tq
    t = kv_ref.shape[1]

    @pl.when(qi == 0)
    def _():
        k2_scr[:, 0:LANE] = kv_ref[0, :, 0:LANE].astype(BF16)
        k2_scr[:, LANE:2 * LANE] = kb_ref[...]
        v_scr[...] = kv_ref[0, :, LANE:2 * LANE].astype(BF16)

    not_sel = (1.0 - sel_ref[0]) * NEG
    lane_head = lax.broadcasted_iota(I32, (tq, LANE), 1) // 64
    for hh in range(NSA_HEADS):
        q2_scr[hh * tq:(hh + 1) * tq, 0:LANE] = qn_ref[0, :, hh * LANE:(hh + 1) * LANE].astype(BF16)
        q2_scr[hh * tq:(hh + 1) * tq, LANE:2 * LANE] = jnp.where(lane_head == hh // NSA_GROUP, not_sel, 0.0).astype(BF16)
    q2 = q2_scr[...]
    last = (qi * tq + tq - 1) // tk

    def scores(kt):
        k0 = pl.multiple_of(kt * tk, tk)
        return _dot_nt(q2, k2_scr[pl.ds(k0, tk), :])

    def pass1(kt, m_acc):
        s = scores(kt)
        s_scr[kt] = s
        return jnp.maximum(m_acc, _lane_fold(s, jnp.maximum))

    m_acc = lax.fori_loop(0, last, pass1, jnp.full((rows, LANE), NEG, F32))
    rel = (qi * tq + lax.broadcasted_iota(I32, (tq, tk), 0)) - (last * tk + lax.broadcasted_iota(I32, (tq, tk), 1))
    causal = jnp.where(rel >= 0, 0.0, NEG)
    s_last = scores(last) + jnp.concatenate([causal] * NSA_HEADS, axis=0)
    m = jnp.max(jnp.maximum(m_acc, _lane_fold(s_last, jnp.maximum)), axis=-1, keepdims=True)

    def accumulate(s, v, l_acc, acc):
        p = jnp.exp(s - m)
        return l_acc + _lane_fold(p, jnp.add), acc + jnp.dot(p.astype(BF16), v, preferred_element_type=F32)

    def pass2(kt, carry):
        k0 = pl.multiple_of(kt * tk, tk)
        return accumulate(s_scr[kt], v_scr[pl.ds(k0, tk), :], *carry)

    l_acc, acc = lax.fori_loop(0, last, pass2, (jnp.zeros((rows, LANE), F32), jnp.zeros((rows, LANE), F32)))
    l_acc, acc = accumulate(s_last, v_scr[pl.ds(pl.multiple_of(last * tk, tk), tk), :], l_acc, acc)
    out = acc / jnp.sum(l_acc, axis=-1, keepdims=True)
    for hh in range(NSA_HEADS):
        o_ref[0, :, hh * LANE:(hh + 1) * LANE] = out[hh * tq:(hh + 1) * tq]


def _sel_attn(qn, kv, sel, tq, tk):
    b, t, _ = qn.shape
    rows = NSA_HEADS * tq
    return pl.pallas_call(
        functools.partial(_sel_attn_kernel, tq=tq, tk=tk),
        grid=(b, t // tq),
        in_specs=[pl.BlockSpec((1, tq, NSA_HEADS * LANE), lambda i, j: (i, j, 0)),
                  pl.BlockSpec((1, t, 2 * LANE), lambda i, j: (i, 0, 0)),
                  pl.BlockSpec((1, tq, LANE), lambda i, j: (i, j, 0)),
                  pl.BlockSpec((t, LANE), lambda i, j: (0, 0))],
        out_specs=pl.BlockSpec((1, tq, NSA_HEADS * LANE), lambda i, j: (i, j, 0)),
        out_shape=jax.ShapeDtypeStruct((b, t, NSA_HEADS * LANE), F32),
        scratch_shapes=[pltpu.VMEM((rows, 2 * LANE), BF16), pltpu.VMEM((t, 2 * LANE), BF16), pltpu.VMEM((t, LANE), BF16),
                        pltpu.VMEM((t // tk, rows, tk), F32)],
        compiler_params=_cparams("parallel", "arbitrary"),
        name="sel_attn",
    )(qn, kv, sel, _key_block_table(t))


def _win_attn_kernel(qn_ref, kv_ref, o_ref, q_scr, k_scr, v_scr, *, tq):
    qi = pl.program_id(1)
    span = WINDOW + tq

    @pl.when(qi == 0)
    def _():
        k_scr[...] = kv_ref[0, :, 0:LANE].astype(BF16)
        v_scr[...] = kv_ref[0, :, LANE:2 * LANE].astype(BF16)

    for hh in range(NSA_HEADS):
        q_scr[hh * tq:(hh + 1) * tq, :] = qn_ref[0, :, hh * LANE:(hh + 1) * LANE].astype(BF16)
    k0 = pl.multiple_of(jnp.maximum(qi * tq - WINDOW, 0), tq)
    rel = (qi * tq + lax.broadcasted_iota(I32, (tq, span), 0)) - (k0 + lax.broadcasted_iota(I32, (tq, span), 1))
    bias = jnp.where((rel >= 0) & (rel <= WINDOW), 0.0, NEG)
    s = _dot_nt(q_scr[...], k_scr[pl.ds(k0, span), :]) + jnp.concatenate([bias] * NSA_HEADS, axis=0)
    p = jnp.exp(s - jnp.max(s, axis=-1, keepdims=True))
    out = jnp.dot(p.astype(BF16), v_scr[pl.ds(k0, span), :], preferred_element_type=F32) / jnp.sum(p, axis=-1, keepdims=True)
    for hh in range(NSA_HEADS):
        o_ref[0, :, hh * LANE:(hh + 1) * LANE] = out[hh * tq:(hh + 1) * tq]


def _win_attn(qn, kv, tq):
    b, t, _ = qn.shape
    assert t >= WINDOW + tq
    rows = NSA_HEADS * tq
    return pl.pallas_call(
        functools.partial(_win_attn_kernel, tq=tq),
        grid=(b, t // tq),
        in_specs=[pl.BlockSpec((1, tq, NSA_HEADS * LANE), lambda i, j: (i, j, 0)),
                  pl.BlockSpec((1, t, 2 * LANE), lambda i, j: (i, 0, 0))],
        out_specs=pl.BlockSpec((1, tq, NSA_HEADS * LANE), lambda i, j: (i, j, 0)),
        out_shape=jax.ShapeDtypeStruct((b, t, NSA_HEADS * LANE), F32),
        scratch_shapes=[pltpu.VMEM((rows, LANE), BF16), pltpu.VMEM((t, LANE), BF16), pltpu.VMEM((t, LANE), BF16)],
        compiler_params=_cparams("parallel", "arbitrary"),
        name="win_attn",
    )(qn, kv)


def _dec_softmax(scores, vals):
    m = scores[0].max(axis=-1, keepdims=True)
    for s in scores[1:]:
        m = jnp.maximum(m, s.max(axis=-1, keepdims=True))
    den = jnp.zeros_like(m)
    out = jnp.zeros((m.shape[0], LANE), F32)
    for s, v in zip(scores, vals):
        p = jnp.exp(s - m)
        den = den + p.sum(axis=-1, keepdims=True)
        out = out + (p * v if s.shape[1] == 1 else _dot_nt(p.astype(BF16), v))
    return out / den


def _nsa_decode_kernel(pt_ref, *refs, past_len, n_sel):
    pages = refs[:PAGES]
    (qn_ref, kc_ref, vc_ref, kvs_ref, kvw_ref, win_ref, cov_ref,
     ocmp_ref, osel_ref, owin_ref, wnew_ref) = refs[PAGES:]
    nh = NSA_HEADS
    q = jnp.concatenate([qn_ref[0, :, hh * LANE:(hh + 1) * LANE] for hh in range(nh)], axis=0)
    qb = q.astype(BF16)
    cidx = lax.broadcasted_iota(I32, (nh, CHUNKS), 1)
    valid = (cidx * CMP_STRIDE + CMP_LEN - 1 <= past_len) & (cidx < CHUNKS - 1)
    p = _softmax_rows(_dot_nt(qb, kc_ref[0].astype(BF16)), valid)
    o_cmp = jnp.dot(p.astype(BF16), vc_ref[0].astype(BF16), preferred_element_type=F32)
    imp = None
    for h in range(NSA_KV_HEADS):
        ps = jnp.sum(p[h * NSA_GROUP:(h + 1) * NSA_GROUP], axis=0, keepdims=True)
        term = _dot_nt(cov_ref[h], jnp.broadcast_to(ps, (LANE, CHUNKS)), lax.Precision.HIGHEST)
        imp = term if imp is None else imp + term
    blk = lax.broadcasted_iota(I32, (LANE, LANE), 0) & 63
    cur = past_len // SEL_BLOCK
    forced = (blk == 0) | (blk == cur) | (blk == cur - 1)
    imp = jnp.where(forced, FORCE_SCORE, jnp.where(blk * SEL_BLOCK <= past_len, imp, -FORCE_SCORE))
    sel_t = jnp.concatenate([_select_blocks(imp[0:64], n_sel, min(SEL_TOP_N, n_sel)),
                             _select_blocks(imp[64:128], n_sel, min(SEL_TOP_N, n_sel))], axis=0)
    sel = sel_t.T[0:1, :]
    head_of_row = lax.broadcasted_iota(I32, (nh, 1), 0) // NSA_GROUP

    def picked(s):
        return jnp.where(head_of_row == 0, sel[:, s:s + 1], sel[:, 64 + s:64 + s + 1])

    first_half = lax.broadcasted_iota(I32, (nh, PAGE_ROWS), 1) < SEL_BLOCK
    scores, vals = [], []
    for pg, pr in enumerate(pages):
        s = jnp.dot(qb, pr[0, 0:LANE, :].astype(BF16), preferred_element_type=F32)
        ok = jnp.where(first_half, picked(2 * pg), picked(2 * pg + 1)) > 0.5
        scores.append(jnp.where(ok, s, NEG))
        vals.append(pr[0, LANE:2 * LANE, :].astype(BF16))
    s_new = jnp.sum(q * kvs_ref[0, :, 0:LANE], axis=-1, keepdims=True)
    scores.append(jnp.where(picked(past_len // SEL_BLOCK) > 0.5, s_new, NEG))
    vals.append(kvs_ref[0, :, LANE:2 * LANE])
    o_sel = _dec_softmax(scores, vals)
    n_win = win_ref.shape[2]
    kpos = past_len - n_win + lax.broadcasted_iota(I32, (nh, n_win), 1)
    rel = past_len - kpos
    s_win = jnp.dot(qb, win_ref[0, 0:LANE, :].astype(BF16), preferred_element_type=F32)
    s_win = jnp.where((kpos >= 0) & (rel >= 0) & (rel <= WINDOW), s_win, NEG)
    s_new = jnp.sum(q * kvw_ref[0, :, 0:LANE], axis=-1, keepdims=True)
    o_win = _dec_softmax([s_win, s_new], [win_ref[0, LANE:2 * LANE, :].astype(BF16), kvw_ref[0, :, LANE:2 * LANE]])
    for hh in range(nh):
        ocmp_ref[0, :, hh * LANE:(hh + 1) * LANE] = o_cmp[hh:hh + 1]
        osel_ref[0, :, hh * LANE:(hh + 1) * LANE] = o_sel[hh:hh + 1]
        owin_ref[0, :, hh * LANE:(hh + 1) * LANE] = o_win[hh:hh + 1]
    for c in range(0, 2 * LANE, LANE):
        new_col = jnp.broadcast_to(kvw_ref[0, :, c:c + LANE], (LANE, LANE)).T
        shifted = pltpu.roll(win_ref[0, c:c + LANE, :], n_win - 1, 1)
        lane = lax.broadcasted_iota(I32, shifted.shape, 1)
        wnew_ref[0, c:c + LANE, :] = jnp.where(lane == n_win - 1, jnp.concatenate([new_col] * (n_win // LANE), axis=1), shifted)


def _nsa_decode(qn, kc, vc, sel_pages, page_ids, kvs_new, kvw_new, win_state):
    b = qn.shape[0]
    n_win = win_state.shape[2]
    past_len = PAGES * PAGE_ROWS
    n_sel = -(-(past_len + 1) // SEL_BLOCK)
    cov = _cover_tables(n_sel)

    def per_b(shape):
        nd = len(shape)
        return pl.BlockSpec((1,) + shape[1:], lambda i, pt: (i,) + (0,) * (nd - 1))

    slots = NSA_HEADS * LANE
    grid_spec = pltpu.PrefetchScalarGridSpec(
        num_scalar_prefetch=1,
        grid=(b,),
        in_specs=[pl.BlockSpec((1, 2 * LANE, PAGE_ROWS), (lambda i, pt, p=p: (pt[i * PAGES + p], 0, 0))) for p in range(PAGES)]
        + [per_b(qn.shape), per_b(kc.shape), per_b(vc.shape), per_b(kvs_new.shape), per_b(kvw_new.shape), per_b(win_state.shape),
           _const_spec(cov.shape)],
        out_specs=[per_b((b, 1, slots))] * 3 + [per_b(win_state.shape)],
    )
    return pl.pallas_call(
        functools.partial(_nsa_decode_kernel, past_len=past_len, n_sel=n_sel),
        grid_spec=grid_spec,
        out_shape=[jax.ShapeDtypeStruct((b, 1, slots), F32)] * 3 + [jax.ShapeDtypeStruct(win_state.shape, F32)],
        compiler_params=_cparams("parallel"),
        name="nsa_decode",
    )(page_ids, *([sel_pages] * PAGES), qn, kc, vc, kvs_new, kvw_new, win_state, cov)


def _dot_tn(a, b):
    return lax.dot_general(a, b, (((0,), (0,)), ((), ())), preferred_element_type=F32)


def _cumsum_table():
    r = np.arange(LANE)
    return jnp.asarray(((r[:, None] // GLA_SUB == r[None, :] // GLA_SUB) & (r[None, :] <= r[:, None])).astype(np.float32))


def _gla_kernel(*refs, t, t_valid, has_state):
    if has_state:
        qg_ref, kg_ref, vg_ref, misc_ref, wa_ref, ba_ref, lt_ref, s0_ref, o_ref, s_ref, b_scr, st_scr = refs
    else:
        qg_ref, kg_ref, vg_ref, misc_ref, wa_ref, ba_ref, lt_ref, o_ref, s_ref, b_scr, st_scr = refs
    z = jnp.dot(misc_ref[0], wa_ref[...], preferred_element_type=F32, precision=lax.Precision.HIGHEST) + ba_ref[...]
    la = (jnp.minimum(z, 0.0) - jnp.log1p(jnp.exp(-jnp.abs(z)))) * (1.0 / GLA_TAU)
    if t_valid < t:
        la = jnp.where(lax.broadcasted_iota(I32, la.shape, 0) < t_valid, la, 0.0)
    tile = min(t, LANE)
    for r in range(0, t, tile):
        b_scr[r:r + tile, :] = jnp.dot(lt_ref[0:tile, 0:tile], la[r:r + tile, :], preferred_element_type=F32,
                                       precision=lax.Precision.HIGHEST)
    pairs = GLA_HEADS // 2
    for p in range(pairs):
        if has_state:
            st_scr[p] = s0_ref[0, 2 * p:2 * p + 2].reshape(2 * GLA_DK, GLA_DV).T
        else:
            st_scr[p] = jnp.zeros((GLA_DV, LANE), F32)
    head_a = lax.broadcasted_iota(I32, (GLA_SUB, LANE), 1) < GLA_DK
    row = lax.broadcasted_iota(I32, (GLA_SUB, LANE), 0)

    def pair_chunk(q, k, v, b, st):
        b_last = b[GLA_SUB - 1:GLA_SUB, :]
        st_b = st.astype(BF16)
        qe = q * jnp.exp(b)
        o_a = _dot_nt(jnp.where(head_a, qe, 0.0).astype(BF16), st_b)
        o_b = _dot_nt(jnp.where(head_a, 0.0, qe).astype(BF16), st_b)
        for j in range(GLA_SUB):
            w = q * k[j:j + 1, :] * jnp.exp(jnp.minimum(b - b[j:j + 1, :], 0.0))
            w = jnp.where(row >= j, w, 0.0)
            a_a = jnp.sum(jnp.where(head_a, w, 0.0), axis=-1, keepdims=True)
            a_b = jnp.sum(jnp.where(head_a, 0.0, w), axis=-1, keepdims=True)
            o_a = o_a + a_a * v[j:j + 1, 0:GLA_DV]
            o_b = o_b + a_b * v[j:j + 1, GLA_DV:2 * GLA_DV]
        kd = k * jnp.exp(b_last - b)
        upd = (_dot_tn(v[:, 0:GLA_DV].astype(BF16), jnp.where(head_a, kd, 0.0).astype(BF16))
               + _dot_tn(v[:, GLA_DV:2 * GLA_DV].astype(BF16), jnp.where(head_a, 0.0, kd).astype(BF16)))
        return o_a, o_b, jnp.exp(b_last) * st + upd

    def chunk(c, carry):
        r0 = pl.multiple_of(c * GLA_SUB, GLA_SUB)
        for p in range(pairs):
            o_a, o_b, st_new = pair_chunk(qg_ref[0, pl.ds(r0, GLA_SUB), p * LANE:(p + 1) * LANE],
                                          kg_ref[0, pl.ds(r0, GLA_SUB), p * LANE:(p + 1) * LANE],
                                          vg_ref[0, pl.ds(r0, GLA_SUB), 2 * p * GLA_DV:2 * (p + 1) * GLA_DV],
                                          b_scr[pl.ds(r0, GLA_SUB), p * LANE:(p + 1) * LANE], st_scr[p])
            o_ref[0, pl.ds(r0, GLA_SUB), 2 * p * GLA_DV:(2 * p + 1) * GLA_DV] = o_a
            o_ref[0, pl.ds(r0, GLA_SUB), (2 * p + 1) * GLA_DV:(2 * p + 2) * GLA_DV] = o_b
            st_scr[p] = st_new
        return carry

    lax.fori_loop(0, t // GLA_SUB, chunk, 0)
    for p in range(pairs):
        s_ref[0, 2 * p:2 * p + 2] = st_scr[p].T.reshape(2, GLA_DK, GLA_DV)


def _gla(qg, kg, vg, misc, wa_pad, ba, s0, t_valid):
    b, t, _ = qg.shape
    has_state = s0 is not None
    hk = GLA_HEADS * GLA_DK

    def per_b(shape):
        nd = len(shape)
        return pl.BlockSpec((1,) + shape[1:], lambda i: (i,) + (0,) * (nd - 1))

    def const(shape):
        nd = len(shape)
        return pl.BlockSpec(shape, lambda i: (0,) * nd)

    in_specs = [per_b(qg.shape), per_b(kg.shape), per_b(vg.shape), per_b(misc.shape),
                const((LANE, hk)), const((1, hk)), const((LANE, LANE))]
    args = [qg, kg, vg, misc, wa_pad, ba.reshape(1, -1), _cumsum_table()]
    state_shape = (b, GLA_HEADS, GLA_DK, GLA_DV)
    if has_state:
        in_specs.append(per_b(state_shape))
        args.append(s0)
    return pl.pallas_call(
        functools.partial(_gla_kernel, t=t, t_valid=t_valid, has_state=has_state),
        grid=(b,),
        in_specs=in_specs,
        out_specs=[per_b(vg.shape), per_b(state_shape)],
        out_shape=[jax.ShapeDtypeStruct(vg.shape, F32), jax.ShapeDtypeStruct(state_shape, F32)],
        scratch_shapes=[pltpu.VMEM((t, hk), F32), pltpu.VMEM((GLA_HEADS // 2, GLA_DV, LANE), F32)],
        compiler_params=_cparams("parallel"),
        name="gla",
    )(*args)


def _gate_expand_table():
    out = np.zeros((3, LANE, NSA_HEADS * LANE), np.float32)
    for hh in range(NSA_HEADS):
        for j in range(3):
            out[j, MISC_GN + 3 * hh + j, hh * LANE:(hh + 1) * LANE] = 1.0
    return jnp.asarray(out, dtype=BF16)


def _pad_br_a(w_br_a):
    zero = jnp.zeros((HEAD_DIM, w_br_a.shape[1]), w_br_a.dtype)
    parts = []
    for hh in range(NSA_HEADS):
        wh = w_br_a[hh * HEAD_DIM:(hh + 1) * HEAD_DIM]
        parts += [wh, zero] if hh // NSA_GROUP == 0 else [zero, wh]
    return jnp.concatenate(parts, axis=0).astype(BF16)


def _layer_norm(v, g, b):
    mu = jnp.mean(v, axis=-1, keepdims=True)
    var = jnp.mean(jnp.square(v - mu), axis=-1, keepdims=True)
    return (v - mu) * lax.rsqrt(var + LN_EPS) * g + b


def _mixer_tail_kernel(ocmp_ref, osel_ref, owin_ref, misc_ref, ogla_ref, rg_ref, gm_ref, x_ref, gate_ref, scf_ref, shf_ref,
                       ex_ref, ng_ref, wa_ref, wb_ref, wo_ref, lg_ref, lb_ref, x1_ref, xm_ref):
    sig = jax.nn.sigmoid(misc_ref[0])
    sig_hi = sig.astype(BF16)
    sig_lo = (sig - sig_hi.astype(F32)).astype(BF16)
    o_nsa = None
    for j, ref in enumerate((ocmp_ref, osel_ref, owin_ref)):
        g = (jnp.dot(sig_hi, ex_ref[j], preferred_element_type=F32) + jnp.dot(sig_lo, ex_ref[j], preferred_element_type=F32))
        o_nsa = g * ref[0] if o_nsa is None else o_nsa + g * ref[0]
    br_a = _bdot(o_nsa, wa_ref[...])
    heads = []
    for h in range(GLA_HEADS):
        seg = ogla_ref[0, :, h * GLA_DV:(h + 1) * GLA_DV]
        mu = jnp.mean(seg, axis=-1, keepdims=True)
        var = jnp.mean(jnp.square(seg - mu), axis=-1, keepdims=True)
        r = rg_ref[0, :, h * GLA_DV:(h + 1) * GLA_DV]
        heads.append((seg - mu) * lax.rsqrt(var + LN_EPS) * ng_ref[:, h * GLA_DV:(h + 1) * GLA_DV] * (r * jax.nn.sigmoid(r)))
    br_b = _bdot(jnp.concatenate(heads, axis=1), wb_ref[...])
    gm_a = jax.nn.sigmoid(gm_ref[0, :, 0:D_MODEL])
    gm_b = jax.nn.sigmoid(gm_ref[0, :, D_MODEL:2 * D_MODEL])
    y = _bdot(gm_a * br_a + gm_b * br_b, wo_ref[...])
    x1 = _layer_norm(DN_ALPHA * x_ref[0] + gate_ref[0] * y, lg_ref[...], lb_ref[...])
    x1_ref[0] = x1
    xm_ref[0] = x1 * (1.0 + scf_ref[0]) + shf_ref[0]


def _mixer_tail(ocmp, osel, owin, misc, ogla, rg, gm, x, gate_m, scale_f, shift_f, consts, tm):
    b, t, d = x.shape
    per_tok = gate_m.shape[1] != 1

    def tok(w):
        return pl.BlockSpec((1, tm, w), lambda i, j: (i, j, 0))

    mod_spec = tok(d) if per_tok else pl.BlockSpec((1, 1, d), lambda i, j: (i, 0, 0))

    def const(a):
        nd = a.ndim
        return pl.BlockSpec(a.shape, lambda i, j: (0,) * nd)

    return pl.pallas_call(
        _mixer_tail_kernel,
        grid=(b, t // tm),
        in_specs=[tok(NSA_HEADS * LANE)] * 3 + [tok(LANE), tok(GLA_HEADS * GLA_DV), tok(GLA_HEADS * GLA_DV), tok(2 * d), tok(d),
                                               mod_spec, mod_spec, mod_spec] + [const(c) for c in consts],
        out_specs=[tok(d), tok(d)],
        out_shape=[jax.ShapeDtypeStruct((b, t, d), F32)] * 2,
        compiler_params=_cparams("parallel", "parallel"),
        name="mixer_tail",
    )(ocmp, osel, owin, misc, ogla, rg, gm, x, gate_m, scale_f, shift_f, *consts)


ROUTE_TILE = LANE


def _first_index(hit, iota, size, axis):
    return jnp.min(jnp.where(hit, iota, size), axis=axis, keepdims=True)


def _router_kernel(xm_ref, wr_ref, bias_ref, tri_ref, eidx_ref, rank_ref, wrow_ref, cnt_ref, carry_ref):
    i = pl.program_id(0)
    tm = ROUTE_TILE
    per = N_EXPERTS // N_GROUPS

    @pl.when(i == 0)
    def _():
        carry_ref[...] = jnp.zeros_like(carry_ref)

    logits = _dot_nt(wr_ref[...], xm_ref[...], lax.Precision.HIGHEST)
    s = jax.nn.sigmoid(logits)
    sb = s + bias_ref[...]
    sb3 = sb.reshape(N_GROUPS, per, tm)
    in_grp = lax.broadcasted_iota(I32, sb3.shape, 1)
    m1 = jnp.max(sb3, axis=1, keepdims=True)
    first = _first_index(sb3 == m1, in_grp, per, 1)
    m2 = jnp.max(jnp.where(in_grp == first, NEG, sb3), axis=1, keepdims=True)
    gs = (m1 + m2).reshape(N_GROUPS, tm)
    g_iota = lax.broadcasted_iota(I32, gs.shape, 0)
    g_keep = jnp.zeros(gs.shape, jnp.bool_)
    for _ in range(TOPK_GROUPS):
        pick = g_iota == _first_index(gs == jnp.max(gs, axis=0, keepdims=True), g_iota, N_GROUPS, 0)
        g_keep = g_keep | pick
        gs = jnp.where(pick, NEG, gs)
    sbm = jnp.where(g_keep.reshape(N_GROUPS, 1, tm), sb3, NEG).reshape(N_EXPERTS, tm)
    e_iota = lax.broadcasted_iota(I32, sbm.shape, 0)
    idxs, sels = [], []
    onehot = jnp.zeros(sbm.shape, F32)
    for _ in range(TOP_K):
        idx = _first_index(sbm == jnp.max(sbm, axis=0, keepdims=True), e_iota, N_EXPERTS, 0)
        pick = e_iota == idx
        idxs.append(idx)
        sels.append(jnp.sum(jnp.where(pick, s, 0.0), axis=0, keepdims=True))
        sbm = jnp.where(pick, NEG, sbm)
        onehot = onehot + jnp.where(pick, 1.0, 0.0)
    sel = jnp.concatenate(sels, axis=0)
    wts = sel / jnp.sum(sel, axis=0, keepdims=True) * ROUTED_SCALE
    carry = carry_ref[...]
    before = carry + jnp.dot(onehot.astype(BF16), tri_ref[...], preferred_element_type=F32)
    ranks = [jnp.sum(jnp.where(e_iota == idx, before, 0.0), axis=0, keepdims=True) for idx in idxs]
    eidx_ref[...] = jnp.concatenate(idxs, axis=0)
    rank_ref[...] = jnp.concatenate(ranks, axis=0).astype(I32)
    wrow_ref[...] = jnp.concatenate([wts, jnp.zeros((LANE - TOP_K, tm), F32)], axis=0).T
    carry = carry + jnp.sum(onehot, axis=1, keepdims=True)
    carry_ref[...] = carry
    cnt_ref[...] = carry


def _router(xm, w_router, router_bias):
    n, d = xm.shape
    tm = ROUTE_TILE
    r = np.arange(tm)
    tri = jnp.asarray((r[:, None] < r[None, :]).astype(np.float32), dtype=BF16)
    return pl.pallas_call(
        _router_kernel,
        grid=(n // tm,),
        in_specs=[pl.BlockSpec((tm, d), lambda i: (i, 0)),
                  pl.BlockSpec((N_EXPERTS, d), lambda i: (0, 0)),
                  pl.BlockSpec((N_EXPERTS, 1), lambda i: (0, 0)),
                  pl.BlockSpec((tm, tm), lambda i: (0, 0))],
        out_specs=[pl.BlockSpec((TOP_K, tm), lambda i: (0, i)),
                   pl.BlockSpec((TOP_K, tm), lambda i: (0, i)),
                   pl.BlockSpec((tm, LANE), lambda i: (i, 0)),
                   pl.BlockSpec((N_EXPERTS, LANE), lambda i: (0, 0))],
        out_shape=[jax.ShapeDtypeStruct((TOP_K, n), I32), jax.ShapeDtypeStruct((TOP_K, n), I32),
                   jax.ShapeDtypeStruct((n, LANE), F32), jax.ShapeDtypeStruct((N_EXPERTS, LANE), F32)],
        scratch_shapes=[pltpu.VMEM((N_EXPERTS, LANE), F32)],
        compiler_params=_cparams("arbitrary"),
        name="router",
    )(xm, w_router.T, router_bias.reshape(N_EXPERTS, 1), tri)


def _dest_kernel(eidx_ref, rank_ref, start_ref, dest_ref):
    e_iota = lax.broadcasted_iota(I32, (N_EXPERTS, ROUTE_TILE), 0)
    start = start_ref[...]
    rows = [jnp.sum(jnp.where(e_iota == eidx_ref[k:k + 1, :], start, 0.0), axis=0, keepdims=True) for k in range(TOP_K)]
    dest_ref[0] = jnp.concatenate(rows, axis=0).astype(I32) + rank_ref[...]


def _dest(eidx, rank, pad_start):
    n = eidx.shape[1]
    tm = ROUTE_TILE
    return pl.pallas_call(
        _dest_kernel,
        grid=(n // tm,),
        in_specs=[pl.BlockSpec((TOP_K, tm), lambda i: (0, i)),
                  pl.BlockSpec((TOP_K, tm), lambda i: (0, i)),
                  pl.BlockSpec((N_EXPERTS, 1), lambda i: (0, 0))],
        out_specs=pl.BlockSpec((1, TOP_K, tm), lambda i: (i, 0, 0)),
        out_shape=jax.ShapeDtypeStruct((n // tm, TOP_K, tm), I32),
        compiler_params=_cparams("parallel"),
        name="dest",
    )(eidx, rank, pad_start.astype(F32).reshape(N_EXPERTS, 1))


def _row_copy(src_ref, src_row, dst_ref, dst_row, sem):
    return pltpu.make_async_copy(src_ref.at[pl.ds(src_row, 1)], dst_ref.at[pl.ds(dst_row, 1)], sem)


def _dispatch_kernel(dest_ref, pad_ref, nb_ref, xm_ref, xs_ref, zero_ref, sems, sem):
    i = pl.program_id(0)
    tm = ROUTE_TILE
    bm = EXPERT_BLOCK
    last = pl.num_programs(0) - 1

    def start_row(r, c):
        for k in range(TOP_K):
            _row_copy(xm_ref, i * tm + r, xs_ref, dest_ref[0, k, r], sems.at[i % 2]).start()
        return c

    def wait_step(slot):
        for _ in range(TOP_K):
            pltpu.make_async_copy(xm_ref.at[pl.ds(0, tm)], xs_ref.at[pl.ds(0, tm)], sems.at[slot]).wait()

    lax.fori_loop(0, tm, start_row, 0)

    @pl.when(i >= 1)
    def _():
        wait_step((i - 1) % 2)

    @pl.when(i == last)
    def _():
        wait_step(i % 2)
        zero_ref[...] = jnp.zeros_like(zero_ref)

        def per_expert(e, c):
            lo, hi = pad_ref[0, e], pad_ref[1, e]
            lax.fori_loop(lo, hi, lambda r, cc: (_row_copy(zero_ref, 0, xs_ref, r, sem).start(), cc)[1], 0)
            lax.fori_loop(lo, hi, lambda r, cc: (_row_copy(zero_ref, 0, xs_ref, r, sem).wait(), cc)[1], 0)
            return c

        lax.fori_loop(0, N_EXPERTS, per_expert, 0)

        def tail_copy(blk):
            return pltpu.make_async_copy(zero_ref, xs_ref.at[pl.ds(blk * bm, bm)], sem)

        n_blocks = xs_ref.shape[0] // bm
        lax.fori_loop(nb_ref[0], n_blocks, lambda blk, c: (tail_copy(blk).start(), c)[1], 0)
        lax.fori_loop(nb_ref[0], n_blocks, lambda blk, c: (tail_copy(blk).wait(), c)[1], 0)


def _dispatch(xm, dest_tiles, pad_range, n_used, n_rows):
    n, d = xm.shape
    tm = ROUTE_TILE
    return pl.pallas_call(
        _dispatch_kernel,
        grid=(n // tm,),
        in_specs=[pl.BlockSpec((1, TOP_K, tm), lambda i: (i, 0, 0), memory_space=pltpu.SMEM),
                  pl.BlockSpec(memory_space=pltpu.SMEM),
                  pl.BlockSpec(memory_space=pltpu.SMEM),
                  pl.BlockSpec(memory_space=pl.ANY)],
        out_specs=pl.BlockSpec(memory_space=pl.ANY),
        out_shape=jax.ShapeDtypeStruct((n_rows, d), F32),
        scratch_shapes=[pltpu.VMEM((EXPERT_BLOCK, d), F32), pltpu.SemaphoreType.DMA((2,)), pltpu.SemaphoreType.DMA(())],
        compiler_params=_cparams("arbitrary"),
        name="dispatch",
    )(dest_tiles, pad_range, n_used, xm)


def _experts_kernel(first_ref, cnt_ref, nb_ref, xs_ref, wg_ref, wu_ref, wd_ref, ys_ref,
                    xbuf, ybuf, wg_s, wu_s, wd_s, in_sems, out_sems):
    e = pl.program_id(0)
    bm = EXPERT_BLOCK
    total = nb_ref[0]
    n_blocks = xs_ref.shape[0] // bm

    def x_copy(g, slot):
        return pltpu.make_async_copy(xs_ref.at[pl.ds(g * bm, bm)], xbuf.at[slot], in_sems.at[slot])

    def y_copy(g, slot):
        return pltpu.make_async_copy(ybuf.at[slot], ys_ref.at[pl.ds(g * bm, bm)], out_sems.at[slot])

    @pl.when((e == 0) & (total > 0))
    def _():
        x_copy(0, 0).start()

    @pl.when(cnt_ref[e] > 0)
    def _():
        wg_s[...] = wg_ref[0].astype(BF16)
        wu_s[...] = wu_ref[0].astype(BF16)
        wd_s[...] = wd_ref[0].astype(BF16)

    def block(c, carry):
        g = first_ref[e] + c
        slot = g % 2
        x_copy(g, slot).wait()

        @pl.when(g + 1 < total)
        def _():
            x_copy(g + 1, 1 - slot).start()

        x = xbuf[slot].astype(BF16)
        gate = jnp.dot(x, wg_s[...], preferred_element_type=F32)
        up = jnp.dot(x, wu_s[...], preferred_element_type=F32)
        y = jnp.dot((gate * jax.nn.sigmoid(gate) * up).astype(BF16), wd_s[...], preferred_element_type=F32)

        @pl.when(g >= 2)
        def _():
            y_copy(g - 2, slot).wait()

        ybuf[slot] = y
        y_copy(g, slot).start()
        return carry

    lax.fori_loop(0, cnt_ref[e], block, 0)

    @pl.when(e == pl.num_programs(0) - 1)
    def _():
        @pl.when(total >= 2)
        def _():
            y_copy(total - 2, total % 2).wait()

        @pl.when(total >= 1)
        def _():
            y_copy(total - 1, (total - 1) % 2).wait()

        ybuf[0] = jnp.zeros((bm, ybuf.shape[2]), F32)
        lax.fori_loop(total, n_blocks, lambda g, c: (y_copy(g, 0).start(), c)[1], 0)
        lax.fori_loop(total, n_blocks, lambda g, c: (y_copy(g, 0).wait(), c)[1], 0)


def _experts(xs, first_block, n_block, n_used, w_gate, w_up, w_down):
    n_rows, d = xs.shape
    bm = EXPERT_BLOCK
    n_exp, _, f = w_gate.shape
    grid_spec = pltpu.PrefetchScalarGridSpec(
        num_scalar_prefetch=3,
        grid=(n_exp,),
        in_specs=[pl.BlockSpec(memory_space=pl.ANY),
                  pl.BlockSpec((1, d, f), lambda e, *_: (e, 0, 0)),
                  pl.BlockSpec((1, d, f), lambda e, *_: (e, 0, 0)),
                  pl.BlockSpec((1, f, d), lambda e, *_: (e, 0, 0))],
        out_specs=pl.BlockSpec(memory_space=pl.ANY),
        scratch_shapes=[pltpu.VMEM((2, bm, d), F32), pltpu.VMEM((2, bm, d), F32),
                        pltpu.VMEM((d, f), BF16), pltpu.VMEM((d, f), BF16), pltpu.VMEM((f, d), BF16),
                        pltpu.SemaphoreType.DMA((2,)), pltpu.SemaphoreType.DMA((2,))],
    )
    return pl.pallas_call(
        _experts_kernel,
        grid_spec=grid_spec,
        out_shape=jax.ShapeDtypeStruct((n_rows, d), F32),
        compiler_params=_cparams("arbitrary"),
        name="experts",
    )(first_block, n_block, n_used, xs, w_gate, w_up, w_down)


def _combine_kernel(dest_ref, dnext_ref, ys_ref, wrow_ref, xm_ref, sg_ref, su_ref, sd_ref, out_ref, buf_ref, sems):
    i = pl.program_id(0)
    tm = ROUTE_TILE
    slot = i % 2

    def issue(d_ref, s):
        def start_row(r, c):
            for k in range(TOP_K):
                _row_copy(ys_ref, d_ref[0, k, r], buf_ref.at[s, k], r, sems.at[s]).start()
            return c

        lax.fori_loop(0, tm, start_row, 0)

    @pl.when(i == 0)
    def _():
        issue(dest_ref, 0)

    @pl.when(i < pl.num_programs(0) - 1)
    def _():
        issue(dnext_ref, 1 - slot)

    x = xm_ref[...].astype(BF16)
    g = jnp.dot(x, sg_ref[...], preferred_element_type=F32)
    u = jnp.dot(x, su_ref[...], preferred_element_type=F32)
    shared = jnp.dot((g * jax.nn.sigmoid(g) * u).astype(BF16), sd_ref[...], preferred_element_type=F32)
    for k in range(TOP_K):
        pltpu.make_async_copy(ys_ref.at[pl.ds(0, tm)], buf_ref.at[slot, k], sems.at[slot]).wait()
    w = wrow_ref[...]
    routed = w[:, 0:1] * buf_ref[slot, 0]
    for k in range(1, TOP_K):
        routed = routed + w[:, k:k + 1] * buf_ref[slot, k]
    out_ref[...] = routed + shared


def _combine(ys, dest_tiles, wrow, xm, ws_gate, ws_up, ws_down):
    n, d = xm.shape
    tm = ROUTE_TILE
    f = ws_gate.shape[1]
    return pl.pallas_call(
        _combine_kernel,
        grid=(n // tm,),
        in_specs=[pl.BlockSpec((1, TOP_K, tm), lambda i: (i, 0, 0), memory_space=pltpu.SMEM),
                  pl.BlockSpec((1, TOP_K, tm), lambda i: (jnp.minimum(i + 1, n // tm - 1), 0, 0), memory_space=pltpu.SMEM),
                  pl.BlockSpec(memory_space=pl.ANY),
                  pl.BlockSpec((tm, LANE), lambda i: (i, 0)),
                  pl.BlockSpec((tm, d), lambda i: (i, 0)),
                  pl.BlockSpec((d, f), lambda i: (0, 0)),
                  pl.BlockSpec((d, f), lambda i: (0, 0)),
                  pl.BlockSpec((f, d), lambda i: (0, 0))],
        out_specs=pl.BlockSpec((tm, d), lambda i: (i, 0)),
        out_shape=jax.ShapeDtypeStruct((n, d), F32),
        scratch_shapes=[pltpu.VMEM((2, TOP_K, tm, d), F32), pltpu.SemaphoreType.DMA((2,))],
        compiler_params=_cparams("arbitrary"),
        name="combine",
    )(dest_tiles, dest_tiles, ys, wrow, xm, ws_gate.astype(BF16), ws_up.astype(BF16), ws_down.astype(BF16))


def _final_ln_kernel(x1_ref, moe_ref, gate_ref, g_ref, b_ref, y_ref):
    y_ref[0] = _layer_norm(DN_ALPHA * x1_ref[0] + gate_ref[0] * moe_ref[0], g_ref[...], b_ref[...])


def _final_ln(x1, moe, gate_f, ln_g, ln_b, tm):
    b, t, d = x1.shape
    per_tok = gate_f.shape[1] != 1
    tok = pl.BlockSpec((1, tm, d), lambda i, j: (i, j, 0))
    mod_spec = tok if per_tok else pl.BlockSpec((1, 1, d), lambda i, j: (i, 0, 0))
    vec = pl.BlockSpec((1, d), lambda i, j: (0, 0))
    return pl.pallas_call(
        _final_ln_kernel,
        grid=(b, t // tm),
        in_specs=[tok, tok, mod_spec, vec, vec],
        out_specs=tok,
        out_shape=jax.ShapeDtypeStruct((b, t, d), F32),
        compiler_params=_cparams("parallel", "parallel"),
        name="final_ln",
    )(x1, moe, gate_f, ln_g.reshape(1, d), ln_b.reshape(1, d))


def _moe(xm, w_router, router_bias, w_e_gate, w_e_up, w_e_down, w_s_gate, w_s_up, w_s_down):
    n = xm.shape[0]
    eidx, rank, wrow, cnt = _router(xm, w_router, router_bias)
    counts = cnt[:, 0].astype(I32)
    padded = (counts + EXPERT_BLOCK - 1) // EXPERT_BLOCK * EXPERT_BLOCK
    pad_end = jnp.cumsum(padded)
    pad_start = pad_end - padded
    dest_tiles = _dest(eidx, rank, pad_start)
    n_blocks = -(-(n * TOP_K) // EXPERT_BLOCK) + N_EXPERTS
    n_used = (pad_end[-1:] // EXPERT_BLOCK).astype(I32)
    pad_range = jnp.stack([pad_start + counts, pad_end]).astype(I32)
    xs = _dispatch(xm, dest_tiles, pad_range, n_used, n_blocks * EXPERT_BLOCK)
    ys = _experts(xs, (pad_start // EXPERT_BLOCK).astype(I32), (padded // EXPERT_BLOCK).astype(I32), n_used,
                  w_e_gate, w_e_up, w_e_down)
    return _combine(ys, dest_tiles, wrow, xm, w_s_gate, w_s_up, w_s_down)


def kernel(x_prompt, x_sample, cache_kv_cmp, cache_kv_sel, state_kv_win, state_gla, page_table, c_prompt, c_sample, w_in, b_in, cmp_k_pos, cmp_k_w1, cmp_k_w2, cmp_v_pos, cmp_v_w1, cmp_v_w2, gla_w_a2, gla_b_a, gla_norm_g, w_br_a, w_br_b, w_out, ln1_g, ln1_b, w_ada, b_ada, w_router, router_bias, w_e_gate, w_e_up, w_e_down, w_s_gate, w_s_up, w_s_down, ln2_g, ln2_b):
    bp, tp, d = x_prompt.shape
    nd, td = x_sample.shape[:2]
    n_pool, page_rows = cache_kv_cmp.shape[:2]
    past_len = page_table.shape[1] * page_rows
    assert td == 1 and d == D_MODEL and page_rows == PAGE_ROWS and page_table.shape[1] == PAGES and tp == PAGES * PAGE_ROWS
    kv_w = 2 * NSA_KV_HEADS * HEAD_DIM

    mod = _adaln(jnp.concatenate([c_prompt, c_sample], axis=0), w_ada, b_ada)
    mod_p = [m.reshape(bp, 1, d) for m in jnp.split(mod[:bp], 6, axis=-1)]
    mod_s = [m.reshape(1, nd, d) for m in jnp.split(mod[bp:], 6, axis=-1)]

    w_pack, b_pack = _pack_in_weights(w_in, b_in)
    cmp_wk = _pack_cmp_weights(cmp_k_pos, cmp_k_w1, cmp_k_w2)
    cmp_wv = _pack_cmp_weights(cmp_v_pos, cmp_v_w1, cmp_v_w2)
    wa_pad = jnp.zeros((LANE, GLA_HEADS * GLA_DK), F32).at[MISC_AG:MISC_AG + GLA_GATE_RANK].set(gla_w_a2)
    tail_consts = (_gate_expand_table(), gla_norm_g.reshape(1, -1), _pad_br_a(w_br_a), w_br_b.astype(BF16), w_out.astype(BF16),
                   ln1_g.reshape(1, d), ln1_b.reshape(1, d))

    gm, qn, vg, rg, kvc, kvs, kvw, qg, kg, misc = _inproj(
        x_prompt, mod_p[0], mod_p[1], w_pack, b_pack, _rope_tables(jnp.arange(tp, dtype=I32)), 256)
    kc, vc = _compress(kvc.reshape(bp * PAGES, PAGE_ROWS, kv_w), jnp.arange(bp * PAGES, dtype=I32), cmp_wk, cmp_wv)
    ocmp, sel = _cmp_attn(qn, kc, vc, 256)
    osel = _sel_attn(qn, kvs, sel, 128, 256)
    owin = _win_attn(qn, kvw, 128)
    ogla, gla_p = _gla(qg, kg, vg, misc, wa_pad, gla_b_a, None, tp)
    x1_p, xm_p = _mixer_tail(ocmp, osel, owin, misc, ogla, rg, gm, x_prompt, mod_p[2], mod_p[4], mod_p[3], tail_consts, 256)
    n_win = min(WINDOW, tp)
    outs_p = (kvc.reshape(bp, tp, 2, NSA_KV_HEADS, HEAD_DIM), kvs.reshape(bp, tp, 2, NSA_KV_HEADS, HEAD_DIM),
              kvw[:, tp - n_win:].reshape(bp, n_win, 2, NSA_KV_HEADS, HEAD_DIM), gla_p)

    gm, qn, vg, rg, kvc, kvs, kvw, qg, kg, misc = _inproj(
        x_sample.reshape(1, nd, d), mod_s[0], mod_s[1], w_pack, b_pack, _rope_tables(jnp.full((nd,), past_len, I32)), nd)
    page_ids = page_table.reshape(-1).astype(I32)
    kc, vc = _compress(cache_kv_cmp.reshape(n_pool, PAGE_ROWS, kv_w), page_ids, cmp_wk, cmp_wv)
    ocmp, osel, owin, win_new = _nsa_decode(
        qn.reshape(nd, 1, -1), kc, vc, cache_kv_sel.reshape(n_pool, PAGE_ROWS, kv_w).transpose(0, 2, 1), page_ids,
        kvs.reshape(nd, 1, kv_w), kvw.reshape(nd, 1, kv_w), state_kv_win.reshape(nd, -1, kv_w).transpose(0, 2, 1))
    win_new = win_new.transpose(0, 2, 1)

    def pad_rows(a):
        return jnp.pad(a.reshape(nd, 1, -1), ((0, 0), (0, GLA_SUB - 1), (0, 0)))

    ogla, gla_s = _gla(pad_rows(qg), pad_rows(kg), pad_rows(vg), pad_rows(misc), wa_pad, gla_b_a, state_gla, 1)
    x1_s, xm_s = _mixer_tail(ocmp.reshape(1, nd, -1), osel.reshape(1, nd, -1), owin.reshape(1, nd, -1), misc,
                             ogla[:, 0].reshape(1, nd, -1), rg, gm, x_sample.reshape(1, nd, d),
                             mod_s[2], mod_s[4], mod_s[3], tail_consts, nd)
    outs_s = (kvc.reshape(nd, 1, 2, NSA_KV_HEADS, HEAD_DIM), kvs.reshape(nd, 1, 2, NSA_KV_HEADS, HEAD_DIM),
              win_new.reshape(state_kv_win.shape), gla_s)

    n_p = bp * tp
    moe = _moe(jnp.concatenate([xm_p.reshape(n_p, d), xm_s.reshape(nd, d)], axis=0),
               w_router, router_bias, w_e_gate, w_e_up, w_e_down, w_s_gate, w_s_up, w_s_down)
    y_p = _final_ln(x1_p, moe[:n_p].reshape(bp, tp, d), mod_p[5], ln2_g, ln2_b, 256)
    y_s = _final_ln(x1_s, moe[n_p:].reshape(1, nd, d), mod_s[5], ln2_g, ln2_b, nd).reshape(nd, 1, d)
    return (y_p, y_s) + outs_p + outs_s
```

```python
import functools

import numpy as np
import jax
import jax.numpy as jnp
from jax import lax
from jax.experimental import pallas as pl
from jax.experimental.pallas import tpu as pltpu

F32 = jnp.float32
BF16 = jnp.bfloat16
I32 = jnp.int32

D_MODEL = 1024
NSA_HEADS = 8
NSA_KV_HEADS = 2
NSA_GROUP = NSA_HEADS // NSA_KV_HEADS
HEAD_DIM = 64
ROT_DIM = HEAD_DIM // 4
ROPE_THETA = 500000.0
CMP_LEN = 32
CMP_STRIDE = 16
CMP_HIDDEN = 256
SEL_BLOCK = 64
SEL_TOP_N = 16
WINDOW = 512
FORCE_SCORE = 1.0e4
GLA_HEADS = 4
GLA_DK = 64
GLA_DV = 128
GLA_GATE_RANK = 16
GLA_TAU = 16.0
GLA_SUB = 16
N_EXPERTS = 256
TOP_K = 8
N_GROUPS = 8
TOPK_GROUPS = 4
EXPERT_DIM = 256
SHARED_DIM = 256
ROUTED_SCALE = 2.5
EXPERT_BLOCK = 128
EXPERT_RING = 4
DN_ALPHA = 2.0 ** 0.25
LN_EPS = 1e-5
LANE = 128
NEG = -1.0e30
VMEM_LIMIT = 56 * 1024 * 1024

SEG_GM = (0, 2 * D_MODEL)
SEG_QN = (SEG_GM[0] + SEG_GM[1], NSA_HEADS * LANE)
SEG_VG = (SEG_QN[0] + SEG_QN[1], GLA_HEADS * GLA_DV)
SEG_RG = (SEG_VG[0] + SEG_VG[1], GLA_HEADS * GLA_DV)
SEG_KVC = (SEG_RG[0] + SEG_RG[1], 2 * LANE)
SEG_KVS = (SEG_KVC[0] + SEG_KVC[1], 2 * LANE)
SEG_KVW = (SEG_KVS[0] + SEG_KVS[1], 2 * LANE)
SEG_QG = (SEG_KVW[0] + SEG_KVW[1], GLA_HEADS * GLA_DK)
SEG_KG = (SEG_QG[0] + SEG_QG[1], GLA_HEADS * GLA_DK)
SEG_MISC = (SEG_KG[0] + SEG_KG[1], LANE)
IN_PACKED = SEG_MISC[0] + SEG_MISC[1]
MISC_GN = 0
MISC_AG = NSA_HEADS * 3


def _cparams(*sem):
    return pltpu.CompilerParams(dimension_semantics=sem, vmem_limit_bytes=VMEM_LIMIT)


def _bdot(a, b):
    return jnp.dot(a.astype(BF16), b.astype(BF16), preferred_element_type=F32)


def _dot_nt(a, b, precision=None):
    return lax.dot_general(a, b, (((1,), (1,)), ((), ())), preferred_element_type=F32, precision=precision)


def _adaln_kernel(c_ref, w_ref, b_ref, o_ref):
    c = c_ref[...]
    o_ref[...] = _bdot(c * jax.nn.sigmoid(c), w_ref[...]) + b_ref[...]


def _adaln(c, w_ada, b_ada):
    n, d = c.shape
    m = w_ada.shape[1]
    tn = 512
    return pl.pallas_call(
        _adaln_kernel,
        grid=(m // tn,),
        in_specs=[pl.BlockSpec((n, d), lambda j: (0, 0)),
                  pl.BlockSpec((d, tn), lambda j: (0, j)),
                  pl.BlockSpec((1, tn), lambda j: (0, j))],
        out_specs=pl.BlockSpec((n, tn), lambda j: (0, j)),
        out_shape=jax.ShapeDtypeStruct((n, m), F32),
        compiler_params=_cparams("parallel"),
        name="adaln",
    )(c, w_ada, b_ada.reshape(1, m))


def _rope_tables(pos):
    half = ROT_DIM // 2
    inv = jnp.power(ROPE_THETA, -jnp.arange(half, dtype=F32) * 2.0 / ROT_DIM)
    ang = pos.astype(F32)[:, None] * inv[None, :]
    cos, sin = jnp.cos(ang), jnp.sin(ang)
    t = pos.shape[0]
    one = jnp.ones((t, HEAD_DIM - ROT_DIM), F32)
    z8 = jnp.zeros((t, half), F32)
    z48 = jnp.zeros((t, HEAD_DIM - ROT_DIM), F32)
    c = jnp.concatenate([cos, cos, one, cos, cos, one], axis=1)
    s1 = jnp.concatenate([-sin, z8, z48, -sin, z8, z48], axis=1)
    s2 = jnp.concatenate([z8, sin, z48, z8, sin, z48], axis=1)
    return c, s1, s2


def _pack_in_weights(w_in, b_in):
    sizes = (512, 128, 128, 128, 128, 128, 128, 24, 256, 256, 512, 512, 16, 2048)
    offs = np.concatenate([[0], np.cumsum(sizes)])

    def pack(w):
        seg = [w[..., offs[i]:offs[i + 1]] for i in range(len(sizes))]
        q_n, k_c, v_c, k_s, v_s, k_w, v_w, g_n, q_g, k_g, v_g, r_g, a_g, g_m = seg
        zero = jnp.zeros_like(q_n[..., :HEAD_DIM])
        q_slots = []
        for hh in range(NSA_HEADS):
            qh = q_n[..., hh * HEAD_DIM:(hh + 1) * HEAD_DIM] * (HEAD_DIM ** -0.5)
            q_slots += [qh, zero] if hh // NSA_GROUP == 0 else [zero, qh]
        misc_pad = jnp.zeros_like(w[..., :LANE - g_n.shape[-1] - a_g.shape[-1]])
        return jnp.concatenate([g_m] + q_slots + [v_g, r_g, k_c, v_c, k_s, v_s, k_w, v_w,
                                                   q_g * (GLA_DK ** -0.5), k_g, g_n, a_g, misc_pad], axis=-1)

    return pack(w_in).astype(BF16), pack(b_in.reshape(1, -1))


def _inproj_kernel(x_ref, sh_ref, sc_ref, w_ref, b_ref, rc_ref, rs1_ref, rs2_ref,
                   gm_ref, qn_ref, vg_ref, rg_ref, kvc_ref, kvs_ref, kvw_ref, qg_ref, kg_ref, misc_ref):
    h = (x_ref[0] * (1.0 + sc_ref[0]) + sh_ref[0]).astype(BF16)
    rc, rs1, rs2 = rc_ref[...], rs1_ref[...], rs2_ref[...]

    def proj(off, width):
        return jnp.dot(h, w_ref[:, off:off + width], preferred_element_type=F32) + b_ref[:, off:off + width]

    def rope(z):
        return z * rc + pltpu.roll(z, LANE - ROT_DIM // 2, 1) * rs1 + pltpu.roll(z, ROT_DIM // 2, 1) * rs2

    def plain(ref, seg):
        off, width = seg
        step = min(width, 512)
        for c in range(0, width, step):
            ref[0, :, c:c + step] = proj(off + c, step)

    plain(gm_ref, SEG_GM)
    for c in range(0, SEG_QN[1], 512):
        z = proj(SEG_QN[0] + c, 512)
        for s in range(0, 512, LANE):
            qn_ref[0, :, c + s:c + s + LANE] = rope(z[:, s:s + LANE])
    plain(vg_ref, SEG_VG)
    plain(rg_ref, SEG_RG)
    for ref, seg in ((kvc_ref, SEG_KVC), (kvs_ref, SEG_KVS), (kvw_ref, SEG_KVW)):
        z = proj(seg[0], seg[1])
        ref[0, :, 0:LANE] = rope(z[:, 0:LANE])
        ref[0, :, LANE:2 * LANE] = z[:, LANE:2 * LANE]
    plain(qg_ref, SEG_QG)
    plain(kg_ref, SEG_KG)
    plain(misc_ref, SEG_MISC)


def _inproj(x, shift, scale, w_pack, b_pack, tables, tm):
    b, t, d = x.shape
    per_tok = shift.shape[1] != 1
    mod_spec = (pl.BlockSpec((1, tm, d), lambda i, j: (i, j, 0)) if per_tok
                else pl.BlockSpec((1, 1, d), lambda i, j: (i, 0, 0)))
    segs = (SEG_GM, SEG_QN, SEG_VG, SEG_RG, SEG_KVC, SEG_KVS, SEG_KVW, SEG_QG, SEG_KG, SEG_MISC)
    tab_spec = pl.BlockSpec((tm, LANE), lambda i, j: (j, 0))
    return pl.pallas_call(
        _inproj_kernel,
        grid=(b, t // tm),
        in_specs=[pl.BlockSpec((1, tm, d), lambda i, j: (i, j, 0)), mod_spec, mod_spec,
                  pl.BlockSpec((d, IN_PACKED), lambda i, j: (0, 0)),
                  pl.BlockSpec((1, IN_PACKED), lambda i, j: (0, 0)),
                  tab_spec, tab_spec, tab_spec],
        out_specs=[pl.BlockSpec((1, tm, w), lambda i, j: (i, j, 0)) for _, w in segs],
        out_shape=[jax.ShapeDtypeStruct((b, t, w), F32) for _, w in segs],
        compiler_params=_cparams("parallel", "parallel"),
        name="inproj",
    )(x, shift, scale, w_pack, b_pack, *tables)


CHUNKS = 128
PAGE_ROWS = 128
PAGES = 16


def _pack_cmp_weights(pos, w1, w2):
    pos2 = jnp.concatenate([pos, pos], axis=1)
    z1 = jnp.zeros_like(w1)
    bd1 = jnp.concatenate([jnp.concatenate([w1, z1], axis=2), jnp.concatenate([z1, w1], axis=2)], axis=1)
    w1p = jnp.concatenate([bd1[:CMP_STRIDE], bd1[CMP_STRIDE:]], axis=2).astype(BF16)
    z2 = jnp.zeros_like(w2)
    w2p = jnp.concatenate([jnp.concatenate([w2, z2], axis=1), jnp.concatenate([z2, w2], axis=1)], axis=0).astype(BF16)
    return pos2, w1p, w2p


def _fill_chunks(page_refs, xs_ref):
    for p, pr in enumerate(page_refs):
        for l in range(CMP_STRIDE):
            xs_ref[l, 8 * p:8 * p + 8, :] = pr[0, pl.ds(l, PAGE_ROWS // CMP_STRIDE, stride=CMP_STRIDE), :]


def _compress_chunks(xs_ref, pos_ref, w1_ref, w2_ref):
    hid2 = 2 * CMP_HIDDEN
    acc_a = jnp.zeros((CHUNKS, hid2), F32)
    acc_b = jnp.zeros((CHUNKS, hid2), F32)
    for l in range(CMP_STRIDE):
        x = xs_ref[l]
        acc_a = acc_a + jnp.dot((x + pos_ref[l:l + 1, :]).astype(BF16), w1_ref[l, :, 0:hid2], preferred_element_type=F32)
        acc_b = acc_b + jnp.dot((x + pos_ref[CMP_STRIDE + l:CMP_STRIDE + l + 1, :]).astype(BF16), w1_ref[l, :, hid2:2 * hid2],
                                preferred_element_type=F32)
    hid = acc_a + pltpu.roll(acc_b, CHUNKS - 1, 0)
    out = jnp.dot(jax.nn.gelu(hid).astype(BF16), w2_ref[...], preferred_element_type=F32)
    row = lax.broadcasted_iota(I32, out.shape, 0)
    return jnp.where(row < CHUNKS - 1, out, 0.0)


def _compress_kernel(pt_ref, *refs):
    k_pages, v_pages = refs[:PAGES], refs[PAGES:2 * PAGES]
    posk_ref, w1k_ref, w2k_ref, posv_ref, w1v_ref, w2v_ref, kc_ref, vc_ref, xk_ref, xv_ref = refs[2 * PAGES:]
    _fill_chunks(k_pages, xk_ref)
    _fill_chunks(v_pages, xv_ref)
    kc_ref[0] = _compress_chunks(xk_ref, posk_ref, w1k_ref, w2k_ref)
    vc_ref[0] = _compress_chunks(xv_ref, posv_ref, w1v_ref, w2v_ref)


def _page_spec(p, half):
    return pl.BlockSpec((1, PAGE_ROWS, LANE), lambda i, pt: (pt[i * PAGES + p], 0, half))


def _const_spec(shape):
    nd = len(shape)
    return pl.BlockSpec(shape, lambda i, pt: (0,) * nd)


def _compress(pages, page_ids, cmp_wk, cmp_wv):
    n_b = page_ids.shape[0] // PAGES
    consts = list(cmp_wk) + list(cmp_wv)
    grid_spec = pltpu.PrefetchScalarGridSpec(
        num_scalar_prefetch=1,
        grid=(n_b,),
        in_specs=[_page_spec(p, h) for h in range(2) for p in range(PAGES)] + [_const_spec(c.shape) for c in consts],
        out_specs=[pl.BlockSpec((1, CHUNKS, LANE), lambda i, pt: (i, 0, 0))] * 2,
        scratch_shapes=[pltpu.VMEM((CMP_STRIDE, CHUNKS, LANE), F32)] * 2,
    )
    return pl.pallas_call(
        _compress_kernel,
        grid_spec=grid_spec,
        out_shape=[jax.ShapeDtypeStruct((n_b, CHUNKS, LANE), F32)] * 2,
        compiler_params=_cparams("parallel"),
        name="compress",
    )(page_ids, *([pages] * (2 * PAGES)), *consts)


def _cover_tables(n_sel):
    c_start = np.arange(CHUNKS) * CMP_STRIDE
    s_start = np.arange(n_sel) * SEL_BLOCK
    cover = ((c_start[:, None] < s_start[None, :] + SEL_BLOCK) & (c_start[:, None] + CMP_LEN > s_start[None, :])).astype(np.float32)
    cover[CHUNKS - 1] = 0.0
    out = np.zeros((NSA_KV_HEADS, LANE, CHUNKS), np.float32)
    for h in range(NSA_KV_HEADS):
        out[h, h * 64:h * 64 + n_sel] = cover.T
    return jnp.asarray(out)


def _softmax_rows(s, valid):
    s = jnp.where(valid, s, NEG)
    m = jnp.max(s, axis=-1, keepdims=True)
    m = jnp.where(m > 0.5 * NEG, m, 0.0)
    p = jnp.where(valid, jnp.exp(s - m), 0.0)
    return p / jnp.maximum(jnp.sum(p, axis=-1, keepdims=True), 1e-30)


def _select_blocks(imp, n_sel, top_n):
    ridx = lax.broadcasted_iota(I32, imp.shape, 0)
    cnt = jnp.zeros(imp.shape, F32)
    for i in range(n_sel):
        vi = imp[i:i + 1, :]
        ahead = (vi > imp) | ((vi == imp) & (ridx > i))
        cnt = cnt + jnp.where(ahead, 1.0, 0.0)
    return jnp.where((cnt < top_n) & (ridx < n_sel), 1.0, 0.0)


def _cmp_attn_kernel(qn_ref, kc_ref, vc_ref, cov_ref, o_ref, sel_ref, *, tq, n_sel):
    qi = pl.program_id(1)
    kc = kc_ref[0].astype(BF16)
    vc = vc_ref[0].astype(BF16)
    qpos = qi * tq + lax.broadcasted_iota(I32, (tq, CHUNKS), 0)
    cidx = lax.broadcasted_iota(I32, (tq, CHUNKS), 1)
    valid = (cidx * CMP_STRIDE + CMP_LEN - 1 <= qpos) & (cidx < CHUNKS - 1)
    psum = [jnp.zeros((tq, CHUNKS), F32) for _ in range(NSA_KV_HEADS)]
    for hh in range(NSA_HEADS):
        q = qn_ref[0, :, hh * LANE:(hh + 1) * LANE].astype(BF16)
        p = _softmax_rows(_dot_nt(q, kc), valid)
        o_ref[0, :, hh * LANE:(hh + 1) * LANE] = jnp.dot(p.astype(BF16), vc, preferred_element_type=F32)
        psum[hh // NSA_GROUP] = psum[hh // NSA_GROUP] + p
    imp = (_dot_nt(cov_ref[0], psum[0], lax.Precision.HIGHEST) + _dot_nt(cov_ref[1], psum[1], lax.Precision.HIGHEST))
    blk = lax.broadcasted_iota(I32, (LANE, tq), 0) & 63
    qpos_t = qi * tq + lax.broadcasted_iota(I32, (LANE, tq), 1)
    cur = qpos_t // SEL_BLOCK
    forced = (blk == 0) | (blk == cur) | (blk == cur - 1)
    imp = jnp.where(forced, FORCE_SCORE, jnp.where(blk * SEL_BLOCK <= qpos_t, imp, -FORCE_SCORE))
    sel_t = jnp.concatenate([_select_blocks(imp[0:64], n_sel, SEL_TOP_N), _select_blocks(imp[64:128], n_sel, SEL_TOP_N)], axis=0)
    sel_ref[0] = sel_t.T


def _cmp_attn(qn, kc, vc, tq):
    b, t, _ = qn.shape
    n_sel = -(-t // SEL_BLOCK)
    cov = _cover_tables(n_sel)
    return pl.pallas_call(
        functools.partial(_cmp_attn_kernel, tq=tq, n_sel=n_sel),
        grid=(b, t // tq),
        in_specs=[pl.BlockSpec((1, tq, NSA_HEADS * LANE), lambda i, j: (i, j, 0)),
                  pl.BlockSpec((1, CHUNKS, LANE), lambda i, j: (i, 0, 0)),
                  pl.BlockSpec((1, CHUNKS, LANE), lambda i, j: (i, 0, 0)),
                  pl.BlockSpec((NSA_KV_HEADS, LANE, CHUNKS), lambda i, j: (0, 0, 0))],
        out_specs=[pl.BlockSpec((1, tq, NSA_HEADS * LANE), lambda i, j: (i, j, 0)),
                   pl.BlockSpec((1, tq, LANE), lambda i, j: (i, j, 0))],
        out_shape=[jax.ShapeDtypeStruct((b, t, NSA_HEADS * LANE), F32), jax.ShapeDtypeStruct((b, t, LANE), F32)],
        compiler_params=_cparams("parallel", "parallel"),
        name="cmp_attn",
    )(qn, kc, vc, cov)


def _key_block_table(t):
    blk = np.arange(t)[:, None] // SEL_BLOCK
    lanes = np.arange(LANE)[None, :] & 63
    return jnp.asarray((blk == lanes).astype(np.float32), dtype=BF16)


def _lane_fold(x, op):
    out = x[:, 0:LANE]
    for c in range(LANE, x.shape[1], LANE):
        out = op(out, x[:, c:c + LANE])
    return out


def _sel_attn_kernel(qn_ref, kv_ref, sel_ref, kb_ref, o_ref, q2_scr, k2_scr, v_scr, s_scr, *, tq, tk):
    qi = pl.program_id(1)
    rows = NSA_HEADS * tq
    t = kv_ref.shape[1]

    @pl.when(qi == 0)
    def _():
        k2_scr[:, 0:LANE] = kv_ref[0, :, 0:LANE].astype(BF16)
        k2_scr[:, LANE:2 * LANE] = kb_ref[...]
        v_scr[...] = kv_ref[0, :, LANE:2 * LANE].astype(BF16)

    not_sel = (1.0 - sel_ref[0]) * NEG
    lane_head = lax.broadcasted_iota(I32, (tq, LANE), 1) // 64
    for hh in range(NSA_HEADS):
        q2_scr[hh * tq:(hh + 1) * tq, 0:LANE] = qn_ref[0, :, hh * LANE:(hh + 1) * LANE].astype(BF16)
        q2_scr[hh * tq:(hh + 1) * tq, LANE:2 * LANE] = jnp.where(lane_head == hh // NSA_GROUP, not_sel, 0.0).astype(BF16)
    q2 = q2_scr[...]
    last = (qi * tq + tq - 1) // tk

    def scores(kt):
        k0 = pl.multiple_of(kt * tk, tk)
        return _dot_nt(q2, k2_scr[pl.ds(k0, tk), :])

    def pass1(kt, m_acc):
        s = scores(kt)
        s_scr[kt] = s
        return jnp.maximum(m_acc, _lane_fold(s, jnp.maximum))

    m_acc = lax.fori_loop(0, last, pass1, jnp.full((rows, LANE), NEG, F32))
    rel = (qi * tq + lax.broadcasted_iota(I32, (tq, tk), 0)) - (last * tk + lax.broadcasted_iota(I32, (tq, tk), 1))
    causal = jnp.where(rel >= 0, 0.0, NEG)
    s_last = scores(last) + jnp.concatenate([causal] * NSA_HEADS, axis=0)
    m = jnp.max(jnp.maximum(m_acc, _lane_fold(s_last, jnp.maximum)), axis=-1, keepdims=True)

    def accumulate(s, v, l_acc, acc):
        p = jnp.exp(s - m)
        return l_acc + _lane_fold(p, jnp.add), acc + jnp.dot(p.astype(BF16), v, preferred_element_type=F32)

    def pass2(kt, carry):
        k0 = pl.multiple_of(kt * tk, tk)
        return accumulate(s_scr[kt], v_scr[pl.ds(k0, tk), :], *carry)

    l_acc, acc = lax.fori_loop(0, last, pass2, (jnp.zeros((rows, LANE), F32), jnp.zeros((rows, LANE), F32)))
    l_acc, acc = accumulate(s_last, v_scr[pl.ds(pl.multiple_of(last * tk, tk), tk), :], l_acc, acc)
    out = acc / jnp.sum(l_acc, axis=-1, keepdims=True)
    for hh in range(NSA_HEADS):
        o_ref[0, :, hh * LANE:(hh + 1) * LANE] = out[hh * tq:(hh + 1) * tq]


def _sel_attn(qn, kv, sel, tq, tk):
    b, t, _ = qn.shape
    rows = NSA_HEADS * tq
    return pl.pallas_call(
        functools.partial(_sel_attn_kernel, tq=tq, tk=tk),
        grid=(b, t // tq),
        in_specs=[pl.BlockSpec((1, tq, NSA_HEADS * LANE), lambda i, j: (i, j, 0)),
                  pl.BlockSpec((1, t, 2 * LANE), lambda i, j: (i, 0, 0)),
                  pl.BlockSpec((1, tq, LANE), lambda i, j: (i, j, 0)),
                  pl.BlockSpec((t, LANE), lambda i, j: (0, 0))],
        out_specs=pl.BlockSpec((1, tq, NSA_HEADS * LANE), lambda i, j: (i, j, 0)),
        out_shape=jax.ShapeDtypeStruct((b, t, NSA_HEADS * LANE), F32),
        scratch_shapes=[pltpu.VMEM((rows, 2 * LANE), BF16), pltpu.VMEM((t, 2 * LANE), BF16), pltpu.VMEM((t, LANE), BF16),
                        pltpu.VMEM((t // tk, rows, tk), F32)],
        compiler_params=_cparams("parallel", "arbitrary"),
        name="sel_attn",
    )(qn, kv, sel, _key_block_table(t))


def _win_attn_kernel(qn_ref, kv_ref, o_ref, q_scr, k_scr, v_scr, *, tq):
    qi = pl.program_id(1)
    span = WINDOW + tq

    @pl.when(qi == 0)
    def _():
        k_scr[...] = kv_ref[0, :, 0:LANE].astype(BF16)
        v_scr[...] = kv_ref[0, :, LANE:2 * LANE].astype(BF16)

    for hh in range(NSA_HEADS):
        q_scr[hh * tq:(hh + 1) * tq, :] = qn_ref[0, :, hh * LANE:(hh + 1) * LANE].astype(BF16)
    k0 = pl.multiple_of(jnp.maximum(qi * tq - WINDOW, 0), tq)
    rel = (qi * tq + lax.broadcasted_iota(I32, (tq, span), 0)) - (k0 + lax.broadcasted_iota(I32, (tq, span), 1))
    bias = jnp.where((rel >= 0) & (rel <= WINDOW), 0.0, NEG)
    s = _dot_nt(q_scr[...], k_scr[pl.ds(k0, span), :]) + jnp.concatenate([bias] * NSA_HEADS, axis=0)
    p = jnp.exp(s - jnp.max(s, axis=-1, keepdims=True))
    out = jnp.dot(p.astype(BF16), v_scr[pl.ds(k0, span), :], preferred_element_type=F32) / jnp.sum(p, axis=-1, keepdims=True)
    for hh in range(NSA_HEADS):
        o_ref[0, :, hh * LANE:(hh + 1) * LANE] = out[hh * tq:(hh + 1) * tq]


def _win_attn(qn, kv, tq):
    b, t, _ = qn.shape
    assert t >= WINDOW + tq
    rows = NSA_HEADS * tq
    return pl.pallas_call(
        functools.partial(_win_attn_kernel, tq=tq),
        grid=(b, t // tq),
        in_specs=[pl.BlockSpec((1, tq, NSA_HEADS * LANE), lambda i, j: (i, j, 0)),
                  pl.BlockSpec((1, t, 2 * LANE), lambda i, j: (i, 0, 0))],
        out_specs=pl.BlockSpec((1, tq, NSA_HEADS * LANE), lambda i, j: (i, j, 0)),
        out_shape=jax.ShapeDtypeStruct((b, t, NSA_HEADS * LANE), F32),
        scratch_shapes=[pltpu.VMEM((rows, LANE), BF16), pltpu.VMEM((t, LANE), BF16), pltpu.VMEM((t, LANE), BF16)],
        compiler_params=_cparams("parallel", "arbitrary"),
        name="win_attn",
    )(qn, kv)


def _dec_softmax(scores, vals):
    m = scores[0].max(axis=-1, keepdims=True)
    for s in scores[1:]:
        m = jnp.maximum(m, s.max(axis=-1, keepdims=True))
    den = jnp.zeros_like(m)
    out = jnp.zeros((m.shape[0], LANE), F32)
    for s, v in zip(scores, vals):
        p = jnp.exp(s - m)
        den = den + p.sum(axis=-1, keepdims=True)
        out = out + (p * v if s.shape[1] == 1 else _dot_nt(p.astype(BF16), v))
    return out / den


def _nsa_decode_kernel(pt_ref, *refs, past_len, n_sel):
    pages = refs[:PAGES]
    (qn_ref, kc_ref, vc_ref, kvs_ref, kvw_ref, win_ref, cov_ref,
     ocmp_ref, osel_ref, owin_ref, wnew_ref) = refs[PAGES:]
    nh = NSA_HEADS
    q = jnp.concatenate([qn_ref[0, :, hh * LANE:(hh + 1) * LANE] for hh in range(nh)], axis=0)
    qb = q.astype(BF16)
    cidx = lax.broadcasted_iota(I32, (nh, CHUNKS), 1)
    valid = (cidx * CMP_STRIDE + CMP_LEN - 1 <= past_len) & (cidx < CHUNKS - 1)
    p = _softmax_rows(_dot_nt(qb, kc_ref[0].astype(BF16)), valid)
    o_cmp = jnp.dot(p.astype(BF16), vc_ref[0].astype(BF16), preferred_element_type=F32)
    imp = None
    for h in range(NSA_KV_HEADS):
        ps = jnp.sum(p[h * NSA_GROUP:(h + 1) * NSA_GROUP], axis=0, keepdims=True)
        term = _dot_nt(cov_ref[h], jnp.broadcast_to(ps, (LANE, CHUNKS)), lax.Precision.HIGHEST)
        imp = term if imp is None else imp + term
    blk = lax.broadcasted_iota(I32, (LANE, LANE), 0) & 63
    cur = past_len // SEL_BLOCK
    forced = (blk == 0) | (blk == cur) | (blk == cur - 1)
    imp = jnp.where(forced, FORCE_SCORE, jnp.where(blk * SEL_BLOCK <= past_len, imp, -FORCE_SCORE))
    sel_t = jnp.concatenate([_select_blocks(imp[0:64], n_sel, min(SEL_TOP_N, n_sel)),
                             _select_blocks(imp[64:128], n_sel, min(SEL_TOP_N, n_sel))], axis=0)
    sel = sel_t.T[0:1, :]
    head_of_row = lax.broadcasted_iota(I32, (nh, 1), 0) // NSA_GROUP

    def picked(s):
        return jnp.where(head_of_row == 0, sel[:, s:s + 1], sel[:, 64 + s:64 + s + 1])

    first_half = lax.broadcasted_iota(I32, (nh, PAGE_ROWS), 1) < SEL_BLOCK
    scores, vals = [], []
    for pg, pr in enumerate(pages):
        s = jnp.dot(qb, pr[0, 0:LANE, :].astype(BF16), preferred_element_type=F32)
        ok = jnp.where(first_half, picked(2 * pg), picked(2 * pg + 1)) > 0.5
        scores.append(jnp.where(ok, s, NEG))
        vals.append(pr[0, LANE:2 * LANE, :].astype(BF16))
    s_new = jnp.sum(q * kvs_ref[0, :, 0:LANE], axis=-1, keepdims=True)
    scores.append(jnp.where(picked(past_len // SEL_BLOCK) > 0.5, s_new, NEG))
    vals.append(kvs_ref[0, :, LANE:2 * LANE])
    o_sel = _dec_softmax(scores, vals)
    n_win = win_ref.shape[2]
    kpos = past_len - n_win + lax.broadcasted_iota(I32, (nh, n_win), 1)
    rel = past_len - kpos
    s_win = jnp.dot(qb, win_ref[0, 0:LANE, :].astype(BF16), preferred_element_type=F32)
    s_win = jnp.where((kpos >= 0) & (rel >= 0) & (rel <= WINDOW), s_win, NEG)
    s_new = jnp.sum(q * kvw_ref[0, :, 0:LANE], axis=-1, keepdims=True)
    o_win = _dec_softmax([s_win, s_new], [win_ref[0, LANE:2 * LANE, :].astype(BF16), kvw_ref[0, :, LANE:2 * LANE]])
    for hh in range(nh):
        ocmp_ref[0, :, hh * LANE:(hh + 1) * LANE] = o_cmp[hh:hh + 1]
        osel_ref[0, :, hh * LANE:(hh + 1) * LANE] = o_sel[hh:hh + 1]
        owin_ref[0, :, hh * LANE:(hh + 1) * LANE] = o_win[hh:hh + 1]
    for c in range(0, 2 * LANE, LANE):
        new_col = jnp.broadcast_to(kvw_ref[0, :, c:c + LANE], (LANE, LANE)).T
        shifted = pltpu.roll(win_ref[0, c:c + LANE, :], n_win - 1, 1)
        lane = lax.broadcasted_iota(I32, shifted.shape, 1)
        wnew_ref[0, c:c + LANE, :] = jnp.where(lane == n_win - 1, jnp.concatenate([new_col] * (n_win // LANE), axis=1), shifted)


def _nsa_decode(qn, kc, vc, sel_pages, page_ids, kvs_new, kvw_new, win_state):
    b = qn.shape[0]
    n_win = win_state.shape[2]
    past_len = PAGES * PAGE_ROWS
    n_sel = -(-(past_len + 1) // SEL_BLOCK)
    cov = _cover_tables(n_sel)

    def per_b(shape):
        nd = len(shape)
        return pl.BlockSpec((1,) + shape[1:], lambda i, pt: (i,) + (0,) * (nd - 1))

    slots = NSA_HEADS * LANE
    grid_spec = pltpu.PrefetchScalarGridSpec(
        num_scalar_prefetch=1,
        grid=(b,),
        in_specs=[pl.BlockSpec((1, 2 * LANE, PAGE_ROWS), (lambda i, pt, p=p: (pt[i * PAGES + p], 0, 0))) for p in range(PAGES)]
        + [per_b(qn.shape), per_b(kc.shape), per_b(vc.shape), per_b(kvs_new.shape), per_b(kvw_new.shape), per_b(win_state.shape),
           _const_spec(cov.shape)],
        out_specs=[per_b((b, 1, slots))] * 3 + [per_b(win_state.shape)],
    )
    return pl.pallas_call(
        functools.partial(_nsa_decode_kernel, past_len=past_len, n_sel=n_sel),
        grid_spec=grid_spec,
        out_shape=[jax.ShapeDtypeStruct((b, 1, slots), F32)] * 3 + [jax.ShapeDtypeStruct(win_state.shape, F32)],
        compiler_params=_cparams("parallel"),
        name="nsa_decode",
    )(page_ids, *([sel_pages] * PAGES), qn, kc, vc, kvs_new, kvw_new, win_state, cov)


def _dot_tn(a, b):
    return lax.dot_general(a, b, (((0,), (0,)), ((), ())), preferred_element_type=F32)


def _cumsum_table():
    r = np.arange(LANE)
    return jnp.asarray(((r[:, None] // GLA_SUB == r[None, :] // GLA_SUB) & (r[None, :] <= r[:, None])).astype(np.float32))


def _gla_kernel(*refs, t, t_valid, has_state):
    if has_state:
        qg_ref, kg_ref, vg_ref, misc_ref, wa_ref, ba_ref, lt_ref, s0_ref, o_ref, s_ref, b_scr, st_scr = refs
    else:
        qg_ref, kg_ref, vg_ref, misc_ref, wa_ref, ba_ref, lt_ref, o_ref, s_ref, b_scr, st_scr = refs
    z = jnp.dot(misc_ref[0], wa_ref[...], preferred_element_type=F32, precision=lax.Precision.HIGHEST) + ba_ref[...]
    la = (jnp.minimum(z, 0.0) - jnp.log1p(jnp.exp(-jnp.abs(z)))) * (1.0 / GLA_TAU)
    if t_valid < t:
        la = jnp.where(lax.broadcasted_iota(I32, la.shape, 0) < t_valid, la, 0.0)
    tile = min(t, LANE)
    for r in range(0, t, tile):
        b_scr[r:r + tile, :] = jnp.dot(lt_ref[0:tile, 0:tile], la[r:r + tile, :], preferred_element_type=F32,
                                       precision=lax.Precision.HIGHEST)
    pairs = GLA_HEADS // 2
    for p in range(pairs):
        if has_state:
            st_scr[p] = s0_ref[0, 2 * p:2 * p + 2].reshape(2 * GLA_DK, GLA_DV).T
        else:
            st_scr[p] = jnp.zeros((GLA_DV, LANE), F32)
    head_a = lax.broadcasted_iota(I32, (GLA_SUB, LANE), 1) < GLA_DK
    row = lax.broadcasted_iota(I32, (GLA_SUB, LANE), 0)

    def pair_chunk(q, k, v, b, st):
        b_last = b[GLA_SUB - 1:GLA_SUB, :]
        st_b = st.astype(BF16)
        qe = q * jnp.exp(b)
        o_a = _dot_nt(jnp.where(head_a, qe, 0.0).astype(BF16), st_b)
        o_b = _dot_nt(jnp.where(head_a, 0.0, qe).astype(BF16), st_b)
        for j in range(GLA_SUB):
            w = q * k[j:j + 1, :] * jnp.exp(jnp.minimum(b - b[j:j + 1, :], 0.0))
            w = jnp.where(row >= j, w, 0.0)
            a_a = jnp.sum(jnp.where(head_a, w, 0.0), axis=-1, keepdims=True)
            a_b = jnp.sum(jnp.where(head_a, 0.0, w), axis=-1, keepdims=True)
            o_a = o_a + a_a * v[j:j + 1, 0:GLA_DV]
            o_b = o_b + a_b * v[j:j + 1, GLA_DV:2 * GLA_DV]
        kd = k * jnp.exp(b_last - b)
        upd = (_dot_tn(v[:, 0:GLA_DV].astype(BF16), jnp.where(head_a, kd, 0.0).astype(BF16))
               + _dot_tn(v[:, GLA_DV:2 * GLA_DV].astype(BF16), jnp.where(head_a, 0.0, kd).astype(BF16)))
        return o_a, o_b, jnp.exp(b_last) * st + upd

    def chunk(c, carry):
        r0 = pl.multiple_of(c * GLA_SUB, GLA_SUB)
        for p in range(pairs):
            o_a, o_b, st_new = pair_chunk(qg_ref[0, pl.ds(r0, GLA_SUB), p * LANE:(p + 1) * LANE],
                                          kg_ref[0, pl.ds(r0, GLA_SUB), p * LANE:(p + 1) * LANE],
                                          vg_ref[0, pl.ds(r0, GLA_SUB), 2 * p * GLA_DV:2 * (p + 1) * GLA_DV],
                                          b_scr[pl.ds(r0, GLA_SUB), p * LANE:(p + 1) * LANE], st_scr[p])
            o_ref[0, pl.ds(r0, GLA_SUB), 2 * p * GLA_DV:(2 * p + 1) * GLA_DV] = o_a
            o_ref[0, pl.ds(r0, GLA_SUB), (2 * p + 1) * GLA_DV:(2 * p + 2) * GLA_DV] = o_b
            st_scr[p] = st_new
        return carry

    lax.fori_loop(0, t // GLA_SUB, chunk, 0)
    for p in range(pairs):
        s_ref[0, 2 * p:2 * p + 2] = st_scr[p].T.reshape(2, GLA_DK, GLA_DV)


def _gla(qg, kg, vg, misc, wa_pad, ba, s0, t_valid):
    b, t, _ = qg.shape
    has_state = s0 is not None
    hk = GLA_HEADS * GLA_DK

    def per_b(shape):
        nd = len(shape)
        return pl.BlockSpec((1,) + shape[1:], lambda i: (i,) + (0,) * (nd - 1))

    def const(shape):
        nd = len(shape)
        return pl.BlockSpec(shape, lambda i: (0,) * nd)

    in_specs = [per_b(qg.shape), per_b(kg.shape), per_b(vg.shape), per_b(misc.shape),
                const((LANE, hk)), const((1, hk)), const((LANE, LANE))]
    args = [qg, kg, vg, misc, wa_pad, ba.reshape(1, -1), _cumsum_table()]
    state_shape = (b, GLA_HEADS, GLA_DK, GLA_DV)
    if has_state:
        in_specs.append(per_b(state_shape))
        args.append(s0)
    return pl.pallas_call(
        functools.partial(_gla_kernel, t=t, t_valid=t_valid, has_state=has_state),
        grid=(b,),
        in_specs=in_specs,
        out_specs=[per_b(vg.shape), per_b(state_shape)],
        out_shape=[jax.ShapeDtypeStruct(vg.shape, F32), jax.ShapeDtypeStruct(state_shape, F32)],
        scratch_shapes=[pltpu.VMEM((t, hk), F32), pltpu.VMEM((GLA_HEADS // 2, GLA_DV, LANE), F32)],
        compiler_params=_cparams("parallel"),
        name="gla",
    )(*args)


def _gate_expand_table():
    out = np.zeros((3, LANE, NSA_HEADS * LANE), np.float32)
    for hh in range(NSA_HEADS):
        for j in range(3):
            out[j, MISC_GN + 3 * hh + j, hh * LANE:(hh + 1) * LANE] = 1.0
    return jnp.asarray(out, dtype=BF16)


def _pad_br_a(w_br_a):
    zero = jnp.zeros((HEAD_DIM, w_br_a.shape[1]), w_br_a.dtype)
    parts = []
    for hh in range(NSA_HEADS):
        wh = w_br_a[hh * HEAD_DIM:(hh + 1) * HEAD_DIM]
        parts += [wh, zero] if hh // NSA_GROUP == 0 else [zero, wh]
    return jnp.concatenate(parts, axis=0).astype(BF16)


def _layer_norm(v, g, b):
    mu = jnp.mean(v, axis=-1, keepdims=True)
    var = jnp.mean(jnp.square(v - mu), axis=-1, keepdims=True)
    return (v - mu) * lax.rsqrt(var + LN_EPS) * g + b


def _mixer_tail_kernel(ocmp_ref, osel_ref, owin_ref, misc_ref, ogla_ref, rg_ref, gm_ref, x_ref, gate_ref, scf_ref, shf_ref,
                       ex_ref, ng_ref, wa_ref, wb_ref, wo_ref, lg_ref, lb_ref, x1_ref, xm_ref):
    sig = jax.nn.sigmoid(misc_ref[0])
    sig_hi = sig.astype(BF16)
    sig_lo = (sig - sig_hi.astype(F32)).astype(BF16)
    o_nsa = None
    for j, ref in enumerate((ocmp_ref, osel_ref, owin_ref)):
        g = (jnp.dot(sig_hi, ex_ref[j], preferred_element_type=F32) + jnp.dot(sig_lo, ex_ref[j], preferred_element_type=F32))
        o_nsa = g * ref[0] if o_nsa is None else o_nsa + g * ref[0]
    br_a = _bdot(o_nsa, wa_ref[...])
    heads = []
    for h in range(GLA_HEADS):
        seg = ogla_ref[0, :, h * GLA_DV:(h + 1) * GLA_DV]
        mu = jnp.mean(seg, axis=-1, keepdims=True)
        var = jnp.mean(jnp.square(seg - mu), axis=-1, keepdims=True)
        r = rg_ref[0, :, h * GLA_DV:(h + 1) * GLA_DV]
        heads.append((seg - mu) * lax.rsqrt(var + LN_EPS) * ng_ref[:, h * GLA_DV:(h + 1) * GLA_DV] * (r * jax.nn.sigmoid(r)))
    br_b = _bdot(jnp.concatenate(heads, axis=1), wb_ref[...])
    gm_a = jax.nn.sigmoid(gm_ref[0, :, 0:D_MODEL])
    gm_b = jax.nn.sigmoid(gm_ref[0, :, D_MODEL:2 * D_MODEL])
    y = _bdot(gm_a * br_a + gm_b * br_b, wo_ref[...])
    x1 = _layer_norm(DN_ALPHA * x_ref[0] + gate_ref[0] * y, lg_ref[...], lb_ref[...])
    x1_ref[0] = x1
    xm_ref[0] = x1 * (1.0 + scf_ref[0]) + shf_ref[0]


def _mixer_tail(ocmp, osel, owin, misc, ogla, rg, gm, x, gate_m, scale_f, shift_f, consts, tm):
    b, t, d = x.shape
    per_tok = gate_m.shape[1] != 1

    def tok(w):
        return pl.BlockSpec((1, tm, w), lambda i, j: (i, j, 0))

    mod_spec = tok(d) if per_tok else pl.BlockSpec((1, 1, d), lambda i, j: (i, 0, 0))

    def const(a):
        nd = a.ndim
        return pl.BlockSpec(a.shape, lambda i, j: (0,) * nd)

    return pl.pallas_call(
        _mixer_tail_kernel,
        grid=(b, t // tm),
        in_specs=[tok(NSA_HEADS * LANE)] * 3 + [tok(LANE), tok(GLA_HEADS * GLA_DV), tok(GLA_HEADS * GLA_DV), tok(2 * d), tok(d),
                                               mod_spec, mod_spec, mod_spec] + [const(c) for c in consts],
        out_specs=[tok(d), tok(d)],
        out_shape=[jax.ShapeDtypeStruct((b, t, d), F32)] * 2,
        compiler_params=_cparams("parallel", "parallel"),
        name="mixer_tail",
    )(ocmp, osel, owin, misc, ogla, rg, gm, x, gate_m, scale_f, shift_f, *consts)


ROUTE_TILE = LANE


def _first_index(hit, iota, size, axis):
    return jnp.min(jnp.where(hit, iota, size), axis=axis, keepdims=True)


def _router_kernel(xm_ref, wr_ref, bias_ref, tri_ref, eidx_ref, rank_ref, wrow_ref, cnt_ref, carry_ref):
    i = pl.program_id(0)
    tm = ROUTE_TILE
    per = N_EXPERTS // N_GROUPS

    @pl.when(i == 0)
    def _():
        carry_ref[...] = jnp.zeros_like(carry_ref)

    logits = _dot_nt(wr_ref[...], xm_ref[...], lax.Precision.HIGHEST)
    s = jax.nn.sigmoid(logits)
    sb = s + bias_ref[...]
    sb3 = sb.reshape(N_GROUPS, per, tm)
    in_grp = lax.broadcasted_iota(I32, sb3.shape, 1)
    m1 = jnp.max(sb3, axis=1, keepdims=True)
    first = _first_index(sb3 == m1, in_grp, per, 1)
    m2 = jnp.max(jnp.where(in_grp == first, NEG, sb3), axis=1, keepdims=True)
    gs = (m1 + m2).reshape(N_GROUPS, tm)
    g_iota = lax.broadcasted_iota(I32, gs.shape, 0)
    g_keep = jnp.zeros(gs.shape, jnp.bool_)
    for _ in range(TOPK_GROUPS):
        pick = g_iota == _first_index(gs == jnp.max(gs, axis=0, keepdims=True), g_iota, N_GROUPS, 0)
        g_keep = g_keep | pick
        gs = jnp.where(pick, NEG, gs)
    sbm = jnp.where(g_keep.reshape(N_GROUPS, 1, tm), sb3, NEG).reshape(N_EXPERTS, tm)
    e_iota = lax.broadcasted_iota(I32, sbm.shape, 0)
    idxs, sels = [], []
    onehot = jnp.zeros(sbm.shape, F32)
    for _ in range(TOP_K):
        idx = _first_index(sbm == jnp.max(sbm, axis=0, keepdims=True), e_iota, N_EXPERTS, 0)
        pick = e_iota == idx
        idxs.append(idx)
        sels.append(jnp.sum(jnp.where(pick, s, 0.0), axis=0, keepdims=True))
        sbm = jnp.where(pick, NEG, sbm)
        onehot = onehot + jnp.where(pick, 1.0, 0.0)
    sel = jnp.concatenate(sels, axis=0)
    wts = sel / jnp.sum(sel, axis=0, keepdims=True) * ROUTED_SCALE
    carry = carry_ref[...]
    before = carry + jnp.dot(onehot.astype(BF16), tri_ref[...], preferred_element_type=F32)
    ranks = [jnp.sum(jnp.where(e_iota == idx, before, 0.0), axis=0, keepdims=True) for idx in idxs]
    eidx_ref[...] = jnp.concatenate(idxs, axis=0)
    rank_ref[...] = jnp.concatenate(ranks, axis=0).astype(I32)
    wrow_ref[...] = jnp.concatenate([wts, jnp.zeros((LANE - TOP_K, tm), F32)], axis=0).T
    carry = carry + jnp.sum(onehot, axis=1, keepdims=True)
    carry_ref[...] = carry
    cnt_ref[...] = carry


def _router(xm, w_router, router_bias):
    n, d = xm.shape
    tm = ROUTE_TILE
    r = np.arange(tm)
    tri = jnp.asarray((r[:, None] < r[None, :]).astype(np.float32), dtype=BF16)
    return pl.pallas_call(
        _router_kernel,
        grid=(n // tm,),
        in_specs=[pl.BlockSpec((tm, d), lambda i: (i, 0)),
                  pl.BlockSpec((N_EXPERTS, d), lambda i: (0, 0)),
                  pl.BlockSpec((N_EXPERTS, 1), lambda i: (0, 0)),
                  pl.BlockSpec((tm, tm), lambda i: (0, 0))],
        out_specs=[pl.BlockSpec((TOP_K, tm), lambda i: (0, i)),
                   pl.BlockSpec((TOP_K, tm), lambda i: (0, i)),
                   pl.BlockSpec((tm, LANE), lambda i: (i, 0)),
                   pl.BlockSpec((N_EXPERTS, LANE), lambda i: (0, 0))],
        out_shape=[jax.ShapeDtypeStruct((TOP_K, n), I32), jax.ShapeDtypeStruct((TOP_K, n), I32),
                   jax.ShapeDtypeStruct((n, LANE), F32), jax.ShapeDtypeStruct((N_EXPERTS, LANE), F32)],
        scratch_shapes=[pltpu.VMEM((N_EXPERTS, LANE), F32)],
        compiler_params=_cparams("arbitrary"),
        name="router",
    )(xm, w_router.T, router_bias.reshape(N_EXPERTS, 1), tri)


def _dest_kernel(eidx_ref, rank_ref, start_ref, dest_ref):
    e_iota = lax.broadcasted_iota(I32, (N_EXPERTS, ROUTE_TILE), 0)
    start = start_ref[...]
    rows = [jnp.sum(jnp.where(e_iota == eidx_ref[k:k + 1, :], start, 0.0), axis=0, keepdims=True) for k in range(TOP_K)]
    dest_ref[0] = jnp.concatenate(rows, axis=0).astype(I32) + rank_ref[...]


def _dest(eidx, rank, pad_start):
    n = eidx.shape[1]
    tm = ROUTE_TILE
    return pl.pallas_call(
        _dest_kernel,
        grid=(n // tm,),
        in_specs=[pl.BlockSpec((TOP_K, tm), lambda i: (0, i)),
                  pl.BlockSpec((TOP_K, tm), lambda i: (0, i)),
                  pl.BlockSpec((N_EXPERTS, 1), lambda i: (0, 0))],
        out_specs=pl.BlockSpec((1, TOP_K, tm), lambda i: (i, 0, 0)),
        out_shape=jax.ShapeDtypeStruct((n // tm, TOP_K, tm), I32),
        compiler_params=_cparams("parallel"),
        name="dest",
    )(eidx, rank, pad_start.astype(F32).reshape(N_EXPERTS, 1))


def _row_copy(src_ref, src_row, dst_ref, dst_row, sem):
    return pltpu.make_async_copy(src_ref.at[pl.ds(src_row, 1)], dst_ref.at[pl.ds(dst_row, 1)], sem)


def _dispatch_kernel(dest_ref, pad_ref, nb_ref, xm_ref, xs_ref, tile_ref, zero_ref, load_sems, sems, sem):
    i = pl.program_id(0)
    tm = ROUTE_TILE
    bm = EXPERT_BLOCK
    last = pl.num_programs(0) - 1

    def load(tile, slot):
        return pltpu.make_async_copy(xm_ref.at[pl.ds(tile * tm, tm)], tile_ref.at[slot], load_sems.at[slot])

    @pl.when(i == 0)
    def _():
        load(0, 0).start()

    @pl.when(i < last)
    def _():
        load(i + 1, (i + 1) % 3).start()

    load(i, i % 3).wait()
    src = tile_ref.at[i % 3]

    def start_row(r, c):
        for k in range(TOP_K):
            _row_copy(src, r, xs_ref, dest_ref[0, k, r], sems.at[i % 2]).start()
        return c

    def wait_step(slot):
        for _ in range(TOP_K):
            pltpu.make_async_copy(tile_ref.at[0], xs_ref.at[pl.ds(0, tm)], sems.at[slot]).wait()

    lax.fori_loop(0, tm, start_row, 0)

    @pl.when(i >= 1)
    def _():
        wait_step((i - 1) % 2)

    @pl.when(i == last)
    def _():
        wait_step(i % 2)
        zero_ref[...] = jnp.zeros_like(zero_ref)

        def per_expert(e, c):
            lo, hi = pad_ref[0, e], pad_ref[1, e]
            lax.fori_loop(lo, hi, lambda r, cc: (_row_copy(zero_ref, 0, xs_ref, r, sem).start(), cc)[1], 0)
            lax.fori_loop(lo, hi, lambda r, cc: (_row_copy(zero_ref, 0, xs_ref, r, sem).wait(), cc)[1], 0)
            return c

        lax.fori_loop(0, N_EXPERTS, per_expert, 0)

        def tail_copy(blk):
            return pltpu.make_async_copy(zero_ref, xs_ref.at[pl.ds(blk * bm, bm)], sem)

        n_blocks = xs_ref.shape[0] // bm
        lax.fori_loop(nb_ref[0], n_blocks, lambda blk, c: (tail_copy(blk).start(), c)[1], 0)
        lax.fori_loop(nb_ref[0], n_blocks, lambda blk, c: (tail_copy(blk).wait(), c)[1], 0)


def _dispatch(xm, dest_tiles, pad_range, n_used, n_rows):
    n, d = xm.shape
    tm = ROUTE_TILE
    return pl.pallas_call(
        _dispatch_kernel,
        grid=(n // tm,),
        in_specs=[pl.BlockSpec((1, TOP_K, tm), lambda i: (i, 0, 0), memory_space=pltpu.SMEM),
                  pl.BlockSpec(memory_space=pltpu.SMEM),
                  pl.BlockSpec(memory_space=pltpu.SMEM),
                  pl.BlockSpec(memory_space=pl.ANY)],
        out_specs=pl.BlockSpec(memory_space=pl.ANY),
        out_shape=jax.ShapeDtypeStruct((n_rows, d), F32),
        scratch_shapes=[pltpu.VMEM((3, tm, d), F32), pltpu.VMEM((EXPERT_BLOCK, d), F32), pltpu.SemaphoreType.DMA((3,)),
                        pltpu.SemaphoreType.DMA((2,)), pltpu.SemaphoreType.DMA(())],
        compiler_params=_cparams("arbitrary"),
        name="dispatch",
    )(dest_tiles, pad_range, n_used, xm)


def _experts_kernel(first_ref, cnt_ref, nb_ref, xs_ref, wg_ref, wu_ref, wd_ref, ys_ref,
                    xbuf, ybuf, wg_s, wu_s, wd_s, in_sems, out_sems):
    e = pl.program_id(0)
    bm = EXPERT_BLOCK
    total = nb_ref[0]
    n_blocks = xs_ref.shape[0] // bm

    def x_copy(g, slot):
        return pltpu.make_async_copy(xs_ref.at[pl.ds(g * bm, bm)], xbuf.at[slot], in_sems.at[slot])

    def y_copy(g, slot):
        return pltpu.make_async_copy(ybuf.at[slot], ys_ref.at[pl.ds(g * bm, bm)], out_sems.at[slot])

    depth = xbuf.shape[0]

    @pl.when(e == 0)
    def _():
        for g0 in range(depth - 1):
            @pl.when(g0 < total)
            def _():
                x_copy(g0, g0).start()

    @pl.when(cnt_ref[e] > 0)
    def _():
        wg_s[...] = wg_ref[0].astype(BF16)
        wu_s[...] = wu_ref[0].astype(BF16)
        wd_s[...] = wd_ref[0].astype(BF16)

    def block(c, carry):
        g = first_ref[e] + c
        slot = g % depth
        x_copy(g, slot).wait()

        @pl.when(g + depth - 1 < total)
        def _():
            x_copy(g + depth - 1, (g + depth - 1) % depth).start()

        x = xbuf[slot].astype(BF16)
        gate = jnp.dot(x, wg_s[...], preferred_element_type=F32)
        up = jnp.dot(x, wu_s[...], preferred_element_type=F32)
        y = jnp.dot((gate * jax.nn.sigmoid(gate) * up).astype(BF16), wd_s[...], preferred_element_type=F32)

        @pl.when(g >= depth)
        def _():
            y_copy(g - depth, slot).wait()

        ybuf[slot] = y
        y_copy(g, slot).start()
        return carry

    lax.fori_loop(0, cnt_ref[e], block, 0)

    @pl.when(e == pl.num_programs(0) - 1)
    def _():
        for back in range(depth, 0, -1):
            @pl.when(total >= back)
            def _():
                y_copy(total - back, (total - back) % depth).wait()

        ybuf[0] = jnp.zeros((bm, ybuf.shape[2]), F32)
        lax.fori_loop(total, n_blocks, lambda g, c: (y_copy(g, 0).start(), c)[1], 0)
        lax.fori_loop(total, n_blocks, lambda g, c: (y_copy(g, 0).wait(), c)[1], 0)


def _experts(xs, first_block, n_block, n_used, w_gate, w_up, w_down):
    n_rows, d = xs.shape
    bm = EXPERT_BLOCK
    n_exp, _, f = w_gate.shape
    grid_spec = pltpu.PrefetchScalarGridSpec(
        num_scalar_prefetch=3,
        grid=(n_exp,),
        in_specs=[pl.BlockSpec(memory_space=pl.ANY),
                  pl.BlockSpec((1, d, f), lambda e, *_: (e, 0, 0)),
                  pl.BlockSpec((1, d, f), lambda e, *_: (e, 0, 0)),
                  pl.BlockSpec((1, f, d), lambda e, *_: (e, 0, 0))],
        out_specs=pl.BlockSpec(memory_space=pl.ANY),
        scratch_shapes=[pltpu.VMEM((EXPERT_RING, bm, d), F32), pltpu.VMEM((EXPERT_RING, bm, d), F32),
                        pltpu.VMEM((d, f), BF16), pltpu.VMEM((d, f), BF16), pltpu.VMEM((f, d), BF16),
                        pltpu.SemaphoreType.DMA((EXPERT_RING,)), pltpu.SemaphoreType.DMA((EXPERT_RING,))],
    )
    return pl.pallas_call(
        _experts_kernel,
        grid_spec=grid_spec,
        out_shape=jax.ShapeDtypeStruct((n_rows, d), F32),
        compiler_params=_cparams("arbitrary"),
        name="experts",
    )(first_block, n_block, n_used, xs, w_gate, w_up, w_down)


def _combine_kernel(dest_ref, dnext_ref, ys_ref, wrow_ref, xm_ref, sg_ref, su_ref, sd_ref, out_ref, buf_ref, sems):
    i = pl.program_id(0)
    tm = ROUTE_TILE
    slot = i % 2

    def issue(d_ref, s):
        def start_row(r, c):
            for k in range(TOP_K):
                _row_copy(ys_ref, d_ref[0, k, r], buf_ref.at[s, k], r, sems.at[s]).start()
            return c

        lax.fori_loop(0, tm, start_row, 0)

    @pl.when(i == 0)
    def _():
        issue(dest_ref, 0)

    @pl.when(i < pl.num_programs(0) - 1)
    def _():
        issue(dnext_ref, 1 - slot)

    x = xm_ref[...].astype(BF16)
    g = jnp.dot(x, sg_ref[...], preferred_element_type=F32)
    u = jnp.dot(x, su_ref[...], preferred_element_type=F32)
    shared = jnp.dot((g * jax.nn.sigmoid(g) * u).astype(BF16), sd_ref[...], preferred_element_type=F32)
    for k in range(TOP_K):
        pltpu.make_async_copy(ys_ref.at[pl.ds(0, tm)], buf_ref.at[slot, k], sems.at[slot]).wait()
    w = wrow_ref[...]
    routed = w[:, 0:1] * buf_ref[slot, 0]
    for k in range(1, TOP_K):
        routed = routed + w[:, k:k + 1] * buf_ref[slot, k]
    out_ref[...] = routed + shared


def _combine(ys, dest_tiles, wrow, xm, ws_gate, ws_up, ws_down):
    n, d = xm.shape
    tm = ROUTE_TILE
    f = ws_gate.shape[1]
    return pl.pallas_call(
        _combine_kernel,
        grid=(n // tm,),
        in_specs=[pl.BlockSpec((1, TOP_K, tm), lambda i: (i, 0, 0), memory_space=pltpu.SMEM),
                  pl.BlockSpec((1, TOP_K, tm), lambda i: (jnp.minimum(i + 1, n // tm - 1), 0, 0), memory_space=pltpu.SMEM),
                  pl.BlockSpec(memory_space=pl.ANY),
                  pl.BlockSpec((tm, LANE), lambda i: (i, 0)),
                  pl.BlockSpec((tm, d), lambda i: (i, 0)),
                  pl.BlockSpec((d, f), lambda i: (0, 0)),
                  pl.BlockSpec((d, f), lambda i: (0, 0)),
                  pl.BlockSpec((f, d), lambda i: (0, 0))],
        out_specs=pl.BlockSpec((tm, d), lambda i: (i, 0)),
        out_shape=jax.ShapeDtypeStruct((n, d), F32),
        scratch_shapes=[pltpu.VMEM((2, TOP_K, tm, d), F32), pltpu.SemaphoreType.DMA((2,))],
        compiler_params=_cparams("arbitrary"),
        name="combine",
    )(dest_tiles, dest_tiles, ys, wrow, xm, ws_gate.astype(BF16), ws_up.astype(BF16), ws_down.astype(BF16))


def _final_ln_kernel(x1_ref, moe_ref, gate_ref, g_ref, b_ref, y_ref):
    y_ref[0] = _layer_norm(DN_ALPHA * x1_ref[0] + gate_ref[0] * moe_ref[0], g_ref[...], b_ref[...])


def _final_ln(x1, moe, gate_f, ln_g, ln_b, tm):
    b, t, d = x1.shape
    per_tok = gate_f.shape[1] != 1
    tok = pl.BlockSpec((1, tm, d), lambda i, j: (i, j, 0))
    mod_spec = tok if per_tok else pl.BlockSpec((1, 1, d), lambda i, j: (i, 0, 0))
    vec = pl.BlockSpec((1, d), lambda i, j: (0, 0))
    return pl.pallas_call(
        _final_ln_kernel,
        grid=(b, t // tm),
        in_specs=[tok, tok, mod_spec, vec, vec],
        out_specs=tok,
        out_shape=jax.ShapeDtypeStruct((b, t, d), F32),
        compiler_params=_cparams("parallel", "parallel"),
        name="final_ln",
    )(x1, moe, gate_f, ln_g.reshape(1, d), ln_b.reshape(1, d))


def _moe(xm, w_router, router_bias, w_e_gate, w_e_up, w_e_down, w_s_gate, w_s_up, w_s_down):
    n = xm.shape[0]
    eidx, rank, wrow, cnt = _router(xm, w_router, router_bias)
    counts = cnt[:, 0].astype(I32)
    padded = (counts + EXPERT_BLOCK - 1) // EXPERT_BLOCK * EXPERT_BLOCK
    pad_end = jnp.cumsum(padded)
    pad_start = pad_end - padded
    dest_tiles = _dest(eidx, rank, pad_start)
    n_blocks = -(-(n * TOP_K) // EXPERT_BLOCK) + N_EXPERTS
    n_used = (pad_end[-1:] // EXPERT_BLOCK).astype(I32)
    pad_range = jnp.stack([pad_start + counts, pad_end]).astype(I32)
    xs = _dispatch(xm, dest_tiles, pad_range, n_used, n_blocks * EXPERT_BLOCK)
    ys = _experts(xs, (pad_start // EXPERT_BLOCK).astype(I32), (padded // EXPERT_BLOCK).astype(I32), n_used,
                  w_e_gate, w_e_up, w_e_down)
    return _combine(ys, dest_tiles, wrow, xm, w_s_gate, w_s_up, w_s_down)


def kernel(x_prompt, x_sample, cache_kv_cmp, cache_kv_sel, state_kv_win, state_gla, page_table, c_prompt, c_sample, w_in, b_in, cmp_k_pos, cmp_k_w1, cmp_k_w2, cmp_v_pos, cmp_v_w1, cmp_v_w2, gla_w_a2, gla_b_a, gla_norm_g, w_br_a, w_br_b, w_out, ln1_g, ln1_b, w_ada, b_ada, w_router, router_bias, w_e_gate, w_e_up, w_e_down, w_s_gate, w_s_up, w_s_down, ln2_g, ln2_b):
    bp, tp, d = x_prompt.shape
    nd, td = x_sample.shape[:2]
    n_pool, page_rows = cache_kv_cmp.shape[:2]
    past_len = page_table.shape[1] * page_rows
    assert td == 1 and d == D_MODEL and page_rows == PAGE_ROWS and page_table.shape[1] == PAGES and tp == PAGES * PAGE_ROWS
    kv_w = 2 * NSA_KV_HEADS * HEAD_DIM

    mod = _adaln(jnp.concatenate([c_prompt, c_sample], axis=0), w_ada, b_ada)
    mod_p = [m.reshape(bp, 1, d) for m in jnp.split(mod[:bp], 6, axis=-1)]
    mod_s = [m.reshape(1, nd, d) for m in jnp.split(mod[bp:], 6, axis=-1)]

    w_pack, b_pack = _pack_in_weights(w_in, b_in)
    cmp_wk = _pack_cmp_weights(cmp_k_pos, cmp_k_w1, cmp_k_w2)
    cmp_wv = _pack_cmp_weights(cmp_v_pos, cmp_v_w1, cmp_v_w2)
    wa_pad = jnp.zeros((LANE, GLA_HEADS * GLA_DK), F32).at[MISC_AG:MISC_AG + GLA_GATE_RANK].set(gla_w_a2)
    tail_consts = (_gate_expand_table(), gla_norm_g.reshape(1, -1), _pad_br_a(w_br_a), w_br_b.astype(BF16), w_out.astype(BF16),
                   ln1_g.reshape(1, d), ln1_b.reshape(1, d))

    gm, qn, vg, rg, kvc, kvs, kvw, qg, kg, misc = _inproj(
        x_prompt, mod_p[0], mod_p[1], w_pack, b_pack, _rope_tables(jnp.arange(tp, dtype=I32)), 256)
    kc, vc = _compress(kvc.reshape(bp * PAGES, PAGE_ROWS, kv_w), jnp.arange(bp * PAGES, dtype=I32), cmp_wk, cmp_wv)
    ocmp, sel = _cmp_attn(qn, kc, vc, 256)
    osel = _sel_attn(qn, kvs, sel, 128, 256)
    owin = _win_attn(qn, kvw, 128)
    ogla, gla_p = _gla(qg, kg, vg, misc, wa_pad, gla_b_a, None, tp)
    x1_p, xm_p = _mixer_tail(ocmp, osel, owin, misc, ogla, rg, gm, x_prompt, mod_p[2], mod_p[4], mod_p[3], tail_consts, 256)
    n_win = min(WINDOW, tp)
    outs_p = (kvc.reshape(bp, tp, 2, NSA_KV_HEADS, HEAD_DIM), kvs.reshape(bp, tp, 2, NSA_KV_HEADS, HEAD_DIM),
              kvw[:, tp - n_win:].reshape(bp, n_win, 2, NSA_KV_HEADS, HEAD_DIM), gla_p)

    gm, qn, vg, rg, kvc, kvs, kvw, qg, kg, misc = _inproj(
        x_sample.reshape(1, nd, d), mod_s[0], mod_s[1], w_pack, b_pack, _rope_tables(jnp.full((nd,), past_len, I32)), nd)
    page_ids = page_table.reshape(-1).astype(I32)
    kc, vc = _compress(cache_kv_cmp.reshape(n_pool, PAGE_ROWS, kv_w), page_ids, cmp_wk, cmp_wv)
    ocmp, osel, owin, win_new = _nsa_decode(
        qn.reshape(nd, 1, -1), kc, vc, cache_kv_sel.reshape(n_pool, PAGE_ROWS, kv_w).transpose(0, 2, 1), page_ids,
        kvs.reshape(nd, 1, kv_w), kvw.reshape(nd, 1, kv_w), state_kv_win.reshape(nd, -1, kv_w).transpose(0, 2, 1))
    win_new = win_new.transpose(0, 2, 1)

    def pad_rows(a):
        return jnp.pad(a.reshape(nd, 1, -1), ((0, 0), (0, GLA_SUB - 1), (0, 0)))

    ogla, gla_s = _gla(pad_rows(qg), pad_rows(kg), pad_rows(vg), pad_rows(misc), wa_pad, gla_b_a, state_gla, 1)
    x1_s, xm_s = _mixer_tail(ocmp.reshape(1, nd, -1), osel.reshape(1, nd, -1), owin.reshape(1, nd, -1), misc,
                             ogla[:, 0].reshape(1, nd, -1), rg, gm, x_sample.reshape(1, nd, d),
                             mod_s[2], mod_s[4], mod_s[3], tail_consts, nd)
    outs_s = (kvc.reshape(nd, 1, 2, NSA_KV_HEADS, HEAD_DIM), kvs.reshape(nd, 1, 2, NSA_KV_HEADS, HEAD_DIM),
              win_new.reshape(state_kv_win.shape), gla_s)

    n_p = bp * tp
    moe = _moe(jnp.concatenate([xm_p.reshape(n_p, d), xm_s.reshape(nd, d)], axis=0),
               w_router, router_bias, w_e_gate, w_e_up, w_e_down, w_s_gate, w_s_up, w_s_down)
    y_p = _final_ln(x1_p, moe[:n_p].reshape(bp, tp, d), mod_p[5], ln2_g, ln2_b, 256)
    y_s = _final_ln(x1_s, moe[n_p:].reshape(1, nd, d), mod_s[5], ln2_g, ln2_b, nd).reshape(nd, 1, d)
    return (y_p, y_s) + outs_p + outs_s
```

```python
import functools

import numpy as np
import jax
import jax.numpy as jnp
from jax import lax
from jax.experimental import pallas as pl
from jax.experimental.pallas import tpu as pltpu

F32 = jnp.float32
BF16 = jnp.bfloat16
I32 = jnp.int32

D_MODEL = 1024
NSA_HEADS = 8
NSA_KV_HEADS = 2
NSA_GROUP = NSA_HEADS // NSA_KV_HEADS
HEAD_DIM = 64
ROT_DIM = HEAD_DIM // 4
ROPE_THETA = 500000.0
CMP_LEN = 32
CMP_STRIDE = 16
CMP_HIDDEN = 256
SEL_BLOCK = 64
SEL_TOP_N = 16
WINDOW = 512
FORCE_SCORE = 1.0e4
GLA_HEADS = 4
GLA_DK = 64
GLA_DV = 128
GLA_GATE_RANK = 16
GLA_TAU = 16.0
GLA_SUB = 16
N_EXPERTS = 256
TOP_K = 8
N_GROUPS = 8
TOPK_GROUPS = 4
EXPERT_DIM = 256
SHARED_DIM = 256
ROUTED_SCALE = 2.5
EXPERT_BLOCK = 128
EXPERT_RING = 4
DN_ALPHA = 2.0 ** 0.25
LN_EPS = 1e-5
LANE = 128
NEG = -1.0e30
VMEM_LIMIT = 56 * 1024 * 1024

SEG_GM = (0, 2 * D_MODEL)
SEG_QN = (SEG_GM[0] + SEG_GM[1], NSA_HEADS * LANE)
SEG_VG = (SEG_QN[0] + SEG_QN[1], GLA_HEADS * GLA_DV)
SEG_RG = (SEG_VG[0] + SEG_VG[1], GLA_HEADS * GLA_DV)
SEG_KVC = (SEG_RG[0] + SEG_RG[1], 2 * LANE)
SEG_KVS = (SEG_KVC[0] + SEG_KVC[1], 2 * LANE)
SEG_KVW = (SEG_KVS[0] + SEG_KVS[1], 2 * LANE)
SEG_QG = (SEG_KVW[0] + SEG_KVW[1], GLA_HEADS * GLA_DK)
SEG_KG = (SEG_QG[0] + SEG_QG[1], GLA_HEADS * GLA_DK)
SEG_MISC = (SEG_KG[0] + SEG_KG[1], LANE)
IN_PACKED = SEG_MISC[0] + SEG_MISC[1]
MISC_GN = 0
MISC_AG = NSA_HEADS * 3


def _cparams(*sem):
    return pltpu.CompilerParams(dimension_semantics=sem, vmem_limit_bytes=VMEM_LIMIT)


def _bdot(a, b):
    return jnp.dot(a.astype(BF16), b.astype(BF16), preferred_element_type=F32)


def _dot_nt(a, b, precision=None):
    return lax.dot_general(a, b, (((1,), (1,)), ((), ())), preferred_element_type=F32, precision=precision)


def _adaln_kernel(c_ref, w_ref, b_ref, o_ref):
    c = c_ref[...]
    o_ref[...] = _bdot(c * jax.nn.sigmoid(c), w_ref[...]) + b_ref[...]


def _adaln(c, w_ada, b_ada):
    n, d = c.shape
    m = w_ada.shape[1]
    tn = 512
    return pl.pallas_call(
        _adaln_kernel,
        grid=(m // tn,),
        in_specs=[pl.BlockSpec((n, d), lambda j: (0, 0)),
                  pl.BlockSpec((d, tn), lambda j: (0, j)),
                  pl.BlockSpec((1, tn), lambda j: (0, j))],
        out_specs=pl.BlockSpec((n, tn), lambda j: (0, j)),
        out_shape=jax.ShapeDtypeStruct((n, m), F32),
        compiler_params=_cparams("parallel"),
        name="adaln",
    )(c, w_ada, b_ada.reshape(1, m))


def _rope_tables(pos):
    half = ROT_DIM // 2
    inv = jnp.power(ROPE_THETA, -jnp.arange(half, dtype=F32) * 2.0 / ROT_DIM)
    ang = pos.astype(F32)[:, None] * inv[None, :]
    cos, sin = jnp.cos(ang), jnp.sin(ang)
    t = pos.shape[0]
    one = jnp.ones((t, HEAD_DIM - ROT_DIM), F32)
    z8 = jnp.zeros((t, half), F32)
    z48 = jnp.zeros((t, HEAD_DIM - ROT_DIM), F32)
    c = jnp.concatenate([cos, cos, one, cos, cos, one], axis=1)
    s1 = jnp.concatenate([-sin, z8, z48, -sin, z8, z48], axis=1)
    s2 = jnp.concatenate([z8, sin, z48, z8, sin, z48], axis=1)
    return c, s1, s2


def _pack_in_weights(w_in, b_in):
    sizes = (512, 128, 128, 128, 128, 128, 128, 24, 256, 256, 512, 512, 16, 2048)
    offs = np.concatenate([[0], np.cumsum(sizes)])

    def pack(w):
        seg = [w[..., offs[i]:offs[i + 1]] for i in range(len(sizes))]
        q_n, k_c, v_c, k_s, v_s, k_w, v_w, g_n, q_g, k_g, v_g, r_g, a_g, g_m = seg
        zero = jnp.zeros_like(q_n[..., :HEAD_DIM])
        q_slots = []
        for hh in range(NSA_HEADS):
            qh = q_n[..., hh * HEAD_DIM:(hh + 1) * HEAD_DIM] * (HEAD_DIM ** -0.5)
            q_slots += [qh, zero] if hh // NSA_GROUP == 0 else [zero, qh]
        misc_pad = jnp.zeros_like(w[..., :LANE - g_n.shape[-1] - a_g.shape[-1]])
        return jnp.concatenate([g_m] + q_slots + [v_g, r_g, k_c, v_c, k_s, v_s, k_w, v_w,
                                                   q_g * (GLA_DK ** -0.5), k_g, g_n, a_g, misc_pad], axis=-1)

    return pack(w_in).astype(BF16), pack(b_in.reshape(1, -1))


def _inproj_kernel(x_ref, sh_ref, sc_ref, w_ref, b_ref, rc_ref, rs1_ref, rs2_ref,
                   gm_ref, qn_ref, vg_ref, rg_ref, kvc_ref, kvs_ref, kvw_ref, qg_ref, kg_ref, misc_ref):
    h = (x_ref[0] * (1.0 + sc_ref[0]) + sh_ref[0]).astype(BF16)
    rc, rs1, rs2 = rc_ref[...], rs1_ref[...], rs2_ref[...]

    def proj(off, width):
        return jnp.dot(h, w_ref[:, off:off + width], preferred_element_type=F32) + b_ref[:, off:off + width]

    def rope(z):
        return z * rc + pltpu.roll(z, LANE - ROT_DIM // 2, 1) * rs1 + pltpu.roll(z, ROT_DIM // 2, 1) * rs2

    def plain(ref, seg):
        off, width = seg
        step = min(width, 512)
        for c in range(0, width, step):
            ref[0, :, c:c + step] = proj(off + c, step)

    plain(gm_ref, SEG_GM)
    for c in range(0, SEG_QN[1], 512):
        z = proj(SEG_QN[0] + c, 512)
        for s in range(0, 512, LANE):
            qn_ref[0, :, c + s:c + s + LANE] = rope(z[:, s:s + LANE])
    plain(vg_ref, SEG_VG)
    plain(rg_ref, SEG_RG)
    for ref, seg in ((kvc_ref, SEG_KVC), (kvs_ref, SEG_KVS), (kvw_ref, SEG_KVW)):
        z = proj(seg[0], seg[1])
        ref[0, :, 0:LANE] = rope(z[:, 0:LANE])
        ref[0, :, LANE:2 * LANE] = z[:, LANE:2 * LANE]
    plain(qg_ref, SEG_QG)
    plain(kg_ref, SEG_KG)
    plain(misc_ref, SEG_MISC)


def _inproj(x, shift, scale, w_pack, b_pack, tables, tm):
    b, t, d = x.shape
    per_tok = shift.shape[1] != 1
    mod_spec = (pl.BlockSpec((1, tm, d), lambda i, j: (i, j, 0)) if per_tok
                else pl.BlockSpec((1, 1, d), lambda i, j: (i, 0, 0)))
    segs = (SEG_GM, SEG_QN, SEG_VG, SEG_RG, SEG_KVC, SEG_KVS, SEG_KVW, SEG_QG, SEG_KG, SEG_MISC)
    tab_spec = pl.BlockSpec((tm, LANE), lambda i, j: (j, 0))
    return pl.pallas_call(
        _inproj_kernel,
        grid=(b, t // tm),
        in_specs=[pl.BlockSpec((1, tm, d), lambda i, j: (i, j, 0)), mod_spec, mod_spec,
                  pl.BlockSpec((d, IN_PACKED), lambda i, j: (0, 0)),
                  pl.BlockSpec((1, IN_PACKED), lambda i, j: (0, 0)),
                  tab_spec, tab_spec, tab_spec],
        out_specs=[pl.BlockSpec((1, tm, w), lambda i, j: (i, j, 0)) for _, w in segs],
        out_shape=[jax.ShapeDtypeStruct((b, t, w), F32) for _, w in segs],
        compiler_params=_cparams("parallel", "parallel"),
        name="inproj",
    )(x, shift, scale, w_pack, b_pack, *tables)


CHUNKS = 128
PAGE_ROWS = 128
PAGES = 16


def _pack_cmp_weights(pos, w1, w2):
    pos2 = jnp.concatenate([pos, pos], axis=1)
    z1 = jnp.zeros_like(w1)
    bd1 = jnp.concatenate([jnp.concatenate([w1, z1], axis=2), jnp.concatenate([z1, w1], axis=2)], axis=1)
    w1p = jnp.concatenate([bd1[:CMP_STRIDE], bd1[CMP_STRIDE:]], axis=2).astype(BF16)
    z2 = jnp.zeros_like(w2)
    w2p = jnp.concatenate([jnp.concatenate([w2, z2], axis=1), jnp.concatenate([z2, w2], axis=1)], axis=0).astype(BF16)
    return pos2, w1p, w2p


def _fill_chunks(page_refs, xs_ref, rows_ref, feature_major):
    if feature_major:
        for p, pr in enumerate(page_refs):
            rows_ref[p * PAGE_ROWS:(p + 1) * PAGE_ROWS, :] = pr[0].T
        for l in range(CMP_STRIDE):
            xs_ref[l] = rows_ref[pl.ds(l, CHUNKS, stride=CMP_STRIDE), :]
        return
    for p, pr in enumerate(page_refs):
        for l in range(CMP_STRIDE):
            xs_ref[l, 8 * p:8 * p + 8, :] = pr[0, pl.ds(l, PAGE_ROWS // CMP_STRIDE, stride=CMP_STRIDE), :]


def _compress_chunks(xs_ref, pos_ref, w1_ref, w2_ref):
    hid2 = 2 * CMP_HIDDEN
    acc_a = jnp.zeros((CHUNKS, hid2), F32)
    acc_b = jnp.zeros((CHUNKS, hid2), F32)
    for l in range(CMP_STRIDE):
        x = xs_ref[l]
        acc_a = acc_a + jnp.dot((x + pos_ref[l:l + 1, :]).astype(BF16), w1_ref[l, :, 0:hid2], preferred_element_type=F32)
        acc_b = acc_b + jnp.dot((x + pos_ref[CMP_STRIDE + l:CMP_STRIDE + l + 1, :]).astype(BF16), w1_ref[l, :, hid2:2 * hid2],
                                preferred_element_type=F32)
    hid = acc_a + pltpu.roll(acc_b, CHUNKS - 1, 0)
    out = jnp.dot(jax.nn.gelu(hid).astype(BF16), w2_ref[...], preferred_element_type=F32)
    row = lax.broadcasted_iota(I32, out.shape, 0)
    return jnp.where(row < CHUNKS - 1, out, 0.0)


def _compress_kernel(pt_ref, *refs, feature_major):
    k_pages, v_pages = refs[:PAGES], refs[PAGES:2 * PAGES]
    posk_ref, w1k_ref, w2k_ref, posv_ref, w1v_ref, w2v_ref, kc_ref, vc_ref, xk_ref, xv_ref, rows_ref = refs[2 * PAGES:]
    _fill_chunks(k_pages, xk_ref, rows_ref, feature_major)
    kc_ref[0] = _compress_chunks(xk_ref, posk_ref, w1k_ref, w2k_ref)
    _fill_chunks(v_pages, xv_ref, rows_ref, feature_major)
    vc_ref[0] = _compress_chunks(xv_ref, posv_ref, w1v_ref, w2v_ref)


def _page_spec(p, half, feature_major):
    if feature_major:
        return pl.BlockSpec((1, LANE, PAGE_ROWS), lambda i, pt: (pt[i * PAGES + p], half, 0))
    return pl.BlockSpec((1, PAGE_ROWS, LANE), lambda i, pt: (pt[i * PAGES + p], 0, half))


def _const_spec(shape):
    nd = len(shape)
    return pl.BlockSpec(shape, lambda i, pt: (0,) * nd)


def _compress(pages, page_ids, cmp_wk, cmp_wv, feature_major=False):
    n_b = page_ids.shape[0] // PAGES
    consts = list(cmp_wk) + list(cmp_wv)
    grid_spec = pltpu.PrefetchScalarGridSpec(
        num_scalar_prefetch=1,
        grid=(n_b,),
        in_specs=[_page_spec(p, h, feature_major) for h in range(2) for p in range(PAGES)] + [_const_spec(c.shape) for c in consts],
        out_specs=[pl.BlockSpec((1, CHUNKS, LANE), lambda i, pt: (i, 0, 0))] * 2,
        scratch_shapes=[pltpu.VMEM((CMP_STRIDE, CHUNKS, LANE), F32)] * 2 + [pltpu.VMEM((PAGES * PAGE_ROWS, LANE), F32)],
    )
    return pl.pallas_call(
        functools.partial(_compress_kernel, feature_major=feature_major),
        grid_spec=grid_spec,
        out_shape=[jax.ShapeDtypeStruct((n_b, CHUNKS, LANE), F32)] * 2,
        compiler_params=_cparams("parallel"),
        name="compress",
    )(page_ids, *([pages] * (2 * PAGES)), *consts)


def _cover_tables(n_sel):
    c_start = np.arange(CHUNKS) * CMP_STRIDE
    s_start = np.arange(n_sel) * SEL_BLOCK
    cover = ((c_start[:, None] < s_start[None, :] + SEL_BLOCK) & (c_start[:, None] + CMP_LEN > s_start[None, :])).astype(np.float32)
    cover[CHUNKS - 1] = 0.0
    out = np.zeros((NSA_KV_HEADS, LANE, CHUNKS), np.float32)
    for h in range(NSA_KV_HEADS):
        out[h, h * 64:h * 64 + n_sel] = cover.T
    return jnp.asarray(out)


def _softmax_rows(s, valid):
    s = jnp.where(valid, s, NEG)
    m = jnp.max(s, axis=-1, keepdims=True)
    m = jnp.where(m > 0.5 * NEG, m, 0.0)
    p = jnp.where(valid, jnp.exp(s - m), 0.0)
    return p / jnp.maximum(jnp.sum(p, axis=-1, keepdims=True), 1e-30)


def _select_blocks(imp, n_sel, top_n):
    ridx = lax.broadcasted_iota(I32, imp.shape, 0)
    cnt = jnp.zeros(imp.shape, F32)
    for i in range(n_sel):
        vi = imp[i:i + 1, :]
        ahead = (vi > imp) | ((vi == imp) & (ridx > i))
        cnt = cnt + jnp.where(ahead, 1.0, 0.0)
    return jnp.where((cnt < top_n) & (ridx < n_sel), 1.0, 0.0)


def _cmp_attn_kernel(qn_ref, kc_ref, vc_ref, cov_ref, o_ref, sel_ref, *, tq, n_sel):
    qi = pl.program_id(1)
    kc = kc_ref[0].astype(BF16)
    vc = vc_ref[0].astype(BF16)
    qpos = qi * tq + lax.broadcasted_iota(I32, (tq, CHUNKS), 0)
    cidx = lax.broadcasted_iota(I32, (tq, CHUNKS), 1)
    valid = (cidx * CMP_STRIDE + CMP_LEN - 1 <= qpos) & (cidx < CHUNKS - 1)
    psum = [jnp.zeros((tq, CHUNKS), F32) for _ in range(NSA_KV_HEADS)]
    for hh in range(NSA_HEADS):
        q = qn_ref[0, :, hh * LANE:(hh + 1) * LANE].astype(BF16)
        p = _softmax_rows(_dot_nt(q, kc), valid)
        o_ref[0, :, hh * LANE:(hh + 1) * LANE] = jnp.dot(p.astype(BF16), vc, preferred_element_type=F32)
        psum[hh // NSA_GROUP] = psum[hh // NSA_GROUP] + p
    imp = (_dot_nt(cov_ref[0], psum[0], lax.Precision.HIGHEST) + _dot_nt(cov_ref[1], psum[1], lax.Precision.HIGHEST))
    blk = lax.broadcasted_iota(I32, (LANE, tq), 0) & 63
    qpos_t = qi * tq + lax.broadcasted_iota(I32, (LANE, tq), 1)
    cur = qpos_t // SEL_BLOCK
    forced = (blk == 0) | (blk == cur) | (blk == cur - 1)
    imp = jnp.where(forced, FORCE_SCORE, jnp.where(blk * SEL_BLOCK <= qpos_t, imp, -FORCE_SCORE))
    sel_t = jnp.concatenate([_select_blocks(imp[0:64], n_sel, SEL_TOP_N), _select_blocks(imp[64:128], n_sel, SEL_TOP_N)], axis=0)
    sel_ref[0] = sel_t.T


def _cmp_attn(qn, kc, vc, tq):
    b, t, _ = qn.shape
    n_sel = -(-t // SEL_BLOCK)
    cov = _cover_tables(n_sel)
    return pl.pallas_call(
        functools.partial(_cmp_attn_kernel, tq=tq, n_sel=n_sel),
        grid=(b, t // tq),
        in_specs=[pl.BlockSpec((1, tq, NSA_HEADS * LANE), lambda i, j: (i, j, 0)),
                  pl.BlockSpec((1, CHUNKS, LANE), lambda i, j: (i, 0, 0)),
                  pl.BlockSpec((1, CHUNKS, LANE), lambda i, j: (i, 0, 0)),
                  pl.BlockSpec((NSA_KV_HEADS, LANE, CHUNKS), lambda i, j: (0, 0, 0))],
        out_specs=[pl.BlockSpec((1, tq, NSA_HEADS * LANE), lambda i, j: (i, j, 0)),
                   pl.BlockSpec((1, tq, LANE), lambda i, j: (i, j, 0))],
        out_shape=[jax.ShapeDtypeStruct((b, t, NSA_HEADS * LANE), F32), jax.ShapeDtypeStruct((b, t, LANE), F32)],
        compiler_params=_cparams("parallel", "parallel"),
        name="cmp_attn",
    )(qn, kc, vc, cov)


def _key_block_table(t):
    blk = np.arange(t)[:, None] // SEL_BLOCK
    lanes = np.arange(LANE)[None, :] & 63
    return jnp.asarray((blk == lanes).astype(np.float32), dtype=BF16)


def _lane_fold(x, op):
    out = x[:, 0:LANE]
    for c in range(LANE, x.shape[1], LANE):
        out = op(out, x[:, c:c + LANE])
    return out


def _sel_attn_kernel(qn_ref, kv_ref, sel_ref, kb_ref, o_ref, q2_scr, k2_scr, v_scr, s_scr, *, tq, tk):
    qi = pl.program_id(1)
    rows = NSA_HEADS * tq
    t = kv_ref.shape[1]

    @pl.when(qi == 0)
    def _():
        k2_scr[:, 0:LANE] = kv_ref[0, :, 0:LANE].astype(BF16)
        k2_scr[:, LANE:2 * LANE] = kb_ref[...]
        v_scr[...] = kv_ref[0, :, LANE:2 * LANE].astype(BF16)

    not_sel = (1.0 - sel_ref[0]) * NEG
    lane_head = lax.broadcasted_iota(I32, (tq, LANE), 1) // 64
    for hh in range(NSA_HEADS):
        q2_scr[hh * tq:(hh + 1) * tq, 0:LANE] = qn_ref[0, :, hh * LANE:(hh + 1) * LANE].astype(BF16)
        q2_scr[hh * tq:(hh + 1) * tq, LANE:2 * LANE] = jnp.where(lane_head == hh // NSA_GROUP, not_sel, 0.0).astype(BF16)
    q2 = q2_scr[...]
    last = (qi * tq + tq - 1) // tk

    def scores(kt):
        k0 = pl.multiple_of(kt * tk, tk)
        return _dot_nt(q2, k2_scr[pl.ds(k0, tk), :])

    def pass1(kt, m_acc):
        s = scores(kt)
        s_scr[kt] = s
        return jnp.maximum(m_acc, _lane_fold(s, jnp.maximum))

    m_acc = lax.fori_loop(0, last, pass1, jnp.full((rows, LANE), NEG, F32))
    rel = (qi * tq + lax.broadcasted_iota(I32, (tq, tk), 0)) - (last * tk + lax.broadcasted_iota(I32, (tq, tk), 1))
    causal = jnp.where(rel >= 0, 0.0, NEG)
    s_last = scores(last) + jnp.concatenate([causal] * NSA_HEADS, axis=0)
    m = jnp.max(jnp.maximum(m_acc, _lane_fold(s_last, jnp.maximum)), axis=-1, keepdims=True)

    def accumulate(s, v, l_acc, acc):
        p = jnp.exp(s - m)
        return l_acc + _lane_fold(p, jnp.add), acc + jnp.dot(p.astype(BF16), v, preferred_element_type=F32)

    def pass2(kt, carry):
        k0 = pl.multiple_of(kt * tk, tk)
        return accumulate(s_scr[kt], v_scr[pl.ds(k0, tk), :], *carry)

    l_acc, acc = lax.fori_loop(0, last, pass2, (jnp.zeros((rows, LANE), F32), jnp.zeros((rows, LANE), F32)))
    l_acc, acc = accumulate(s_last, v_scr[pl.ds(pl.multiple_of(last * tk, tk), tk), :], l_acc, acc)
    out = acc / jnp.sum(l_acc, axis=-1, keepdims=True)
    for hh in range(NSA_HEADS):
        o_ref[0, :, hh * LANE:(hh + 1) * LANE] = out[hh * tq:(hh + 1) * tq]


def _sel_attn(qn, kv, sel, tq, tk):
    b, t, _ = qn.shape
    rows = NSA_HEADS * tq
    return pl.pallas_call(
        functools.partial(_sel_attn_kernel, tq=tq, tk=tk),
        grid=(b, t // tq),
        in_specs=[pl.BlockSpec((1, tq, NSA_HEADS * LANE), lambda i, j: (i, j, 0)),
                  pl.BlockSpec((1, t, 2 * LANE), lambda i, j: (i, 0, 0)),
                  pl.BlockSpec((1, tq, LANE), lambda i, j: (i, j, 0)),
                  pl.BlockSpec((t, LANE), lambda i, j: (0, 0))],
        out_specs=pl.BlockSpec((1, tq, NSA_HEADS * LANE), lambda i, j: (i, j, 0)),
        out_shape=jax.ShapeDtypeStruct((b, t, NSA_HEADS * LANE), F32),
        scratch_shapes=[pltpu.VMEM((rows, 2 * LANE), BF16), pltpu.VMEM((t, 2 * LANE), BF16), pltpu.VMEM((t, LANE), BF16),
                        pltpu.VMEM((t // tk, rows, tk), F32)],
        compiler_params=_cparams("parallel", "arbitrary"),
        name="sel_attn",
    )(qn, kv, sel, _key_block_table(t))


def _win_attn_kernel(qn_ref, kv_ref, o_ref, q_scr, k_scr, v_scr, *, tq):
    qi = pl.program_id(1)
    span = WINDOW + tq

    @pl.when(qi == 0)
    def _():
        k_scr[...] = kv_ref[0, :, 0:LANE].astype(BF16)
        v_scr[...] = kv_ref[0, :, LANE:2 * LANE].astype(BF16)

    for hh in range(NSA_HEADS):
        q_scr[hh * tq:(hh + 1) * tq, :] = qn_ref[0, :, hh * LANE:(hh + 1) * LANE].astype(BF16)
    k0 = pl.multiple_of(jnp.maximum(qi * tq - WINDOW, 0), tq)
    rel = (qi * tq + lax.broadcasted_iota(I32, (tq, span), 0)) - (k0 + lax.broadcasted_iota(I32, (tq, span), 1))
    bias = jnp.where((rel >= 0) & (rel <= WINDOW), 0.0, NEG)
    s = _dot_nt(q_scr[...], k_scr[pl.ds(k0, span), :]) + jnp.concatenate([bias] * NSA_HEADS, axis=0)
    p = jnp.exp(s - jnp.max(s, axis=-1, keepdims=True))
    out = jnp.dot(p.astype(BF16), v_scr[pl.ds(k0, span), :], preferred_element_type=F32) / jnp.sum(p, axis=-1, keepdims=True)
    for hh in range(NSA_HEADS):
        o_ref[0, :, hh * LANE:(hh + 1) * LANE] = out[hh * tq:(hh + 1) * tq]


def _win_attn(qn, kv, tq):
    b, t, _ = qn.shape
    assert t >= WINDOW + tq
    rows = NSA_HEADS * tq
    return pl.pallas_call(
        functools.partial(_win_attn_kernel, tq=tq),
        grid=(b, t // tq),
        in_specs=[pl.BlockSpec((1, tq, NSA_HEADS * LANE), lambda i, j: (i, j, 0)),
                  pl.BlockSpec((1, t, 2 * LANE), lambda i, j: (i, 0, 0))],
        out_specs=pl.BlockSpec((1, tq, NSA_HEADS * LANE), lambda i, j: (i, j, 0)),
        out_shape=jax.ShapeDtypeStruct((b, t, NSA_HEADS * LANE), F32),
        scratch_shapes=[pltpu.VMEM((rows, LANE), BF16), pltpu.VMEM((t, LANE), BF16), pltpu.VMEM((t, LANE), BF16)],
        compiler_params=_cparams("parallel", "arbitrary"),
        name="win_attn",
    )(qn, kv)


def _dec_softmax(scores, vals):
    m = scores[0].max(axis=-1, keepdims=True)
    for s in scores[1:]:
        m = jnp.maximum(m, s.max(axis=-1, keepdims=True))
    den = jnp.zeros_like(m)
    out = jnp.zeros((m.shape[0], LANE), F32)
    for s, v in zip(scores, vals):
        p = jnp.exp(s - m)
        den = den + p.sum(axis=-1, keepdims=True)
        out = out + (p * v if s.shape[1] == 1 else _dot_nt(p.astype(BF16), v))
    return out / den


def _nsa_decode_kernel(pt_ref, *refs, past_len, n_sel):
    pages = refs[:PAGES]
    (qn_ref, kc_ref, vc_ref, kvs_ref, kvw_ref, win_ref, cov_ref,
     ocmp_ref, osel_ref, owin_ref, wnew_ref) = refs[PAGES:]
    nh = NSA_HEADS
    q = jnp.concatenate([qn_ref[0, :, hh * LANE:(hh + 1) * LANE] for hh in range(nh)], axis=0)
    qb = q.astype(BF16)
    cidx = lax.broadcasted_iota(I32, (nh, CHUNKS), 1)
    valid = (cidx * CMP_STRIDE + CMP_LEN - 1 <= past_len) & (cidx < CHUNKS - 1)
    p = _softmax_rows(_dot_nt(qb, kc_ref[0].astype(BF16)), valid)
    o_cmp = jnp.dot(p.astype(BF16), vc_ref[0].astype(BF16), preferred_element_type=F32)
    imp = None
    for h in range(NSA_KV_HEADS):
        ps = jnp.sum(p[h * NSA_GROUP:(h + 1) * NSA_GROUP], axis=0, keepdims=True)
        term = _dot_nt(cov_ref[h], jnp.broadcast_to(ps, (LANE, CHUNKS)), lax.Precision.HIGHEST)
        imp = term if imp is None else imp + term
    blk = lax.broadcasted_iota(I32, (LANE, LANE), 0) & 63
    cur = past_len // SEL_BLOCK
    forced = (blk == 0) | (blk == cur) | (blk == cur - 1)
    imp = jnp.where(forced, FORCE_SCORE, jnp.where(blk * SEL_BLOCK <= past_len, imp, -FORCE_SCORE))
    sel_t = jnp.concatenate([_select_blocks(imp[0:64], n_sel, min(SEL_TOP_N, n_sel)),
                             _select_blocks(imp[64:128], n_sel, min(SEL_TOP_N, n_sel))], axis=0)
    sel = sel_t.T[0:1, :]
    head_of_row = lax.broadcasted_iota(I32, (nh, 1), 0) // NSA_GROUP

    def picked(s):
        return jnp.where(head_of_row == 0, sel[:, s:s + 1], sel[:, 64 + s:64 + s + 1])

    first_half = lax.broadcasted_iota(I32, (nh, PAGE_ROWS), 1) < SEL_BLOCK
    scores, vals = [], []
    for pg, pr in enumerate(pages):
        s = jnp.dot(qb, pr[0, 0:LANE, :].astype(BF16), preferred_element_type=F32)
        ok = jnp.where(first_half, picked(2 * pg), picked(2 * pg + 1)) > 0.5
        scores.append(jnp.where(ok, s, NEG))
        vals.append(pr[0, LANE:2 * LANE, :].astype(BF16))
    s_new = jnp.sum(q * kvs_ref[0, :, 0:LANE], axis=-1, keepdims=True)
    scores.append(jnp.where(picked(past_len // SEL_BLOCK) > 0.5, s_new, NEG))
    vals.append(kvs_ref[0, :, LANE:2 * LANE])
    o_sel = _dec_softmax(scores, vals)
    n_win = win_ref.shape[2]
    kpos = past_len - n_win + lax.broadcasted_iota(I32, (nh, n_win), 1)
    rel = past_len - kpos
    s_win = jnp.dot(qb, win_ref[0, 0:LANE, :].astype(BF16), preferred_element_type=F32)
    s_win = jnp.where((kpos >= 0) & (rel >= 0) & (rel <= WINDOW), s_win, NEG)
    s_new = jnp.sum(q * kvw_ref[0, :, 0:LANE], axis=-1, keepdims=True)
    o_win = _dec_softmax([s_win, s_new], [win_ref[0, LANE:2 * LANE, :].astype(BF16), kvw_ref[0, :, LANE:2 * LANE]])
    for hh in range(nh):
        ocmp_ref[0, :, hh * LANE:(hh + 1) * LANE] = o_cmp[hh:hh + 1]
        osel_ref[0, :, hh * LANE:(hh + 1) * LANE] = o_sel[hh:hh + 1]
        owin_ref[0, :, hh * LANE:(hh + 1) * LANE] = o_win[hh:hh + 1]
    for c in range(0, 2 * LANE, LANE):
        new_col = jnp.broadcast_to(kvw_ref[0, :, c:c + LANE], (LANE, LANE)).T
        shifted = pltpu.roll(win_ref[0, c:c + LANE, :], n_win - 1, 1)
        lane = lax.broadcasted_iota(I32, shifted.shape, 1)
        wnew_ref[0, c:c + LANE, :] = jnp.where(lane == n_win - 1, jnp.concatenate([new_col] * (n_win // LANE), axis=1), shifted)


def _nsa_decode(qn, kc, vc, sel_pages, page_ids, kvs_new, kvw_new, win_state):
    b = qn.shape[0]
    n_win = win_state.shape[2]
    past_len = PAGES * PAGE_ROWS
    n_sel = -(-(past_len + 1) // SEL_BLOCK)
    cov = _cover_tables(n_sel)

    def per_b(shape):
        nd = len(shape)
        return pl.BlockSpec((1,) + shape[1:], lambda i, pt: (i,) + (0,) * (nd - 1))

    slots = NSA_HEADS * LANE
    grid_spec = pltpu.PrefetchScalarGridSpec(
        num_scalar_prefetch=1,
        grid=(b,),
        in_specs=[pl.BlockSpec((1, 2 * LANE, PAGE_ROWS), (lambda i, pt, p=p: (pt[i * PAGES + p], 0, 0))) for p in range(PAGES)]
        + [per_b(qn.shape), per_b(kc.shape), per_b(vc.shape), per_b(kvs_new.shape), per_b(kvw_new.shape), per_b(win_state.shape),
           _const_spec(cov.shape)],
        out_specs=[per_b((b, 1, slots))] * 3 + [per_b(win_state.shape)],
    )
    return pl.pallas_call(
        functools.partial(_nsa_decode_kernel, past_len=past_len, n_sel=n_sel),
        grid_spec=grid_spec,
        out_shape=[jax.ShapeDtypeStruct((b, 1, slots), F32)] * 3 + [jax.ShapeDtypeStruct(win_state.shape, F32)],
        compiler_params=_cparams("parallel"),
        name="nsa_decode",
    )(page_ids, *([sel_pages] * PAGES), qn, kc, vc, kvs_new, kvw_new, win_state, cov)


def _dot_tn(a, b):
    return lax.dot_general(a, b, (((0,), (0,)), ((), ())), preferred_element_type=F32)


def _cumsum_table():
    r = np.arange(LANE)
    return jnp.asarray(((r[:, None] // GLA_SUB == r[None, :] // GLA_SUB) & (r[None, :] <= r[:, None])).astype(np.float32))


def _gla_kernel(*refs, t, t_valid, has_state):
    if has_state:
        qg_ref, kg_ref, vg_ref, misc_ref, wa_ref, ba_ref, lt_ref, s0_ref, o_ref, s_ref, b_scr, st_scr = refs
    else:
        qg_ref, kg_ref, vg_ref, misc_ref, wa_ref, ba_ref, lt_ref, o_ref, s_ref, b_scr, st_scr = refs
    z = jnp.dot(misc_ref[0], wa_ref[...], preferred_element_type=F32, precision=lax.Precision.HIGHEST) + ba_ref[...]
    la = (jnp.minimum(z, 0.0) - jnp.log1p(jnp.exp(-jnp.abs(z)))) * (1.0 / GLA_TAU)
    if t_valid < t:
        la = jnp.where(lax.broadcasted_iota(I32, la.shape, 0) < t_valid, la, 0.0)
    tile = min(t, LANE)
    for r in range(0, t, tile):
        b_scr[r:r + tile, :] = jnp.dot(lt_ref[0:tile, 0:tile], la[r:r + tile, :], preferred_element_type=F32,
                                       precision=lax.Precision.HIGHEST)
    pairs = GLA_HEADS // 2
    for p in range(pairs):
        if has_state:
            st_scr[p] = s0_ref[0, 2 * p:2 * p + 2].reshape(2 * GLA_DK, GLA_DV).T
        else:
            st_scr[p] = jnp.zeros((GLA_DV, LANE), F32)
    head_a = lax.broadcasted_iota(I32, (GLA_SUB, LANE), 1) < GLA_DK
    row = lax.broadcasted_iota(I32, (GLA_SUB, LANE), 0)

    def pair_chunk(q, k, v, b, st):
        b_last = b[GLA_SUB - 1:GLA_SUB, :]
        st_b = st.astype(BF16)
        qe = q * jnp.exp(b)
        o_a = _dot_nt(jnp.where(head_a, qe, 0.0).astype(BF16), st_b)
        o_b = _dot_nt(jnp.where(head_a, 0.0, qe).astype(BF16), st_b)
        for j in range(GLA_SUB):
            w = q * k[j:j + 1, :] * jnp.exp(jnp.minimum(b - b[j:j + 1, :], 0.0))
            w = jnp.where(row >= j, w, 0.0)
            a_a = jnp.sum(jnp.where(head_a, w, 0.0), axis=-1, keepdims=True)
            a_b = jnp.sum(jnp.where(head_a, 0.0, w), axis=-1, keepdims=True)
            o_a = o_a + a_a * v[j:j + 1, 0:GLA_DV]
            o_b = o_b + a_b * v[j:j + 1, GLA_DV:2 * GLA_DV]
        kd = k * jnp.exp(b_last - b)
        upd = (_dot_tn(v[:, 0:GLA_DV].astype(BF16), jnp.where(head_a, kd, 0.0).astype(BF16))
               + _dot_tn(v[:, GLA_DV:2 * GLA_DV].astype(BF16), jnp.where(head_a, 0.0, kd).astype(BF16)))
        return o_a, o_b, jnp.exp(b_last) * st + upd

    def chunk(c, carry):
        r0 = pl.multiple_of(c * GLA_SUB, GLA_SUB)
        for p in range(pairs):
            o_a, o_b, st_new = pair_chunk(qg_ref[0, pl.ds(r0, GLA_SUB), p * LANE:(p + 1) * LANE],
                                          kg_ref[0, pl.ds(r0, GLA_SUB), p * LANE:(p + 1) * LANE],
                                          vg_ref[0, pl.ds(r0, GLA_SUB), 2 * p * GLA_DV:2 * (p + 1) * GLA_DV],
                                          b_scr[pl.ds(r0, GLA_SUB), p * LANE:(p + 1) * LANE], st_scr[p])
            o_ref[0, pl.ds(r0, GLA_SUB), 2 * p * GLA_DV:(2 * p + 1) * GLA_DV] = o_a
            o_ref[0, pl.ds(r0, GLA_SUB), (2 * p + 1) * GLA_DV:(2 * p + 2) * GLA_DV] = o_b
            st_scr[p] = st_new
        return carry

    lax.fori_loop(0, t // GLA_SUB, chunk, 0)
    for p in range(pairs):
        s_ref[0, 2 * p:2 * p + 2] = st_scr[p].T.reshape(2, GLA_DK, GLA_DV)


def _gla(qg, kg, vg, misc, wa_pad, ba, s0, t_valid):
    b, t, _ = qg.shape
    has_state = s0 is not None
    hk = GLA_HEADS * GLA_DK

    def per_b(shape):
        nd = len(shape)
        return pl.BlockSpec((1,) + shape[1:], lambda i: (i,) + (0,) * (nd - 1))

    def const(shape):
        nd = len(shape)
        return pl.BlockSpec(shape, lambda i: (0,) * nd)

    in_specs = [per_b(qg.shape), per_b(kg.shape), per_b(vg.shape), per_b(misc.shape),
                const((LANE, hk)), const((1, hk)), const((LANE, LANE))]
    args = [qg, kg, vg, misc, wa_pad, ba.reshape(1, -1), _cumsum_table()]
    state_shape = (b, GLA_HEADS, GLA_DK, GLA_DV)
    if has_state:
        in_specs.append(per_b(state_shape))
        args.append(s0)
    return pl.pallas_call(
        functools.partial(_gla_kernel, t=t, t_valid=t_valid, has_state=has_state),
        grid=(b,),
        in_specs=in_specs,
        out_specs=[per_b(vg.shape), per_b(state_shape)],
        out_shape=[jax.ShapeDtypeStruct(vg.shape, F32), jax.ShapeDtypeStruct(state_shape, F32)],
        scratch_shapes=[pltpu.VMEM((t, hk), F32), pltpu.VMEM((GLA_HEADS // 2, GLA_DV, LANE), F32)],
        compiler_params=_cparams("parallel"),
        name="gla",
    )(*args)


def _gate_expand_table():
    out = np.zeros((3, LANE, NSA_HEADS * LANE), np.float32)
    for hh in range(NSA_HEADS):
        for j in range(3):
            out[j, MISC_GN + 3 * hh + j, hh * LANE:(hh + 1) * LANE] = 1.0
    return jnp.asarray(out, dtype=BF16)


def _pad_br_a(w_br_a):
    zero = jnp.zeros((HEAD_DIM, w_br_a.shape[1]), w_br_a.dtype)
    parts = []
    for hh in range(NSA_HEADS):
        wh = w_br_a[hh * HEAD_DIM:(hh + 1) * HEAD_DIM]
        parts += [wh, zero] if hh // NSA_GROUP == 0 else [zero, wh]
    return jnp.concatenate(parts, axis=0).astype(BF16)


def _layer_norm(v, g, b):
    mu = jnp.mean(v, axis=-1, keepdims=True)
    var = jnp.mean(jnp.square(v - mu), axis=-1, keepdims=True)
    return (v - mu) * lax.rsqrt(var + LN_EPS) * g + b


def _mixer_tail_kernel(ocmp_ref, osel_ref, owin_ref, misc_ref, ogla_ref, rg_ref, gm_ref, x_ref, gate_ref, scf_ref, shf_ref,
                       ex_ref, ng_ref, wa_ref, wb_ref, wo_ref, lg_ref, lb_ref, x1_ref, xm_ref):
    sig = jax.nn.sigmoid(misc_ref[0])
    sig_hi = sig.astype(BF16)
    sig_lo = (sig - sig_hi.astype(F32)).astype(BF16)
    o_nsa = None
    for j, ref in enumerate((ocmp_ref, osel_ref, owin_ref)):
        g = (jnp.dot(sig_hi, ex_ref[j], preferred_element_type=F32) + jnp.dot(sig_lo, ex_ref[j], preferred_element_type=F32))
        o_nsa = g * ref[0] if o_nsa is None else o_nsa + g * ref[0]
    br_a = _bdot(o_nsa, wa_ref[...])
    heads = []
    for h in range(GLA_HEADS):
        seg = ogla_ref[0, :, h * GLA_DV:(h + 1) * GLA_DV]
        mu = jnp.mean(seg, axis=-1, keepdims=True)
        var = jnp.mean(jnp.square(seg - mu), axis=-1, keepdims=True)
        r = rg_ref[0, :, h * GLA_DV:(h + 1) * GLA_DV]
        heads.append((seg - mu) * lax.rsqrt(var + LN_EPS) * ng_ref[:, h * GLA_DV:(h + 1) * GLA_DV] * (r * jax.nn.sigmoid(r)))
    br_b = _bdot(jnp.concatenate(heads, axis=1), wb_ref[...])
    gm_a = jax.nn.sigmoid(gm_ref[0, :, 0:D_MODEL])
    gm_b = jax.nn.sigmoid(gm_ref[0, :, D_MODEL:2 * D_MODEL])
    y = _bdot(gm_a * br_a + gm_b * br_b, wo_ref[...])
    x1 = _layer_norm(DN_ALPHA * x_ref[0] + gate_ref[0] * y, lg_ref[...], lb_ref[...])
    x1_ref[0] = x1
    xm_ref[0] = x1 * (1.0 + scf_ref[0]) + shf_ref[0]


def _mixer_tail(ocmp, osel, owin, misc, ogla, rg, gm, x, gate_m, scale_f, shift_f, consts, tm):
    b, t, d = x.shape
    per_tok = gate_m.shape[1] != 1

    def tok(w):
        return pl.BlockSpec((1, tm, w), lambda i, j: (i, j, 0))

    mod_spec = tok(d) if per_tok else pl.BlockSpec((1, 1, d), lambda i, j: (i, 0, 0))

    def const(a):
        nd = a.ndim
        return pl.BlockSpec(a.shape, lambda i, j: (0,) * nd)

    return pl.pallas_call(
        _mixer_tail_kernel,
        grid=(b, t // tm),
        in_specs=[tok(NSA_HEADS * LANE)] * 3 + [tok(LANE), tok(GLA_HEADS * GLA_DV), tok(GLA_HEADS * GLA_DV), tok(2 * d), tok(d),
                                               mod_spec, mod_spec, mod_spec] + [const(c) for c in consts],
        out_specs=[tok(d), tok(d)],
        out_shape=[jax.ShapeDtypeStruct((b, t, d), F32)] * 2,
        compiler_params=_cparams("parallel", "parallel"),
        name="mixer_tail",
    )(ocmp, osel, owin, misc, ogla, rg, gm, x, gate_m, scale_f, shift_f, *consts)


ROUTE_TILE = LANE


def _first_index(hit, iota, size, axis):
    return jnp.min(jnp.where(hit, iota, size), axis=axis, keepdims=True)


def _router_kernel(xm_ref, wr_ref, bias_ref, tri_ref, eidx_ref, rank_ref, wrow_ref, cnt_ref, carry_ref):
    i = pl.program_id(0)
    tm = ROUTE_TILE
    per = N_EXPERTS // N_GROUPS

    @pl.when(i == 0)
    def _():
        carry_ref[...] = jnp.zeros_like(carry_ref)

    logits = _dot_nt(wr_ref[...], xm_ref[...], lax.Precision.HIGHEST)
    s = jax.nn.sigmoid(logits)
    sb = s + bias_ref[...]
    sb3 = sb.reshape(N_GROUPS, per, tm)
    in_grp = lax.broadcasted_iota(I32, sb3.shape, 1)
    m1 = jnp.max(sb3, axis=1, keepdims=True)
    first = _first_index(sb3 == m1, in_grp, per, 1)
    m2 = jnp.max(jnp.where(in_grp == first, NEG, sb3), axis=1, keepdims=True)
    gs = (m1 + m2).reshape(N_GROUPS, tm)
    g_iota = lax.broadcasted_iota(I32, gs.shape, 0)
    g_keep = jnp.zeros(gs.shape, jnp.bool_)
    for _ in range(TOPK_GROUPS):
        pick = g_iota == _first_index(gs == jnp.max(gs, axis=0, keepdims=True), g_iota, N_GROUPS, 0)
        g_keep = g_keep | pick
        gs = jnp.where(pick, NEG, gs)
    sbm = jnp.where(g_keep.reshape(N_GROUPS, 1, tm), sb3, NEG).reshape(N_EXPERTS, tm)
    e_iota = lax.broadcasted_iota(I32, sbm.shape, 0)
    idxs, sels = [], []
    onehot = jnp.zeros(sbm.shape, F32)
    for _ in range(TOP_K):
        idx = _first_index(sbm == jnp.max(sbm, axis=0, keepdims=True), e_iota, N_EXPERTS, 0)
        pick = e_iota == idx
        idxs.append(idx)
        sels.append(jnp.sum(jnp.where(pick, s, 0.0), axis=0, keepdims=True))
        sbm = jnp.where(pick, NEG, sbm)
        onehot = onehot + jnp.where(pick, 1.0, 0.0)
    sel = jnp.concatenate(sels, axis=0)
    wts = sel / jnp.sum(sel, axis=0, keepdims=True) * ROUTED_SCALE
    carry = carry_ref[...]
    before = carry + jnp.dot(onehot.astype(BF16), tri_ref[...], preferred_element_type=F32)
    ranks = [jnp.sum(jnp.where(e_iota == idx, before, 0.0), axis=0, keepdims=True) for idx in idxs]
    eidx_ref[...] = jnp.concatenate(idxs, axis=0)
    rank_ref[...] = jnp.concatenate(ranks, axis=0).astype(I32)
    wrow_ref[...] = jnp.concatenate([wts, jnp.zeros((LANE - TOP_K, tm), F32)], axis=0).T
    carry = carry + jnp.sum(onehot, axis=1, keepdims=True)
    carry_ref[...] = carry
    cnt_ref[...] = carry


def _router(xm, w_router, router_bias):
    n, d = xm.shape
    tm = ROUTE_TILE
    r = np.arange(tm)
    tri = jnp.asarray((r[:, None] < r[None, :]).astype(np.float32), dtype=BF16)
    return pl.pallas_call(
        _router_kernel,
        grid=(n // tm,),
        in_specs=[pl.BlockSpec((tm, d), lambda i: (i, 0)),
                  pl.BlockSpec((N_EXPERTS, d), lambda i: (0, 0)),
                  pl.BlockSpec((N_EXPERTS, 1), lambda i: (0, 0)),
                  pl.BlockSpec((tm, tm), lambda i: (0, 0))],
        out_specs=[pl.BlockSpec((TOP_K, tm), lambda i: (0, i)),
                   pl.BlockSpec((TOP_K, tm), lambda i: (0, i)),
                   pl.BlockSpec((tm, LANE), lambda i: (i, 0)),
                   pl.BlockSpec((N_EXPERTS, LANE), lambda i: (0, 0))],
        out_shape=[jax.ShapeDtypeStruct((TOP_K, n), I32), jax.ShapeDtypeStruct((TOP_K, n), I32),
                   jax.ShapeDtypeStruct((n, LANE), F32), jax.ShapeDtypeStruct((N_EXPERTS, LANE), F32)],
        scratch_shapes=[pltpu.VMEM((N_EXPERTS, LANE), F32)],
        compiler_params=_cparams("arbitrary"),
        name="router",
    )(xm, w_router.T, router_bias.reshape(N_EXPERTS, 1), tri)


def _dest_kernel(eidx_ref, rank_ref, start_ref, dest_ref):
    e_iota = lax.broadcasted_iota(I32, (N_EXPERTS, ROUTE_TILE), 0)
    start = start_ref[...]
    rows = [jnp.sum(jnp.where(e_iota == eidx_ref[k:k + 1, :], start, 0.0), axis=0, keepdims=True) for k in range(TOP_K)]
    dest_ref[0] = jnp.concatenate(rows, axis=0).astype(I32) + rank_ref[...]


def _dest(eidx, rank, pad_start):
    n = eidx.shape[1]
    tm = ROUTE_TILE
    return pl.pallas_call(
        _dest_kernel,
        grid=(n // tm,),
        in_specs=[pl.BlockSpec((TOP_K, tm), lambda i: (0, i)),
                  pl.BlockSpec((TOP_K, tm), lambda i: (0, i)),
                  pl.BlockSpec((N_EXPERTS, 1), lambda i: (0, 0))],
        out_specs=pl.BlockSpec((1, TOP_K, tm), lambda i: (i, 0, 0)),
        out_shape=jax.ShapeDtypeStruct((n // tm, TOP_K, tm), I32),
        compiler_params=_cparams("parallel"),
        name="dest",
    )(eidx, rank, pad_start.astype(F32).reshape(N_EXPERTS, 1))


def _row_copy(src_ref, src_row, dst_ref, dst_row, sem):
    return pltpu.make_async_copy(src_ref.at[pl.ds(src_row, 1)], dst_ref.at[pl.ds(dst_row, 1)], sem)


def _dispatch_kernel(dest_ref, pad_ref, nb_ref, xm_ref, xs_ref, tile_ref, zero_ref, load_sems, sems, sem):
    i = pl.program_id(0)
    tm = ROUTE_TILE
    bm = EXPERT_BLOCK
    last = pl.num_programs(0) - 1

    def load(tile, slot):
        return pltpu.make_async_copy(xm_ref.at[pl.ds(tile * tm, tm)], tile_ref.at[slot], load_sems.at[slot])

    @pl.when(i == 0)
    def _():
        load(0, 0).start()

    @pl.when(i < last)
    def _():
        load(i + 1, (i + 1) % 3).start()

    load(i, i % 3).wait()
    src = tile_ref.at[i % 3]

    def start_row(r, c):
        for k in range(TOP_K):
            _row_copy(src, r, xs_ref, dest_ref[0, k, r], sems.at[i % 2]).start(priority=k % 2)
        return c

    def wait_step(slot):
        for _ in range(TOP_K):
            pltpu.make_async_copy(tile_ref.at[0], xs_ref.at[pl.ds(0, tm)], sems.at[slot]).wait()

    lax.fori_loop(0, tm, start_row, 0)

    @pl.when(i >= 1)
    def _():
        wait_step((i - 1) % 2)

    @pl.when(i == last)
    def _():
        wait_step(i % 2)
        zero_ref[...] = jnp.zeros_like(zero_ref)

        def per_expert(e, c):
            lo, hi = pad_ref[0, e], pad_ref[1, e]
            lax.fori_loop(lo, hi, lambda r, cc: (_row_copy(zero_ref, 0, xs_ref, r, sem).start(), cc)[1], 0)
            lax.fori_loop(lo, hi, lambda r, cc: (_row_copy(zero_ref, 0, xs_ref, r, sem).wait(), cc)[1], 0)
            return c

        lax.fori_loop(0, N_EXPERTS, per_expert, 0)

        def tail_copy(blk):
            return pltpu.make_async_copy(zero_ref, xs_ref.at[pl.ds(blk * bm, bm)], sem)

        n_blocks = xs_ref.shape[0] // bm
        lax.fori_loop(nb_ref[0], n_blocks, lambda blk, c: (tail_copy(blk).start(), c)[1], 0)
        lax.fori_loop(nb_ref[0], n_blocks, lambda blk, c: (tail_copy(blk).wait(), c)[1], 0)


def _dispatch(xm, dest_tiles, pad_range, n_used, n_rows):
    n, d = xm.shape
    tm = ROUTE_TILE
    return pl.pallas_call(
        _dispatch_kernel,
        grid=(n // tm,),
        in_specs=[pl.BlockSpec((1, TOP_K, tm), lambda i: (i, 0, 0), memory_space=pltpu.SMEM),
                  pl.BlockSpec(memory_space=pltpu.SMEM),
                  pl.BlockSpec(memory_space=pltpu.SMEM),
                  pl.BlockSpec(memory_space=pl.ANY)],
        out_specs=pl.BlockSpec(memory_space=pl.ANY),
        out_shape=jax.ShapeDtypeStruct((n_rows, d), F32),
        scratch_shapes=[pltpu.VMEM((3, tm, d), F32), pltpu.VMEM((EXPERT_BLOCK, d), F32), pltpu.SemaphoreType.DMA((3,)),
                        pltpu.SemaphoreType.DMA((2,)), pltpu.SemaphoreType.DMA(())],
        compiler_params=_cparams("arbitrary"),
        name="dispatch",
    )(dest_tiles, pad_range, n_used, xm)


def _experts_kernel(first_ref, cnt_ref, nb_ref, xs_ref, wg_ref, wu_ref, wd_ref, ys_ref,
                    xbuf, ybuf, wg_s, wu_s, wd_s, in_sems, out_sems):
    e = pl.program_id(0)
    bm = EXPERT_BLOCK
    total = nb_ref[0]
    n_blocks = xs_ref.shape[0] // bm

    def x_copy(g, slot):
        return pltpu.make_async_copy(xs_ref.at[pl.ds(g * bm, bm)], xbuf.at[slot], in_sems.at[slot])

    def y_copy(g, slot):
        return pltpu.make_async_copy(ybuf.at[slot], ys_ref.at[pl.ds(g * bm, bm)], out_sems.at[slot])

    depth = xbuf.shape[0]

    @pl.when(e == 0)
    def _():
        for g0 in range(depth - 1):
            @pl.when(g0 < total)
            def _():
                x_copy(g0, g0).start()

    @pl.when(cnt_ref[e] > 0)
    def _():
        wg_s[...] = wg_ref[0].astype(BF16)
        wu_s[...] = wu_ref[0].astype(BF16)
        wd_s[...] = wd_ref[0].astype(BF16)

    def block(c, carry):
        g = first_ref[e] + c
        slot = g % depth
        x_copy(g, slot).wait()

        @pl.when(g + depth - 1 < total)
        def _():
            x_copy(g + depth - 1, (g + depth - 1) % depth).start()

        x = xbuf[slot].astype(BF16)
        gate = jnp.dot(x, wg_s[...], preferred_element_type=F32)
        up = jnp.dot(x, wu_s[...], preferred_element_type=F32)
        y = jnp.dot((gate * jax.nn.sigmoid(gate) * up).astype(BF16), wd_s[...], preferred_element_type=F32)

        @pl.when(g >= depth)
        def _():
            y_copy(g - depth, slot).wait()

        ybuf[slot] = y
        y_copy(g, slot).start()
        return carry

    lax.fori_loop(0, cnt_ref[e], block, 0)

    @pl.when(e == pl.num_programs(0) - 1)
    def _():
        for back in range(depth, 0, -1):
            @pl.when(total >= back)
            def _():
                y_copy(total - back, (total - back) % depth).wait()

        ybuf[0] = jnp.zeros((bm, ybuf.shape[2]), F32)
        lax.fori_loop(total, n_blocks, lambda g, c: (y_copy(g, 0).start(), c)[1], 0)
        lax.fori_loop(total, n_blocks, lambda g, c: (y_copy(g, 0).wait(), c)[1], 0)


def _experts(xs, first_block, n_block, n_used, w_gate, w_up, w_down):
    n_rows, d = xs.shape
    bm = EXPERT_BLOCK
    n_exp, _, f = w_gate.shape
    grid_spec = pltpu.PrefetchScalarGridSpec(
        num_scalar_prefetch=3,
        grid=(n_exp,),
        in_specs=[pl.BlockSpec(memory_space=pl.ANY),
                  pl.BlockSpec((1, d, f), lambda e, *_: (e, 0, 0)),
                  pl.BlockSpec((1, d, f), lambda e, *_: (e, 0, 0)),
                  pl.BlockSpec((1, f, d), lambda e, *_: (e, 0, 0))],
        out_specs=pl.BlockSpec(memory_space=pl.ANY),
        scratch_shapes=[pltpu.VMEM((EXPERT_RING, bm, d), F32), pltpu.VMEM((EXPERT_RING, bm, d), F32),
                        pltpu.VMEM((d, f), BF16), pltpu.VMEM((d, f), BF16), pltpu.VMEM((f, d), BF16),
                        pltpu.SemaphoreType.DMA((EXPERT_RING,)), pltpu.SemaphoreType.DMA((EXPERT_RING,))],
    )
    return pl.pallas_call(
        _experts_kernel,
        grid_spec=grid_spec,
        out_shape=jax.ShapeDtypeStruct((n_rows, d), F32),
        compiler_params=_cparams("arbitrary"),
        name="experts",
    )(first_block, n_block, n_used, xs, w_gate, w_up, w_down)


def _combine_kernel(dest_ref, dnext_ref, ys_ref, wrow_ref, xm_ref, sg_ref, su_ref, sd_ref, out_ref, buf_ref, sems):
    i = pl.program_id(0)
    tm = ROUTE_TILE
    slot = i % 2

    def issue(d_ref, s):
        def start_row(r, c):
            for k in range(TOP_K):
                _row_copy(ys_ref, d_ref[0, k, r], buf_ref.at[s, k], r, sems.at[s]).start(priority=k % 2)
            return c

        lax.fori_loop(0, tm, start_row, 0)

    @pl.when(i == 0)
    def _():
        issue(dest_ref, 0)

    @pl.when(i < pl.num_programs(0) - 1)
    def _():
        issue(dnext_ref, 1 - slot)

    x = xm_ref[...].astype(BF16)
    g = jnp.dot(x, sg_ref[...], preferred_element_type=F32)
    u = jnp.dot(x, su_ref[...], preferred_element_type=F32)
    shared = jnp.dot((g * jax.nn.sigmoid(g) * u).astype(BF16), sd_ref[...], preferred_element_type=F32)
    for k in range(TOP_K):
        pltpu.make_async_copy(ys_ref.at[pl.ds(0, tm)], buf_ref.at[slot, k], sems.at[slot]).wait()
    w = wrow_ref[...]
    routed = w[:, 0:1] * buf_ref[slot, 0]
    for k in range(1, TOP_K):
        routed = routed + w[:, k:k + 1] * buf_ref[slot, k]
    out_ref[...] = routed + shared


def _combine(ys, dest_tiles, wrow, xm, ws_gate, ws_up, ws_down):
    n, d = xm.shape
    tm = ROUTE_TILE
    f = ws_gate.shape[1]
    return pl.pallas_call(
        _combine_kernel,
        grid=(n // tm,),
        in_specs=[pl.BlockSpec((1, TOP_K, tm), lambda i: (i, 0, 0), memory_space=pltpu.SMEM),
                  pl.BlockSpec((1, TOP_K, tm), lambda i: (jnp.minimum(i + 1, n // tm - 1), 0, 0), memory_space=pltpu.SMEM),
                  pl.BlockSpec(memory_space=pl.ANY),
                  pl.BlockSpec((tm, LANE), lambda i: (i, 0)),
                  pl.BlockSpec((tm, d), lambda i: (i, 0)),
                  pl.BlockSpec((d, f), lambda i: (0, 0)),
                  pl.BlockSpec((d, f), lambda i: (0, 0)),
                  pl.BlockSpec((f, d), lambda i: (0, 0))],
        out_specs=pl.BlockSpec((tm, d), lambda i: (i, 0)),
        out_shape=jax.ShapeDtypeStruct((n, d), F32),
        scratch_shapes=[pltpu.VMEM((2, TOP_K, tm, d), F32), pltpu.SemaphoreType.DMA((2,))],
        compiler_params=_cparams("arbitrary"),
        name="combine",
    )(dest_tiles, dest_tiles, ys, wrow, xm, ws_gate.astype(BF16), ws_up.astype(BF16), ws_down.astype(BF16))


def _final_ln_kernel(x1_ref, moe_ref, gate_ref, g_ref, b_ref, y_ref):
    y_ref[0] = _layer_norm(DN_ALPHA * x1_ref[0] + gate_ref[0] * moe_ref[...], g_ref[...], b_ref[...])


def _final_ln(x1, moe, row0, gate_f, ln_g, ln_b, tm):
    b, t, d = x1.shape
    per_tok = gate_f.shape[1] != 1
    tok = pl.BlockSpec((1, tm, d), lambda i, j: (i, j, 0))
    mod_spec = tok if per_tok else pl.BlockSpec((1, 1, d), lambda i, j: (i, 0, 0))
    vec = pl.BlockSpec((1, d), lambda i, j: (0, 0))
    moe_spec = pl.BlockSpec((tm, d), lambda i, j: (row0 // tm + i * (t // tm) + j, 0))
    return pl.pallas_call(
        _final_ln_kernel,
        grid=(b, t // tm),
        in_specs=[tok, moe_spec, mod_spec, vec, vec],
        out_specs=tok,
        out_shape=jax.ShapeDtypeStruct((b, t, d), F32),
        compiler_params=_cparams("parallel", "parallel"),
        name="final_ln",
    )(x1, moe, gate_f, ln_g.reshape(1, d), ln_b.reshape(1, d))


def _moe(xm, w_router, router_bias, w_e_gate, w_e_up, w_e_down, w_s_gate, w_s_up, w_s_down):
    n = xm.shape[0]
    eidx, rank, wrow, cnt = _router(xm, w_router, router_bias)
    counts = cnt[:, 0].astype(I32)
    padded = (counts + EXPERT_BLOCK - 1) // EXPERT_BLOCK * EXPERT_BLOCK
    pad_end = jnp.cumsum(padded)
    pad_start = pad_end - padded
    dest_tiles = _dest(eidx, rank, pad_start)
    n_blocks = -(-(n * TOP_K) // EXPERT_BLOCK) + N_EXPERTS
    n_used = (pad_end[-1:] // EXPERT_BLOCK).astype(I32)
    pad_range = jnp.stack([pad_start + counts, pad_end]).astype(I32)
    xs = _dispatch(xm, dest_tiles, pad_range, n_used, n_blocks * EXPERT_BLOCK)
    ys = _experts(xs, (pad_start // EXPERT_BLOCK).astype(I32), (padded // EXPERT_BLOCK).astype(I32), n_used,
                  w_e_gate, w_e_up, w_e_down)
    return _combine(ys, dest_tiles, wrow, xm, w_s_gate, w_s_up, w_s_down)


def kernel(x_prompt, x_sample, cache_kv_cmp, cache_kv_sel, state_kv_win, state_gla, page_table, c_prompt, c_sample, w_in, b_in, cmp_k_pos, cmp_k_w1, cmp_k_w2, cmp_v_pos, cmp_v_w1, cmp_v_w2, gla_w_a2, gla_b_a, gla_norm_g, w_br_a, w_br_b, w_out, ln1_g, ln1_b, w_ada, b_ada, w_router, router_bias, w_e_gate, w_e_up, w_e_down, w_s_gate, w_s_up, w_s_down, ln2_g, ln2_b):
    bp, tp, d = x_prompt.shape
    nd, td = x_sample.shape[:2]
    n_pool, page_rows = cache_kv_cmp.shape[:2]
    past_len = page_table.shape[1] * page_rows
    assert td == 1 and d == D_MODEL and page_rows == PAGE_ROWS and page_table.shape[1] == PAGES and tp == PAGES * PAGE_ROWS
    kv_w = 2 * NSA_KV_HEADS * HEAD_DIM

    mod = _adaln(jnp.concatenate([c_prompt, c_sample], axis=0), w_ada, b_ada)
    mod_p = [m.reshape(bp, 1, d) for m in jnp.split(mod[:bp], 6, axis=-1)]
    mod_s = [m.reshape(1, nd, d) for m in jnp.split(mod[bp:], 6, axis=-1)]

    w_pack, b_pack = _pack_in_weights(w_in, b_in)
    cmp_wk = _pack_cmp_weights(cmp_k_pos, cmp_k_w1, cmp_k_w2)
    cmp_wv = _pack_cmp_weights(cmp_v_pos, cmp_v_w1, cmp_v_w2)
    wa_pad = jnp.zeros((LANE, GLA_HEADS * GLA_DK), F32).at[MISC_AG:MISC_AG + GLA_GATE_RANK].set(gla_w_a2)
    tail_consts = (_gate_expand_table(), gla_norm_g.reshape(1, -1), _pad_br_a(w_br_a), w_br_b.astype(BF16), w_out.astype(BF16),
                   ln1_g.reshape(1, d), ln1_b.reshape(1, d))

    gm, qn, vg, rg, kvc, kvs, kvw, qg, kg, misc = _inproj(
        x_prompt, mod_p[0], mod_p[1], w_pack, b_pack, _rope_tables(jnp.arange(tp, dtype=I32)), 256)
    kc, vc = _compress(kvc.reshape(bp * PAGES, PAGE_ROWS, kv_w), jnp.arange(bp * PAGES, dtype=I32), cmp_wk, cmp_wv)
    ocmp, sel = _cmp_attn(qn, kc, vc, 256)
    osel = _sel_attn(qn, kvs, sel, 128, 256)
    owin = _win_attn(qn, kvw, 128)
    ogla, gla_p = _gla(qg, kg, vg, misc, wa_pad, gla_b_a, None, tp)
    x1_p, xm_p = _mixer_tail(ocmp, osel, owin, misc, ogla, rg, gm, x_prompt, mod_p[2], mod_p[4], mod_p[3], tail_consts, 256)
    n_win = min(WINDOW, tp)
    outs_p = (kvc.reshape(bp, tp, 2, NSA_KV_HEADS, HEAD_DIM), kvs.reshape(bp, tp, 2, NSA_KV_HEADS, HEAD_DIM),
              kvw[:, tp - n_win:].reshape(bp, n_win, 2, NSA_KV_HEADS, HEAD_DIM), gla_p)

    gm, qn, vg, rg, kvc, kvs, kvw, qg, kg, misc = _inproj(
        x_sample.reshape(1, nd, d), mod_s[0], mod_s[1], w_pack, b_pack, _rope_tables(jnp.full((nd,), past_len, I32)), nd)
    page_ids = page_table.reshape(-1).astype(I32)
    kc, vc = _compress(cache_kv_cmp.reshape(n_pool, PAGE_ROWS, kv_w).transpose(0, 2, 1), page_ids, cmp_wk, cmp_wv,
                       feature_major=True)
    ocmp, osel, owin, win_new = _nsa_decode(
        qn.reshape(nd, 1, -1), kc, vc, cache_kv_sel.reshape(n_pool, PAGE_ROWS, kv_w).transpose(0, 2, 1), page_ids,
        kvs.reshape(nd, 1, kv_w), kvw.reshape(nd, 1, kv_w), state_kv_win.reshape(nd, -1, kv_w).transpose(0, 2, 1))
    win_new = win_new.transpose(0, 2, 1)

    def pad_rows(a):
        return jnp.pad(a.reshape(nd, 1, -1), ((0, 0), (0, GLA_SUB - 1), (0, 0)))

    ogla, gla_s = _gla(pad_rows(qg), pad_rows(kg), pad_rows(vg), pad_rows(misc), wa_pad, gla_b_a, state_gla, 1)
    x1_s, xm_s = _mixer_tail(ocmp.reshape(1, nd, -1), osel.reshape(1, nd, -1), owin.reshape(1, nd, -1), misc,
                             ogla[:, 0].reshape(1, nd, -1), rg, gm, x_sample.reshape(1, nd, d),
                             mod_s[2], mod_s[4], mod_s[3], tail_consts, nd)
    outs_s = (kvc.reshape(nd, 1, 2, NSA_KV_HEADS, HEAD_DIM), kvs.reshape(nd, 1, 2, NSA_KV_HEADS, HEAD_DIM),
              win_new.reshape(state_kv_win.shape), gla_s)

    n_p = bp * tp
    moe = _moe(jnp.concatenate([xm_p.reshape(n_p, d), xm_s.reshape(nd, d)], axis=0),
               w_router, router_bias, w_e_gate, w_e_up, w_e_down, w_s_gate, w_s_up, w_s_down)
    y_p = _final_ln(x1_p, moe, 0, mod_p[5], ln2_g, ln2_b, 256)
    y_s = _final_ln(x1_s, moe, n_p, mod_s[5], ln2_g, ln2_b, nd).reshape(nd, 1, d)
    return (y_p, y_s) + outs_p + outs_s
```

```python
import functools

import numpy as np
import jax
import jax.numpy as jnp
from jax import lax
from jax.experimental import pallas as pl
from jax.experimental.pallas import tpu as pltpu

F32 = jnp.float32
BF16 = jnp.bfloat16
I32 = jnp.int32

D_MODEL = 1024
NSA_HEADS = 8
NSA_KV_HEADS = 2
NSA_GROUP = NSA_HEADS // NSA_KV_HEADS
HEAD_DIM = 64
ROT_DIM = HEAD_DIM // 4
ROPE_THETA = 500000.0
CMP_LEN = 32
CMP_STRIDE = 16
CMP_HIDDEN = 256
SEL_BLOCK = 64
SEL_TOP_N = 16
WINDOW = 512
FORCE_SCORE = 1.0e4
GLA_HEADS = 4
GLA_DK = 64
GLA_DV = 128
GLA_GATE_RANK = 16
GLA_TAU = 16.0
GLA_SUB = 16
N_EXPERTS = 256
TOP_K = 8
N_GROUPS = 8
TOPK_GROUPS = 4
EXPERT_DIM = 256
SHARED_DIM = 256
ROUTED_SCALE = 2.5
EXPERT_BLOCK = 128
EXPERT_RING = 4
DN_ALPHA = 2.0 ** 0.25
LN_EPS = 1e-5
LANE = 128
NEG = -1.0e30
VMEM_LIMIT = 56 * 1024 * 1024

SEG_GM = (0, 2 * D_MODEL)
SEG_QN = (SEG_GM[0] + SEG_GM[1], NSA_HEADS * LANE)
SEG_VG = (SEG_QN[0] + SEG_QN[1], GLA_HEADS * GLA_DV)
SEG_RG = (SEG_VG[0] + SEG_VG[1], GLA_HEADS * GLA_DV)
SEG_KVC = (SEG_RG[0] + SEG_RG[1], 2 * LANE)
SEG_KVS = (SEG_KVC[0] + SEG_KVC[1], 2 * LANE)
SEG_KVW = (SEG_KVS[0] + SEG_KVS[1], 2 * LANE)
SEG_QG = (SEG_KVW[0] + SEG_KVW[1], GLA_HEADS * GLA_DK)
SEG_KG = (SEG_QG[0] + SEG_QG[1], GLA_HEADS * GLA_DK)
SEG_MISC = (SEG_KG[0] + SEG_KG[1], LANE)
IN_PACKED = SEG_MISC[0] + SEG_MISC[1]
MISC_GN = 0
MISC_AG = NSA_HEADS * 3


def _cparams(*sem):
    return pltpu.CompilerParams(dimension_semantics=sem, vmem_limit_bytes=VMEM_LIMIT)


def _bdot(a, b):
    return jnp.dot(a.astype(BF16), b.astype(BF16), preferred_element_type=F32)


def _dot_nt(a, b, precision=None):
    return lax.dot_general(a, b, (((1,), (1,)), ((), ())), preferred_element_type=F32, precision=precision)


def _adaln_kernel(c_ref, w_ref, b_ref, o_ref):
    c = c_ref[...]
    o_ref[...] = _bdot(c * jax.nn.sigmoid(c), w_ref[...]) + b_ref[...]


def _adaln(c, w_ada, b_ada):
    n, d = c.shape
    m = w_ada.shape[1]
    tn = 512
    return pl.pallas_call(
        _adaln_kernel,
        grid=(m // tn,),
        in_specs=[pl.BlockSpec((n, d), lambda j: (0, 0)),
                  pl.BlockSpec((d, tn), lambda j: (0, j)),
                  pl.BlockSpec((1, tn), lambda j: (0, j))],
        out_specs=pl.BlockSpec((n, tn), lambda j: (0, j)),
        out_shape=jax.ShapeDtypeStruct((n, m), F32),
        compiler_params=_cparams("parallel"),
        name="adaln",
    )(c, w_ada, b_ada.reshape(1, m))


def _rope_tables(pos):
    half = ROT_DIM // 2
    inv = jnp.power(ROPE_THETA, -jnp.arange(half, dtype=F32) * 2.0 / ROT_DIM)
    ang = pos.astype(F32)[:, None] * inv[None, :]
    cos, sin = jnp.cos(ang), jnp.sin(ang)
    t = pos.shape[0]
    one = jnp.ones((t, HEAD_DIM - ROT_DIM), F32)
    z8 = jnp.zeros((t, half), F32)
    z48 = jnp.zeros((t, HEAD_DIM - ROT_DIM), F32)
    c = jnp.concatenate([cos, cos, one, cos, cos, one], axis=1)
    s1 = jnp.concatenate([-sin, z8, z48, -sin, z8, z48], axis=1)
    s2 = jnp.concatenate([z8, sin, z48, z8, sin, z48], axis=1)
    return c, s1, s2


def _pack_in_weights(w_in, b_in):
    sizes = (512, 128, 128, 128, 128, 128, 128, 24, 256, 256, 512, 512, 16, 2048)
    offs = np.concatenate([[0], np.cumsum(sizes)])

    def pack(w):
        seg = [w[..., offs[i]:offs[i + 1]] for i in range(len(sizes))]
        q_n, k_c, v_c, k_s, v_s, k_w, v_w, g_n, q_g, k_g, v_g, r_g, a_g, g_m = seg
        zero = jnp.zeros_like(q_n[..., :HEAD_DIM])
        q_slots = []
        for hh in range(NSA_HEADS):
            qh = q_n[..., hh * HEAD_DIM:(hh + 1) * HEAD_DIM] * (HEAD_DIM ** -0.5)
            q_slots += [qh, zero] if hh // NSA_GROUP == 0 else [zero, qh]
        misc_pad = jnp.zeros_like(w[..., :LANE - g_n.shape[-1] - a_g.shape[-1]])
        return jnp.concatenate([g_m] + q_slots + [v_g, r_g, k_c, v_c, k_s, v_s, k_w, v_w,
                                                   q_g * (GLA_DK ** -0.5), k_g, g_n, a_g, misc_pad], axis=-1)

    return pack(w_in).astype(BF16), pack(b_in.reshape(1, -1))


def _inproj_kernel(x_ref, sh_ref, sc_ref, w_ref, b_ref, rc_ref, rs1_ref, rs2_ref,
                   gm_ref, qn_ref, vg_ref, rg_ref, kvc_ref, kvs_ref, kvw_ref, qg_ref, kg_ref, misc_ref):
    h = (x_ref[0] * (1.0 + sc_ref[0]) + sh_ref[0]).astype(BF16)
    rc, rs1, rs2 = rc_ref[...], rs1_ref[...], rs2_ref[...]

    def proj(off, width):
        return jnp.dot(h, w_ref[:, off:off + width], preferred_element_type=F32) + b_ref[:, off:off + width]

    def rope(z):
        return z * rc + pltpu.roll(z, LANE - ROT_DIM // 2, 1) * rs1 + pltpu.roll(z, ROT_DIM // 2, 1) * rs2

    def plain(ref, seg):
        off, width = seg
        step = min(width, 512)
        for c in range(0, width, step):
            ref[0, :, c:c + step] = proj(off + c, step)

    plain(gm_ref, SEG_GM)
    for c in range(0, SEG_QN[1], 512):
        z = proj(SEG_QN[0] + c, 512)
        for s in range(0, 512, LANE):
            qn_ref[0, :, c + s:c + s + LANE] = rope(z[:, s:s + LANE])
    plain(vg_ref, SEG_VG)
    plain(rg_ref, SEG_RG)
    for ref, seg in ((kvc_ref, SEG_KVC), (kvs_ref, SEG_KVS), (kvw_ref, SEG_KVW)):
        z = proj(seg[0], seg[1])
        ref[0, :, 0:LANE] = rope(z[:, 0:LANE])
        ref[0, :, LANE:2 * LANE] = z[:, LANE:2 * LANE]
    plain(qg_ref, SEG_QG)
    plain(kg_ref, SEG_KG)
    plain(misc_ref, SEG_MISC)


def _inproj(x, shift, scale, w_pack, b_pack, tables, tm):
    b, t, d = x.shape
    per_tok = shift.shape[1] != 1
    mod_spec = (pl.BlockSpec((1, tm, d), lambda i, j: (i, j, 0)) if per_tok
                else pl.BlockSpec((1, 1, d), lambda i, j: (i, 0, 0)))
    segs = (SEG_GM, SEG_QN, SEG_VG, SEG_RG, SEG_KVC, SEG_KVS, SEG_KVW, SEG_QG, SEG_KG, SEG_MISC)
    tab_spec = pl.BlockSpec((tm, LANE), lambda i, j: (j, 0))
    return pl.pallas_call(
        _inproj_kernel,
        grid=(b, t // tm),
        in_specs=[pl.BlockSpec((1, tm, d), lambda i, j: (i, j, 0)), mod_spec, mod_spec,
                  pl.BlockSpec((d, IN_PACKED), lambda i, j: (0, 0)),
                  pl.BlockSpec((1, IN_PACKED), lambda i, j: (0, 0)),
                  tab_spec, tab_spec, tab_spec],
        out_specs=[pl.BlockSpec((1, tm, w), lambda i, j: (i, j, 0)) for _, w in segs],
        out_shape=[jax.ShapeDtypeStruct((b, t, w), F32) for _, w in segs],
        compiler_params=_cparams("parallel", "parallel"),
        name="inproj",
    )(x, shift, scale, w_pack, b_pack, *tables)


CHUNKS = 128
PAGE_ROWS = 128
PAGES = 16


def _pack_cmp_weights(pos, w1, w2):
    pos2 = jnp.concatenate([pos, pos], axis=1)
    z1 = jnp.zeros_like(w1)
    bd1 = jnp.concatenate([jnp.concatenate([w1, z1], axis=2), jnp.concatenate([z1, w1], axis=2)], axis=1)
    w1p = jnp.concatenate([bd1[:CMP_STRIDE], bd1[CMP_STRIDE:]], axis=2).astype(BF16)
    z2 = jnp.zeros_like(w2)
    w2p = jnp.concatenate([jnp.concatenate([w2, z2], axis=1), jnp.concatenate([z2, w2], axis=1)], axis=0).astype(BF16)
    return pos2, w1p, w2p


def _fill_chunks(page_refs, xs_ref, rows_ref, feature_major):
    if feature_major:
        for p, pr in enumerate(page_refs):
            rows_ref[p * PAGE_ROWS:(p + 1) * PAGE_ROWS, :] = pr[0].T
        for l in range(CMP_STRIDE):
            xs_ref[l] = rows_ref[pl.ds(l, CHUNKS, stride=CMP_STRIDE), :]
        return
    for p, pr in enumerate(page_refs):
        for l in range(CMP_STRIDE):
            xs_ref[l, 8 * p:8 * p + 8, :] = pr[0, pl.ds(l, PAGE_ROWS // CMP_STRIDE, stride=CMP_STRIDE), :]


def _compress_chunks(xs_ref, pos_ref, w1_ref, w2_ref):
    hid2 = 2 * CMP_HIDDEN
    acc_a = jnp.zeros((CHUNKS, hid2), F32)
    acc_b = jnp.zeros((CHUNKS, hid2), F32)
    for l in range(CMP_STRIDE):
        x = xs_ref[l]
        acc_a = acc_a + jnp.dot((x + pos_ref[l:l + 1, :]).astype(BF16), w1_ref[l, :, 0:hid2], preferred_element_type=F32)
        acc_b = acc_b + jnp.dot((x + pos_ref[CMP_STRIDE + l:CMP_STRIDE + l + 1, :]).astype(BF16), w1_ref[l, :, hid2:2 * hid2],
                                preferred_element_type=F32)
    hid = acc_a + pltpu.roll(acc_b, CHUNKS - 1, 0)
    out = jnp.dot(jax.nn.gelu(hid).astype(BF16), w2_ref[...], preferred_element_type=F32)
    row = lax.broadcasted_iota(I32, out.shape, 0)
    return jnp.where(row < CHUNKS - 1, out, 0.0)


def _compress_kernel(pt_ref, *refs, feature_major):
    k_pages, v_pages = refs[:PAGES], refs[PAGES:2 * PAGES]
    posk_ref, w1k_ref, w2k_ref, posv_ref, w1v_ref, w2v_ref, kc_ref, vc_ref, xk_ref, xv_ref, rows_ref = refs[2 * PAGES:]
    _fill_chunks(k_pages, xk_ref, rows_ref, feature_major)
    kc_ref[0] = _compress_chunks(xk_ref, posk_ref, w1k_ref, w2k_ref)
    _fill_chunks(v_pages, xv_ref, rows_ref, feature_major)
    vc_ref[0] = _compress_chunks(xv_ref, posv_ref, w1v_ref, w2v_ref)


def _page_spec(p, half, feature_major):
    if feature_major:
        return pl.BlockSpec((1, LANE, PAGE_ROWS), lambda i, pt: (pt[i * PAGES + p], half, 0))
    return pl.BlockSpec((1, PAGE_ROWS, LANE), lambda i, pt: (pt[i * PAGES + p], 0, half))


def _const_spec(shape):
    nd = len(shape)
    return pl.BlockSpec(shape, lambda i, pt: (0,) * nd)


def _compress(pages, page_ids, cmp_wk, cmp_wv, feature_major=False):
    n_b = page_ids.shape[0] // PAGES
    consts = list(cmp_wk) + list(cmp_wv)
    grid_spec = pltpu.PrefetchScalarGridSpec(
        num_scalar_prefetch=1,
        grid=(n_b,),
        in_specs=[_page_spec(p, h, feature_major) for h in range(2) for p in range(PAGES)] + [_const_spec(c.shape) for c in consts],
        out_specs=[pl.BlockSpec((1, CHUNKS, LANE), lambda i, pt: (i, 0, 0))] * 2,
        scratch_shapes=[pltpu.VMEM((CMP_STRIDE, CHUNKS, LANE), F32)] * 2 + [pltpu.VMEM((PAGES * PAGE_ROWS, LANE), F32)],
    )
    return pl.pallas_call(
        functools.partial(_compress_kernel, feature_major=feature_major),
        grid_spec=grid_spec,
        out_shape=[jax.ShapeDtypeStruct((n_b, CHUNKS, LANE), F32)] * 2,
        compiler_params=_cparams("parallel"),
        name="compress",
    )(page_ids, *([pages] * (2 * PAGES)), *consts)


def _cover_tables(n_sel):
    c_start = np.arange(CHUNKS) * CMP_STRIDE
    s_start = np.arange(n_sel) * SEL_BLOCK
    cover = ((c_start[:, None] < s_start[None, :] + SEL_BLOCK) & (c_start[:, None] + CMP_LEN > s_start[None, :])).astype(np.float32)
    cover[CHUNKS - 1] = 0.0
    out = np.zeros((NSA_KV_HEADS, LANE, CHUNKS), np.float32)
    for h in range(NSA_KV_HEADS):
        out[h, h * 64:h * 64 + n_sel] = cover.T
    return jnp.asarray(out)


def _softmax_rows(s, valid):
    s = jnp.where(valid, s, NEG)
    m = jnp.max(s, axis=-1, keepdims=True)
    m = jnp.where(m > 0.5 * NEG, m, 0.0)
    p = jnp.where(valid, jnp.exp(s - m), 0.0)
    return p / jnp.maximum(jnp.sum(p, axis=-1, keepdims=True), 1e-30)


def _select_blocks(imp, n_sel, top_n):
    ridx = lax.broadcasted_iota(I32, imp.shape, 0)
    cnt = jnp.zeros(imp.shape, F32)
    for i in range(n_sel):
        vi = imp[i:i + 1, :]
        ahead = (vi > imp) | ((vi == imp) & (ridx > i))
        cnt = cnt + jnp.where(ahead, 1.0, 0.0)
    return jnp.where((cnt < top_n) & (ridx < n_sel), 1.0, 0.0)


def _cmp_attn_kernel(qn_ref, kc_ref, vc_ref, cov_ref, o_ref, sel_ref, *, tq, n_sel):
    qi = pl.program_id(1)
    kc = kc_ref[0].astype(BF16)
    vc = vc_ref[0].astype(BF16)
    qpos = qi * tq + lax.broadcasted_iota(I32, (tq, CHUNKS), 0)
    cidx = lax.broadcasted_iota(I32, (tq, CHUNKS), 1)
    valid = (cidx * CMP_STRIDE + CMP_LEN - 1 <= qpos) & (cidx < CHUNKS - 1)
    psum = [jnp.zeros((tq, CHUNKS), F32) for _ in range(NSA_KV_HEADS)]
    for hh in range(NSA_HEADS):
        q = qn_ref[0, :, hh * LANE:(hh + 1) * LANE].astype(BF16)
        p = _softmax_rows(_dot_nt(q, kc), valid)
        o_ref[0, :, hh * LANE:(hh + 1) * LANE] = jnp.dot(p.astype(BF16), vc, preferred_element_type=F32)
        psum[hh // NSA_GROUP] = psum[hh // NSA_GROUP] + p
    imp = (_dot_nt(cov_ref[0], psum[0], lax.Precision.HIGHEST) + _dot_nt(cov_ref[1], psum[1], lax.Precision.HIGHEST))
    blk = lax.broadcasted_iota(I32, (LANE, tq), 0) & 63
    qpos_t = qi * tq + lax.broadcasted_iota(I32, (LANE, tq), 1)
    cur = qpos_t // SEL_BLOCK
    forced = (blk == 0) | (blk == cur) | (blk == cur - 1)
    imp = jnp.where(forced, FORCE_SCORE, jnp.where(blk * SEL_BLOCK <= qpos_t, imp, -FORCE_SCORE))
    sel_t = jnp.concatenate([_select_blocks(imp[0:64], n_sel, SEL_TOP_N), _select_blocks(imp[64:128], n_sel, SEL_TOP_N)], axis=0)
    sel_ref[0] = sel_t.T


def _cmp_attn(qn, kc, vc, tq):
    b, t, _ = qn.shape
    n_sel = -(-t // SEL_BLOCK)
    cov = _cover_tables(n_sel)
    return pl.pallas_call(
        functools.partial(_cmp_attn_kernel, tq=tq, n_sel=n_sel),
        grid=(b, t // tq),
        in_specs=[pl.BlockSpec((1, tq, NSA_HEADS * LANE), lambda i, j: (i, j, 0)),
                  pl.BlockSpec((1, CHUNKS, LANE), lambda i, j: (i, 0, 0)),
                  pl.BlockSpec((1, CHUNKS, LANE), lambda i, j: (i, 0, 0)),
                  pl.BlockSpec((NSA_KV_HEADS, LANE, CHUNKS), lambda i, j: (0, 0, 0))],
        out_specs=[pl.BlockSpec((1, tq, NSA_HEADS * LANE), lambda i, j: (i, j, 0)),
                   pl.BlockSpec((1, tq, LANE), lambda i, j: (i, j, 0))],
        out_shape=[jax.ShapeDtypeStruct((b, t, NSA_HEADS * LANE), F32), jax.ShapeDtypeStruct((b, t, LANE), F32)],
        compiler_params=_cparams("parallel", "parallel"),
        name="cmp_attn",
    )(qn, kc, vc, cov)


def _key_block_table(t):
    blk = np.arange(t)[:, None] // SEL_BLOCK
    lanes = np.arange(LANE)[None, :] & 63
    return jnp.asarray((blk == lanes).astype(np.float32), dtype=BF16)


def _lane_fold(x, op):
    out = x[:, 0:LANE]
    for c in range(LANE, x.shape[1], LANE):
        out = op(out, x[:, c:c + LANE])
    return out


def _sel_attn_kernel(qn_ref, kv_ref, sel_ref, kb_ref, o_ref, q2_scr, k2_scr, v_scr, s_scr, *, tq, tk):
    qi = pl.program_id(1)
    rows = NSA_HEADS * tq
    t = kv_ref.shape[1]

    @pl.when(qi == 0)
    def _():
        k2_scr[:, 0:LANE] = kv_ref[0, :, 0:LANE].astype(BF16)
        k2_scr[:, LANE:2 * LANE] = kb_ref[...]
        v_scr[...] = kv_ref[0, :, LANE:2 * LANE].astype(BF16)

    not_sel = (1.0 - sel_ref[0]) * NEG
    lane_head = lax.broadcasted_iota(I32, (tq, LANE), 1) // 64
    for hh in range(NSA_HEADS):
        q2_scr[hh * tq:(hh + 1) * tq, 0:LANE] = qn_ref[0, :, hh * LANE:(hh + 1) * LANE].astype(BF16)
        q2_scr[hh * tq:(hh + 1) * tq, LANE:2 * LANE] = jnp.where(lane_head == hh // NSA_GROUP, not_sel, 0.0).astype(BF16)
    q2 = q2_scr[...]
    last = (qi * tq + tq - 1) // tk

    def scores(kt):
        k0 = pl.multiple_of(kt * tk, tk)
        return _dot_nt(q2, k2_scr[pl.ds(k0, tk), :])

    def pass1(kt, m_acc):
        s = scores(kt)
        s_scr[kt] = s
        return jnp.maximum(m_acc, _lane_fold(s, jnp.maximum))

    m_acc = lax.fori_loop(0, last, pass1, jnp.full((rows, LANE), NEG, F32))
    rel = (qi * tq + lax.broadcasted_iota(I32, (tq, tk), 0)) - (last * tk + lax.broadcasted_iota(I32, (tq, tk), 1))
    causal = jnp.where(rel >= 0, 0.0, NEG)
    s_last = scores(last) + jnp.concatenate([causal] * NSA_HEADS, axis=0)
    m = jnp.max(jnp.maximum(m_acc, _lane_fold(s_last, jnp.maximum)), axis=-1, keepdims=True)

    def accumulate(s, v, l_acc, acc):
        p = jnp.exp(s - m)
        return l_acc + _lane_fold(p, jnp.add), acc + jnp.dot(p.astype(BF16), v, preferred_element_type=F32)

    def pass2(kt, carry):
        k0 = pl.multiple_of(kt * tk, tk)
        return accumulate(s_scr[kt], v_scr[pl.ds(k0, tk), :], *carry)

    l_acc, acc = lax.fori_loop(0, last, pass2, (jnp.zeros((rows, LANE), F32), jnp.zeros((rows, LANE), F32)))
    l_acc, acc = accumulate(s_last, v_scr[pl.ds(pl.multiple_of(last * tk, tk), tk), :], l_acc, acc)
    out = acc / jnp.sum(l_acc, axis=-1, keepdims=True)
    for hh in range(NSA_HEADS):
        o_ref[0, :, hh * LANE:(hh + 1) * LANE] = out[hh * tq:(hh + 1) * tq]


def _sel_attn(qn, kv, sel, tq, tk):
    b, t, _ = qn.shape
    rows = NSA_HEADS * tq
    return pl.pallas_call(
        functools.partial(_sel_attn_kernel, tq=tq, tk=tk),
        grid=(b, t // tq),
        in_specs=[pl.BlockSpec((1, tq, NSA_HEADS * LANE), lambda i, j: (i, j, 0)),
                  pl.BlockSpec((1, t, 2 * LANE), lambda i, j: (i, 0, 0)),
                  pl.BlockSpec((1, tq, LANE), lambda i, j: (i, j, 0)),
                  pl.BlockSpec((t, LANE), lambda i, j: (0, 0))],
        out_specs=pl.BlockSpec((1, tq, NSA_HEADS * LANE), lambda i, j: (i, j, 0)),
        out_shape=jax.ShapeDtypeStruct((b, t, NSA_HEADS * LANE), F32),
        scratch_shapes=[pltpu.VMEM((rows, 2 * LANE), BF16), pltpu.VMEM((t, 2 * LANE), BF16), pltpu.VMEM((t, LANE), BF16),
                        pltpu.VMEM((t // tk, rows, tk), F32)],
        compiler_params=_cparams("parallel", "arbitrary"),
        name="sel_attn",
    )(qn, kv, sel, _key_block_table(t))


def _win_attn_kernel(qn_ref, kv_ref, o_ref, q_scr, k_scr, v_scr, *, tq):
    qi = pl.program_id(1)
    span = WINDOW + tq

    @pl.when(qi == 0)
    def _():
        k_scr[...] = kv_ref[0, :, 0:LANE].astype(BF16)
        v_scr[...] = kv_ref[0, :, LANE:2 * LANE].astype(BF16)

    for hh in range(NSA_HEADS):
        q_scr[hh * tq:(hh + 1) * tq, :] = qn_ref[0, :, hh * LANE:(hh + 1) * LANE].astype(BF16)
    k0 = pl.multiple_of(jnp.maximum(qi * tq - WINDOW, 0), tq)
    rel = (qi * tq + lax.broadcasted_iota(I32, (tq, span), 0)) - (k0 + lax.broadcasted_iota(I32, (tq, span), 1))
    bias = jnp.where((rel >= 0) & (rel <= WINDOW), 0.0, NEG)
    s = _dot_nt(q_scr[...], k_scr[pl.ds(k0, span), :]) + jnp.concatenate([bias] * NSA_HEADS, axis=0)
    p = jnp.exp(s - jnp.max(s, axis=-1, keepdims=True))
    out = jnp.dot(p.astype(BF16), v_scr[pl.ds(k0, span), :], preferred_element_type=F32) / jnp.sum(p, axis=-1, keepdims=True)
    for hh in range(NSA_HEADS):
        o_ref[0, :, hh * LANE:(hh + 1) * LANE] = out[hh * tq:(hh + 1) * tq]


def _win_attn(qn, kv, tq):
    b, t, _ = qn.shape
    assert t >= WINDOW + tq
    rows = NSA_HEADS * tq
    return pl.pallas_call(
        functools.partial(_win_attn_kernel, tq=tq),
        grid=(b, t // tq),
        in_specs=[pl.BlockSpec((1, tq, NSA_HEADS * LANE), lambda i, j: (i, j, 0)),
                  pl.BlockSpec((1, t, 2 * LANE), lambda i, j: (i, 0, 0))],
        out_specs=pl.BlockSpec((1, tq, NSA_HEADS * LANE), lambda i, j: (i, j, 0)),
        out_shape=jax.ShapeDtypeStruct((b, t, NSA_HEADS * LANE), F32),
        scratch_shapes=[pltpu.VMEM((rows, LANE), BF16), pltpu.VMEM((t, LANE), BF16), pltpu.VMEM((t, LANE), BF16)],
        compiler_params=_cparams("parallel", "arbitrary"),
        name="win_attn",
    )(qn, kv)


def _dec_softmax(scores, vals):
    m = scores[0].max(axis=-1, keepdims=True)
    for s in scores[1:]:
        m = jnp.maximum(m, s.max(axis=-1, keepdims=True))
    den = jnp.zeros_like(m)
    out = jnp.zeros((m.shape[0], LANE), F32)
    for s, v in zip(scores, vals):
        p = jnp.exp(s - m)
        den = den + p.sum(axis=-1, keepdims=True)
        out = out + (p * v if s.shape[1] == 1 else _dot_nt(p.astype(BF16), v))
    return out / den


def _nsa_decode_kernel(pt_ref, *refs, past_len, n_sel):
    pages = refs[:PAGES]
    (qn_ref, kc_ref, vc_ref, kvs_ref, kvw_ref, win_ref, cov_ref,
     ocmp_ref, osel_ref, owin_ref, wnew_ref) = refs[PAGES:]
    nh = NSA_HEADS
    q = jnp.concatenate([qn_ref[0, :, hh * LANE:(hh + 1) * LANE] for hh in range(nh)], axis=0)
    qb = q.astype(BF16)
    cidx = lax.broadcasted_iota(I32, (nh, CHUNKS), 1)
    valid = (cidx * CMP_STRIDE + CMP_LEN - 1 <= past_len) & (cidx < CHUNKS - 1)
    p = _softmax_rows(_dot_nt(qb, kc_ref[0].astype(BF16)), valid)
    o_cmp = jnp.dot(p.astype(BF16), vc_ref[0].astype(BF16), preferred_element_type=F32)
    imp = None
    for h in range(NSA_KV_HEADS):
        ps = jnp.sum(p[h * NSA_GROUP:(h + 1) * NSA_GROUP], axis=0, keepdims=True)
        term = _dot_nt(cov_ref[h], jnp.broadcast_to(ps, (LANE, CHUNKS)), lax.Precision.HIGHEST)
        imp = term if imp is None else imp + term
    blk = lax.broadcasted_iota(I32, (LANE, LANE), 0) & 63
    cur = past_len // SEL_BLOCK
    forced = (blk == 0) | (blk == cur) | (blk == cur - 1)
    imp = jnp.where(forced, FORCE_SCORE, jnp.where(blk * SEL_BLOCK <= past_len, imp, -FORCE_SCORE))
    sel_t = jnp.concatenate([_select_blocks(imp[0:64], n_sel, min(SEL_TOP_N, n_sel)),
                             _select_blocks(imp[64:128], n_sel, min(SEL_TOP_N, n_sel))], axis=0)
    sel = sel_t.T[0:1, :]
    head_of_row = lax.broadcasted_iota(I32, (nh, 1), 0) // NSA_GROUP

    def picked(s):
        return jnp.where(head_of_row == 0, sel[:, s:s + 1], sel[:, 64 + s:64 + s + 1])

    first_half = lax.broadcasted_iota(I32, (nh, PAGE_ROWS), 1) < SEL_BLOCK
    scores, vals = [], []
    for pg, pr in enumerate(pages):
        s = jnp.dot(qb, pr[0, 0:LANE, :].astype(BF16), preferred_element_type=F32)
        ok = jnp.where(first_half, picked(2 * pg), picked(2 * pg + 1)) > 0.5
        scores.append(jnp.where(ok, s, NEG))
        vals.append(pr[0, LANE:2 * LANE, :].astype(BF16))
    s_new = jnp.sum(q * kvs_ref[0, :, 0:LANE], axis=-1, keepdims=True)
    scores.append(jnp.where(picked(past_len // SEL_BLOCK) > 0.5, s_new, NEG))
    vals.append(kvs_ref[0, :, LANE:2 * LANE])
    o_sel = _dec_softmax(scores, vals)
    n_win = win_ref.shape[2]
    kpos = past_len - n_win + lax.broadcasted_iota(I32, (nh, n_win), 1)
    rel = past_len - kpos
    s_win = jnp.dot(qb, win_ref[0, 0:LANE, :].astype(BF16), preferred_element_type=F32)
    s_win = jnp.where((kpos >= 0) & (rel >= 0) & (rel <= WINDOW), s_win, NEG)
    s_new = jnp.sum(q * kvw_ref[0, :, 0:LANE], axis=-1, keepdims=True)
    o_win = _dec_softmax([s_win, s_new], [win_ref[0, LANE:2 * LANE, :].astype(BF16), kvw_ref[0, :, LANE:2 * LANE]])
    for hh in range(nh):
        ocmp_ref[0, :, hh * LANE:(hh + 1) * LANE] = o_cmp[hh:hh + 1]
        osel_ref[0, :, hh * LANE:(hh + 1) * LANE] = o_sel[hh:hh + 1]
        owin_ref[0, :, hh * LANE:(hh + 1) * LANE] = o_win[hh:hh + 1]
    for c in range(0, 2 * LANE, LANE):
        new_col = jnp.broadcast_to(kvw_ref[0, :, c:c + LANE], (LANE, LANE)).T
        shifted = pltpu.roll(win_ref[0, c:c + LANE, :], n_win - 1, 1)
        lane = lax.broadcasted_iota(I32, shifted.shape, 1)
        wnew_ref[0, c:c + LANE, :] = jnp.where(lane == n_win - 1, jnp.concatenate([new_col] * (n_win // LANE), axis=1), shifted)


def _nsa_decode(qn, kc, vc, sel_pages, page_ids, kvs_new, kvw_new, win_state):
    b = qn.shape[0]
    n_win = win_state.shape[2]
    past_len = PAGES * PAGE_ROWS
    n_sel = -(-(past_len + 1) // SEL_BLOCK)
    cov = _cover_tables(n_sel)

    def per_b(shape):
        nd = len(shape)
        return pl.BlockSpec((1,) + shape[1:], lambda i, pt: (i,) + (0,) * (nd - 1))

    slots = NSA_HEADS * LANE
    grid_spec = pltpu.PrefetchScalarGridSpec(
        num_scalar_prefetch=1,
        grid=(b,),
        in_specs=[pl.BlockSpec((1, 2 * LANE, PAGE_ROWS), (lambda i, pt, p=p: (pt[i * PAGES + p], 0, 0))) for p in range(PAGES)]
        + [per_b(qn.shape), per_b(kc.shape), per_b(vc.shape), per_b(kvs_new.shape), per_b(kvw_new.shape), per_b(win_state.shape),
           _const_spec(cov.shape)],
        out_specs=[per_b((b, 1, slots))] * 3 + [per_b(win_state.shape)],
    )
    return pl.pallas_call(
        functools.partial(_nsa_decode_kernel, past_len=past_len, n_sel=n_sel),
        grid_spec=grid_spec,
        out_shape=[jax.ShapeDtypeStruct((b, 1, slots), F32)] * 3 + [jax.ShapeDtypeStruct(win_state.shape, F32)],
        compiler_params=_cparams("parallel"),
        name="nsa_decode",
    )(page_ids, *([sel_pages] * PAGES), qn, kc, vc, kvs_new, kvw_new, win_state, cov)


def _dot_tn(a, b):
    return lax.dot_general(a, b, (((0,), (0,)), ((), ())), preferred_element_type=F32)


def _cumsum_table():
    r = np.arange(LANE)
    return jnp.asarray(((r[:, None] // GLA_SUB == r[None, :] // GLA_SUB) & (r[None, :] <= r[:, None])).astype(np.float32))


def _gla_kernel(*refs, t, t_valid, has_state):
    if has_state:
        qg_ref, kg_ref, vg_ref, misc_ref, wa_ref, ba_ref, lt_ref, s0_ref, o_ref, s_ref, b_scr, st_scr = refs
    else:
        qg_ref, kg_ref, vg_ref, misc_ref, wa_ref, ba_ref, lt_ref, o_ref, s_ref, b_scr, st_scr = refs
    z = jnp.dot(misc_ref[0], wa_ref[...], preferred_element_type=F32, precision=lax.Precision.HIGHEST) + ba_ref[...]
    la = (jnp.minimum(z, 0.0) - jnp.log1p(jnp.exp(-jnp.abs(z)))) * (1.0 / GLA_TAU)
    if t_valid < t:
        la = jnp.where(lax.broadcasted_iota(I32, la.shape, 0) < t_valid, la, 0.0)
    tile = min(t, LANE)
    for r in range(0, t, tile):
        b_scr[r:r + tile, :] = jnp.dot(lt_ref[0:tile, 0:tile], la[r:r + tile, :], preferred_element_type=F32,
                                       precision=lax.Precision.HIGHEST)
    pairs = GLA_HEADS // 2
    for p in range(pairs):
        if has_state:
            st_scr[p] = s0_ref[0, 2 * p:2 * p + 2].reshape(2 * GLA_DK, GLA_DV).T
        else:
            st_scr[p] = jnp.zeros((GLA_DV, LANE), F32)
    head_a = lax.broadcasted_iota(I32, (GLA_SUB, LANE), 1) < GLA_DK
    row = lax.broadcasted_iota(I32, (GLA_SUB, LANE), 0)

    def pair_chunk(q, k, v, b, st):
        b_last = b[GLA_SUB - 1:GLA_SUB, :]
        st_b = st.astype(BF16)
        qe = q * jnp.exp(b)
        o_a = _dot_nt(jnp.where(head_a, qe, 0.0).astype(BF16), st_b)
        o_b = _dot_nt(jnp.where(head_a, 0.0, qe).astype(BF16), st_b)
        for j in range(GLA_SUB):
            w = q * k[j:j + 1, :] * jnp.exp(jnp.minimum(b - b[j:j + 1, :], 0.0))
            w = jnp.where(row >= j, w, 0.0)
            a_a = jnp.sum(jnp.where(head_a, w, 0.0), axis=-1, keepdims=True)
            a_b = jnp.sum(jnp.where(head_a, 0.0, w), axis=-1, keepdims=True)
            o_a = o_a + a_a * v[j:j + 1, 0:GLA_DV]
            o_b = o_b + a_b * v[j:j + 1, GLA_DV:2 * GLA_DV]
        kd = k * jnp.exp(b_last - b)
        upd = (_dot_tn(v[:, 0:GLA_DV].astype(BF16), jnp.where(head_a, kd, 0.0).astype(BF16))
               + _dot_tn(v[:, GLA_DV:2 * GLA_DV].astype(BF16), jnp.where(head_a, 0.0, kd).astype(BF16)))
        return o_a, o_b, jnp.exp(b_last) * st + upd

    def chunk(c, carry):
        r0 = pl.multiple_of(c * GLA_SUB, GLA_SUB)
        for p in range(pairs):
            o_a, o_b, st_new = pair_chunk(qg_ref[0, pl.ds(r0, GLA_SUB), p * LANE:(p + 1) * LANE],
                                          kg_ref[0, pl.ds(r0, GLA_SUB), p * LANE:(p + 1) * LANE],
                                          vg_ref[0, pl.ds(r0, GLA_SUB), 2 * p * GLA_DV:2 * (p + 1) * GLA_DV],
                                          b_scr[pl.ds(r0, GLA_SUB), p * LANE:(p + 1) * LANE], st_scr[p])
            o_ref[0, pl.ds(r0, GLA_SUB), 2 * p * GLA_DV:(2 * p + 1) * GLA_DV] = o_a
            o_ref[0, pl.ds(r0, GLA_SUB), (2 * p + 1) * GLA_DV:(2 * p + 2) * GLA_DV] = o_b
            st_scr[p] = st_new
        return carry

    lax.fori_loop(0, t // GLA_SUB, chunk, 0)
    for p in range(pairs):
        s_ref[0, 2 * p:2 * p + 2] = st_scr[p].T.reshape(2, GLA_DK, GLA_DV)


def _gla(qg, kg, vg, misc, wa_pad, ba, s0, t_valid):
    b, t, _ = qg.shape
    has_state = s0 is not None
    hk = GLA_HEADS * GLA_DK

    def per_b(shape):
        nd = len(shape)
        return pl.BlockSpec((1,) + shape[1:], lambda i: (i,) + (0,) * (nd - 1))

    def const(shape):
        nd = len(shape)
        return pl.BlockSpec(shape, lambda i: (0,) * nd)

    in_specs = [per_b(qg.shape), per_b(kg.shape), per_b(vg.shape), per_b(misc.shape),
                const((LANE, hk)), const((1, hk)), const((LANE, LANE))]
    args = [qg, kg, vg, misc, wa_pad, ba.reshape(1, -1), _cumsum_table()]
    state_shape = (b, GLA_HEADS, GLA_DK, GLA_DV)
    if has_state:
        in_specs.append(per_b(state_shape))
        args.append(s0)
    return pl.pallas_call(
        functools.partial(_gla_kernel, t=t, t_valid=t_valid, has_state=has_state),
        grid=(b,),
        in_specs=in_specs,
        out_specs=[per_b(vg.shape), per_b(state_shape)],
        out_shape=[jax.ShapeDtypeStruct(vg.shape, F32), jax.ShapeDtypeStruct(state_shape, F32)],
        scratch_shapes=[pltpu.VMEM((t, hk), F32), pltpu.VMEM((GLA_HEADS // 2, GLA_DV, LANE), F32)],
        compiler_params=_cparams("parallel"),
        name="gla",
    )(*args)


def _gate_expand_table():
    out = np.zeros((3, LANE, NSA_HEADS * LANE), np.float32)
    for hh in range(NSA_HEADS):
        for j in range(3):
            out[j, MISC_GN + 3 * hh + j, hh * LANE:(hh + 1) * LANE] = 1.0
    return jnp.asarray(out, dtype=BF16)


def _pad_br_a(w_br_a):
    zero = jnp.zeros((HEAD_DIM, w_br_a.shape[1]), w_br_a.dtype)
    parts = []
    for hh in range(NSA_HEADS):
        wh = w_br_a[hh * HEAD_DIM:(hh + 1) * HEAD_DIM]
        parts += [wh, zero] if hh // NSA_GROUP == 0 else [zero, wh]
    return jnp.concatenate(parts, axis=0).astype(BF16)


def _layer_norm(v, g, b):
    mu = jnp.mean(v, axis=-1, keepdims=True)
    var = jnp.mean(jnp.square(v - mu), axis=-1, keepdims=True)
    return (v - mu) * lax.rsqrt(var + LN_EPS) * g + b


def _mixer_tail_kernel(ocmp_ref, osel_ref, owin_ref, misc_ref, ogla_ref, rg_ref, gm_ref, x_ref, gate_ref, scf_ref, shf_ref,
                       ex_ref, ng_ref, wa_ref, wb_ref, wo_ref, lg_ref, lb_ref, x1_ref, xm_ref):
    sig = jax.nn.sigmoid(misc_ref[0])
    sig_hi = sig.astype(BF16)
    sig_lo = (sig - sig_hi.astype(F32)).astype(BF16)
    o_nsa = None
    for j, ref in enumerate((ocmp_ref, osel_ref, owin_ref)):
        g = (jnp.dot(sig_hi, ex_ref[j], preferred_element_type=F32) + jnp.dot(sig_lo, ex_ref[j], preferred_element_type=F32))
        o_nsa = g * ref[0] if o_nsa is None else o_nsa + g * ref[0]
    br_a = _bdot(o_nsa, wa_ref[...])
    heads = []
    for h in range(GLA_HEADS):
        seg = ogla_ref[0, :, h * GLA_DV:(h + 1) * GLA_DV]
        mu = jnp.mean(seg, axis=-1, keepdims=True)
        var = jnp.mean(jnp.square(seg - mu), axis=-1, keepdims=True)
        r = rg_ref[0, :, h * GLA_DV:(h + 1) * GLA_DV]
        heads.append((seg - mu) * lax.rsqrt(var + LN_EPS) * ng_ref[:, h * GLA_DV:(h + 1) * GLA_DV] * (r * jax.nn.sigmoid(r)))
    br_b = _bdot(jnp.concatenate(heads, axis=1), wb_ref[...])
    gm_a = jax.nn.sigmoid(gm_ref[0, :, 0:D_MODEL])
    gm_b = jax.nn.sigmoid(gm_ref[0, :, D_MODEL:2 * D_MODEL])
    y = _bdot(gm_a * br_a + gm_b * br_b, wo_ref[...])
    x1 = _layer_norm(DN_ALPHA * x_ref[0] + gate_ref[0] * y, lg_ref[...], lb_ref[...])
    x1_ref[0] = x1
    xm_ref[0] = x1 * (1.0 + scf_ref[0]) + shf_ref[0]


def _mixer_tail(ocmp, osel, owin, misc, ogla, rg, gm, x, gate_m, scale_f, shift_f, consts, tm):
    b, t, d = x.shape
    per_tok = gate_m.shape[1] != 1

    def tok(w):
        return pl.BlockSpec((1, tm, w), lambda i, j: (i, j, 0))

    mod_spec = tok(d) if per_tok else pl.BlockSpec((1, 1, d), lambda i, j: (i, 0, 0))

    def const(a):
        nd = a.ndim
        return pl.BlockSpec(a.shape, lambda i, j: (0,) * nd)

    return pl.pallas_call(
        _mixer_tail_kernel,
        grid=(b, t // tm),
        in_specs=[tok(NSA_HEADS * LANE)] * 3 + [tok(LANE), tok(GLA_HEADS * GLA_DV), tok(GLA_HEADS * GLA_DV), tok(2 * d), tok(d),
                                               mod_spec, mod_spec, mod_spec] + [const(c) for c in consts],
        out_specs=[tok(d), tok(d)],
        out_shape=[jax.ShapeDtypeStruct((b, t, d), F32)] * 2,
        compiler_params=_cparams("parallel", "parallel"),
        name="mixer_tail",
    )(ocmp, osel, owin, misc, ogla, rg, gm, x, gate_m, scale_f, shift_f, *consts)


ROUTE_TILE = LANE


def _first_index(hit, iota, size, axis):
    return jnp.min(jnp.where(hit, iota, size), axis=axis, keepdims=True)


def _router_kernel(xm_ref, wr_ref, bias_ref, tri_ref, eidx_ref, rank_ref, wrow_ref, cnt_ref, carry_ref):
    i = pl.program_id(0)
    tm = ROUTE_TILE
    per = N_EXPERTS // N_GROUPS

    @pl.when(i == 0)
    def _():
        carry_ref[...] = jnp.zeros_like(carry_ref)

    logits = _dot_nt(wr_ref[...], xm_ref[...], lax.Precision.HIGHEST)
    s = jax.nn.sigmoid(logits)
    sb = s + bias_ref[...]
    sb3 = sb.reshape(N_GROUPS, per, tm)
    in_grp = lax.broadcasted_iota(I32, sb3.shape, 1)
    m1 = jnp.max(sb3, axis=1, keepdims=True)
    first = _first_index(sb3 == m1, in_grp, per, 1)
    m2 = jnp.max(jnp.where(in_grp == first, NEG, sb3), axis=1, keepdims=True)
    gs = (m1 + m2).reshape(N_GROUPS, tm)
    g_iota = lax.broadcasted_iota(I32, gs.shape, 0)
    g_keep = jnp.zeros(gs.shape, jnp.bool_)
    for _ in range(TOPK_GROUPS):
        pick = g_iota == _first_index(gs == jnp.max(gs, axis=0, keepdims=True), g_iota, N_GROUPS, 0)
        g_keep = g_keep | pick
        gs = jnp.where(pick, NEG, gs)
    sbm = jnp.where(g_keep.reshape(N_GROUPS, 1, tm), sb3, NEG).reshape(N_EXPERTS, tm)
    e_iota = lax.broadcasted_iota(I32, sbm.shape, 0)
    idxs, sels = [], []
    onehot = jnp.zeros(sbm.shape, F32)
    for _ in range(TOP_K):
        idx = _first_index(sbm == jnp.max(sbm, axis=0, keepdims=True), e_iota, N_EXPERTS, 0)
        pick = e_iota == idx
        idxs.append(idx)
        sels.append(jnp.sum(jnp.where(pick, s, 0.0), axis=0, keepdims=True))
        sbm = jnp.where(pick, NEG, sbm)
        onehot = onehot + jnp.where(pick, 1.0, 0.0)
    sel = jnp.concatenate(sels, axis=0)
    wts = sel / jnp.sum(sel, axis=0, keepdims=True) * ROUTED_SCALE
    carry = carry_ref[...]
    before = carry + jnp.dot(onehot.astype(BF16), tri_ref[...], preferred_element_type=F32)
    ranks = [jnp.sum(jnp.where(e_iota == idx, before, 0.0), axis=0, keepdims=True) for idx in idxs]
    eidx_ref[...] = jnp.concatenate(idxs, axis=0)
    rank_ref[...] = jnp.concatenate(ranks, axis=0).astype(I32)
    wrow_ref[...] = jnp.concatenate([wts, jnp.zeros((LANE - TOP_K, tm), F32)], axis=0).T
    carry = carry + jnp.sum(onehot, axis=1, keepdims=True)
    carry_ref[...] = carry
    cnt_ref[...] = carry


def _router(xm, w_router, router_bias):
    n, d = xm.shape
    tm = ROUTE_TILE
    r = np.arange(tm)
    tri = jnp.asarray((r[:, None] < r[None, :]).astype(np.float32), dtype=BF16)
    return pl.pallas_call(
        _router_kernel,
        grid=(n // tm,),
        in_specs=[pl.BlockSpec((tm, d), lambda i: (i, 0)),
                  pl.BlockSpec((N_EXPERTS, d), lambda i: (0, 0)),
                  pl.BlockSpec((N_EXPERTS, 1), lambda i: (0, 0)),
                  pl.BlockSpec((tm, tm), lambda i: (0, 0))],
        out_specs=[pl.BlockSpec((TOP_K, tm), lambda i: (0, i)),
                   pl.BlockSpec((TOP_K, tm), lambda i: (0, i)),
                   pl.BlockSpec((tm, LANE), lambda i: (i, 0)),
                   pl.BlockSpec((N_EXPERTS, LANE), lambda i: (0, 0))],
        out_shape=[jax.ShapeDtypeStruct((TOP_K, n), I32), jax.ShapeDtypeStruct((TOP_K, n), I32),
                   jax.ShapeDtypeStruct((n, LANE), F32), jax.ShapeDtypeStruct((N_EXPERTS, LANE), F32)],
        scratch_shapes=[pltpu.VMEM((N_EXPERTS, LANE), F32)],
        compiler_params=_cparams("arbitrary"),
        name="router",
    )(xm, w_router.T, router_bias.reshape(N_EXPERTS, 1), tri)


def _dest_kernel(eidx_ref, rank_ref, start_ref, dest_ref):
    e_iota = lax.broadcasted_iota(I32, (N_EXPERTS, ROUTE_TILE), 0)
    start = start_ref[...]
    rows = [jnp.sum(jnp.where(e_iota == eidx_ref[k:k + 1, :], start, 0.0), axis=0, keepdims=True) for k in range(TOP_K)]
    dest_ref[0] = jnp.concatenate(rows, axis=0).astype(I32) + rank_ref[...]


def _dest(eidx, rank, pad_start):
    n = eidx.shape[1]
    tm = ROUTE_TILE
    return pl.pallas_call(
        _dest_kernel,
        grid=(n // tm,),
        in_specs=[pl.BlockSpec((TOP_K, tm), lambda i: (0, i)),
                  pl.BlockSpec((TOP_K, tm), lambda i: (0, i)),
                  pl.BlockSpec((N_EXPERTS, 1), lambda i: (0, 0))],
        out_specs=pl.BlockSpec((1, TOP_K, tm), lambda i: (i, 0, 0)),
        out_shape=jax.ShapeDtypeStruct((n // tm, TOP_K, tm), I32),
        compiler_params=_cparams("parallel"),
        name="dest",
    )(eidx, rank, pad_start.astype(F32).reshape(N_EXPERTS, 1))


TOKEN_SUB = D_MODEL // LANE


def _to_token_tiles(x, ref):
    m = x.shape[0]
    for s in range(TOKEN_SUB):
        ref[pl.ds(s, m, stride=TOKEN_SUB), :] = x[:, s * LANE:(s + 1) * LANE]


def _from_token_tiles(ref, m, s):
    return ref[pl.ds(s, m, stride=TOKEN_SUB), :]


def _row_copy(src_ref, src_row, dst_ref, dst_row, sem):
    return pltpu.make_async_copy(src_ref.at[pl.ds(src_row * TOKEN_SUB, TOKEN_SUB)],
                                 dst_ref.at[pl.ds(dst_row * TOKEN_SUB, TOKEN_SUB)], sem)


def _dispatch_kernel(dest_ref, pad_ref, nb_ref, xm_ref, xs_ref, tile_ref, zero_ref, sems, sem):
    i = pl.program_id(0)
    tm = ROUTE_TILE
    bm = EXPERT_BLOCK
    last = pl.num_programs(0) - 1
    src = tile_ref.at[i % 2]
    _to_token_tiles(xm_ref[...], src)

    def start_row(r, c):
        for k in range(TOP_K):
            _row_copy(src, r, xs_ref, dest_ref[0, k, r], sems.at[i % 2]).start()
        return c

    def wait_step(slot):
        for _ in range(TOP_K):
            pltpu.make_async_copy(tile_ref.at[0], xs_ref.at[pl.ds(0, tm * TOKEN_SUB)], sems.at[slot]).wait()

    lax.fori_loop(0, tm, start_row, 0)

    @pl.when(i >= 1)
    def _():
        wait_step((i - 1) % 2)

    @pl.when(i == last)
    def _():
        wait_step(i % 2)
        zero_ref[...] = jnp.zeros_like(zero_ref)

        def per_expert(e, c):
            lo, hi = pad_ref[0, e], pad_ref[1, e]
            lax.fori_loop(lo, hi, lambda r, cc: (_row_copy(zero_ref, 0, xs_ref, r, sem).start(), cc)[1], 0)
            lax.fori_loop(lo, hi, lambda r, cc: (_row_copy(zero_ref, 0, xs_ref, r, sem).wait(), cc)[1], 0)
            return c

        lax.fori_loop(0, N_EXPERTS, per_expert, 0)

        def tail_copy(blk):
            return pltpu.make_async_copy(zero_ref, xs_ref.at[pl.ds(blk * bm * TOKEN_SUB, bm * TOKEN_SUB)], sem)

        n_blocks = xs_ref.shape[0] // (bm * TOKEN_SUB)
        lax.fori_loop(nb_ref[0], n_blocks, lambda blk, c: (tail_copy(blk).start(), c)[1], 0)
        lax.fori_loop(nb_ref[0], n_blocks, lambda blk, c: (tail_copy(blk).wait(), c)[1], 0)


def _dispatch(xm, dest_tiles, pad_range, n_used, n_rows):
    n, d = xm.shape
    tm = ROUTE_TILE
    return pl.pallas_call(
        _dispatch_kernel,
        grid=(n // tm,),
        in_specs=[pl.BlockSpec((1, TOP_K, tm), lambda i: (i, 0, 0), memory_space=pltpu.SMEM),
                  pl.BlockSpec(memory_space=pltpu.SMEM),
                  pl.BlockSpec(memory_space=pltpu.SMEM),
                  pl.BlockSpec((tm, d), lambda i: (i, 0))],
        out_specs=pl.BlockSpec(memory_space=pl.ANY),
        out_shape=jax.ShapeDtypeStruct((n_rows * TOKEN_SUB, LANE), F32),
        scratch_shapes=[pltpu.VMEM((2, tm * TOKEN_SUB, LANE), F32), pltpu.VMEM((EXPERT_BLOCK * TOKEN_SUB, LANE), F32),
                        pltpu.SemaphoreType.DMA((2,)), pltpu.SemaphoreType.DMA(())],
        compiler_params=_cparams("arbitrary"),
        name="dispatch",
    )(dest_tiles, pad_range, n_used, xm)


def _experts_kernel(first_ref, cnt_ref, nb_ref, xs_ref, wg_ref, wu_ref, wd_ref, ys_ref,
                    xbuf, ybuf, wg_s, wu_s, wd_s, in_sems, out_sems):
    e = pl.program_id(0)
    bm = EXPERT_BLOCK
    brows = bm * TOKEN_SUB
    total = nb_ref[0]
    n_blocks = xs_ref.shape[0] // brows

    def x_copy(g, slot):
        return pltpu.make_async_copy(xs_ref.at[pl.ds(g * brows, brows)], xbuf.at[slot], in_sems.at[slot])

    def y_copy(g, slot):
        return pltpu.make_async_copy(ybuf.at[slot], ys_ref.at[pl.ds(g * brows, brows)], out_sems.at[slot])

    depth = xbuf.shape[0]

    @pl.when(e == 0)
    def _():
        for g0 in range(depth - 1):
            @pl.when(g0 < total)
            def _():
                x_copy(g0, g0).start()

    @pl.when(cnt_ref[e] > 0)
    def _():
        wg_s[...] = wg_ref[0].astype(BF16)
        wu_s[...] = wu_ref[0].astype(BF16)
        wd_s[...] = wd_ref[0].astype(BF16)

    def block(c, carry):
        g = first_ref[e] + c
        slot = g % depth
        x_copy(g, slot).wait()

        @pl.when(g + depth - 1 < total)
        def _():
            x_copy(g + depth - 1, (g + depth - 1) % depth).start()

        x = jnp.concatenate([_from_token_tiles(xbuf.at[slot], bm, s) for s in range(TOKEN_SUB)], axis=1).astype(BF16)
        gate = jnp.dot(x, wg_s[...], preferred_element_type=F32)
        up = jnp.dot(x, wu_s[...], preferred_element_type=F32)
        y = jnp.dot((gate * jax.nn.sigmoid(gate) * up).astype(BF16), wd_s[...], preferred_element_type=F32)

        @pl.when(g >= depth)
        def _():
            y_copy(g - depth, slot).wait()

        _to_token_tiles(y, ybuf.at[slot])
        y_copy(g, slot).start()
        return carry

    lax.fori_loop(0, cnt_ref[e], block, 0)

    @pl.when(e == pl.num_programs(0) - 1)
    def _():
        for back in range(depth, 0, -1):
            @pl.when(total >= back)
            def _():
                y_copy(total - back, (total - back) % depth).wait()

        ybuf[0] = jnp.zeros(ybuf.shape[1:], F32)
        lax.fori_loop(total, n_blocks, lambda g, c: (y_copy(g, 0).start(), c)[1], 0)
        lax.fori_loop(total, n_blocks, lambda g, c: (y_copy(g, 0).wait(), c)[1], 0)


def _experts(xs, first_block, n_block, n_used, w_gate, w_up, w_down):
    brows = EXPERT_BLOCK * TOKEN_SUB
    n_exp, d, f = w_gate.shape
    grid_spec = pltpu.PrefetchScalarGridSpec(
        num_scalar_prefetch=3,
        grid=(n_exp,),
        in_specs=[pl.BlockSpec(memory_space=pl.ANY),
                  pl.BlockSpec((1, d, f), lambda e, *_: (e, 0, 0)),
                  pl.BlockSpec((1, d, f), lambda e, *_: (e, 0, 0)),
                  pl.BlockSpec((1, f, d), lambda e, *_: (e, 0, 0))],
        out_specs=pl.BlockSpec(memory_space=pl.ANY),
        scratch_shapes=[pltpu.VMEM((EXPERT_RING, brows, LANE), F32), pltpu.VMEM((EXPERT_RING, brows, LANE), F32),
                        pltpu.VMEM((d, f), BF16), pltpu.VMEM((d, f), BF16), pltpu.VMEM((f, d), BF16),
                        pltpu.SemaphoreType.DMA((EXPERT_RING,)), pltpu.SemaphoreType.DMA((EXPERT_RING,))],
    )
    return pl.pallas_call(
        _experts_kernel,
        grid_spec=grid_spec,
        out_shape=jax.ShapeDtypeStruct(xs.shape, F32),
        compiler_params=_cparams("arbitrary"),
        name="experts",
    )(first_block, n_block, n_used, xs, w_gate, w_up, w_down)


def _combine_kernel(dest_ref, dnext_ref, ys_ref, wrow_ref, xm_ref, sg_ref, su_ref, sd_ref, out_ref, buf_ref, sems):
    i = pl.program_id(0)
    tm = ROUTE_TILE
    slot = i % 2

    def issue(d_ref, s):
        def start_row(r, c):
            for k in range(TOP_K):
                _row_copy(ys_ref, d_ref[0, k, r], buf_ref.at[s, k], r, sems.at[s]).start()
            return c

        lax.fori_loop(0, tm, start_row, 0)

    @pl.when(i == 0)
    def _():
        issue(dest_ref, 0)

    @pl.when(i < pl.num_programs(0) - 1)
    def _():
        issue(dnext_ref, 1 - slot)

    x = xm_ref[...].astype(BF16)
    g = jnp.dot(x, sg_ref[...], preferred_element_type=F32)
    u = jnp.dot(x, su_ref[...], preferred_element_type=F32)
    shared = jnp.dot((g * jax.nn.sigmoid(g) * u).astype(BF16), sd_ref[...], preferred_element_type=F32)
    for k in range(TOP_K):
        pltpu.make_async_copy(ys_ref.at[pl.ds(0, tm * TOKEN_SUB)], buf_ref.at[slot, k], sems.at[slot]).wait()
    w = wrow_ref[...]
    for s in range(TOKEN_SUB):
        routed = w[:, 0:1] * _from_token_tiles(buf_ref.at[slot, 0], tm, s)
        for k in range(1, TOP_K):
            routed = routed + w[:, k:k + 1] * _from_token_tiles(buf_ref.at[slot, k], tm, s)
        out_ref[:, s * LANE:(s + 1) * LANE] = routed + shared[:, s * LANE:(s + 1) * LANE]


def _combine(ys, dest_tiles, wrow, xm, ws_gate, ws_up, ws_down):
    n, d = xm.shape
    tm = ROUTE_TILE
    f = ws_gate.shape[1]
    return pl.pallas_call(
        _combine_kernel,
        grid=(n // tm,),
        in_specs=[pl.BlockSpec((1, TOP_K, tm), lambda i: (i, 0, 0), memory_space=pltpu.SMEM),
                  pl.BlockSpec((1, TOP_K, tm), lambda i: (jnp.minimum(i + 1, n // tm - 1), 0, 0), memory_space=pltpu.SMEM),
                  pl.BlockSpec(memory_space=pl.ANY),
                  pl.BlockSpec((tm, LANE), lambda i: (i, 0)),
                  pl.BlockSpec((tm, d), lambda i: (i, 0)),
                  pl.BlockSpec((d, f), lambda i: (0, 0)),
                  pl.BlockSpec((d, f), lambda i: (0, 0)),
                  pl.BlockSpec((f, d), lambda i: (0, 0))],
        out_specs=pl.BlockSpec((tm, d), lambda i: (i, 0)),
        out_shape=jax.ShapeDtypeStruct((n, d), F32),
        scratch_shapes=[pltpu.VMEM((2, TOP_K, tm * TOKEN_SUB, LANE), F32), pltpu.SemaphoreType.DMA((2,))],
        compiler_params=_cparams("arbitrary"),
        name="combine",
    )(dest_tiles, dest_tiles, ys, wrow, xm, ws_gate.astype(BF16), ws_up.astype(BF16), ws_down.astype(BF16))


def _final_ln_kernel(x1_ref, moe_ref, gate_ref, g_ref, b_ref, y_ref):
    y_ref[0] = _layer_norm(DN_ALPHA * x1_ref[0] + gate_ref[0] * moe_ref[...], g_ref[...], b_ref[...])


def _final_ln(x1, moe, row0, gate_f, ln_g, ln_b, tm):
    b, t, d = x1.shape
    per_tok = gate_f.shape[1] != 1
    tok = pl.BlockSpec((1, tm, d), lambda i, j: (i, j, 0))
    mod_spec = tok if per_tok else pl.BlockSpec((1, 1, d), lambda i, j: (i, 0, 0))
    vec = pl.BlockSpec((1, d), lambda i, j: (0, 0))
    moe_spec = pl.BlockSpec((tm, d), lambda i, j: (row0 // tm + i * (t // tm) + j, 0))
    return pl.pallas_call(
        _final_ln_kernel,
        grid=(b, t // tm),
        in_specs=[tok, moe_spec, mod_spec, vec, vec],
        out_specs=tok,
        out_shape=jax.ShapeDtypeStruct((b, t, d), F32),
        compiler_params=_cparams("parallel", "parallel"),
        name="final_ln",
    )(x1, moe, gate_f, ln_g.reshape(1, d), ln_b.reshape(1, d))


def _moe(xm, w_router, router_bias, w_e_gate, w_e_up, w_e_down, w_s_gate, w_s_up, w_s_down):
    n = xm.shape[0]
    eidx, rank, wrow, cnt = _router(xm, w_router, router_bias)
    counts = cnt[:, 0].astype(I32)
    padded = (counts + EXPERT_BLOCK - 1) // EXPERT_BLOCK * EXPERT_BLOCK
    pad_end = jnp.cumsum(padded)
    pad_start = pad_end - padded
    dest_tiles = _dest(eidx, rank, pad_start)
    n_blocks = -(-(n * TOP_K) // EXPERT_BLOCK) + N_EXPERTS
    n_used = (pad_end[-1:] // EXPERT_BLOCK).astype(I32)
    pad_range = jnp.stack([pad_start + counts, pad_end]).astype(I32)
    xs = _dispatch(xm, dest_tiles, pad_range, n_used, n_blocks * EXPERT_BLOCK)
    ys = _experts(xs, (pad_start // EXPERT_BLOCK).astype(I32), (padded // EXPERT_BLOCK).astype(I32), n_used,
                  w_e_gate, w_e_up, w_e_down)
    return _combine(ys, dest_tiles, wrow, xm, w_s_gate, w_s_up, w_s_down)


def kernel(x_prompt, x_sample, cache_kv_cmp, cache_kv_sel, state_kv_win, state_gla, page_table, c_prompt, c_sample, w_in, b_in, cmp_k_pos, cmp_k_w1, cmp_k_w2, cmp_v_pos, cmp_v_w1, cmp_v_w2, gla_w_a2, gla_b_a, gla_norm_g, w_br_a, w_br_b, w_out, ln1_g, ln1_b, w_ada, b_ada, w_router, router_bias, w_e_gate, w_e_up, w_e_down, w_s_gate, w_s_up, w_s_down, ln2_g, ln2_b):
    bp, tp, d = x_prompt.shape
    nd, td = x_sample.shape[:2]
    n_pool, page_rows = cache_kv_cmp.shape[:2]
    past_len = page_table.shape[1] * page_rows
    assert td == 1 and d == D_MODEL and page_rows == PAGE_ROWS and page_table.shape[1] == PAGES and tp == PAGES * PAGE_ROWS
    kv_w = 2 * NSA_KV_HEADS * HEAD_DIM

    mod = _adaln(jnp.concatenate([c_prompt, c_sample], axis=0), w_ada, b_ada)
    mod_p = [m.reshape(bp, 1, d) for m in jnp.split(mod[:bp], 6, axis=-1)]
    mod_s = [m.reshape(1, nd, d) for m in jnp.split(mod[bp:], 6, axis=-1)]

    w_pack, b_pack = _pack_in_weights(w_in, b_in)
    cmp_wk = _pack_cmp_weights(cmp_k_pos, cmp_k_w1, cmp_k_w2)
    cmp_wv = _pack_cmp_weights(cmp_v_pos, cmp_v_w1, cmp_v_w2)
    wa_pad = jnp.zeros((LANE, GLA_HEADS * GLA_DK), F32).at[MISC_AG:MISC_AG + GLA_GATE_RANK].set(gla_w_a2)
    tail_consts = (_gate_expand_table(), gla_norm_g.reshape(1, -1), _pad_br_a(w_br_a), w_br_b.astype(BF16), w_out.astype(BF16),
                   ln1_g.reshape(1, d), ln1_b.reshape(1, d))

    gm, qn, vg, rg, kvc, kvs, kvw, qg, kg, misc = _inproj(
        x_prompt, mod_p[0], mod_p[1], w_pack, b_pack, _rope_tables(jnp.arange(tp, dtype=I32)), 256)
    kc, vc = _compress(kvc.reshape(bp * PAGES, PAGE_ROWS, kv_w), jnp.arange(bp * PAGES, dtype=I32), cmp_wk, cmp_wv)
    ocmp, sel = _cmp_attn(qn, kc, vc, 256)
    osel = _sel_attn(qn, kvs, sel, 128, 256)
    owin = _win_attn(qn, kvw, 128)
    ogla, gla_p = _gla(qg, kg, vg, misc, wa_pad, gla_b_a, None, tp)
    x1_p, xm_p = _mixer_tail(ocmp, osel, owin, misc, ogla, rg, gm, x_prompt, mod_p[2], mod_p[4], mod_p[3], tail_consts, 256)
    n_win = min(WINDOW, tp)
    outs_p = (kvc.reshape(bp, tp, 2, NSA_KV_HEADS, HEAD_DIM), kvs.reshape(bp, tp, 2, NSA_KV_HEADS, HEAD_DIM),
              kvw[:, tp - n_win:].reshape(bp, n_win, 2, NSA_KV_HEADS, HEAD_DIM), gla_p)

    gm, qn, vg, rg, kvc, kvs, kvw, qg, kg, misc = _inproj(
        x_sample.reshape(1, nd, d), mod_s[0], mod_s[1], w_pack, b_pack, _rope_tables(jnp.full((nd,), past_len, I32)), nd)
    page_ids = page_table.reshape(-1).astype(I32)
    kc, vc = _compress(cache_kv_cmp.reshape(n_pool, PAGE_ROWS, kv_w).transpose(0, 2, 1), page_ids, cmp_wk, cmp_wv,
                       feature_major=True)
    ocmp, osel, owin, win_new = _nsa_decode(
        qn.reshape(nd, 1, -1), kc, vc, cache_kv_sel.reshape(n_pool, PAGE_ROWS, kv_w).transpose(0, 2, 1), page_ids,
        kvs.reshape(nd, 1, kv_w), kvw.reshape(nd, 1, kv_w), state_kv_win.reshape(nd, -1, kv_w).transpose(0, 2, 1))
    win_new = win_new.transpose(0, 2, 1)

    def pad_rows(a):
        return jnp.pad(a.reshape(nd, 1, -1), ((0, 0), (0, GLA_SUB - 1), (0, 0)))

    ogla, gla_s = _gla(pad_rows(qg), pad_rows(kg), pad_rows(vg), pad_rows(misc), wa_pad, gla_b_a, state_gla, 1)
    x1_s, xm_s = _mixer_tail(ocmp.reshape(1, nd, -1), osel.reshape(1, nd, -1), owin.reshape(1, nd, -1), misc,
                             ogla[:, 0].reshape(1, nd, -1), rg, gm, x_sample.reshape(1, nd, d),
                             mod_s[2], mod_s[4], mod_s[3], tail_consts, nd)
    outs_s = (kvc.reshape(nd, 1, 2, NSA_KV_HEADS, HEAD_DIM), kvs.reshape(nd, 1, 2, NSA_KV_HEADS, HEAD_DIM),
              win_new.reshape(state_kv_win.shape), gla_s)

    n_p = bp * tp
    moe = _moe(jnp.concatenate([xm_p.reshape(n_p, d), xm_s.reshape(nd, d)], axis=0),
               w_router, router_bias, w_e_gate, w_e_up, w_e_down, w_s_gate, w_s_up, w_s_down)
    y_p = _final_ln(x1_p, moe, 0, mod_p[5], ln2_g, ln2_b, 256)
    y_s = _final_ln(x1_s, moe, n_p, mod_s[5], ln2_g, ln2_b, nd).reshape(nd, 1, d)
    return (y_p, y_s) + outs_p + outs_s
```

```python
import functools

import numpy as np
import jax
import jax.numpy as jnp
from jax import lax
from jax.experimental import pallas as pl
from jax.experimental.pallas import tpu as pltpu

F32 = jnp.float32
BF16 = jnp.bfloat16
I32 = jnp.int32

D_MODEL = 1024
NSA_HEADS = 8
NSA_KV_HEADS = 2
NSA_GROUP = NSA_HEADS // NSA_KV_HEADS
HEAD_DIM = 64
ROT_DIM = HEAD_DIM // 4
ROPE_THETA = 500000.0
CMP_LEN = 32
CMP_STRIDE = 16
CMP_HIDDEN = 256
SEL_BLOCK = 64
SEL_TOP_N = 16
WINDOW = 512
FORCE_SCORE = 1.0e4
GLA_HEADS = 4
GLA_DK = 64
GLA_DV = 128
GLA_GATE_RANK = 16
GLA_TAU = 16.0
GLA_SUB = 16
N_EXPERTS = 256
TOP_K = 8
N_GROUPS = 8
TOPK_GROUPS = 4
EXPERT_DIM = 256
SHARED_DIM = 256
ROUTED_SCALE = 2.5
EXPERT_BLOCK = 128
EXPERT_RING = 4
DN_ALPHA = 2.0 ** 0.25
LN_EPS = 1e-5
LANE = 128
NEG = -1.0e30
VMEM_LIMIT = 56 * 1024 * 1024

SEG_GM = (0, 2 * D_MODEL)
SEG_QN = (SEG_GM[0] + SEG_GM[1], NSA_HEADS * LANE)
SEG_VG = (SEG_QN[0] + SEG_QN[1], GLA_HEADS * GLA_DV)
SEG_RG = (SEG_VG[0] + SEG_VG[1], GLA_HEADS * GLA_DV)
SEG_KVC = (SEG_RG[0] + SEG_RG[1], 2 * LANE)
SEG_KVS = (SEG_KVC[0] + SEG_KVC[1], 2 * LANE)
SEG_KVW = (SEG_KVS[0] + SEG_KVS[1], 2 * LANE)
SEG_QG = (SEG_KVW[0] + SEG_KVW[1], GLA_HEADS * GLA_DK)
SEG_KG = (SEG_QG[0] + SEG_QG[1], GLA_HEADS * GLA_DK)
SEG_MISC = (SEG_KG[0] + SEG_KG[1], LANE)
IN_PACKED = SEG_MISC[0] + SEG_MISC[1]
MISC_GN = 0
MISC_AG = NSA_HEADS * 3


def _cparams(*sem):
    return pltpu.CompilerParams(dimension_semantics=sem, vmem_limit_bytes=VMEM_LIMIT)


def _bdot(a, b):
    return jnp.dot(a.astype(BF16), b.astype(BF16), preferred_element_type=F32)


def _dot_nt(a, b, precision=None):
    return lax.dot_general(a, b, (((1,), (1,)), ((), ())), preferred_element_type=F32, precision=precision)


def _adaln_kernel(c_ref, w_ref, b_ref, o_ref):
    c = c_ref[...]
    o_ref[...] = _bdot(c * jax.nn.sigmoid(c), w_ref[...]) + b_ref[...]


def _adaln(c, w_ada, b_ada):
    n, d = c.shape
    m = w_ada.shape[1]
    tn = 512
    return pl.pallas_call(
        _adaln_kernel,
        grid=(m // tn,),
        in_specs=[pl.BlockSpec((n, d), lambda j: (0, 0)),
                  pl.BlockSpec((d, tn), lambda j: (0, j)),
                  pl.BlockSpec((1, tn), lambda j: (0, j))],
        out_specs=pl.BlockSpec((n, tn), lambda j: (0, j)),
        out_shape=jax.ShapeDtypeStruct((n, m), F32),
        compiler_params=_cparams("parallel"),
        name="adaln",
    )(c, w_ada, b_ada.reshape(1, m))


def _rope_tables(pos):
    half = ROT_DIM // 2
    inv = jnp.power(ROPE_THETA, -jnp.arange(half, dtype=F32) * 2.0 / ROT_DIM)
    ang = pos.astype(F32)[:, None] * inv[None, :]
    cos, sin = jnp.cos(ang), jnp.sin(ang)
    t = pos.shape[0]
    one = jnp.ones((t, HEAD_DIM - ROT_DIM), F32)
    z8 = jnp.zeros((t, half), F32)
    z48 = jnp.zeros((t, HEAD_DIM - ROT_DIM), F32)
    c = jnp.concatenate([cos, cos, one, cos, cos, one], axis=1)
    s1 = jnp.concatenate([-sin, z8, z48, -sin, z8, z48], axis=1)
    s2 = jnp.concatenate([z8, sin, z48, z8, sin, z48], axis=1)
    return c, s1, s2


def _pack_in_weights(w_in, b_in):
    sizes = (512, 128, 128, 128, 128, 128, 128, 24, 256, 256, 512, 512, 16, 2048)
    offs = np.concatenate([[0], np.cumsum(sizes)])

    def pack(w):
        seg = [w[..., offs[i]:offs[i + 1]] for i in range(len(sizes))]
        q_n, k_c, v_c, k_s, v_s, k_w, v_w, g_n, q_g, k_g, v_g, r_g, a_g, g_m = seg
        zero = jnp.zeros_like(q_n[..., :HEAD_DIM])
        q_slots = []
        for hh in range(NSA_HEADS):
            qh = q_n[..., hh * HEAD_DIM:(hh + 1) * HEAD_DIM] * (HEAD_DIM ** -0.5)
            q_slots += [qh, zero] if hh // NSA_GROUP == 0 else [zero, qh]
        misc_pad = jnp.zeros_like(w[..., :LANE - g_n.shape[-1] - a_g.shape[-1]])
        return jnp.concatenate([g_m] + q_slots + [v_g, r_g, k_c, v_c, k_s, v_s, k_w, v_w,
                                                   q_g * (GLA_DK ** -0.5), k_g, g_n, a_g, misc_pad], axis=-1)

    return pack(w_in).astype(BF16), pack(b_in.reshape(1, -1))


def _inproj_kernel(x_ref, sh_ref, sc_ref, w_ref, b_ref, rc_ref, rs1_ref, rs2_ref,
                   gm_ref, qn_ref, vg_ref, rg_ref, kvc_ref, kvs_ref, kvw_ref, qg_ref, kg_ref, misc_ref):
    h = (x_ref[0] * (1.0 + sc_ref[0]) + sh_ref[0]).astype(BF16)
    rc, rs1, rs2 = rc_ref[...], rs1_ref[...], rs2_ref[...]

    def proj(off, width):
        return jnp.dot(h, w_ref[:, off:off + width], preferred_element_type=F32) + b_ref[:, off:off + width]

    def rope(z):
        return z * rc + pltpu.roll(z, LANE - ROT_DIM // 2, 1) * rs1 + pltpu.roll(z, ROT_DIM // 2, 1) * rs2

    def plain(ref, seg):
        off, width = seg
        step = min(width, 512)
        for c in range(0, width, step):
            ref[0, :, c:c + step] = proj(off + c, step)

    plain(gm_ref, SEG_GM)
    for c in range(0, SEG_QN[1], 512):
        z = proj(SEG_QN[0] + c, 512)
        for s in range(0, 512, LANE):
            qn_ref[0, :, c + s:c + s + LANE] = rope(z[:, s:s + LANE])
    plain(vg_ref, SEG_VG)
    plain(rg_ref, SEG_RG)
    for ref, seg in ((kvc_ref, SEG_KVC), (kvs_ref, SEG_KVS), (kvw_ref, SEG_KVW)):
        z = proj(seg[0], seg[1])
        ref[0, :, 0:LANE] = rope(z[:, 0:LANE])
        ref[0, :, LANE:2 * LANE] = z[:, LANE:2 * LANE]
    plain(qg_ref, SEG_QG)
    plain(kg_ref, SEG_KG)
    plain(misc_ref, SEG_MISC)


def _inproj(x, shift, scale, w_pack, b_pack, tables, tm):
    b, t, d = x.shape
    per_tok = shift.shape[1] != 1
    mod_spec = (pl.BlockSpec((1, tm, d), lambda i, j: (i, j, 0)) if per_tok
                else pl.BlockSpec((1, 1, d), lambda i, j: (i, 0, 0)))
    segs = (SEG_GM, SEG_QN, SEG_VG, SEG_RG, SEG_KVC, SEG_KVS, SEG_KVW, SEG_QG, SEG_KG, SEG_MISC)
    tab_spec = pl.BlockSpec((tm, LANE), lambda i, j: (j, 0))
    return pl.pallas_call(
        _inproj_kernel,
        grid=(b, t // tm),
        in_specs=[pl.BlockSpec((1, tm, d), lambda i, j: (i, j, 0)), mod_spec, mod_spec,
                  pl.BlockSpec((d, IN_PACKED), lambda i, j: (0, 0)),
                  pl.BlockSpec((1, IN_PACKED), lambda i, j: (0, 0)),
                  tab_spec, tab_spec, tab_spec],
        out_specs=[pl.BlockSpec((1, tm, w), lambda i, j: (i, j, 0)) for _, w in segs],
        out_shape=[jax.ShapeDtypeStruct((b, t, w), F32) for _, w in segs],
        compiler_params=_cparams("parallel", "parallel"),
        name="inproj",
    )(x, shift, scale, w_pack, b_pack, *tables)


CHUNKS = 128
PAGE_ROWS = 128
PAGES = 16


def _pack_cmp_weights(pos, w1, w2):
    pos2 = jnp.concatenate([pos, pos], axis=1)
    z1 = jnp.zeros_like(w1)
    bd1 = jnp.concatenate([jnp.concatenate([w1, z1], axis=2), jnp.concatenate([z1, w1], axis=2)], axis=1)
    w1p = jnp.concatenate([bd1[:CMP_STRIDE], bd1[CMP_STRIDE:]], axis=2).astype(BF16)
    z2 = jnp.zeros_like(w2)
    w2p = jnp.concatenate([jnp.concatenate([w2, z2], axis=1), jnp.concatenate([z2, w2], axis=1)], axis=0).astype(BF16)
    return pos2, w1p, w2p


def _fill_chunks(page_refs, xs_ref, rows_ref, feature_major):
    if feature_major:
        for p, pr in enumerate(page_refs):
            rows_ref[p * PAGE_ROWS:(p + 1) * PAGE_ROWS, :] = pr[0].T
        for l in range(CMP_STRIDE):
            xs_ref[l] = rows_ref[pl.ds(l, CHUNKS, stride=CMP_STRIDE), :]
        return
    for p, pr in enumerate(page_refs):
        for l in range(CMP_STRIDE):
            xs_ref[l, 8 * p:8 * p + 8, :] = pr[0, pl.ds(l, PAGE_ROWS // CMP_STRIDE, stride=CMP_STRIDE), :]


def _compress_chunks(xs_ref, pos_ref, w1_ref, w2_ref):
    hid2 = 2 * CMP_HIDDEN
    acc_a = jnp.zeros((CHUNKS, hid2), F32)
    acc_b = jnp.zeros((CHUNKS, hid2), F32)
    for l in range(CMP_STRIDE):
        x = xs_ref[l]
        acc_a = acc_a + jnp.dot((x + pos_ref[l:l + 1, :]).astype(BF16), w1_ref[l, :, 0:hid2], preferred_element_type=F32)
        acc_b = acc_b + jnp.dot((x + pos_ref[CMP_STRIDE + l:CMP_STRIDE + l + 1, :]).astype(BF16), w1_ref[l, :, hid2:2 * hid2],
                                preferred_element_type=F32)
    hid = acc_a + pltpu.roll(acc_b, CHUNKS - 1, 0)
    out = jnp.dot(jax.nn.gelu(hid).astype(BF16), w2_ref[...], preferred_element_type=F32)
    row = lax.broadcasted_iota(I32, out.shape, 0)
    return jnp.where(row < CHUNKS - 1, out, 0.0)


def _compress_kernel(pt_ref, *refs, feature_major):
    k_pages, v_pages = refs[:PAGES], refs[PAGES:2 * PAGES]
    posk_ref, w1k_ref, w2k_ref, posv_ref, w1v_ref, w2v_ref, kc_ref, vc_ref, xk_ref, xv_ref, rows_ref = refs[2 * PAGES:]
    _fill_chunks(k_pages, xk_ref, rows_ref, feature_major)
    kc_ref[0] = _compress_chunks(xk_ref, posk_ref, w1k_ref, w2k_ref)
    _fill_chunks(v_pages, xv_ref, rows_ref, feature_major)
    vc_ref[0] = _compress_chunks(xv_ref, posv_ref, w1v_ref, w2v_ref)


def _page_spec(p, half, feature_major):
    if feature_major:
        return pl.BlockSpec((1, LANE, PAGE_ROWS), lambda i, pt: (pt[i * PAGES + p], half, 0))
    return pl.BlockSpec((1, PAGE_ROWS, LANE), lambda i, pt: (pt[i * PAGES + p], 0, half))


def _const_spec(shape):
    nd = len(shape)
    return pl.BlockSpec(shape, lambda i, pt: (0,) * nd)


def _compress(pages, page_ids, cmp_wk, cmp_wv, feature_major=False):
    n_b = page_ids.shape[0] // PAGES
    consts = list(cmp_wk) + list(cmp_wv)
    grid_spec = pltpu.PrefetchScalarGridSpec(
        num_scalar_prefetch=1,
        grid=(n_b,),
        in_specs=[_page_spec(p, h, feature_major) for h in range(2) for p in range(PAGES)] + [_const_spec(c.shape) for c in consts],
        out_specs=[pl.BlockSpec((1, CHUNKS, LANE), lambda i, pt: (i, 0, 0))] * 2,
        scratch_shapes=[pltpu.VMEM((CMP_STRIDE, CHUNKS, LANE), F32)] * 2 + [pltpu.VMEM((PAGES * PAGE_ROWS, LANE), F32)],
    )
    return pl.pallas_call(
        functools.partial(_compress_kernel, feature_major=feature_major),
        grid_spec=grid_spec,
        out_shape=[jax.ShapeDtypeStruct((n_b, CHUNKS, LANE), F32)] * 2,
        compiler_params=_cparams("parallel"),
        name="compress",
    )(page_ids, *([pages] * (2 * PAGES)), *consts)


def _cover_tables(n_sel):
    c_start = np.arange(CHUNKS) * CMP_STRIDE
    s_start = np.arange(n_sel) * SEL_BLOCK
    cover = ((c_start[:, None] < s_start[None, :] + SEL_BLOCK) & (c_start[:, None] + CMP_LEN > s_start[None, :])).astype(np.float32)
    cover[CHUNKS - 1] = 0.0
    out = np.zeros((NSA_KV_HEADS, LANE, CHUNKS), np.float32)
    for h in range(NSA_KV_HEADS):
        out[h, h * 64:h * 64 + n_sel] = cover.T
    return jnp.asarray(out)


def _softmax_rows(s, valid):
    s = jnp.where(valid, s, NEG)
    m = jnp.max(s, axis=-1, keepdims=True)
    m = jnp.where(m > 0.5 * NEG, m, 0.0)
    p = jnp.where(valid, jnp.exp(s - m), 0.0)
    return p / jnp.maximum(jnp.sum(p, axis=-1, keepdims=True), 1e-30)


def _select_blocks(imp, n_sel, top_n):
    ridx = lax.broadcasted_iota(I32, imp.shape, 0)
    cnt = jnp.zeros(imp.shape, F32)
    for i in range(n_sel):
        vi = imp[i:i + 1, :]
        ahead = (vi > imp) | ((vi == imp) & (ridx > i))
        cnt = cnt + jnp.where(ahead, 1.0, 0.0)
    return jnp.where((cnt < top_n) & (ridx < n_sel), 1.0, 0.0)


def _cmp_attn_kernel(qn_ref, kc_ref, vc_ref, cov_ref, o_ref, sel_ref, *, tq, n_sel):
    qi = pl.program_id(1)
    kc = kc_ref[0].astype(BF16)
    vc = vc_ref[0].astype(BF16)
    qpos = qi * tq + lax.broadcasted_iota(I32, (tq, CHUNKS), 0)
    cidx = lax.broadcasted_iota(I32, (tq, CHUNKS), 1)
    valid = (cidx * CMP_STRIDE + CMP_LEN - 1 <= qpos) & (cidx < CHUNKS - 1)
    psum = [jnp.zeros((tq, CHUNKS), F32) for _ in range(NSA_KV_HEADS)]
    for hh in range(NSA_HEADS):
        q = qn_ref[0, :, hh * LANE:(hh + 1) * LANE].astype(BF16)
        p = _softmax_rows(_dot_nt(q, kc), valid)
        o_ref[0, :, hh * LANE:(hh + 1) * LANE] = jnp.dot(p.astype(BF16), vc, preferred_element_type=F32)
        psum[hh // NSA_GROUP] = psum[hh // NSA_GROUP] + p
    imp = (_dot_nt(cov_ref[0], psum[0], lax.Precision.HIGHEST) + _dot_nt(cov_ref[1], psum[1], lax.Precision.HIGHEST))
    blk = lax.broadcasted_iota(I32, (LANE, tq), 0) & 63
    qpos_t = qi * tq + lax.broadcasted_iota(I32, (LANE, tq), 1)
    cur = qpos_t // SEL_BLOCK
    forced = (blk == 0) | (blk == cur) | (blk == cur - 1)
    imp = jnp.where(forced, FORCE_SCORE, jnp.where(blk * SEL_BLOCK <= qpos_t, imp, -FORCE_SCORE))
    sel_t = jnp.concatenate([_select_blocks(imp[0:64], n_sel, SEL_TOP_N), _select_blocks(imp[64:128], n_sel, SEL_TOP_N)], axis=0)
    sel_ref[0] = sel_t.T


def _cmp_attn(qn, kc, vc, tq):
    b, t, _ = qn.shape
    n_sel = -(-t // SEL_BLOCK)
    cov = _cover_tables(n_sel)
    return pl.pallas_call(
        functools.partial(_cmp_attn_kernel, tq=tq, n_sel=n_sel),
        grid=(b, t // tq),
        in_specs=[pl.BlockSpec((1, tq, NSA_HEADS * LANE), lambda i, j: (i, j, 0)),
                  pl.BlockSpec((1, CHUNKS, LANE), lambda i, j: (i, 0, 0)),
                  pl.BlockSpec((1, CHUNKS, LANE), lambda i, j: (i, 0, 0)),
                  pl.BlockSpec((NSA_KV_HEADS, LANE, CHUNKS), lambda i, j: (0, 0, 0))],
        out_specs=[pl.BlockSpec((1, tq, NSA_HEADS * LANE), lambda i, j: (i, j, 0)),
                   pl.BlockSpec((1, tq, LANE), lambda i, j: (i, j, 0))],
        out_shape=[jax.ShapeDtypeStruct((b, t, NSA_HEADS * LANE), F32), jax.ShapeDtypeStruct((b, t, LANE), F32)],
        compiler_params=_cparams("parallel", "parallel"),
        name="cmp_attn",
    )(qn, kc, vc, cov)


def _key_block_table(t):
    blk = np.arange(t)[:, None] // SEL_BLOCK
    lanes = np.arange(LANE)[None, :] & 63
    return jnp.asarray((blk == lanes).astype(np.float32), dtype=BF16)


def _lane_fold(x, op):
    out = x[:, 0:LANE]
    for c in range(LANE, x.shape[1], LANE):
        out = op(out, x[:, c:c + LANE])
    return out


def _sel_attn_kernel(qn_ref, kv_ref, sel_ref, kb_ref, o_ref, q2_scr, k2_scr, v_scr, s_scr, *, tq, tk):
    qi = pl.program_id(1)
    rows = NSA_HEADS * tq
    t = kv_ref.shape[1]

    @pl.when(qi == 0)
    def _():
        k2_scr[:, 0:LANE] = kv_ref[0, :, 0:LANE].astype(BF16)
        k2_scr[:, LANE:2 * LANE] = kb_ref[...]
        v_scr[...] = kv_ref[0, :, LANE:2 * LANE].astype(BF16)

    not_sel = (1.0 - sel_ref[0]) * NEG
    lane_head = lax.broadcasted_iota(I32, (tq, LANE), 1) // 64
    for hh in range(NSA_HEADS):
        q2_scr[hh * tq:(hh + 1) * tq, 0:LANE] = qn_ref[0, :, hh * LANE:(hh + 1) * LANE].astype(BF16)
        q2_scr[hh * tq:(hh + 1) * tq, LANE:2 * LANE] = jnp.where(lane_head == hh // NSA_GROUP, not_sel, 0.0).astype(BF16)
    q2 = q2_scr[...]
    last = (qi * tq + tq - 1) // tk

    def scores(kt):
        k0 = pl.multiple_of(kt * tk, tk)
        return _dot_nt(q2, k2_scr[pl.ds(k0, tk), :])

    def pass1(kt, m_acc):
        s = scores(kt)
        s_scr[kt] = s
        return jnp.maximum(m_acc, _lane_fold(s, jnp.maximum))

    m_acc = lax.fori_loop(0, last, pass1, jnp.full((rows, LANE), NEG, F32))
    rel = (qi * tq + lax.broadcasted_iota(I32, (tq, tk), 0)) - (last * tk + lax.broadcasted_iota(I32, (tq, tk), 1))
    causal = jnp.where(rel >= 0, 0.0, NEG)
    s_last = scores(last) + jnp.concatenate([causal] * NSA_HEADS, axis=0)
    m = jnp.max(jnp.maximum(m_acc, _lane_fold(s_last, jnp.maximum)), axis=-1, keepdims=True)

    def accumulate(s, v, l_acc, acc):
        p = jnp.exp(s - m)
        return l_acc + _lane_fold(p, jnp.add), acc + jnp.dot(p.astype(BF16), v, preferred_element_type=F32)

    def pass2(kt, carry):
        k0 = pl.multiple_of(kt * tk, tk)
        return accumulate(s_scr[kt], v_scr[pl.ds(k0, tk), :], *carry)

    l_acc, acc = lax.fori_loop(0, last, pass2, (jnp.zeros((rows, LANE), F32), jnp.zeros((rows, LANE), F32)))
    l_acc, acc = accumulate(s_last, v_scr[pl.ds(pl.multiple_of(last * tk, tk), tk), :], l_acc, acc)
    out = acc / jnp.sum(l_acc, axis=-1, keepdims=True)
    for hh in range(NSA_HEADS):
        o_ref[0, :, hh * LANE:(hh + 1) * LANE] = out[hh * tq:(hh + 1) * tq]


def _sel_attn(qn, kv, sel, tq, tk):
    b, t, _ = qn.shape
    rows = NSA_HEADS * tq
    return pl.pallas_call(
        functools.partial(_sel_attn_kernel, tq=tq, tk=tk),
        grid=(b, t // tq),
        in_specs=[pl.BlockSpec((1, tq, NSA_HEADS * LANE), lambda i, j: (i, j, 0)),
                  pl.BlockSpec((1, t, 2 * LANE), lambda i, j: (i, 0, 0)),
                  pl.BlockSpec((1, tq, LANE), lambda i, j: (i, j, 0)),
                  pl.BlockSpec((t, LANE), lambda i, j: (0, 0))],
        out_specs=pl.BlockSpec((1, tq, NSA_HEADS * LANE), lambda i, j: (i, j, 0)),
        out_shape=jax.ShapeDtypeStruct((b, t, NSA_HEADS * LANE), F32),
        scratch_shapes=[pltpu.VMEM((rows, 2 * LANE), BF16), pltpu.VMEM((t, 2 * LANE), BF16), pltpu.VMEM((t, LANE), BF16),
                        pltpu.VMEM((t // tk, rows, tk), F32)],
        compiler_params=_cparams("parallel", "arbitrary"),
        name="sel_attn",
    )(qn, kv, sel, _key_block_table(t))


def _win_attn_kernel(qn_ref, kv_ref, o_ref, q_scr, k_scr, v_scr, *, tq):
    qi = pl.program_id(1)
    span = WINDOW + tq

    @pl.when(qi == 0)
    def _():
        k_scr[...] = kv_ref[0, :, 0:LANE].astype(BF16)
        v_scr[...] = kv_ref[0, :, LANE:2 * LANE].astype(BF16)

    for hh in range(NSA_HEADS):
        q_scr[hh * tq:(hh + 1) * tq, :] = qn_ref[0, :, hh * LANE:(hh + 1) * LANE].astype(BF16)
    k0 = pl.multiple_of(jnp.maximum(qi * tq - WINDOW, 0), tq)
    rel = (qi * tq + lax.broadcasted_iota(I32, (tq, span), 0)) - (k0 + lax.broadcasted_iota(I32, (tq, span), 1))
    bias = jnp.where((rel >= 0) & (rel <= WINDOW), 0.0, NEG)
    s = _dot_nt(q_scr[...], k_scr[pl.ds(k0, span), :]) + jnp.concatenate([bias] * NSA_HEADS, axis=0)
    p = jnp.exp(s - jnp.max(s, axis=-1, keepdims=True))
    out = jnp.dot(p.astype(BF16), v_scr[pl.ds(k0, span), :], preferred_element_type=F32) / jnp.sum(p, axis=-1, keepdims=True)
    for hh in range(NSA_HEADS):
        o_ref[0, :, hh * LANE:(hh + 1) * LANE] = out[hh * tq:(hh + 1) * tq]


def _win_attn(qn, kv, tq):
    b, t, _ = qn.shape
    assert t >= WINDOW + tq
    rows = NSA_HEADS * tq
    return pl.pallas_call(
        functools.partial(_win_attn_kernel, tq=tq),
        grid=(b, t // tq),
        in_specs=[pl.BlockSpec((1, tq, NSA_HEADS * LANE), lambda i, j: (i, j, 0)),
                  pl.BlockSpec((1, t, 2 * LANE), lambda i, j: (i, 0, 0))],
        out_specs=pl.BlockSpec((1, tq, NSA_HEADS * LANE), lambda i, j: (i, j, 0)),
        out_shape=jax.ShapeDtypeStruct((b, t, NSA_HEADS * LANE), F32),
        scratch_shapes=[pltpu.VMEM((rows, LANE), BF16), pltpu.VMEM((t, LANE), BF16), pltpu.VMEM((t, LANE), BF16)],
        compiler_params=_cparams("parallel", "arbitrary"),
        name="win_attn",
    )(qn, kv)


def _dec_softmax(scores, vals):
    m = scores[0].max(axis=-1, keepdims=True)
    for s in scores[1:]:
        m = jnp.maximum(m, s.max(axis=-1, keepdims=True))
    den = jnp.zeros_like(m)
    out = jnp.zeros((m.shape[0], LANE), F32)
    for s, v in zip(scores, vals):
        p = jnp.exp(s - m)
        den = den + p.sum(axis=-1, keepdims=True)
        out = out + (p * v if s.shape[1] == 1 else _dot_nt(p.astype(BF16), v))
    return out / den


DEC_GROUP = 2


def _nsa_decode_kernel(pt_ref, *refs, past_len, n_sel):
    per_seq = refs[DEC_GROUP * PAGES:]
    cov_ref = per_seq[6]
    for bb in range(DEC_GROUP):
        views = [r.at[pl.ds(bb, 1)] for i, r in enumerate(per_seq) if i != 6]
        _nsa_decode_one(refs[bb * PAGES:(bb + 1) * PAGES], *views[:6], cov_ref, *views[6:], past_len=past_len, n_sel=n_sel)


def _nsa_decode_one(pages, qn_ref, kc_ref, vc_ref, kvs_ref, kvw_ref, win_ref, cov_ref,
                    ocmp_ref, osel_ref, owin_ref, wnew_ref, *, past_len, n_sel):
    nh = NSA_HEADS
    q = jnp.concatenate([qn_ref[0, :, hh * LANE:(hh + 1) * LANE] for hh in range(nh)], axis=0)
    qb = q.astype(BF16)
    cidx = lax.broadcasted_iota(I32, (nh, CHUNKS), 1)
    valid = (cidx * CMP_STRIDE + CMP_LEN - 1 <= past_len) & (cidx < CHUNKS - 1)
    p = _softmax_rows(_dot_nt(qb, kc_ref[0].astype(BF16)), valid)
    o_cmp = jnp.dot(p.astype(BF16), vc_ref[0].astype(BF16), preferred_element_type=F32)
    imp = None
    for h in range(NSA_KV_HEADS):
        ps = jnp.sum(p[h * NSA_GROUP:(h + 1) * NSA_GROUP], axis=0, keepdims=True)
        term = _dot_nt(cov_ref[h], jnp.broadcast_to(ps, (LANE, CHUNKS)), lax.Precision.HIGHEST)
        imp = term if imp is None else imp + term
    blk = lax.broadcasted_iota(I32, (LANE, LANE), 0) & 63
    cur = past_len // SEL_BLOCK
    forced = (blk == 0) | (blk == cur) | (blk == cur - 1)
    imp = jnp.where(forced, FORCE_SCORE, jnp.where(blk * SEL_BLOCK <= past_len, imp, -FORCE_SCORE))
    sel_t = jnp.concatenate([_select_blocks(imp[0:64], n_sel, min(SEL_TOP_N, n_sel)),
                             _select_blocks(imp[64:128], n_sel, min(SEL_TOP_N, n_sel))], axis=0)
    sel = sel_t.T[0:1, :]
    head_of_row = lax.broadcasted_iota(I32, (nh, 1), 0) // NSA_GROUP

    def picked(s):
        return jnp.where(head_of_row == 0, sel[:, s:s + 1], sel[:, 64 + s:64 + s + 1])

    first_half = lax.broadcasted_iota(I32, (nh, PAGE_ROWS), 1) < SEL_BLOCK
    scores, vals = [], []
    for pg, pr in enumerate(pages):
        s = jnp.dot(qb, pr[0, 0:LANE, :].astype(BF16), preferred_element_type=F32)
        ok = jnp.where(first_half, picked(2 * pg), picked(2 * pg + 1)) > 0.5
        scores.append(jnp.where(ok, s, NEG))
        vals.append(pr[0, LANE:2 * LANE, :].astype(BF16))
    s_new = jnp.sum(q * kvs_ref[0, :, 0:LANE], axis=-1, keepdims=True)
    scores.append(jnp.where(picked(past_len // SEL_BLOCK) > 0.5, s_new, NEG))
    vals.append(kvs_ref[0, :, LANE:2 * LANE])
    o_sel = _dec_softmax(scores, vals)
    n_win = win_ref.shape[2]
    kpos = past_len - n_win + lax.broadcasted_iota(I32, (nh, n_win), 1)
    rel = past_len - kpos
    s_win = jnp.dot(qb, win_ref[0, 0:LANE, :].astype(BF16), preferred_element_type=F32)
    s_win = jnp.where((kpos >= 0) & (rel >= 0) & (rel <= WINDOW), s_win, NEG)
    s_new = jnp.sum(q * kvw_ref[0, :, 0:LANE], axis=-1, keepdims=True)
    o_win = _dec_softmax([s_win, s_new], [win_ref[0, LANE:2 * LANE, :].astype(BF16), kvw_ref[0, :, LANE:2 * LANE]])
    for hh in range(nh):
        ocmp_ref[0, :, hh * LANE:(hh + 1) * LANE] = o_cmp[hh:hh + 1]
        osel_ref[0, :, hh * LANE:(hh + 1) * LANE] = o_sel[hh:hh + 1]
        owin_ref[0, :, hh * LANE:(hh + 1) * LANE] = o_win[hh:hh + 1]
    for c in range(0, 2 * LANE, LANE):
        new_col = jnp.broadcast_to(kvw_ref[0, :, c:c + LANE], (LANE, LANE)).T
        shifted = pltpu.roll(win_ref[0, c:c + LANE, :], n_win - 1, 1)
        lane = lax.broadcasted_iota(I32, shifted.shape, 1)
        wnew_ref[0, c:c + LANE, :] = jnp.where(lane == n_win - 1, jnp.concatenate([new_col] * (n_win // LANE), axis=1), shifted)


def _nsa_decode(qn, kc, vc, sel_pages, page_ids, kvs_new, kvw_new, win_state):
    b = qn.shape[0]
    n_win = win_state.shape[2]
    past_len = PAGES * PAGE_ROWS
    n_sel = -(-(past_len + 1) // SEL_BLOCK)
    cov = _cover_tables(n_sel)

    assert b % DEC_GROUP == 0

    def per_b(shape):
        nd = len(shape)
        return pl.BlockSpec((DEC_GROUP,) + shape[1:], lambda i, pt: (i,) + (0,) * (nd - 1))

    slots = NSA_HEADS * LANE
    grid_spec = pltpu.PrefetchScalarGridSpec(
        num_scalar_prefetch=1,
        grid=(b // DEC_GROUP,),
        in_specs=[pl.BlockSpec((1, 2 * LANE, PAGE_ROWS), (lambda i, pt, p=p: (pt[i * DEC_GROUP * PAGES + p], 0, 0)))
                  for p in range(DEC_GROUP * PAGES)]
        + [per_b(qn.shape), per_b(kc.shape), per_b(vc.shape), per_b(kvs_new.shape), per_b(kvw_new.shape), per_b(win_state.shape),
           _const_spec(cov.shape)],
        out_specs=[per_b((b, 1, slots))] * 3 + [per_b(win_state.shape)],
    )
    return pl.pallas_call(
        functools.partial(_nsa_decode_kernel, past_len=past_len, n_sel=n_sel),
        grid_spec=grid_spec,
        out_shape=[jax.ShapeDtypeStruct((b, 1, slots), F32)] * 3 + [jax.ShapeDtypeStruct(win_state.shape, F32)],
        compiler_params=_cparams("parallel"),
        name="nsa_decode",
    )(page_ids, *([sel_pages] * (DEC_GROUP * PAGES)), qn, kc, vc, kvs_new, kvw_new, win_state, cov)


def _dot_tn(a, b):
    return lax.dot_general(a, b, (((0,), (0,)), ((), ())), preferred_element_type=F32)


def _cumsum_table():
    r = np.arange(LANE)
    return jnp.asarray(((r[:, None] // GLA_SUB == r[None, :] // GLA_SUB) & (r[None, :] <= r[:, None])).astype(np.float32))


def _gla_kernel(*refs, t, t_valid, has_state):
    if has_state:
        qg_ref, kg_ref, vg_ref, misc_ref, wa_ref, ba_ref, lt_ref, seg_ref, s0_ref, o_ref, s_ref, b_scr, st_scr = refs
    else:
        qg_ref, kg_ref, vg_ref, misc_ref, wa_ref, ba_ref, lt_ref, seg_ref, o_ref, s_ref, b_scr, st_scr = refs
    z = jnp.dot(misc_ref[0], wa_ref[...], preferred_element_type=F32, precision=lax.Precision.HIGHEST) + ba_ref[...]
    la = (jnp.minimum(z, 0.0) - jnp.log1p(jnp.exp(-jnp.abs(z)))) * (1.0 / GLA_TAU)
    if t_valid < t:
        la = jnp.where(lax.broadcasted_iota(I32, la.shape, 0) < t_valid, la, 0.0)
    tile = min(t, LANE)
    for r in range(0, t, tile):
        b_scr[r:r + tile, :] = jnp.dot(lt_ref[0:tile, 0:tile], la[r:r + tile, :], preferred_element_type=F32,
                                       precision=lax.Precision.HIGHEST)
    pairs = GLA_HEADS // 2
    for p in range(pairs):
        if has_state:
            st_scr[p] = s0_ref[0, 2 * p:2 * p + 2].reshape(2 * GLA_DK, GLA_DV).T
        else:
            st_scr[p] = jnp.zeros((GLA_DV, LANE), F32)
    head_a = lax.broadcasted_iota(I32, (GLA_SUB, LANE), 1) < GLA_DK
    row = lax.broadcasted_iota(I32, (GLA_SUB, LANE), 0)

    def pair_chunk(q, k, v, b, st):
        b_last = b[GLA_SUB - 1:GLA_SUB, :]
        st_b = st.astype(BF16)
        qe = q * jnp.exp(b)
        o = jnp.concatenate([_dot_nt(jnp.where(head_a, qe, 0.0).astype(BF16), st_b),
                             _dot_nt(jnp.where(head_a, 0.0, qe).astype(BF16), st_b)], axis=1)
        ws = []
        for j in range(GLA_SUB):
            w = q * k[j:j + 1, :] * jnp.exp(jnp.minimum(b - b[j:j + 1, :], 0.0))
            ws.append(jnp.where(row >= j, w, 0.0))
        a_all = jnp.dot(jnp.concatenate(ws, axis=0).astype(BF16), seg_ref[...], preferred_element_type=F32)
        for j in range(GLA_SUB):
            o = o + a_all[j * GLA_SUB:(j + 1) * GLA_SUB] * v[j:j + 1, :]
        kd = k * jnp.exp(b_last - b)
        upd = (_dot_tn(v[:, 0:GLA_DV].astype(BF16), jnp.where(head_a, kd, 0.0).astype(BF16))
               + _dot_tn(v[:, GLA_DV:2 * GLA_DV].astype(BF16), jnp.where(head_a, 0.0, kd).astype(BF16)))
        return o, jnp.exp(b_last) * st + upd

    def chunk(c, carry):
        r0 = pl.multiple_of(c * GLA_SUB, GLA_SUB)
        for p in range(pairs):
            o, st_new = pair_chunk(qg_ref[0, pl.ds(r0, GLA_SUB), p * LANE:(p + 1) * LANE],
                                          kg_ref[0, pl.ds(r0, GLA_SUB), p * LANE:(p + 1) * LANE],
                                          vg_ref[0, pl.ds(r0, GLA_SUB), 2 * p * GLA_DV:2 * (p + 1) * GLA_DV],
                                          b_scr[pl.ds(r0, GLA_SUB), p * LANE:(p + 1) * LANE], st_scr[p])
            o_ref[0, pl.ds(r0, GLA_SUB), 2 * p * GLA_DV:2 * (p + 1) * GLA_DV] = o
            st_scr[p] = st_new
        return carry

    lax.fori_loop(0, t // GLA_SUB, chunk, 0, unroll=4 if t // GLA_SUB % 4 == 0 else 1)
    for p in range(pairs):
        s_ref[0, 2 * p:2 * p + 2] = st_scr[p].T.reshape(2, GLA_DK, GLA_DV)


def _gla(qg, kg, vg, misc, wa_pad, ba, s0, t_valid):
    b, t, _ = qg.shape
    has_state = s0 is not None
    hk = GLA_HEADS * GLA_DK

    def per_b(shape):
        nd = len(shape)
        return pl.BlockSpec((1,) + shape[1:], lambda i: (i,) + (0,) * (nd - 1))

    def const(shape):
        nd = len(shape)
        return pl.BlockSpec(shape, lambda i: (0,) * nd)

    lanes = np.arange(LANE)[:, None] < GLA_DK
    seg = jnp.asarray((lanes == (np.arange(2 * GLA_DV)[None, :] < GLA_DV)).astype(np.float32), dtype=BF16)
    in_specs = [per_b(qg.shape), per_b(kg.shape), per_b(vg.shape), per_b(misc.shape),
                const((LANE, hk)), const((1, hk)), const((LANE, LANE)), const((LANE, 2 * GLA_DV))]
    args = [qg, kg, vg, misc, wa_pad, ba.reshape(1, -1), _cumsum_table(), seg]
    state_shape = (b, GLA_HEADS, GLA_DK, GLA_DV)
    if has_state:
        in_specs.append(per_b(state_shape))
        args.append(s0)
    return pl.pallas_call(
        functools.partial(_gla_kernel, t=t, t_valid=t_valid, has_state=has_state),
        grid=(b,),
        in_specs=in_specs,
        out_specs=[per_b(vg.shape), per_b(state_shape)],
        out_shape=[jax.ShapeDtypeStruct(vg.shape, F32), jax.ShapeDtypeStruct(state_shape, F32)],
        scratch_shapes=[pltpu.VMEM((t, hk), F32), pltpu.VMEM((GLA_HEADS // 2, GLA_DV, LANE), F32)],
        compiler_params=_cparams("parallel"),
        name="gla",
    )(*args)


def _gate_expand_table():
    out = np.zeros((3, LANE, NSA_HEADS * LANE), np.float32)
    for hh in range(NSA_HEADS):
        for j in range(3):
            out[j, MISC_GN + 3 * hh + j, hh * LANE:(hh + 1) * LANE] = 1.0
    return jnp.asarray(out, dtype=BF16)


def _pad_br_a(w_br_a):
    zero = jnp.zeros((HEAD_DIM, w_br_a.shape[1]), w_br_a.dtype)
    parts = []
    for hh in range(NSA_HEADS):
        wh = w_br_a[hh * HEAD_DIM:(hh + 1) * HEAD_DIM]
        parts += [wh, zero] if hh // NSA_GROUP == 0 else [zero, wh]
    return jnp.concatenate(parts, axis=0).astype(BF16)


def _layer_norm(v, g, b):
    mu = jnp.mean(v, axis=-1, keepdims=True)
    var = jnp.mean(jnp.square(v - mu), axis=-1, keepdims=True)
    return (v - mu) * lax.rsqrt(var + LN_EPS) * g + b


def _mixer_tail_kernel(ocmp_ref, osel_ref, owin_ref, misc_ref, ogla_ref, rg_ref, gm_ref, x_ref, gate_ref, scf_ref, shf_ref,
                       ex_ref, ng_ref, wa_ref, wb_ref, wo_ref, lg_ref, lb_ref, x1_ref, xm_ref):
    sig = jax.nn.sigmoid(misc_ref[0])
    sig_hi = sig.astype(BF16)
    sig_lo = (sig - sig_hi.astype(F32)).astype(BF16)
    o_nsa = None
    for j, ref in enumerate((ocmp_ref, osel_ref, owin_ref)):
        g = (jnp.dot(sig_hi, ex_ref[j], preferred_element_type=F32) + jnp.dot(sig_lo, ex_ref[j], preferred_element_type=F32))
        o_nsa = g * ref[0] if o_nsa is None else o_nsa + g * ref[0]
    br_a = _bdot(o_nsa, wa_ref[...])
    heads = []
    for h in range(GLA_HEADS):
        seg = ogla_ref[0, :, h * GLA_DV:(h + 1) * GLA_DV]
        mu = jnp.mean(seg, axis=-1, keepdims=True)
        var = jnp.mean(jnp.square(seg - mu), axis=-1, keepdims=True)
        r = rg_ref[0, :, h * GLA_DV:(h + 1) * GLA_DV]
        heads.append((seg - mu) * lax.rsqrt(var + LN_EPS) * ng_ref[:, h * GLA_DV:(h + 1) * GLA_DV] * (r * jax.nn.sigmoid(r)))
    br_b = _bdot(jnp.concatenate(heads, axis=1), wb_ref[...])
    gm_a = jax.nn.sigmoid(gm_ref[0, :, 0:D_MODEL])
    gm_b = jax.nn.sigmoid(gm_ref[0, :, D_MODEL:2 * D_MODEL])
    y = _bdot(gm_a * br_a + gm_b * br_b, wo_ref[...])
    x1 = _layer_norm(DN_ALPHA * x_ref[0] + gate_ref[0] * y, lg_ref[...], lb_ref[...])
    x1_ref[0] = x1
    xm_ref[0] = x1 * (1.0 + scf_ref[0]) + shf_ref[0]


def _mixer_tail(ocmp, osel, owin, misc, ogla, rg, gm, x, gate_m, scale_f, shift_f, consts, tm):
    b, t, d = x.shape
    per_tok = gate_m.shape[1] != 1

    def tok(w):
        return pl.BlockSpec((1, tm, w), lambda i, j: (i, j, 0))

    mod_spec = tok(d) if per_tok else pl.BlockSpec((1, 1, d), lambda i, j: (i, 0, 0))

    def const(a):
        nd = a.ndim
        return pl.BlockSpec(a.shape, lambda i, j: (0,) * nd)

    return pl.pallas_call(
        _mixer_tail_kernel,
        grid=(b, t // tm),
        in_specs=[tok(NSA_HEADS * LANE)] * 3 + [tok(LANE), tok(GLA_HEADS * GLA_DV), tok(GLA_HEADS * GLA_DV), tok(2 * d), tok(d),
                                               mod_spec, mod_spec, mod_spec] + [const(c) for c in consts],
        out_specs=[tok(d), tok(d)],
        out_shape=[jax.ShapeDtypeStruct((b, t, d), F32)] * 2,
        compiler_params=_cparams("parallel", "parallel"),
        name="mixer_tail",
    )(ocmp, osel, owin, misc, ogla, rg, gm, x, gate_m, scale_f, shift_f, *consts)


ROUTE_TILE = LANE


def _first_index(hit, iota, size, axis):
    return jnp.min(jnp.where(hit, iota, size), axis=axis, keepdims=True)


def _part_specs(parts, tm):
    n_a = parts[0].shape[0] // tm
    d = parts[0].shape[1]
    return [pl.BlockSpec((tm, d), lambda i, *_: (jnp.minimum(i, n_a - 1), 0)),
            pl.BlockSpec((tm, d), lambda i, *_: (jnp.maximum(i - n_a, 0), 0))], n_a


def _part_tile(i, n_a, a_ref, b_ref):
    return jnp.where(i < n_a, a_ref[...], b_ref[...])


def _router_kernel(xa_ref, xb_ref, wr_ref, bias_ref, tri_ref, eidx_ref, rank_ref, wrow_ref, cnt_ref, carry_ref, *, n_a):
    i = pl.program_id(0)
    tm = ROUTE_TILE
    per = N_EXPERTS // N_GROUPS

    @pl.when(i == 0)
    def _():
        carry_ref[...] = jnp.zeros_like(carry_ref)

    logits = _dot_nt(wr_ref[...], _part_tile(i, n_a, xa_ref, xb_ref), lax.Precision.HIGHEST)
    s = jax.nn.sigmoid(logits)
    sb = s + bias_ref[...]
    sb3 = sb.reshape(N_GROUPS, per, tm)
    in_grp = lax.broadcasted_iota(I32, sb3.shape, 1)
    m1 = jnp.max(sb3, axis=1, keepdims=True)
    first = _first_index(sb3 == m1, in_grp, per, 1)
    m2 = jnp.max(jnp.where(in_grp == first, NEG, sb3), axis=1, keepdims=True)
    gs = (m1 + m2).reshape(N_GROUPS, tm)
    g_iota = lax.broadcasted_iota(I32, gs.shape, 0)
    g_keep = jnp.zeros(gs.shape, jnp.bool_)
    for _ in range(TOPK_GROUPS):
        pick = g_iota == _first_index(gs == jnp.max(gs, axis=0, keepdims=True), g_iota, N_GROUPS, 0)
        g_keep = g_keep | pick
        gs = jnp.where(pick, NEG, gs)
    sbm = jnp.where(g_keep.reshape(N_GROUPS, 1, tm), sb3, NEG).reshape(N_EXPERTS, tm)
    e_iota = lax.broadcasted_iota(I32, sbm.shape, 0)
    idxs, sels = [], []
    onehot = jnp.zeros(sbm.shape, F32)
    for _ in range(TOP_K):
        idx = _first_index(sbm == jnp.max(sbm, axis=0, keepdims=True), e_iota, N_EXPERTS, 0)
        pick = e_iota == idx
        idxs.append(idx)
        sels.append(jnp.sum(jnp.where(pick, s, 0.0), axis=0, keepdims=True))
        sbm = jnp.where(pick, NEG, sbm)
        onehot = onehot + jnp.where(pick, 1.0, 0.0)
    sel = jnp.concatenate(sels, axis=0)
    wts = sel / jnp.sum(sel, axis=0, keepdims=True) * ROUTED_SCALE
    carry = carry_ref[...]
    before = carry + jnp.dot(onehot.astype(BF16), tri_ref[...], preferred_element_type=F32)
    ranks = [jnp.sum(jnp.where(e_iota == idx, before, 0.0), axis=0, keepdims=True) for idx in idxs]
    eidx_ref[...] = jnp.concatenate(idxs, axis=0)
    rank_ref[...] = jnp.concatenate(ranks, axis=0).astype(I32)
    wrow_ref[...] = jnp.concatenate([wts, jnp.zeros((LANE - TOP_K, tm), F32)], axis=0).T
    carry = carry + jnp.sum(onehot, axis=1, keepdims=True)
    carry_ref[...] = carry
    cnt_ref[...] = carry


def _router(xm_parts, w_router, router_bias):
    n = xm_parts[0].shape[0] + xm_parts[1].shape[0]
    d = xm_parts[0].shape[1]
    tm = ROUTE_TILE
    r = np.arange(tm)
    tri = jnp.asarray((r[:, None] < r[None, :]).astype(np.float32), dtype=BF16)
    x_specs, n_a = _part_specs(xm_parts, tm)
    return pl.pallas_call(
        functools.partial(_router_kernel, n_a=n_a),
        grid=(n // tm,),
        in_specs=x_specs + [
                  pl.BlockSpec((N_EXPERTS, d), lambda i: (0, 0)),
                  pl.BlockSpec((N_EXPERTS, 1), lambda i: (0, 0)),
                  pl.BlockSpec((tm, tm), lambda i: (0, 0))],
        out_specs=[pl.BlockSpec((TOP_K, tm), lambda i: (0, i)),
                   pl.BlockSpec((TOP_K, tm), lambda i: (0, i)),
                   pl.BlockSpec((tm, LANE), lambda i: (i, 0)),
                   pl.BlockSpec((N_EXPERTS, LANE), lambda i: (0, 0))],
        out_shape=[jax.ShapeDtypeStruct((TOP_K, n), I32), jax.ShapeDtypeStruct((TOP_K, n), I32),
                   jax.ShapeDtypeStruct((n, LANE), F32), jax.ShapeDtypeStruct((N_EXPERTS, LANE), F32)],
        scratch_shapes=[pltpu.VMEM((N_EXPERTS, LANE), F32)],
        compiler_params=_cparams("arbitrary"),
        name="router",
    )(*xm_parts, w_router.T, router_bias.reshape(N_EXPERTS, 1), tri)


def _dest_kernel(eidx_ref, rank_ref, start_ref, dest_ref):
    e_iota = lax.broadcasted_iota(I32, (N_EXPERTS, ROUTE_TILE), 0)
    start = start_ref[...]
    rows = [jnp.sum(jnp.where(e_iota == eidx_ref[k:k + 1, :], start, 0.0), axis=0, keepdims=True) for k in range(TOP_K)]
    dest_ref[0] = jnp.concatenate(rows, axis=0).astype(I32) + rank_ref[...]


def _dest(eidx, rank, pad_start):
    n = eidx.shape[1]
    tm = ROUTE_TILE
    return pl.pallas_call(
        _dest_kernel,
        grid=(n // tm,),
        in_specs=[pl.BlockSpec((TOP_K, tm), lambda i: (0, i)),
                  pl.BlockSpec((TOP_K, tm), lambda i: (0, i)),
                  pl.BlockSpec((N_EXPERTS, 1), lambda i: (0, 0))],
        out_specs=pl.BlockSpec((1, TOP_K, tm), lambda i: (i, 0, 0)),
        out_shape=jax.ShapeDtypeStruct((n // tm, TOP_K, tm), I32),
        compiler_params=_cparams("parallel"),
        name="dest",
    )(eidx, rank, pad_start.astype(F32).reshape(N_EXPERTS, 1))


TOKEN_SUB = D_MODEL // LANE


def _to_token_tiles(x, ref):
    m = x.shape[0]
    for s in range(TOKEN_SUB):
        ref[pl.ds(s, m, stride=TOKEN_SUB), :] = x[:, s * LANE:(s + 1) * LANE]


def _from_token_tiles(ref, m, s):
    return ref[pl.ds(s, m, stride=TOKEN_SUB), :]


def _row_copy(src_ref, src_row, dst_ref, dst_row, sem):
    return pltpu.make_async_copy(src_ref.at[pl.ds(src_row * TOKEN_SUB, TOKEN_SUB)],
                                 dst_ref.at[pl.ds(dst_row * TOKEN_SUB, TOKEN_SUB)], sem)


def _dispatch_kernel(dest_ref, pad_ref, nb_ref, xa_ref, xb_ref, xs_ref, tile_ref, zero_ref, sems, sem, *, n_a):
    i = pl.program_id(0)
    tm = ROUTE_TILE
    bm = EXPERT_BLOCK
    last = pl.num_programs(0) - 1
    src = tile_ref.at[i % 2]
    _to_token_tiles(_part_tile(i, n_a, xa_ref, xb_ref), src)

    def start_row(r, c):
        for k in range(TOP_K):
            _row_copy(src, r, xs_ref, dest_ref[0, k, r], sems.at[i % 2]).start()
        return c

    def wait_step(slot):
        for _ in range(TOP_K):
            pltpu.make_async_copy(tile_ref.at[0], xs_ref.at[pl.ds(0, tm * TOKEN_SUB)], sems.at[slot]).wait()

    lax.fori_loop(0, tm, start_row, 0)

    @pl.when(i >= 1)
    def _():
        wait_step((i - 1) % 2)

    @pl.when(i == last)
    def _():
        wait_step(i % 2)
        zero_ref[...] = jnp.zeros_like(zero_ref)

        def per_expert(e, c):
            lo, hi = pad_ref[0, e], pad_ref[1, e]
            lax.fori_loop(lo, hi, lambda r, cc: (_row_copy(zero_ref, 0, xs_ref, r, sem).start(), cc)[1], 0)
            lax.fori_loop(lo, hi, lambda r, cc: (_row_copy(zero_ref, 0, xs_ref, r, sem).wait(), cc)[1], 0)
            return c

        lax.fori_loop(0, N_EXPERTS, per_expert, 0)

        def tail_copy(blk):
            return pltpu.make_async_copy(zero_ref, xs_ref.at[pl.ds(blk * bm * TOKEN_SUB, bm * TOKEN_SUB)], sem)

        n_blocks = xs_ref.shape[0] // (bm * TOKEN_SUB)
        lax.fori_loop(nb_ref[0], n_blocks, lambda blk, c: (tail_copy(blk).start(), c)[1], 0)
        lax.fori_loop(nb_ref[0], n_blocks, lambda blk, c: (tail_copy(blk).wait(), c)[1], 0)


def _dispatch(xm_parts, dest_tiles, pad_range, n_used, n_rows):
    tm = ROUTE_TILE
    x_specs, n_a = _part_specs(xm_parts, tm)
    return pl.pallas_call(
        functools.partial(_dispatch_kernel, n_a=n_a),
        grid=(dest_tiles.shape[0],),
        in_specs=[pl.BlockSpec((1, TOP_K, tm), lambda i: (i, 0, 0), memory_space=pltpu.SMEM),
                  pl.BlockSpec(memory_space=pltpu.SMEM),
                  pl.BlockSpec(memory_space=pltpu.SMEM)] + x_specs,
        out_specs=pl.BlockSpec(memory_space=pl.ANY),
        out_shape=jax.ShapeDtypeStruct((n_rows * TOKEN_SUB, LANE), F32),
        scratch_shapes=[pltpu.VMEM((2, tm * TOKEN_SUB, LANE), F32), pltpu.VMEM((EXPERT_BLOCK * TOKEN_SUB, LANE), F32),
                        pltpu.SemaphoreType.DMA((2,)), pltpu.SemaphoreType.DMA(())],
        compiler_params=_cparams("arbitrary"),
        name="dispatch",
    )(dest_tiles, pad_range, n_used, *xm_parts)


def _experts_kernel(first_ref, cnt_ref, nb_ref, xs_ref, wg_ref, wu_ref, wd_ref, ys_ref,
                    xbuf, ybuf, wg_s, wu_s, wd_s, in_sems, out_sems):
    e = pl.program_id(0)
    bm = EXPERT_BLOCK
    brows = bm * TOKEN_SUB
    total = nb_ref[0]
    n_blocks = xs_ref.shape[0] // brows

    def x_copy(g, slot):
        return pltpu.make_async_copy(xs_ref.at[pl.ds(g * brows, brows)], xbuf.at[slot], in_sems.at[slot])

    def y_copy(g, slot):
        return pltpu.make_async_copy(ybuf.at[slot], ys_ref.at[pl.ds(g * brows, brows)], out_sems.at[slot])

    depth = xbuf.shape[0]

    @pl.when(e == 0)
    def _():
        for g0 in range(depth - 1):
            @pl.when(g0 < total)
            def _():
                x_copy(g0, g0).start()

    @pl.when(cnt_ref[e] > 0)
    def _():
        wg_s[...] = wg_ref[0].astype(BF16)
        wu_s[...] = wu_ref[0].astype(BF16)
        wd_s[...] = wd_ref[0].astype(BF16)

    def block(c, carry):
        g = first_ref[e] + c
        slot = g % depth
        x_copy(g, slot).wait()

        @pl.when(g + depth - 1 < total)
        def _():
            x_copy(g + depth - 1, (g + depth - 1) % depth).start()

        x = jnp.concatenate([_from_token_tiles(xbuf.at[slot], bm, s) for s in range(TOKEN_SUB)], axis=1).astype(BF16)
        gate = jnp.dot(x, wg_s[...], preferred_element_type=F32)
        up = jnp.dot(x, wu_s[...], preferred_element_type=F32)
        y = jnp.dot((gate * jax.nn.sigmoid(gate) * up).astype(BF16), wd_s[...], preferred_element_type=F32)

        @pl.when(g >= depth)
        def _():
            y_copy(g - depth, slot).wait()

        _to_token_tiles(y, ybuf.at[slot])
        y_copy(g, slot).start()
        return carry

    lax.fori_loop(0, cnt_ref[e], block, 0)

    @pl.when(e == pl.num_programs(0) - 1)
    def _():
        for back in range(depth, 0, -1):
            @pl.when(total >= back)
            def _():
                y_copy(total - back, (total - back) % depth).wait()

        ybuf[0] = jnp.zeros(ybuf.shape[1:], F32)
        lax.fori_loop(total, n_blocks, lambda g, c: (y_copy(g, 0).start(), c)[1], 0)
        lax.fori_loop(total, n_blocks, lambda g, c: (y_copy(g, 0).wait(), c)[1], 0)


def _experts(xs, first_block, n_block, n_used, w_gate, w_up, w_down):
    brows = EXPERT_BLOCK * TOKEN_SUB
    n_exp, d, f = w_gate.shape
    grid_spec = pltpu.PrefetchScalarGridSpec(
        num_scalar_prefetch=3,
        grid=(n_exp,),
        in_specs=[pl.BlockSpec(memory_space=pl.ANY),
                  pl.BlockSpec((1, d, f), lambda e, *_: (e, 0, 0)),
                  pl.BlockSpec((1, d, f), lambda e, *_: (e, 0, 0)),
                  pl.BlockSpec((1, f, d), lambda e, *_: (e, 0, 0))],
        out_specs=pl.BlockSpec(memory_space=pl.ANY),
        scratch_shapes=[pltpu.VMEM((EXPERT_RING, brows, LANE), F32), pltpu.VMEM((EXPERT_RING, brows, LANE), F32),
                        pltpu.VMEM((d, f), BF16), pltpu.VMEM((d, f), BF16), pltpu.VMEM((f, d), BF16),
                        pltpu.SemaphoreType.DMA((EXPERT_RING,)), pltpu.SemaphoreType.DMA((EXPERT_RING,))],
    )
    return pl.pallas_call(
        _experts_kernel,
        grid_spec=grid_spec,
        out_shape=jax.ShapeDtypeStruct(xs.shape, F32),
        compiler_params=_cparams("arbitrary"),
        name="experts",
    )(first_block, n_block, n_used, xs, w_gate, w_up, w_down)


def _combine_kernel(dest_ref, dnext_ref, ys_ref, wrow_ref, xa_ref, xb_ref, sg_ref, su_ref, sd_ref, out_ref, buf_ref, sems,
                    *, n_a):
    i = pl.program_id(0)
    tm = ROUTE_TILE
    slot = i % 2

    def issue(d_ref, s):
        def start_row(r, c):
            for k in range(TOP_K):
                _row_copy(ys_ref, d_ref[0, k, r], buf_ref.at[s, k], r, sems.at[s]).start()
            return c

        lax.fori_loop(0, tm, start_row, 0)

    @pl.when(i == 0)
    def _():
        issue(dest_ref, 0)

    @pl.when(i < pl.num_programs(0) - 1)
    def _():
        issue(dnext_ref, 1 - slot)

    x = _part_tile(i, n_a, xa_ref, xb_ref).astype(BF16)
    g = jnp.dot(x, sg_ref[...], preferred_element_type=F32)
    u = jnp.dot(x, su_ref[...], preferred_element_type=F32)
    shared = jnp.dot((g * jax.nn.sigmoid(g) * u).astype(BF16), sd_ref[...], preferred_element_type=F32)
    for k in range(TOP_K):
        pltpu.make_async_copy(ys_ref.at[pl.ds(0, tm * TOKEN_SUB)], buf_ref.at[slot, k], sems.at[slot]).wait()
    w = wrow_ref[...]
    for s in range(TOKEN_SUB):
        routed = w[:, 0:1] * _from_token_tiles(buf_ref.at[slot, 0], tm, s)
        for k in range(1, TOP_K):
            routed = routed + w[:, k:k + 1] * _from_token_tiles(buf_ref.at[slot, k], tm, s)
        out_ref[:, s * LANE:(s + 1) * LANE] = routed + shared[:, s * LANE:(s + 1) * LANE]


def _combine(ys, dest_tiles, wrow, xm_parts, ws_gate, ws_up, ws_down):
    n = xm_parts[0].shape[0] + xm_parts[1].shape[0]
    d = xm_parts[0].shape[1]
    tm = ROUTE_TILE
    f = ws_gate.shape[1]
    x_specs, n_a = _part_specs(xm_parts, tm)
    return pl.pallas_call(
        functools.partial(_combine_kernel, n_a=n_a),
        grid=(n // tm,),
        in_specs=[pl.BlockSpec((1, TOP_K, tm), lambda i: (i, 0, 0), memory_space=pltpu.SMEM),
                  pl.BlockSpec((1, TOP_K, tm), lambda i: (jnp.minimum(i + 1, n // tm - 1), 0, 0), memory_space=pltpu.SMEM),
                  pl.BlockSpec(memory_space=pl.ANY),
                  pl.BlockSpec((tm, LANE), lambda i: (i, 0))] + x_specs + [
                  pl.BlockSpec((d, f), lambda i: (0, 0)),
                  pl.BlockSpec((d, f), lambda i: (0, 0)),
                  pl.BlockSpec((f, d), lambda i: (0, 0))],
        out_specs=pl.BlockSpec((tm, d), lambda i: (i, 0)),
        out_shape=jax.ShapeDtypeStruct((n, d), F32),
        scratch_shapes=[pltpu.VMEM((2, TOP_K, tm * TOKEN_SUB, LANE), F32), pltpu.SemaphoreType.DMA((2,))],
        compiler_params=_cparams("arbitrary"),
        name="combine",
    )(dest_tiles, dest_tiles, ys, wrow, *xm_parts, ws_gate.astype(BF16), ws_up.astype(BF16), ws_down.astype(BF16))


def _final_ln_kernel(x1_ref, moe_ref, gate_ref, g_ref, b_ref, y_ref):
    y_ref[0] = _layer_norm(DN_ALPHA * x1_ref[0] + gate_ref[0] * moe_ref[...], g_ref[...], b_ref[...])


def _final_ln(x1, moe, row0, gate_f, ln_g, ln_b, tm):
    b, t, d = x1.shape
    per_tok = gate_f.shape[1] != 1
    tok = pl.BlockSpec((1, tm, d), lambda i, j: (i, j, 0))
    mod_spec = tok if per_tok else pl.BlockSpec((1, 1, d), lambda i, j: (i, 0, 0))
    vec = pl.BlockSpec((1, d), lambda i, j: (0, 0))
    moe_spec = pl.BlockSpec((tm, d), lambda i, j: (row0 // tm + i * (t // tm) + j, 0))
    return pl.pallas_call(
        _final_ln_kernel,
        grid=(b, t // tm),
        in_specs=[tok, moe_spec, mod_spec, vec, vec],
        out_specs=tok,
        out_shape=jax.ShapeDtypeStruct((b, t, d), F32),
        compiler_params=_cparams("parallel", "parallel"),
        name="final_ln",
    )(x1, moe, gate_f, ln_g.reshape(1, d), ln_b.reshape(1, d))


def _moe(xm, w_router, router_bias, w_e_gate, w_e_up, w_e_down, w_s_gate, w_s_up, w_s_down):
    n = xm[0].shape[0] + xm[1].shape[0]
    eidx, rank, wrow, cnt = _router(xm, w_router, router_bias)
    counts = cnt[:, 0].astype(I32)
    padded = (counts + EXPERT_BLOCK - 1) // EXPERT_BLOCK * EXPERT_BLOCK
    pad_end = jnp.cumsum(padded)
    pad_start = pad_end - padded
    dest_tiles = _dest(eidx, rank, pad_start)
    n_blocks = -(-(n * TOP_K) // EXPERT_BLOCK) + N_EXPERTS
    n_used = (pad_end[-1:] // EXPERT_BLOCK).astype(I32)
    pad_range = jnp.stack([pad_start + counts, pad_end]).astype(I32)
    xs = _dispatch(xm, dest_tiles, pad_range, n_used, n_blocks * EXPERT_BLOCK)
    ys = _experts(xs, (pad_start // EXPERT_BLOCK).astype(I32), (padded // EXPERT_BLOCK).astype(I32), n_used,
                  w_e_gate, w_e_up, w_e_down)
    return _combine(ys, dest_tiles, wrow, xm, w_s_gate, w_s_up, w_s_down)


def kernel(x_prompt, x_sample, cache_kv_cmp, cache_kv_sel, state_kv_win, state_gla, page_table, c_prompt, c_sample, w_in, b_in, cmp_k_pos, cmp_k_w1, cmp_k_w2, cmp_v_pos, cmp_v_w1, cmp_v_w2, gla_w_a2, gla_b_a, gla_norm_g, w_br_a, w_br_b, w_out, ln1_g, ln1_b, w_ada, b_ada, w_router, router_bias, w_e_gate, w_e_up, w_e_down, w_s_gate, w_s_up, w_s_down, ln2_g, ln2_b):
    bp, tp, d = x_prompt.shape
    nd, td = x_sample.shape[:2]
    n_pool, page_rows = cache_kv_cmp.shape[:2]
    past_len = page_table.shape[1] * page_rows
    assert td == 1 and d == D_MODEL and page_rows == PAGE_ROWS and page_table.shape[1] == PAGES and tp == PAGES * PAGE_ROWS
    kv_w = 2 * NSA_KV_HEADS * HEAD_DIM

    mod = _adaln(jnp.concatenate([c_prompt, c_sample], axis=0), w_ada, b_ada)
    mod_p = [m.reshape(bp, 1, d) for m in jnp.split(mod[:bp], 6, axis=-1)]
    mod_s = [m.reshape(1, nd, d) for m in jnp.split(mod[bp:], 6, axis=-1)]

    w_pack, b_pack = _pack_in_weights(w_in, b_in)
    cmp_wk = _pack_cmp_weights(cmp_k_pos, cmp_k_w1, cmp_k_w2)
    cmp_wv = _pack_cmp_weights(cmp_v_pos, cmp_v_w1, cmp_v_w2)
    wa_pad = jnp.zeros((LANE, GLA_HEADS * GLA_DK), F32).at[MISC_AG:MISC_AG + GLA_GATE_RANK].set(gla_w_a2)
    tail_consts = (_gate_expand_table(), gla_norm_g.reshape(1, -1), _pad_br_a(w_br_a), w_br_b.astype(BF16), w_out.astype(BF16),
                   ln1_g.reshape(1, d), ln1_b.reshape(1, d))

    gm, qn, vg, rg, kvc, kvs, kvw, qg, kg, misc = _inproj(
        x_prompt, mod_p[0], mod_p[1], w_pack, b_pack, _rope_tables(jnp.arange(tp, dtype=I32)), 256)
    kc, vc = _compress(kvc.reshape(bp * PAGES, PAGE_ROWS, kv_w), jnp.arange(bp * PAGES, dtype=I32), cmp_wk, cmp_wv)
    ocmp, sel = _cmp_attn(qn, kc, vc, 256)
    osel = _sel_attn(qn, kvs, sel, 128, 256)
    owin = _win_attn(qn, kvw, 128)
    ogla, gla_p = _gla(qg, kg, vg, misc, wa_pad, gla_b_a, None, tp)
    x1_p, xm_p = _mixer_tail(ocmp, osel, owin, misc, ogla, rg, gm, x_prompt, mod_p[2], mod_p[4], mod_p[3], tail_consts, 256)
    n_win = min(WINDOW, tp)
    outs_p = (kvc.reshape(bp, tp, 2, NSA_KV_HEADS, HEAD_DIM), kvs.reshape(bp, tp, 2, NSA_KV_HEADS, HEAD_DIM),
              kvw[:, tp - n_win:].reshape(bp, n_win, 2, NSA_KV_HEADS, HEAD_DIM), gla_p)

    gm, qn, vg, rg, kvc, kvs, kvw, qg, kg, misc = _inproj(
        x_sample.reshape(1, nd, d), mod_s[0], mod_s[1], w_pack, b_pack, _rope_tables(jnp.full((nd,), past_len, I32)), nd)
    page_ids = page_table.reshape(-1).astype(I32)
    kc, vc = _compress(cache_kv_cmp.reshape(n_pool, PAGE_ROWS, kv_w).transpose(0, 2, 1), page_ids, cmp_wk, cmp_wv,
                       feature_major=True)
    ocmp, osel, owin, win_new = _nsa_decode(
        qn.reshape(nd, 1, -1), kc, vc, cache_kv_sel.reshape(n_pool, PAGE_ROWS, kv_w).transpose(0, 2, 1), page_ids,
        kvs.reshape(nd, 1, kv_w), kvw.reshape(nd, 1, kv_w), state_kv_win.reshape(nd, -1, kv_w).transpose(0, 2, 1))
    win_new = win_new.transpose(0, 2, 1)

    def pad_rows(a):
        return jnp.pad(a.reshape(nd, 1, -1), ((0, 0), (0, GLA_SUB - 1), (0, 0)))

    ogla, gla_s = _gla(pad_rows(qg), pad_rows(kg), pad_rows(vg), pad_rows(misc), wa_pad, gla_b_a, state_gla, 1)
    x1_s, xm_s = _mixer_tail(ocmp.reshape(1, nd, -1), osel.reshape(1, nd, -1), owin.reshape(1, nd, -1), misc,
                             ogla[:, 0].reshape(1, nd, -1), rg, gm, x_sample.reshape(1, nd, d),
                             mod_s[2], mod_s[4], mod_s[3], tail_consts, nd)
    outs_s = (kvc.reshape(nd, 1, 2, NSA_KV_HEADS, HEAD_DIM), kvs.reshape(nd, 1, 2, NSA_KV_HEADS, HEAD_DIM),
              win_new.reshape(state_kv_win.shape), gla_s)

    n_p = bp * tp
    moe = _moe((xm_p.reshape(n_p, d), xm_s.reshape(nd, d)),
               w_router, router_bias, w_e_gate, w_e_up, w_e_down, w_s_gate, w_s_up, w_s_down)
    y_p = _final_ln(x1_p, moe, 0, mod_p[5], ln2_g, ln2_b, 256)
    y_s = _final_ln(x1_s, moe, n_p, mod_s[5], ln2_g, ln2_b, nd).reshape(nd, 1, d)
    return (y_p, y_s) + outs_p + outs_s
```

```python
import functools

import numpy as np
import jax
import jax.numpy as jnp
from jax import lax
from jax.experimental import pallas as pl
from jax.experimental.pallas import tpu as pltpu

F32 = jnp.float32
BF16 = jnp.bfloat16
I32 = jnp.int32

D_MODEL = 1024
NSA_HEADS = 8
NSA_KV_HEADS = 2
NSA_GROUP = NSA_HEADS // NSA_KV_HEADS
HEAD_DIM = 64
ROT_DIM = HEAD_DIM // 4
ROPE_THETA = 500000.0
CMP_LEN = 32
CMP_STRIDE = 16
CMP_HIDDEN = 256
SEL_BLOCK = 64
SEL_TOP_N = 16
WINDOW = 512
FORCE_SCORE = 1.0e4
GLA_HEADS = 4
GLA_DK = 64
GLA_DV = 128
GLA_GATE_RANK = 16
GLA_TAU = 16.0
GLA_SUB = 16
N_EXPERTS = 256
TOP_K = 8
N_GROUPS = 8
TOPK_GROUPS = 4
EXPERT_DIM = 256
SHARED_DIM = 256
ROUTED_SCALE = 2.5
EXPERT_BLOCK = 128
EXPERT_RING = 4
DN_ALPHA = 2.0 ** 0.25
LN_EPS = 1e-5
LANE = 128
NEG = -1.0e30
VMEM_LIMIT = 56 * 1024 * 1024

SEG_GM = (0, 2 * D_MODEL)
SEG_QN = (SEG_GM[0] + SEG_GM[1], NSA_HEADS * LANE)
SEG_VG = (SEG_QN[0] + SEG_QN[1], GLA_HEADS * GLA_DV)
SEG_RG = (SEG_VG[0] + SEG_VG[1], GLA_HEADS * GLA_DV)
SEG_KVC = (SEG_RG[0] + SEG_RG[1], 2 * LANE)
SEG_KVS = (SEG_KVC[0] + SEG_KVC[1], 2 * LANE)
SEG_KVW = (SEG_KVS[0] + SEG_KVS[1], 2 * LANE)
SEG_QG = (SEG_KVW[0] + SEG_KVW[1], GLA_HEADS * GLA_DK)
SEG_KG = (SEG_QG[0] + SEG_QG[1], GLA_HEADS * GLA_DK)
SEG_MISC = (SEG_KG[0] + SEG_KG[1], LANE)
IN_PACKED = SEG_MISC[0] + SEG_MISC[1]
MISC_GN = 0
MISC_AG = NSA_HEADS * 3


def _cparams(*sem):
    return pltpu.CompilerParams(dimension_semantics=sem, vmem_limit_bytes=VMEM_LIMIT)


def _bdot(a, b):
    return jnp.dot(a.astype(BF16), b.astype(BF16), preferred_element_type=F32)


def _dot_nt(a, b, precision=None):
    return lax.dot_general(a, b, (((1,), (1,)), ((), ())), preferred_element_type=F32, precision=precision)


def _adaln_kernel(c_ref, w_ref, b_ref, o_ref):
    c = c_ref[...]
    o_ref[...] = _bdot(c * jax.nn.sigmoid(c), w_ref[...]) + b_ref[...]


def _adaln(c, w_ada, b_ada):
    n, d = c.shape
    m = w_ada.shape[1]
    tn = 512
    return pl.pallas_call(
        _adaln_kernel,
        grid=(m // tn,),
        in_specs=[pl.BlockSpec((n, d), lambda j: (0, 0)),
                  pl.BlockSpec((d, tn), lambda j: (0, j)),
                  pl.BlockSpec((1, tn), lambda j: (0, j))],
        out_specs=pl.BlockSpec((n, tn), lambda j: (0, j)),
        out_shape=jax.ShapeDtypeStruct((n, m), F32),
        compiler_params=_cparams("parallel"),
        name="adaln",
    )(c, w_ada, b_ada.reshape(1, m))


def _rope_tables(pos):
    half = ROT_DIM // 2
    inv = jnp.power(ROPE_THETA, -jnp.arange(half, dtype=F32) * 2.0 / ROT_DIM)
    ang = pos.astype(F32)[:, None] * inv[None, :]
    cos, sin = jnp.cos(ang), jnp.sin(ang)
    t = pos.shape[0]
    one = jnp.ones((t, HEAD_DIM - ROT_DIM), F32)
    z8 = jnp.zeros((t, half), F32)
    z48 = jnp.zeros((t, HEAD_DIM - ROT_DIM), F32)
    c = jnp.concatenate([cos, cos, one, cos, cos, one], axis=1)
    s1 = jnp.concatenate([-sin, z8, z48, -sin, z8, z48], axis=1)
    s2 = jnp.concatenate([z8, sin, z48, z8, sin, z48], axis=1)
    return c, s1, s2


def _pack_in_weights(w_in, b_in):
    sizes = (512, 128, 128, 128, 128, 128, 128, 24, 256, 256, 512, 512, 16, 2048)
    offs = np.concatenate([[0], np.cumsum(sizes)])

    def pack(w):
        seg = [w[..., offs[i]:offs[i + 1]] for i in range(len(sizes))]
        q_n, k_c, v_c, k_s, v_s, k_w, v_w, g_n, q_g, k_g, v_g, r_g, a_g, g_m = seg
        zero = jnp.zeros_like(q_n[..., :HEAD_DIM])
        q_slots = []
        for hh in range(NSA_HEADS):
            qh = q_n[..., hh * HEAD_DIM:(hh + 1) * HEAD_DIM] * (HEAD_DIM ** -0.5)
            q_slots += [qh, zero] if hh // NSA_GROUP == 0 else [zero, qh]
        misc_pad = jnp.zeros_like(w[..., :LANE - g_n.shape[-1] - a_g.shape[-1]])
        return jnp.concatenate([g_m] + q_slots + [v_g, r_g, k_c, v_c, k_s, v_s, k_w, v_w,
                                                   q_g * (GLA_DK ** -0.5), k_g, g_n, a_g, misc_pad], axis=-1)

    return pack(w_in).astype(BF16), pack(b_in.reshape(1, -1))


def _inproj_kernel(x_ref, sh_ref, sc_ref, w_ref, b_ref, rc_ref, rs1_ref, rs2_ref,
                   gm_ref, qn_ref, vg_ref, rg_ref, kvc_ref, kvs_ref, kvw_ref, qg_ref, kg_ref, misc_ref):
    h = (x_ref[0] * (1.0 + sc_ref[0]) + sh_ref[0]).astype(BF16)
    rc, rs1, rs2 = rc_ref[...], rs1_ref[...], rs2_ref[...]

    def proj(off, width):
        return jnp.dot(h, w_ref[:, off:off + width], preferred_element_type=F32) + b_ref[:, off:off + width]

    def rope(z):
        return z * rc + pltpu.roll(z, LANE - ROT_DIM // 2, 1) * rs1 + pltpu.roll(z, ROT_DIM // 2, 1) * rs2

    def plain(ref, seg):
        off, width = seg
        step = min(width, 512)
        for c in range(0, width, step):
            ref[0, :, c:c + step] = proj(off + c, step)

    plain(gm_ref, SEG_GM)
    for c in range(0, SEG_QN[1], 512):
        z = proj(SEG_QN[0] + c, 512)
        for s in range(0, 512, LANE):
            qn_ref[0, :, c + s:c + s + LANE] = rope(z[:, s:s + LANE])
    plain(vg_ref, SEG_VG)
    plain(rg_ref, SEG_RG)
    for ref, seg in ((kvc_ref, SEG_KVC), (kvs_ref, SEG_KVS), (kvw_ref, SEG_KVW)):
        z = proj(seg[0], seg[1])
        ref[0, :, 0:LANE] = rope(z[:, 0:LANE])
        ref[0, :, LANE:2 * LANE] = z[:, LANE:2 * LANE]
    plain(qg_ref, SEG_QG)
    plain(kg_ref, SEG_KG)
    plain(misc_ref, SEG_MISC)


def _inproj(x, shift, scale, w_pack, b_pack, tables, tm):
    b, t, d = x.shape
    per_tok = shift.shape[1] != 1
    mod_spec = (pl.BlockSpec((1, tm, d), lambda i, j: (i, j, 0)) if per_tok
                else pl.BlockSpec((1, 1, d), lambda i, j: (i, 0, 0)))
    segs = (SEG_GM, SEG_QN, SEG_VG, SEG_RG, SEG_KVC, SEG_KVS, SEG_KVW, SEG_QG, SEG_KG, SEG_MISC)
    tab_spec = pl.BlockSpec((tm, LANE), lambda i, j: (j, 0))
    return pl.pallas_call(
        _inproj_kernel,
        grid=(b, t // tm),
        in_specs=[pl.BlockSpec((1, tm, d), lambda i, j: (i, j, 0)), mod_spec, mod_spec,
                  pl.BlockSpec((d, IN_PACKED), lambda i, j: (0, 0)),
                  pl.BlockSpec((1, IN_PACKED), lambda i, j: (0, 0)),
                  tab_spec, tab_spec, tab_spec],
        out_specs=[pl.BlockSpec((1, tm, w), lambda i, j: (i, j, 0)) for _, w in segs],
        out_shape=[jax.ShapeDtypeStruct((b, t, w), F32) for _, w in segs],
        compiler_params=_cparams("parallel", "parallel"),
        name="inproj",
    )(x, shift, scale, w_pack, b_pack, *tables)


CHUNKS = 128
PAGE_ROWS = 128
PAGES = 16


def _pack_cmp_weights(pos, w1, w2):
    pos2 = jnp.concatenate([pos, pos], axis=1)
    z1 = jnp.zeros_like(w1)
    bd1 = jnp.concatenate([jnp.concatenate([w1, z1], axis=2), jnp.concatenate([z1, w1], axis=2)], axis=1)
    w1p = jnp.concatenate([bd1[:CMP_STRIDE], bd1[CMP_STRIDE:]], axis=2).astype(BF16)
    w1p = w1p.reshape(CMP_STRIDE // 2, 2 * LANE, 4 * CMP_HIDDEN)
    z2 = jnp.zeros_like(w2)
    w2p = jnp.concatenate([jnp.concatenate([w2, z2], axis=1), jnp.concatenate([z2, w2], axis=1)], axis=0).astype(BF16)
    return pos2, w1p, w2p


CMP_GROUP = 2


def _fill_chunks(page_refs, xs_ref, rows_ref, feature_major):
    for g in range(CMP_GROUP):
        pages = page_refs[g * PAGES:(g + 1) * PAGES]
        if feature_major:
            for p, pr in enumerate(pages):
                rows_ref[p * PAGE_ROWS:(p + 1) * PAGE_ROWS, :] = pr[0].T
            for l in range(CMP_STRIDE):
                xs_ref[l, g * CHUNKS:(g + 1) * CHUNKS, :] = rows_ref[pl.ds(l, CHUNKS, stride=CMP_STRIDE), :]
        else:
            for p, pr in enumerate(pages):
                for l in range(CMP_STRIDE):
                    xs_ref[l, g * CHUNKS + 8 * p:g * CHUNKS + 8 * p + 8, :] = pr[0, pl.ds(l, PAGE_ROWS // CMP_STRIDE, stride=CMP_STRIDE), :]


def _compress_chunks(xs_ref, pos_ref, w1_ref, w2_ref):
    hid2 = 2 * CMP_HIDDEN
    rows = CMP_GROUP * CHUNKS
    acc_a = jnp.zeros((rows, hid2), F32)
    acc_b = jnp.zeros((rows, hid2), F32)
    for lp in range(CMP_STRIDE // 2):
        l0, l1 = 2 * lp, 2 * lp + 1
        x0, x1 = xs_ref[l0], xs_ref[l1]

        def lhs(off):
            return jnp.concatenate([(x0 + pos_ref[off + l0:off + l0 + 1, :]).astype(BF16),
                                    (x1 + pos_ref[off + l1:off + l1 + 1, :]).astype(BF16)], axis=1)

        acc_a = acc_a + jnp.dot(lhs(0), w1_ref[lp, :, 0:hid2], preferred_element_type=F32)
        acc_b = acc_b + jnp.dot(lhs(CMP_STRIDE), w1_ref[lp, :, hid2:2 * hid2], preferred_element_type=F32)
    hid = acc_a + pltpu.roll(acc_b, rows - 1, 0)
    out = jnp.dot(jax.nn.gelu(hid).astype(BF16), w2_ref[...], preferred_element_type=F32)
    row = lax.broadcasted_iota(I32, out.shape, 0) % CHUNKS
    return jnp.where(row < CHUNKS - 1, out, 0.0).reshape(CMP_GROUP, CHUNKS, LANE)


def _compress_kernel(pt_ref, *refs, feature_major):
    n_pages = CMP_GROUP * PAGES
    k_pages, v_pages = refs[:n_pages], refs[n_pages:2 * n_pages]
    posk_ref, w1k_ref, w2k_ref, posv_ref, w1v_ref, w2v_ref, kc_ref, vc_ref, xk_ref, xv_ref, rows_ref = refs[2 * n_pages:]
    _fill_chunks(k_pages, xk_ref, rows_ref, feature_major)
    kc_ref[...] = _compress_chunks(xk_ref, posk_ref, w1k_ref, w2k_ref)
    _fill_chunks(v_pages, xv_ref, rows_ref, feature_major)
    vc_ref[...] = _compress_chunks(xv_ref, posv_ref, w1v_ref, w2v_ref)


def _page_spec(p, half, feature_major):
    if feature_major:
        return pl.BlockSpec((1, LANE, PAGE_ROWS), lambda i, pt: (pt[i * CMP_GROUP * PAGES + p], half, 0))
    return pl.BlockSpec((1, PAGE_ROWS, LANE), lambda i, pt: (pt[i * CMP_GROUP * PAGES + p], 0, half))


def _const_spec(shape):
    nd = len(shape)
    return pl.BlockSpec(shape, lambda i, pt: (0,) * nd)


def _compress(pages, page_ids, cmp_wk, cmp_wv, feature_major=False):
    n_b = page_ids.shape[0] // PAGES
    assert n_b % CMP_GROUP == 0
    n_pages = CMP_GROUP * PAGES
    consts = list(cmp_wk) + list(cmp_wv)
    grid_spec = pltpu.PrefetchScalarGridSpec(
        num_scalar_prefetch=1,
        grid=(n_b // CMP_GROUP,),
        in_specs=[_page_spec(p, h, feature_major) for h in range(2) for p in range(n_pages)] + [_const_spec(c.shape) for c in consts],
        out_specs=[pl.BlockSpec((CMP_GROUP, CHUNKS, LANE), lambda i, pt: (i, 0, 0))] * 2,
        scratch_shapes=[pltpu.VMEM((CMP_STRIDE, CMP_GROUP * CHUNKS, LANE), F32)] * 2 + [pltpu.VMEM((PAGES * PAGE_ROWS, LANE), F32)],
    )
    return pl.pallas_call(
        functools.partial(_compress_kernel, feature_major=feature_major),
        grid_spec=grid_spec,
        out_shape=[jax.ShapeDtypeStruct((n_b, CHUNKS, LANE), F32)] * 2,
        compiler_params=_cparams("parallel"),
        name="compress",
    )(page_ids, *([pages] * (2 * n_pages)), *consts)


def _cover_tables(n_sel):
    c_start = np.arange(CHUNKS) * CMP_STRIDE
    s_start = np.arange(n_sel) * SEL_BLOCK
    cover = ((c_start[:, None] < s_start[None, :] + SEL_BLOCK) & (c_start[:, None] + CMP_LEN > s_start[None, :])).astype(np.float32)
    cover[CHUNKS - 1] = 0.0
    out = np.zeros((NSA_KV_HEADS, LANE, CHUNKS), np.float32)
    for h in range(NSA_KV_HEADS):
        out[h, h * 64:h * 64 + n_sel] = cover.T
    return jnp.asarray(out)


def _softmax_rows(s, valid):
    s = jnp.where(valid, s, NEG)
    m = jnp.max(s, axis=-1, keepdims=True)
    m = jnp.where(m > 0.5 * NEG, m, 0.0)
    p = jnp.where(valid, jnp.exp(s - m), 0.0)
    return p / jnp.maximum(jnp.sum(p, axis=-1, keepdims=True), 1e-30)


def _select_blocks(imp, n_sel, top_n):
    ridx = lax.broadcasted_iota(I32, imp.shape, 0)
    cnt = jnp.zeros(imp.shape, F32)
    for i in range(n_sel):
        vi = imp[i:i + 1, :]
        ahead = (vi > imp) | ((vi == imp) & (ridx > i))
        cnt = cnt + jnp.where(ahead, 1.0, 0.0)
    return jnp.where((cnt < top_n) & (ridx < n_sel), 1.0, 0.0)


def _cmp_attn_kernel(qn_ref, kc_ref, vc_ref, cov_ref, o_ref, sel_ref, *, tq, n_sel):
    qi = pl.program_id(1)
    kc = kc_ref[0].astype(BF16)
    vc = vc_ref[0].astype(BF16)
    qpos = qi * tq + lax.broadcasted_iota(I32, (tq, CHUNKS), 0)
    cidx = lax.broadcasted_iota(I32, (tq, CHUNKS), 1)
    valid = (cidx * CMP_STRIDE + CMP_LEN - 1 <= qpos) & (cidx < CHUNKS - 1)
    psum = [jnp.zeros((tq, CHUNKS), F32) for _ in range(NSA_KV_HEADS)]
    for hh in range(NSA_HEADS):
        q = qn_ref[0, :, hh * LANE:(hh + 1) * LANE].astype(BF16)
        p = _softmax_rows(_dot_nt(q, kc), valid)
        o_ref[0, :, hh * LANE:(hh + 1) * LANE] = jnp.dot(p.astype(BF16), vc, preferred_element_type=F32)
        psum[hh // NSA_GROUP] = psum[hh // NSA_GROUP] + p
    imp = (_dot_nt(cov_ref[0], psum[0], lax.Precision.HIGHEST) + _dot_nt(cov_ref[1], psum[1], lax.Precision.HIGHEST))
    blk = lax.broadcasted_iota(I32, (LANE, tq), 0) & 63
    qpos_t = qi * tq + lax.broadcasted_iota(I32, (LANE, tq), 1)
    cur = qpos_t // SEL_BLOCK
    forced = (blk == 0) | (blk == cur) | (blk == cur - 1)
    imp = jnp.where(forced, FORCE_SCORE, jnp.where(blk * SEL_BLOCK <= qpos_t, imp, -FORCE_SCORE))
    sel_t = jnp.concatenate([_select_blocks(imp[0:64], n_sel, SEL_TOP_N), _select_blocks(imp[64:128], n_sel, SEL_TOP_N)], axis=0)
    sel_ref[0] = sel_t.T


def _cmp_attn(qn, kc, vc, tq):
    b, t, _ = qn.shape
    n_sel = -(-t // SEL_BLOCK)
    cov = _cover_tables(n_sel)
    return pl.pallas_call(
        functools.partial(_cmp_attn_kernel, tq=tq, n_sel=n_sel),
        grid=(b, t // tq),
        in_specs=[pl.BlockSpec((1, tq, NSA_HEADS * LANE), lambda i, j: (i, j, 0)),
                  pl.BlockSpec((1, CHUNKS, LANE), lambda i, j: (i, 0, 0)),
                  pl.BlockSpec((1, CHUNKS, LANE), lambda i, j: (i, 0, 0)),
                  pl.BlockSpec((NSA_KV_HEADS, LANE, CHUNKS), lambda i, j: (0, 0, 0))],
        out_specs=[pl.BlockSpec((1, tq, NSA_HEADS * LANE), lambda i, j: (i, j, 0)),
                   pl.BlockSpec((1, tq, LANE), lambda i, j: (i, j, 0))],
        out_shape=[jax.ShapeDtypeStruct((b, t, NSA_HEADS * LANE), F32), jax.ShapeDtypeStruct((b, t, LANE), F32)],
        compiler_params=_cparams("parallel", "parallel"),
        name="cmp_attn",
    )(qn, kc, vc, cov)


def _key_block_table(t):
    blk = np.arange(t)[:, None] // SEL_BLOCK
    lanes = np.arange(LANE)[None, :] & 63
    return jnp.asarray((blk == lanes).astype(np.float32), dtype=BF16)


def _lane_fold(x, op):
    out = x[:, 0:LANE]
    for c in range(LANE, x.shape[1], LANE):
        out = op(out, x[:, c:c + LANE])
    return out


def _sel_attn_kernel(qn_ref, kv_ref, sel_ref, kb_ref, o_ref, q2_scr, k2_scr, v_scr, s_scr, *, tq, tk):
    qi = pl.program_id(1)
    rows = NSA_HEADS * tq
    t = kv_ref.shape[1]

    @pl.when(qi == 0)
    def _():
        k2_scr[:, 0:LANE] = kv_ref[0, :, 0:LANE].astype(BF16)
        k2_scr[:, LANE:2 * LANE] = kb_ref[...]
        v_scr[...] = kv_ref[0, :, LANE:2 * LANE].astype(BF16)

    not_sel = (1.0 - sel_ref[0]) * NEG
    lane_head = lax.broadcasted_iota(I32, (tq, LANE), 1) // 64
    for hh in range(NSA_HEADS):
        q2_scr[hh * tq:(hh + 1) * tq, 0:LANE] = qn_ref[0, :, hh * LANE:(hh + 1) * LANE].astype(BF16)
        q2_scr[hh * tq:(hh + 1) * tq, LANE:2 * LANE] = jnp.where(lane_head == hh // NSA_GROUP, not_sel, 0.0).astype(BF16)
    q2 = q2_scr[...]
    last = (qi * tq + tq - 1) // tk

    def scores(kt):
        k0 = pl.multiple_of(kt * tk, tk)
        return _dot_nt(q2, k2_scr[pl.ds(k0, tk), :])

    def pass1(kt, m_acc):
        s = scores(kt)
        s_scr[kt] = s
        return jnp.maximum(m_acc, _lane_fold(s, jnp.maximum))

    m_acc = lax.fori_loop(0, last, pass1, jnp.full((rows, LANE), NEG, F32))
    rel = (qi * tq + lax.broadcasted_iota(I32, (tq, tk), 0)) - (last * tk + lax.broadcasted_iota(I32, (tq, tk), 1))
    causal = jnp.where(rel >= 0, 0.0, NEG)
    s_last = scores(last) + jnp.concatenate([causal] * NSA_HEADS, axis=0)
    m = jnp.max(jnp.maximum(m_acc, _lane_fold(s_last, jnp.maximum)), axis=-1, keepdims=True)

    def accumulate(s, v, l_acc, acc):
        p = jnp.exp(s - m)
        return l_acc + _lane_fold(p, jnp.add), acc + jnp.dot(p.astype(BF16), v, preferred_element_type=F32)

    def pass2(kt, carry):
        k0 = pl.multiple_of(kt * tk, tk)
        return accumulate(s_scr[kt], v_scr[pl.ds(k0, tk), :], *carry)

    l_acc, acc = lax.fori_loop(0, last, pass2, (jnp.zeros((rows, LANE), F32), jnp.zeros((rows, LANE), F32)))
    l_acc, acc = accumulate(s_last, v_scr[pl.ds(pl.multiple_of(last * tk, tk), tk), :], l_acc, acc)
    out = acc / jnp.sum(l_acc, axis=-1, keepdims=True)
    for hh in range(NSA_HEADS):
        o_ref[0, :, hh * LANE:(hh + 1) * LANE] = out[hh * tq:(hh + 1) * tq]


def _sel_attn(qn, kv, sel, tq, tk):
    b, t, _ = qn.shape
    rows = NSA_HEADS * tq
    return pl.pallas_call(
        functools.partial(_sel_attn_kernel, tq=tq, tk=tk),
        grid=(b, t // tq),
        in_specs=[pl.BlockSpec((1, tq, NSA_HEADS * LANE), lambda i, j: (i, j, 0)),
                  pl.BlockSpec((1, t, 2 * LANE), lambda i, j: (i, 0, 0)),
                  pl.BlockSpec((1, tq, LANE), lambda i, j: (i, j, 0)),
                  pl.BlockSpec((t, LANE), lambda i, j: (0, 0))],
        out_specs=pl.BlockSpec((1, tq, NSA_HEADS * LANE), lambda i, j: (i, j, 0)),
        out_shape=jax.ShapeDtypeStruct((b, t, NSA_HEADS * LANE), F32),
        scratch_shapes=[pltpu.VMEM((rows, 2 * LANE), BF16), pltpu.VMEM((t, 2 * LANE), BF16), pltpu.VMEM((t, LANE), BF16),
                        pltpu.VMEM((t // tk, rows, tk), F32)],
        compiler_params=_cparams("parallel", "arbitrary"),
        name="sel_attn",
    )(qn, kv, sel, _key_block_table(t))


def _win_attn_kernel(qn_ref, kv_ref, o_ref, q_scr, k_scr, v_scr, *, tq):
    qi = pl.program_id(1)
    span = WINDOW + tq

    @pl.when(qi == 0)
    def _():
        k_scr[...] = kv_ref[0, :, 0:LANE].astype(BF16)
        v_scr[...] = kv_ref[0, :, LANE:2 * LANE].astype(BF16)

    for hh in range(NSA_HEADS):
        q_scr[hh * tq:(hh + 1) * tq, :] = qn_ref[0, :, hh * LANE:(hh + 1) * LANE].astype(BF16)
    k0 = pl.multiple_of(jnp.maximum(qi * tq - WINDOW, 0), tq)
    rel = (qi * tq + lax.broadcasted_iota(I32, (tq, span), 0)) - (k0 + lax.broadcasted_iota(I32, (tq, span), 1))
    bias = jnp.where((rel >= 0) & (rel <= WINDOW), 0.0, NEG)
    s = _dot_nt(q_scr[...], k_scr[pl.ds(k0, span), :]) + jnp.concatenate([bias] * NSA_HEADS, axis=0)
    p = jnp.exp(s - jnp.max(s, axis=-1, keepdims=True))
    out = jnp.dot(p.astype(BF16), v_scr[pl.ds(k0, span), :], preferred_element_type=F32) / jnp.sum(p, axis=-1, keepdims=True)
    for hh in range(NSA_HEADS):
        o_ref[0, :, hh * LANE:(hh + 1) * LANE] = out[hh * tq:(hh + 1) * tq]


def _win_attn(qn, kv, tq):
    b, t, _ = qn.shape
    assert t >= WINDOW + tq
    rows = NSA_HEADS * tq
    return pl.pallas_call(
        functools.partial(_win_attn_kernel, tq=tq),
        grid=(b, t // tq),
        in_specs=[pl.BlockSpec((1, tq, NSA_HEADS * LANE), lambda i, j: (i, j, 0)),
                  pl.BlockSpec((1, t, 2 * LANE), lambda i, j: (i, 0, 0))],
        out_specs=pl.BlockSpec((1, tq, NSA_HEADS * LANE), lambda i, j: (i, j, 0)),
        out_shape=jax.ShapeDtypeStruct((b, t, NSA_HEADS * LANE), F32),
        scratch_shapes=[pltpu.VMEM((rows, LANE), BF16), pltpu.VMEM((t, LANE), BF16), pltpu.VMEM((t, LANE), BF16)],
        compiler_params=_cparams("parallel", "arbitrary"),
        name="win_attn",
    )(qn, kv)


def _dec_softmax(scores, vals):
    m = scores[0].max(axis=-1, keepdims=True)
    for s in scores[1:]:
        m = jnp.maximum(m, s.max(axis=-1, keepdims=True))
    den = jnp.zeros_like(m)
    out = jnp.zeros((m.shape[0], LANE), F32)
    for s, v in zip(scores, vals):
        p = jnp.exp(s - m)
        den = den + p.sum(axis=-1, keepdims=True)
        out = out + (p * v if s.shape[1] == 1 else _dot_nt(p.astype(BF16), v))
    return out / den


DEC_GROUP = 2


def _nsa_decode_kernel(pt_ref, *refs, past_len, n_sel):
    per_seq = refs[DEC_GROUP * PAGES:]
    cov_ref = per_seq[6]
    for bb in range(DEC_GROUP):
        views = [r.at[pl.ds(bb, 1)] for i, r in enumerate(per_seq) if i != 6]
        _nsa_decode_one(refs[bb * PAGES:(bb + 1) * PAGES], *views[:6], cov_ref, *views[6:], past_len=past_len, n_sel=n_sel)


def _nsa_decode_one(pages, qn_ref, kc_ref, vc_ref, kvs_ref, kvw_ref, win_ref, cov_ref,
                    ocmp_ref, osel_ref, owin_ref, wnew_ref, *, past_len, n_sel):
    nh = NSA_HEADS
    q = jnp.concatenate([qn_ref[0, :, hh * LANE:(hh + 1) * LANE] for hh in range(nh)], axis=0)
    qb = q.astype(BF16)
    cidx = lax.broadcasted_iota(I32, (nh, CHUNKS), 1)
    valid = (cidx * CMP_STRIDE + CMP_LEN - 1 <= past_len) & (cidx < CHUNKS - 1)
    p = _softmax_rows(_dot_nt(qb, kc_ref[0].astype(BF16)), valid)
    o_cmp = jnp.dot(p.astype(BF16), vc_ref[0].astype(BF16), preferred_element_type=F32)
    imp = None
    for h in range(NSA_KV_HEADS):
        ps = jnp.sum(p[h * NSA_GROUP:(h + 1) * NSA_GROUP], axis=0, keepdims=True)
        term = _dot_nt(cov_ref[h], jnp.broadcast_to(ps, (LANE, CHUNKS)), lax.Precision.HIGHEST)
        imp = term if imp is None else imp + term
    blk = lax.broadcasted_iota(I32, (LANE, LANE), 0) & 63
    cur = past_len // SEL_BLOCK
    forced = (blk == 0) | (blk == cur) | (blk == cur - 1)
    imp = jnp.where(forced, FORCE_SCORE, jnp.where(blk * SEL_BLOCK <= past_len, imp, -FORCE_SCORE))
    sel_t = jnp.concatenate([_select_blocks(imp[0:64], n_sel, min(SEL_TOP_N, n_sel)),
                             _select_blocks(imp[64:128], n_sel, min(SEL_TOP_N, n_sel))], axis=0)
    sel = sel_t.T[0:1, :]
    head_of_row = lax.broadcasted_iota(I32, (nh, 1), 0) // NSA_GROUP

    def picked(s):
        return jnp.where(head_of_row == 0, sel[:, s:s + 1], sel[:, 64 + s:64 + s + 1])

    first_half = lax.broadcasted_iota(I32, (nh, PAGE_ROWS), 1) < SEL_BLOCK
    scores, vals = [], []
    for pg, pr in enumerate(pages):
        s = jnp.dot(qb, pr[0, 0:LANE, :].astype(BF16), preferred_element_type=F32)
        ok = jnp.where(first_half, picked(2 * pg), picked(2 * pg + 1)) > 0.5
        scores.append(jnp.where(ok, s, NEG))
        vals.append(pr[0, LANE:2 * LANE, :].astype(BF16))
    s_new = jnp.sum(q * kvs_ref[0, :, 0:LANE], axis=-1, keepdims=True)
    scores.append(jnp.where(picked(past_len // SEL_BLOCK) > 0.5, s_new, NEG))
    vals.append(kvs_ref[0, :, LANE:2 * LANE])
    o_sel = _dec_softmax(scores, vals)
    n_win = win_ref.shape[2]
    kpos = past_len - n_win + lax.broadcasted_iota(I32, (nh, n_win), 1)
    rel = past_len - kpos
    s_win = jnp.dot(qb, win_ref[0, 0:LANE, :].astype(BF16), preferred_element_type=F32)
    s_win = jnp.where((kpos >= 0) & (rel >= 0) & (rel <= WINDOW), s_win, NEG)
    s_new = jnp.sum(q * kvw_ref[0, :, 0:LANE], axis=-1, keepdims=True)
    o_win = _dec_softmax([s_win, s_new], [win_ref[0, LANE:2 * LANE, :].astype(BF16), kvw_ref[0, :, LANE:2 * LANE]])
    for hh in range(nh):
        ocmp_ref[0, :, hh * LANE:(hh + 1) * LANE] = o_cmp[hh:hh + 1]
        osel_ref[0, :, hh * LANE:(hh + 1) * LANE] = o_sel[hh:hh + 1]
        owin_ref[0, :, hh * LANE:(hh + 1) * LANE] = o_win[hh:hh + 1]
    for c in range(0, 2 * LANE, LANE):
        new_col = jnp.broadcast_to(kvw_ref[0, :, c:c + LANE], (LANE, LANE)).T
        shifted = pltpu.roll(win_ref[0, c:c + LANE, :], n_win - 1, 1)
        lane = lax.broadcasted_iota(I32, shifted.shape, 1)
        wnew_ref[0, c:c + LANE, :] = jnp.where(lane == n_win - 1, jnp.concatenate([new_col] * (n_win // LANE), axis=1), shifted)


def _nsa_decode(qn, kc, vc, sel_pages, page_ids, kvs_new, kvw_new, win_state):
    b = qn.shape[0]
    n_win = win_state.shape[2]
    past_len = PAGES * PAGE_ROWS
    n_sel = -(-(past_len + 1) // SEL_BLOCK)
    cov = _cover_tables(n_sel)

    assert b % DEC_GROUP == 0

    def per_b(shape):
        nd = len(shape)
        return pl.BlockSpec((DEC_GROUP,) + shape[1:], lambda i, pt: (i,) + (0,) * (nd - 1))

    slots = NSA_HEADS * LANE
    grid_spec = pltpu.PrefetchScalarGridSpec(
        num_scalar_prefetch=1,
        grid=(b // DEC_GROUP,),
        in_specs=[pl.BlockSpec((1, 2 * LANE, PAGE_ROWS), (lambda i, pt, p=p: (pt[i * DEC_GROUP * PAGES + p], 0, 0)))
                  for p in range(DEC_GROUP * PAGES)]
        + [per_b(qn.shape), per_b(kc.shape), per_b(vc.shape), per_b(kvs_new.shape), per_b(kvw_new.shape), per_b(win_state.shape),
           _const_spec(cov.shape)],
        out_specs=[per_b((b, 1, slots))] * 3 + [per_b(win_state.shape)],
    )
    return pl.pallas_call(
        functools.partial(_nsa_decode_kernel, past_len=past_len, n_sel=n_sel),
        grid_spec=grid_spec,
        out_shape=[jax.ShapeDtypeStruct((b, 1, slots), F32)] * 3 + [jax.ShapeDtypeStruct(win_state.shape, F32)],
        compiler_params=_cparams("parallel"),
        name="nsa_decode",
    )(page_ids, *([sel_pages] * (DEC_GROUP * PAGES)), qn, kc, vc, kvs_new, kvw_new, win_state, cov)


def _dot_tn(a, b):
    return lax.dot_general(a, b, (((0,), (0,)), ((), ())), preferred_element_type=F32)


def _cumsum_table():
    r = np.arange(LANE)
    return jnp.asarray(((r[:, None] // GLA_SUB == r[None, :] // GLA_SUB) & (r[None, :] <= r[:, None])).astype(np.float32))


def _gla_kernel(*refs, t, t_valid, has_state):
    if has_state:
        qg_ref, kg_ref, vg_ref, misc_ref, wa_ref, ba_ref, lt_ref, seg_ref, s0_ref, o_ref, s_ref, b_scr, st_scr = refs
    else:
        qg_ref, kg_ref, vg_ref, misc_ref, wa_ref, ba_ref, lt_ref, seg_ref, o_ref, s_ref, b_scr, st_scr = refs
    z = jnp.dot(misc_ref[0], wa_ref[...], preferred_element_type=F32, precision=lax.Precision.HIGHEST) + ba_ref[...]
    la = (jnp.minimum(z, 0.0) - jnp.log1p(jnp.exp(-jnp.abs(z)))) * (1.0 / GLA_TAU)
    if t_valid < t:
        la = jnp.where(lax.broadcasted_iota(I32, la.shape, 0) < t_valid, la, 0.0)
    tile = min(t, LANE)
    for r in range(0, t, tile):
        b_scr[r:r + tile, :] = jnp.dot(lt_ref[0:tile, 0:tile], la[r:r + tile, :], preferred_element_type=F32,
                                       precision=lax.Precision.HIGHEST)
    pairs = GLA_HEADS // 2
    for p in range(pairs):
        if has_state:
            st_scr[p] = s0_ref[0, 2 * p:2 * p + 2].reshape(2 * GLA_DK, GLA_DV).T
        else:
            st_scr[p] = jnp.zeros((GLA_DV, LANE), F32)
    head_a = lax.broadcasted_iota(I32, (GLA_SUB, LANE), 1) < GLA_DK
    row = lax.broadcasted_iota(I32, (GLA_SUB, LANE), 0)

    def pair_chunk(q, k, v, b, st):
        b_last = b[GLA_SUB - 1:GLA_SUB, :]
        st_b = st.astype(BF16)
        qe = q * jnp.exp(b)
        o = jnp.concatenate([_dot_nt(jnp.where(head_a, qe, 0.0).astype(BF16), st_b),
                             _dot_nt(jnp.where(head_a, 0.0, qe).astype(BF16), st_b)], axis=1)
        ws = []
        for j in range(GLA_SUB):
            w = q * k[j:j + 1, :] * jnp.exp(jnp.minimum(b - b[j:j + 1, :], 0.0))
            ws.append(jnp.where(row >= j, w, 0.0))
        a_all = jnp.dot(jnp.concatenate(ws, axis=0).astype(BF16), seg_ref[...], preferred_element_type=F32)
        for j in range(GLA_SUB):
            o = o + a_all[j * GLA_SUB:(j + 1) * GLA_SUB] * v[j:j + 1, :]
        kd = k * jnp.exp(b_last - b)
        upd = (_dot_tn(v[:, 0:GLA_DV].astype(BF16), jnp.where(head_a, kd, 0.0).astype(BF16))
               + _dot_tn(v[:, GLA_DV:2 * GLA_DV].astype(BF16), jnp.where(head_a, 0.0, kd).astype(BF16)))
        return o, jnp.exp(b_last) * st + upd

    def chunk(c, carry):
        r0 = pl.multiple_of(c * GLA_SUB, GLA_SUB)
        for p in range(pairs):
            o, st_new = pair_chunk(qg_ref[0, pl.ds(r0, GLA_SUB), p * LANE:(p + 1) * LANE],
                                          kg_ref[0, pl.ds(r0, GLA_SUB), p * LANE:(p + 1) * LANE],
                                          vg_ref[0, pl.ds(r0, GLA_SUB), 2 * p * GLA_DV:2 * (p + 1) * GLA_DV],
                                          b_scr[pl.ds(r0, GLA_SUB), p * LANE:(p + 1) * LANE], st_scr[p])
            o_ref[0, pl.ds(r0, GLA_SUB), 2 * p * GLA_DV:2 * (p + 1) * GLA_DV] = o
            st_scr[p] = st_new
        return carry

    lax.fori_loop(0, t // GLA_SUB, chunk, 0, unroll=4 if t // GLA_SUB % 4 == 0 else 1)
    for p in range(pairs):
        s_ref[0, 2 * p:2 * p + 2] = st_scr[p].T.reshape(2, GLA_DK, GLA_DV)


def _gla(qg, kg, vg, misc, wa_pad, ba, s0, t_valid):
    b, t, _ = qg.shape
    has_state = s0 is not None
    hk = GLA_HEADS * GLA_DK

    def per_b(shape):
        nd = len(shape)
        return pl.BlockSpec((1,) + shape[1:], lambda i: (i,) + (0,) * (nd - 1))

    def const(shape):
        nd = len(shape)
        return pl.BlockSpec(shape, lambda i: (0,) * nd)

    lanes = np.arange(LANE)[:, None] < GLA_DK
    seg = jnp.asarray((lanes == (np.arange(2 * GLA_DV)[None, :] < GLA_DV)).astype(np.float32), dtype=BF16)
    in_specs = [per_b(qg.shape), per_b(kg.shape), per_b(vg.shape), per_b(misc.shape),
                const((LANE, hk)), const((1, hk)), const((LANE, LANE)), const((LANE, 2 * GLA_DV))]
    args = [qg, kg, vg, misc, wa_pad, ba.reshape(1, -1), _cumsum_table(), seg]
    state_shape = (b, GLA_HEADS, GLA_DK, GLA_DV)
    if has_state:
        in_specs.append(per_b(state_shape))
        args.append(s0)
    return pl.pallas_call(
        functools.partial(_gla_kernel, t=t, t_valid=t_valid, has_state=has_state),
        grid=(b,),
        in_specs=in_specs,
        out_specs=[per_b(vg.shape), per_b(state_shape)],
        out_shape=[jax.ShapeDtypeStruct(vg.shape, F32), jax.ShapeDtypeStruct(state_shape, F32)],
        scratch_shapes=[pltpu.VMEM((t, hk), F32), pltpu.VMEM((GLA_HEADS // 2, GLA_DV, LANE), F32)],
        compiler_params=_cparams("parallel"),
        name="gla",
    )(*args)


def _gate_expand_table():
    out = np.zeros((3, LANE, NSA_HEADS * LANE), np.float32)
    for hh in range(NSA_HEADS):
        for j in range(3):
            out[j, MISC_GN + 3 * hh + j, hh * LANE:(hh + 1) * LANE] = 1.0
    return jnp.asarray(np.concatenate([out, out], axis=1), dtype=BF16)


def _pad_br_a(w_br_a):
    zero = jnp.zeros((HEAD_DIM, w_br_a.shape[1]), w_br_a.dtype)
    parts = []
    for hh in range(NSA_HEADS):
        wh = w_br_a[hh * HEAD_DIM:(hh + 1) * HEAD_DIM]
        parts += [wh, zero] if hh // NSA_GROUP == 0 else [zero, wh]
    return jnp.concatenate(parts, axis=0).astype(BF16)


def _layer_norm(v, g, b):
    mu = jnp.mean(v, axis=-1, keepdims=True)
    var = jnp.mean(jnp.square(v - mu), axis=-1, keepdims=True)
    return (v - mu) * lax.rsqrt(var + LN_EPS) * g + b


def _mixer_tail_kernel(ocmp_ref, osel_ref, owin_ref, misc_ref, ogla_ref, rg_ref, gm_ref, x_ref, gate_ref, scf_ref, shf_ref,
                       ex_ref, ng_ref, wa_ref, wb_ref, wo_ref, lg_ref, lb_ref, x1_ref, xm_ref):
    sig = jax.nn.sigmoid(misc_ref[0])
    sig_hi = sig.astype(BF16)
    sig_lo = (sig - sig_hi.astype(F32)).astype(BF16)
    sig_split = jnp.concatenate([sig_hi, sig_lo], axis=1)
    o_nsa = None
    for j, ref in enumerate((ocmp_ref, osel_ref, owin_ref)):
        g = jnp.dot(sig_split, ex_ref[j], preferred_element_type=F32)
        o_nsa = g * ref[0] if o_nsa is None else o_nsa + g * ref[0]
    br_a = _bdot(o_nsa, wa_ref[...])
    heads = []
    for h in range(GLA_HEADS):
        seg = ogla_ref[0, :, h * GLA_DV:(h + 1) * GLA_DV]
        mu = jnp.mean(seg, axis=-1, keepdims=True)
        var = jnp.mean(jnp.square(seg - mu), axis=-1, keepdims=True)
        r = rg_ref[0, :, h * GLA_DV:(h + 1) * GLA_DV]
        heads.append((seg - mu) * lax.rsqrt(var + LN_EPS) * ng_ref[:, h * GLA_DV:(h + 1) * GLA_DV] * (r * jax.nn.sigmoid(r)))
    br_b = _bdot(jnp.concatenate(heads, axis=1), wb_ref[...])
    gm_a = jax.nn.sigmoid(gm_ref[0, :, 0:D_MODEL])
    gm_b = jax.nn.sigmoid(gm_ref[0, :, D_MODEL:2 * D_MODEL])
    y = _bdot(gm_a * br_a + gm_b * br_b, wo_ref[...])
    x1 = _layer_norm(DN_ALPHA * x_ref[0] + gate_ref[0] * y, lg_ref[...], lb_ref[...])
    x1_ref[0] = x1
    xm_ref[0] = x1 * (1.0 + scf_ref[0]) + shf_ref[0]


def _mixer_tail(ocmp, osel, owin, misc, ogla, rg, gm, x, gate_m, scale_f, shift_f, consts, tm):
    b, t, d = x.shape
    per_tok = gate_m.shape[1] != 1

    def tok(w):
        return pl.BlockSpec((1, tm, w), lambda i, j: (i, j, 0))

    mod_spec = tok(d) if per_tok else pl.BlockSpec((1, 1, d), lambda i, j: (i, 0, 0))

    def const(a):
        nd = a.ndim
        return pl.BlockSpec(a.shape, lambda i, j: (0,) * nd)

    return pl.pallas_call(
        _mixer_tail_kernel,
        grid=(b, t // tm),
        in_specs=[tok(NSA_HEADS * LANE)] * 3 + [tok(LANE), tok(GLA_HEADS * GLA_DV), tok(GLA_HEADS * GLA_DV), tok(2 * d), tok(d),
                                               mod_spec, mod_spec, mod_spec] + [const(c) for c in consts],
        out_specs=[tok(d), tok(d)],
        out_shape=[jax.ShapeDtypeStruct((b, t, d), F32)] * 2,
        compiler_params=_cparams("parallel", "parallel"),
        name="mixer_tail",
    )(ocmp, osel, owin, misc, ogla, rg, gm, x, gate_m, scale_f, shift_f, *consts)


ROUTE_TILE = LANE


def _first_index(hit, iota, size, axis):
    return jnp.min(jnp.where(hit, iota, size), axis=axis, keepdims=True)


def _part_specs(parts, tm):
    n_a = parts[0].shape[0] // tm
    d = parts[0].shape[1]
    return [pl.BlockSpec((tm, d), lambda i, *_: (jnp.minimum(i, n_a - 1), 0)),
            pl.BlockSpec((tm, d), lambda i, *_: (jnp.maximum(i - n_a, 0), 0))], n_a


def _part_tile(i, n_a, a_ref, b_ref):
    return jnp.where(i < n_a, a_ref[...], b_ref[...])


def _router_kernel(xa_ref, xb_ref, wr_ref, bias_ref, tri_ref, eidx_ref, rank_ref, wrow_ref, cnt_ref, carry_ref, *, n_a):
    i = pl.program_id(0)
    tm = ROUTE_TILE
    per = N_EXPERTS // N_GROUPS

    @pl.when(i == 0)
    def _():
        carry_ref[...] = jnp.zeros_like(carry_ref)

    logits = _dot_nt(wr_ref[...], _part_tile(i, n_a, xa_ref, xb_ref), lax.Precision.HIGHEST)
    s = jax.nn.sigmoid(logits)
    sb = s + bias_ref[...]
    sb3 = sb.reshape(N_GROUPS, per, tm)
    in_grp = lax.broadcasted_iota(I32, sb3.shape, 1)
    m1 = jnp.max(sb3, axis=1, keepdims=True)
    first = _first_index(sb3 == m1, in_grp, per, 1)
    m2 = jnp.max(jnp.where(in_grp == first, NEG, sb3), axis=1, keepdims=True)
    gs = (m1 + m2).reshape(N_GROUPS, tm)
    g_iota = lax.broadcasted_iota(I32, gs.shape, 0)
    g_keep = jnp.zeros(gs.shape, jnp.bool_)
    for _ in range(TOPK_GROUPS):
        pick = g_iota == _first_index(gs == jnp.max(gs, axis=0, keepdims=True), g_iota, N_GROUPS, 0)
        g_keep = g_keep | pick
        gs = jnp.where(pick, NEG, gs)
    sbm = jnp.where(g_keep.reshape(N_GROUPS, 1, tm), sb3, NEG).reshape(N_EXPERTS, tm)
    e_iota = lax.broadcasted_iota(I32, sbm.shape, 0)
    idxs, sels = [], []
    onehot = jnp.zeros(sbm.shape, F32)
    for _ in range(TOP_K):
        idx = _first_index(sbm == jnp.max(sbm, axis=0, keepdims=True), e_iota, N_EXPERTS, 0)
        pick = e_iota == idx
        idxs.append(idx)
        sels.append(jnp.sum(jnp.where(pick, s, 0.0), axis=0, keepdims=True))
        sbm = jnp.where(pick, NEG, sbm)
        onehot = onehot + jnp.where(pick, 1.0, 0.0)
    sel = jnp.concatenate(sels, axis=0)
    wts = sel / jnp.sum(sel, axis=0, keepdims=True) * ROUTED_SCALE
    carry = carry_ref[...]
    before = carry + jnp.dot(onehot.astype(BF16), tri_ref[...], preferred_element_type=F32)
    ranks = [jnp.sum(jnp.where(e_iota == idx, before, 0.0), axis=0, keepdims=True) for idx in idxs]
    eidx_ref[...] = jnp.concatenate(idxs, axis=0)
    rank_ref[...] = jnp.concatenate(ranks, axis=0).astype(I32)
    wrow_ref[...] = jnp.concatenate([wts, jnp.zeros((LANE - TOP_K, tm), F32)], axis=0).T
    carry = carry + jnp.sum(onehot, axis=1, keepdims=True)
    carry_ref[...] = carry
    cnt_ref[...] = carry


def _router(xm_parts, w_router, router_bias):
    n = xm_parts[0].shape[0] + xm_parts[1].shape[0]
    d = xm_parts[0].shape[1]
    tm = ROUTE_TILE
    r = np.arange(tm)
    tri = jnp.asarray((r[:, None] < r[None, :]).astype(np.float32), dtype=BF16)
    x_specs, n_a = _part_specs(xm_parts, tm)
    return pl.pallas_call(
        functools.partial(_router_kernel, n_a=n_a),
        grid=(n // tm,),
        in_specs=x_specs + [
                  pl.BlockSpec((N_EXPERTS, d), lambda i: (0, 0)),
                  pl.BlockSpec((N_EXPERTS, 1), lambda i: (0, 0)),
                  pl.BlockSpec((tm, tm), lambda i: (0, 0))],
        out_specs=[pl.BlockSpec((TOP_K, tm), lambda i: (0, i)),
                   pl.BlockSpec((TOP_K, tm), lambda i: (0, i)),
                   pl.BlockSpec((tm, LANE), lambda i: (i, 0)),
                   pl.BlockSpec((N_EXPERTS, LANE), lambda i: (0, 0))],
        out_shape=[jax.ShapeDtypeStruct((TOP_K, n), I32), jax.ShapeDtypeStruct((TOP_K, n), I32),
                   jax.ShapeDtypeStruct((n, LANE), F32), jax.ShapeDtypeStruct((N_EXPERTS, LANE), F32)],
        scratch_shapes=[pltpu.VMEM((N_EXPERTS, LANE), F32)],
        compiler_params=_cparams("arbitrary"),
        name="router",
    )(*xm_parts, w_router.T, router_bias.reshape(N_EXPERTS, 1), tri)


def _dest_kernel(eidx_ref, rank_ref, start_ref, dest_ref):
    e_iota = lax.broadcasted_iota(I32, (N_EXPERTS, ROUTE_TILE), 0)
    start = start_ref[...]
    rows = [jnp.sum(jnp.where(e_iota == eidx_ref[k:k + 1, :], start, 0.0), axis=0, keepdims=True) for k in range(TOP_K)]
    dest_ref[0] = jnp.concatenate(rows, axis=0).astype(I32) + rank_ref[...]


def _dest(eidx, rank, pad_start):
    n = eidx.shape[1]
    tm = ROUTE_TILE
    return pl.pallas_call(
        _dest_kernel,
        grid=(n // tm,),
        in_specs=[pl.BlockSpec((TOP_K, tm), lambda i: (0, i)),
                  pl.BlockSpec((TOP_K, tm), lambda i: (0, i)),
                  pl.BlockSpec((N_EXPERTS, 1), lambda i: (0, 0))],
        out_specs=pl.BlockSpec((1, TOP_K, tm), lambda i: (i, 0, 0)),
        out_shape=jax.ShapeDtypeStruct((n // tm, TOP_K, tm), I32),
        compiler_params=_cparams("parallel"),
        name="dest",
    )(eidx, rank, pad_start.astype(F32).reshape(N_EXPERTS, 1))


TOKEN_SUB = D_MODEL // LANE


def _to_token_tiles(x, ref):
    m = x.shape[0]
    for s in range(TOKEN_SUB):
        ref[pl.ds(s, m, stride=TOKEN_SUB), :] = x[:, s * LANE:(s + 1) * LANE]


def _from_token_tiles(ref, m, s):
    return ref[pl.ds(s, m, stride=TOKEN_SUB), :]


def _row_copy(src_ref, src_row, dst_ref, dst_row, sem):
    return pltpu.make_async_copy(src_ref.at[pl.ds(src_row * TOKEN_SUB, TOKEN_SUB)],
                                 dst_ref.at[pl.ds(dst_row * TOKEN_SUB, TOKEN_SUB)], sem)


def _dispatch_kernel(dest_ref, pad_ref, nb_ref, xa_ref, xb_ref, xs_ref, tile_ref, zero_ref, sems, sem, *, n_a):
    i = pl.program_id(0)
    tm = ROUTE_TILE
    bm = EXPERT_BLOCK
    last = pl.num_programs(0) - 1
    src = tile_ref.at[i % 2]
    _to_token_tiles(_part_tile(i, n_a, xa_ref, xb_ref), src)

    def start_row(r, c):
        for k in range(TOP_K):
            _row_copy(src, r, xs_ref, dest_ref[0, k, r], sems.at[i % 2]).start()
        return c

    def wait_step(slot):
        for _ in range(TOP_K):
            pltpu.make_async_copy(tile_ref.at[0], xs_ref.at[pl.ds(0, tm * TOKEN_SUB)], sems.at[slot]).wait()

    lax.fori_loop(0, tm, start_row, 0)

    @pl.when(i >= 1)
    def _():
        wait_step((i - 1) % 2)

    @pl.when(i == last)
    def _():
        wait_step(i % 2)
        zero_ref[...] = jnp.zeros_like(zero_ref)

        def per_expert(e, c):
            lo, hi = pad_ref[0, e], pad_ref[1, e]
            lax.fori_loop(lo, hi, lambda r, cc: (_row_copy(zero_ref, 0, xs_ref, r, sem).start(), cc)[1], 0)
            lax.fori_loop(lo, hi, lambda r, cc: (_row_copy(zero_ref, 0, xs_ref, r, sem).wait(), cc)[1], 0)
            return c

        lax.fori_loop(0, N_EXPERTS, per_expert, 0)

        def tail_copy(blk):
            return pltpu.make_async_copy(zero_ref, xs_ref.at[pl.ds(blk * bm * TOKEN_SUB, bm * TOKEN_SUB)], sem)

        n_blocks = xs_ref.shape[0] // (bm * TOKEN_SUB)
        lax.fori_loop(nb_ref[0], n_blocks, lambda blk, c: (tail_copy(blk).start(), c)[1], 0)
        lax.fori_loop(nb_ref[0], n_blocks, lambda blk, c: (tail_copy(blk).wait(), c)[1], 0)


def _dispatch(xm_parts, dest_tiles, pad_range, n_used, n_rows):
    tm = ROUTE_TILE
    x_specs, n_a = _part_specs(xm_parts, tm)
    return pl.pallas_call(
        functools.partial(_dispatch_kernel, n_a=n_a),
        grid=(dest_tiles.shape[0],),
        in_specs=[pl.BlockSpec((1, TOP_K, tm), lambda i: (i, 0, 0), memory_space=pltpu.SMEM),
                  pl.BlockSpec(memory_space=pltpu.SMEM),
                  pl.BlockSpec(memory_space=pltpu.SMEM)] + x_specs,
        out_specs=pl.BlockSpec(memory_space=pl.ANY),
        out_shape=jax.ShapeDtypeStruct((n_rows * TOKEN_SUB, LANE), F32),
        scratch_shapes=[pltpu.VMEM((2, tm * TOKEN_SUB, LANE), F32), pltpu.VMEM((EXPERT_BLOCK * TOKEN_SUB, LANE), F32),
                        pltpu.SemaphoreType.DMA((2,)), pltpu.SemaphoreType.DMA(())],
        compiler_params=_cparams("arbitrary"),
        name="dispatch",
    )(dest_tiles, pad_range, n_used, *xm_parts)


def _experts_kernel(first_ref, cnt_ref, nb_ref, xs_ref, wg_ref, wu_ref, wd_ref, ys_ref,
                    xbuf, ybuf, wg_s, wu_s, wd_s, in_sems, out_sems):
    e = pl.program_id(0)
    bm = EXPERT_BLOCK
    brows = bm * TOKEN_SUB
    total = nb_ref[0]
    n_blocks = xs_ref.shape[0] // brows

    def x_copy(g, slot):
        return pltpu.make_async_copy(xs_ref.at[pl.ds(g * brows, brows)], xbuf.at[slot], in_sems.at[slot])

    def y_copy(g, slot):
        return pltpu.make_async_copy(ybuf.at[slot], ys_ref.at[pl.ds(g * brows, brows)], out_sems.at[slot])

    depth = xbuf.shape[0]

    @pl.when(e == 0)
    def _():
        for g0 in range(depth - 1):
            @pl.when(g0 < total)
            def _():
                x_copy(g0, g0).start()

    @pl.when(cnt_ref[e] > 0)
    def _():
        wg_s[...] = wg_ref[0].astype(BF16)
        wu_s[...] = wu_ref[0].astype(BF16)
        wd_s[...] = wd_ref[0].astype(BF16)

    def block(c, carry):
        g = first_ref[e] + c
        slot = g % depth
        x_copy(g, slot).wait()

        @pl.when(g + depth - 1 < total)
        def _():
            x_copy(g + depth - 1, (g + depth - 1) % depth).start()

        x = jnp.concatenate([_from_token_tiles(xbuf.at[slot], bm, s) for s in range(TOKEN_SUB)], axis=1).astype(BF16)
        gate = jnp.dot(x, wg_s[...], preferred_element_type=F32)
        up = jnp.dot(x, wu_s[...], preferred_element_type=F32)
        y = jnp.dot((gate * jax.nn.sigmoid(gate) * up).astype(BF16), wd_s[...], preferred_element_type=F32)

        @pl.when(g >= depth)
        def _():
            y_copy(g - depth, slot).wait()

        _to_token_tiles(y, ybuf.at[slot])
        y_copy(g, slot).start()
        return carry

    lax.fori_loop(0, cnt_ref[e], block, 0)

    @pl.when(e == pl.num_programs(0) - 1)
    def _():
        for back in range(depth, 0, -1):
            @pl.when(total >= back)
            def _():
                y_copy(total - back, (total - back) % depth).wait()

        ybuf[0] = jnp.zeros(ybuf.shape[1:], F32)
        lax.fori_loop(total, n_blocks, lambda g, c: (y_copy(g, 0).start(), c)[1], 0)
        lax.fori_loop(total, n_blocks, lambda g, c: (y_copy(g, 0).wait(), c)[1], 0)


def _experts(xs, first_block, n_block, n_used, w_gate, w_up, w_down):
    brows = EXPERT_BLOCK * TOKEN_SUB
    n_exp, d, f = w_gate.shape
    grid_spec = pltpu.PrefetchScalarGridSpec(
        num_scalar_prefetch=3,
        grid=(n_exp,),
        in_specs=[pl.BlockSpec(memory_space=pl.ANY),
                  pl.BlockSpec((1, d, f), lambda e, *_: (e, 0, 0)),
                  pl.BlockSpec((1, d, f), lambda e, *_: (e, 0, 0)),
                  pl.BlockSpec((1, f, d), lambda e, *_: (e, 0, 0))],
        out_specs=pl.BlockSpec(memory_space=pl.ANY),
        scratch_shapes=[pltpu.VMEM((EXPERT_RING, brows, LANE), F32), pltpu.VMEM((EXPERT_RING, brows, LANE), F32),
                        pltpu.VMEM((d, f), BF16), pltpu.VMEM((d, f), BF16), pltpu.VMEM((f, d), BF16),
                        pltpu.SemaphoreType.DMA((EXPERT_RING,)), pltpu.SemaphoreType.DMA((EXPERT_RING,))],
    )
    return pl.pallas_call(
        _experts_kernel,
        grid_spec=grid_spec,
        out_shape=jax.ShapeDtypeStruct(xs.shape, F32),
        compiler_params=_cparams("arbitrary"),
        name="experts",
    )(first_block, n_block, n_used, xs, w_gate, w_up, w_down)


def _combine_kernel(dest_ref, dnext_ref, ys_ref, wrow_ref, xa_ref, xb_ref, sg_ref, su_ref, sd_ref, out_ref, buf_ref, sems,
                    *, n_a):
    i = pl.program_id(0)
    tm = ROUTE_TILE
    slot = i % 2

    def issue(d_ref, s):
        def start_row(r, c):
            for k in range(TOP_K):
                _row_copy(ys_ref, d_ref[0, k, r], buf_ref.at[s, k], r, sems.at[s]).start()
            return c

        lax.fori_loop(0, tm, start_row, 0)

    @pl.when(i == 0)
    def _():
        issue(dest_ref, 0)

    @pl.when(i < pl.num_programs(0) - 1)
    def _():
        issue(dnext_ref, 1 - slot)

    x = _part_tile(i, n_a, xa_ref, xb_ref).astype(BF16)
    g = jnp.dot(x, sg_ref[...], preferred_element_type=F32)
    u = jnp.dot(x, su_ref[...], preferred_element_type=F32)
    shared = jnp.dot((g * jax.nn.sigmoid(g) * u).astype(BF16), sd_ref[...], preferred_element_type=F32)
    for k in range(TOP_K):
        pltpu.make_async_copy(ys_ref.at[pl.ds(0, tm * TOKEN_SUB)], buf_ref.at[slot, k], sems.at[slot]).wait()
    w = wrow_ref[...]
    for s in range(TOKEN_SUB):
        routed = w[:, 0:1] * _from_token_tiles(buf_ref.at[slot, 0], tm, s)
        for k in range(1, TOP_K):
            routed = routed + w[:, k:k + 1] * _from_token_tiles(buf_ref.at[slot, k], tm, s)
        out_ref[:, s * LANE:(s + 1) * LANE] = routed + shared[:, s * LANE:(s + 1) * LANE]


def _combine(ys, dest_tiles, wrow, xm_parts, ws_gate, ws_up, ws_down):
    n = xm_parts[0].shape[0] + xm_parts[1].shape[0]
    d = xm_parts[0].shape[1]
    tm = ROUTE_TILE
    f = ws_gate.shape[1]
    x_specs, n_a = _part_specs(xm_parts, tm)
    return pl.pallas_call(
        functools.partial(_combine_kernel, n_a=n_a),
        grid=(n // tm,),
        in_specs=[pl.BlockSpec((1, TOP_K, tm), lambda i: (i, 0, 0), memory_space=pltpu.SMEM),
                  pl.BlockSpec((1, TOP_K, tm), lambda i: (jnp.minimum(i + 1, n // tm - 1), 0, 0), memory_space=pltpu.SMEM),
                  pl.BlockSpec(memory_space=pl.ANY),
                  pl.BlockSpec((tm, LANE), lambda i: (i, 0))] + x_specs + [
                  pl.BlockSpec((d, f), lambda i: (0, 0)),
                  pl.BlockSpec((d, f), lambda i: (0, 0)),
                  pl.BlockSpec((f, d), lambda i: (0, 0))],
        out_specs=pl.BlockSpec((tm, d), lambda i: (i, 0)),
        out_shape=jax.ShapeDtypeStruct((n, d), F32),
        scratch_shapes=[pltpu.VMEM((2, TOP_K, tm * TOKEN_SUB, LANE), F32), pltpu.SemaphoreType.DMA((2,))],
        compiler_params=_cparams("arbitrary"),
        name="combine",
    )(dest_tiles, dest_tiles, ys, wrow, *xm_parts, ws_gate.astype(BF16), ws_up.astype(BF16), ws_down.astype(BF16))


def _final_ln_kernel(x1_ref, moe_ref, gate_ref, g_ref, b_ref, y_ref):
    y_ref[0] = _layer_norm(DN_ALPHA * x1_ref[0] + gate_ref[0] * moe_ref[...], g_ref[...], b_ref[...])


def _final_ln(x1, moe, row0, gate_f, ln_g, ln_b, tm):
    b, t, d = x1.shape
    per_tok = gate_f.shape[1] != 1
    tok = pl.BlockSpec((1, tm, d), lambda i, j: (i, j, 0))
    mod_spec = tok if per_tok else pl.BlockSpec((1, 1, d), lambda i, j: (i, 0, 0))
    vec = pl.BlockSpec((1, d), lambda i, j: (0, 0))
    moe_spec = pl.BlockSpec((tm, d), lambda i, j: (row0 // tm + i * (t // tm) + j, 0))
    return pl.pallas_call(
        _final_ln_kernel,
        grid=(b, t // tm),
        in_specs=[tok, moe_spec, mod_spec, vec, vec],
        out_specs=tok,
        out_shape=jax.ShapeDtypeStruct((b, t, d), F32),
        compiler_params=_cparams("parallel", "parallel"),
        name="final_ln",
    )(x1, moe, gate_f, ln_g.reshape(1, d), ln_b.reshape(1, d))


def _moe(xm, w_router, router_bias, w_e_gate, w_e_up, w_e_down, w_s_gate, w_s_up, w_s_down):
    n = xm[0].shape[0] + xm[1].shape[0]
    eidx, rank, wrow, cnt = _router(xm, w_router, router_bias)
    counts = cnt[:, 0].astype(I32)
    padded = (counts + EXPERT_BLOCK - 1) // EXPERT_BLOCK * EXPERT_BLOCK
    pad_end = jnp.cumsum(padded)
    pad_start = pad_end - padded
    dest_tiles = _dest(eidx, rank, pad_start)
    n_blocks = -(-(n * TOP_K) // EXPERT_BLOCK) + N_EXPERTS
    n_used = (pad_end[-1:] // EXPERT_BLOCK).astype(I32)
    pad_range = jnp.stack([pad_start + counts, pad_end]).astype(I32)
    xs = _dispatch(xm, dest_tiles, pad_range, n_used, n_blocks * EXPERT_BLOCK)
    ys = _experts(xs, (pad_start // EXPERT_BLOCK).astype(I32), (padded // EXPERT_BLOCK).astype(I32), n_used,
                  w_e_gate, w_e_up, w_e_down)
    return _combine(ys, dest_tiles, wrow, xm, w_s_gate, w_s_up, w_s_down)


def kernel(x_prompt, x_sample, cache_kv_cmp, cache_kv_sel, state_kv_win, state_gla, page_table, c_prompt, c_sample, w_in, b_in, cmp_k_pos, cmp_k_w1, cmp_k_w2, cmp_v_pos, cmp_v_w1, cmp_v_w2, gla_w_a2, gla_b_a, gla_norm_g, w_br_a, w_br_b, w_out, ln1_g, ln1_b, w_ada, b_ada, w_router, router_bias, w_e_gate, w_e_up, w_e_down, w_s_gate, w_s_up, w_s_down, ln2_g, ln2_b):
    bp, tp, d = x_prompt.shape
    nd, td = x_sample.shape[:2]
    n_pool, page_rows = cache_kv_cmp.shape[:2]
    past_len = page_table.shape[1] * page_rows
    assert td == 1 and d == D_MODEL and page_rows == PAGE_ROWS and page_table.shape[1] == PAGES and tp == PAGES * PAGE_ROWS
    kv_w = 2 * NSA_KV_HEADS * HEAD_DIM

    mod = _adaln(jnp.concatenate([c_prompt, c_sample], axis=0), w_ada, b_ada)
    mod_p = [m.reshape(bp, 1, d) for m in jnp.split(mod[:bp], 6, axis=-1)]
    mod_s = [m.reshape(1, nd, d) for m in jnp.split(mod[bp:], 6, axis=-1)]

    w_pack, b_pack = _pack_in_weights(w_in, b_in)
    cmp_wk = _pack_cmp_weights(cmp_k_pos, cmp_k_w1, cmp_k_w2)
    cmp_wv = _pack_cmp_weights(cmp_v_pos, cmp_v_w1, cmp_v_w2)
    wa_pad = jnp.zeros((LANE, GLA_HEADS * GLA_DK), F32).at[MISC_AG:MISC_AG + GLA_GATE_RANK].set(gla_w_a2)
    tail_consts = (_gate_expand_table(), gla_norm_g.reshape(1, -1), _pad_br_a(w_br_a), w_br_b.astype(BF16), w_out.astype(BF16),
                   ln1_g.reshape(1, d), ln1_b.reshape(1, d))

    gm, qn, vg, rg, kvc, kvs, kvw, qg, kg, misc = _inproj(
        x_prompt, mod_p[0], mod_p[1], w_pack, b_pack, _rope_tables(jnp.arange(tp, dtype=I32)), 256)
    kc, vc = _compress(kvc.reshape(bp * PAGES, PAGE_ROWS, kv_w), jnp.arange(bp * PAGES, dtype=I32), cmp_wk, cmp_wv)
    ocmp, sel = _cmp_attn(qn, kc, vc, 256)
    osel = _sel_attn(qn, kvs, sel, 128, 256)
    owin = _win_attn(qn, kvw, 128)
    ogla, gla_p = _gla(qg, kg, vg, misc, wa_pad, gla_b_a, None, tp)
    x1_p, xm_p = _mixer_tail(ocmp, osel, owin, misc, ogla, rg, gm, x_prompt, mod_p[2], mod_p[4], mod_p[3], tail_consts, 256)
    n_win = min(WINDOW, tp)
    outs_p = (kvc.reshape(bp, tp, 2, NSA_KV_HEADS, HEAD_DIM), kvs.reshape(bp, tp, 2, NSA_KV_HEADS, HEAD_DIM),
              kvw[:, tp - n_win:].reshape(bp, n_win, 2, NSA_KV_HEADS, HEAD_DIM), gla_p)

    gm, qn, vg, rg, kvc, kvs, kvw, qg, kg, misc = _inproj(
        x_sample.reshape(1, nd, d), mod_s[0], mod_s[1], w_pack, b_pack, _rope_tables(jnp.full((nd,), past_len, I32)), nd)
    page_ids = page_table.reshape(-1).astype(I32)
    kc, vc = _compress(cache_kv_cmp.reshape(n_pool, PAGE_ROWS, kv_w).transpose(0, 2, 1), page_ids, cmp_wk, cmp_wv,
                       feature_major=True)
    ocmp, osel, owin, win_new = _nsa_decode(
        qn.reshape(nd, 1, -1), kc, vc, cache_kv_sel.reshape(n_pool, PAGE_ROWS, kv_w).transpose(0, 2, 1), page_ids,
        kvs.reshape(nd, 1, kv_w), kvw.reshape(nd, 1, kv_w), state_kv_win.reshape(nd, -1, kv_w).transpose(0, 2, 1))
    win_new = win_new.transpose(0, 2, 1)

    def pad_rows(a):
        return jnp.pad(a.reshape(nd, 1, -1), ((0, 0), (0, GLA_SUB - 1), (0, 0)))

    ogla, gla_s = _gla(pad_rows(qg), pad_rows(kg), pad_rows(vg), pad_rows(misc), wa_pad, gla_b_a, state_gla, 1)
    x1_s, xm_s = _mixer_tail(ocmp.reshape(1, nd, -1), osel.reshape(1, nd, -1), owin.reshape(1, nd, -1), misc,
                             ogla[:, 0].reshape(1, nd, -1), rg, gm, x_sample.reshape(1, nd, d),
                             mod_s[2], mod_s[4], mod_s[3], tail_consts, nd)
    outs_s = (kvc.reshape(nd, 1, 2, NSA_KV_HEADS, HEAD_DIM), kvs.reshape(nd, 1, 2, NSA_KV_HEADS, HEAD_DIM),
              win_new.reshape(state_kv_win.shape), gla_s)

    n_p = bp * tp
    moe = _moe((xm_p.reshape(n_p, d), xm_s.reshape(nd, d)),
               w_router, router_bias, w_e_gate, w_e_up, w_e_down, w_s_gate, w_s_up, w_s_down)
    y_p = _final_ln(x1_p, moe, 0, mod_p[5], ln2_g, ln2_b, 256)
    y_s = _final_ln(x1_s, moe, n_p, mod_s[5], ln2_g, ln2_b, nd).reshape(nd, 1, d)
    return (y_p, y_s) + outs_p + outs_s
```

```python
import functools

import numpy as np
import jax
import jax.numpy as jnp
from jax import lax
from jax.experimental import pallas as pl
from jax.experimental.pallas import tpu as pltpu

F32 = jnp.float32
BF16 = jnp.bfloat16
I32 = jnp.int32

D_MODEL = 1024
NSA_HEADS = 8
NSA_KV_HEADS = 2
NSA_GROUP = NSA_HEADS // NSA_KV_HEADS
HEAD_DIM = 64
ROT_DIM = HEAD_DIM // 4
ROPE_THETA = 500000.0
CMP_LEN = 32
CMP_STRIDE = 16
CMP_HIDDEN = 256
SEL_BLOCK = 64
SEL_TOP_N = 16
WINDOW = 512
FORCE_SCORE = 1.0e4
GLA_HEADS = 4
GLA_DK = 64
GLA_DV = 128
GLA_GATE_RANK = 16
GLA_TAU = 16.0
GLA_SUB = 16
N_EXPERTS = 256
TOP_K = 8
N_GROUPS = 8
TOPK_GROUPS = 4
EXPERT_DIM = 256
SHARED_DIM = 256
ROUTED_SCALE = 2.5
EXPERT_BLOCK = 128
EXPERT_RING = 4
DN_ALPHA = 2.0 ** 0.25
LN_EPS = 1e-5
LANE = 128
NEG = -1.0e30
VMEM_LIMIT = 56 * 1024 * 1024

SEG_GM = (0, 2 * D_MODEL)
SEG_QN = (SEG_GM[0] + SEG_GM[1], NSA_HEADS * LANE)
SEG_VG = (SEG_QN[0] + SEG_QN[1], GLA_HEADS * GLA_DV)
SEG_RG = (SEG_VG[0] + SEG_VG[1], GLA_HEADS * GLA_DV)
SEG_KVC = (SEG_RG[0] + SEG_RG[1], 2 * LANE)
SEG_KVS = (SEG_KVC[0] + SEG_KVC[1], 2 * LANE)
SEG_KVW = (SEG_KVS[0] + SEG_KVS[1], 2 * LANE)
SEG_QG = (SEG_KVW[0] + SEG_KVW[1], GLA_HEADS * GLA_DK)
SEG_KG = (SEG_QG[0] + SEG_QG[1], GLA_HEADS * GLA_DK)
SEG_MISC = (SEG_KG[0] + SEG_KG[1], LANE)
IN_PACKED = SEG_MISC[0] + SEG_MISC[1]
MISC_GN = 0
MISC_AG = NSA_HEADS * 3


def _cparams(*sem):
    return pltpu.CompilerParams(dimension_semantics=sem, vmem_limit_bytes=VMEM_LIMIT)


def _bdot(a, b):
    return jnp.dot(a.astype(BF16), b.astype(BF16), preferred_element_type=F32)


def _dot_nt(a, b, precision=None):
    return lax.dot_general(a, b, (((1,), (1,)), ((), ())), preferred_element_type=F32, precision=precision)


def _adaln_kernel(c_ref, w_ref, b_ref, o_ref):
    c = c_ref[...]
    o_ref[...] = _bdot(c * jax.nn.sigmoid(c), w_ref[...]) + b_ref[...]


def _adaln(c, w_ada, b_ada):
    n, d = c.shape
    m = w_ada.shape[1]
    tn = 512
    return pl.pallas_call(
        _adaln_kernel,
        grid=(m // tn,),
        in_specs=[pl.BlockSpec((n, d), lambda j: (0, 0)),
                  pl.BlockSpec((d, tn), lambda j: (0, j)),
                  pl.BlockSpec((1, tn), lambda j: (0, j))],
        out_specs=pl.BlockSpec((n, tn), lambda j: (0, j)),
        out_shape=jax.ShapeDtypeStruct((n, m), F32),
        compiler_params=_cparams("parallel"),
        name="adaln",
    )(c, w_ada, b_ada.reshape(1, m))


def _rope_tables(pos):
    half = ROT_DIM // 2
    inv = jnp.power(ROPE_THETA, -jnp.arange(half, dtype=F32) * 2.0 / ROT_DIM)
    ang = pos.astype(F32)[:, None] * inv[None, :]
    cos, sin = jnp.cos(ang), jnp.sin(ang)
    t = pos.shape[0]
    one = jnp.ones((t, HEAD_DIM - ROT_DIM), F32)
    z8 = jnp.zeros((t, half), F32)
    z48 = jnp.zeros((t, HEAD_DIM - ROT_DIM), F32)
    c = jnp.concatenate([cos, cos, one, cos, cos, one], axis=1)
    s1 = jnp.concatenate([-sin, z8, z48, -sin, z8, z48], axis=1)
    s2 = jnp.concatenate([z8, sin, z48, z8, sin, z48], axis=1)
    return c, s1, s2


def _pack_in_weights(w_in, b_in):
    sizes = (512, 128, 128, 128, 128, 128, 128, 24, 256, 256, 512, 512, 16, 2048)
    offs = np.concatenate([[0], np.cumsum(sizes)])

    def pack(w):
        seg = [w[..., offs[i]:offs[i + 1]] for i in range(len(sizes))]
        q_n, k_c, v_c, k_s, v_s, k_w, v_w, g_n, q_g, k_g, v_g, r_g, a_g, g_m = seg
        zero = jnp.zeros_like(q_n[..., :HEAD_DIM])
        q_slots = []
        for hh in range(NSA_HEADS):
            qh = q_n[..., hh * HEAD_DIM:(hh + 1) * HEAD_DIM] * (HEAD_DIM ** -0.5)
            q_slots += [qh, zero] if hh // NSA_GROUP == 0 else [zero, qh]
        misc_pad = jnp.zeros_like(w[..., :LANE - g_n.shape[-1] - a_g.shape[-1]])
        return jnp.concatenate([g_m] + q_slots + [v_g, r_g, k_c, v_c, k_s, v_s, k_w, v_w,
                                                   q_g * (GLA_DK ** -0.5), k_g, g_n, a_g, misc_pad], axis=-1)

    return pack(w_in).astype(BF16), pack(b_in.reshape(1, -1))


def _inproj_kernel(x_ref, sh_ref, sc_ref, w_ref, b_ref, rc_ref, rs1_ref, rs2_ref,
                   gm_ref, qn_ref, vg_ref, rg_ref, kvc_ref, kvs_ref, kvw_ref, qg_ref, kg_ref, misc_ref):
    h = (x_ref[0] * (1.0 + sc_ref[0]) + sh_ref[0]).astype(BF16)
    rc, rs1, rs2 = rc_ref[...], rs1_ref[...], rs2_ref[...]

    def proj(off, width):
        return jnp.dot(h, w_ref[:, off:off + width], preferred_element_type=F32) + b_ref[:, off:off + width]

    def rope(z):
        return z * rc + pltpu.roll(z, LANE - ROT_DIM // 2, 1) * rs1 + pltpu.roll(z, ROT_DIM // 2, 1) * rs2

    def plain(ref, seg):
        off, width = seg
        step = min(width, 512)
        for c in range(0, width, step):
            ref[0, :, c:c + step] = proj(off + c, step)

    plain(gm_ref, SEG_GM)
    for c in range(0, SEG_QN[1], 512):
        z = proj(SEG_QN[0] + c, 512)
        for s in range(0, 512, LANE):
            qn_ref[0, :, c + s:c + s + LANE] = rope(z[:, s:s + LANE])
    plain(vg_ref, SEG_VG)
    plain(rg_ref, SEG_RG)
    for ref, seg in ((kvc_ref, SEG_KVC), (kvs_ref, SEG_KVS), (kvw_ref, SEG_KVW)):
        z = proj(seg[0], seg[1])
        ref[0, :, 0:LANE] = rope(z[:, 0:LANE])
        ref[0, :, LANE:2 * LANE] = z[:, LANE:2 * LANE]
    plain(qg_ref, SEG_QG)
    plain(kg_ref, SEG_KG)
    plain(misc_ref, SEG_MISC)


def _inproj(x, shift, scale, w_pack, b_pack, tables, tm):
    b, t, d = x.shape
    per_tok = shift.shape[1] != 1
    mod_spec = (pl.BlockSpec((1, tm, d), lambda i, j: (i, j, 0)) if per_tok
                else pl.BlockSpec((1, 1, d), lambda i, j: (i, 0, 0)))
    segs = (SEG_GM, SEG_QN, SEG_VG, SEG_RG, SEG_KVC, SEG_KVS, SEG_KVW, SEG_QG, SEG_KG, SEG_MISC)
    tab_spec = pl.BlockSpec((tm, LANE), lambda i, j: (j, 0))
    return pl.pallas_call(
        _inproj_kernel,
        grid=(b, t // tm),
        in_specs=[pl.BlockSpec((1, tm, d), lambda i, j: (i, j, 0)), mod_spec, mod_spec,
                  pl.BlockSpec((d, IN_PACKED), lambda i, j: (0, 0)),
                  pl.BlockSpec((1, IN_PACKED), lambda i, j: (0, 0)),
                  tab_spec, tab_spec, tab_spec],
        out_specs=[pl.BlockSpec((1, tm, w), lambda i, j: (i, j, 0)) for _, w in segs],
        out_shape=[jax.ShapeDtypeStruct((b, t, w), F32) for _, w in segs],
        compiler_params=_cparams("parallel", "parallel"),
        name="inproj",
    )(x, shift, scale, w_pack, b_pack, *tables)


CHUNKS = 128
PAGE_ROWS = 128
PAGES = 16


def _pack_cmp_weights(pos, w1, w2):
    pos2 = jnp.concatenate([pos, pos], axis=1)
    z1 = jnp.zeros_like(w1)
    bd1 = jnp.concatenate([jnp.concatenate([w1, z1], axis=2), jnp.concatenate([z1, w1], axis=2)], axis=1)
    w1p = jnp.concatenate([bd1[:CMP_STRIDE], bd1[CMP_STRIDE:]], axis=2).astype(BF16)
    w1p = w1p.reshape(CMP_STRIDE // 2, 2 * LANE, 4 * CMP_HIDDEN)
    z2 = jnp.zeros_like(w2)
    w2p = jnp.concatenate([jnp.concatenate([w2, z2], axis=1), jnp.concatenate([z2, w2], axis=1)], axis=0).astype(BF16)
    return pos2, w1p, w2p


CMP_GROUP = 2


def _fill_chunks(page_refs, xs_ref, rows_ref, feature_major):
    for g in range(CMP_GROUP):
        pages = page_refs[g * PAGES:(g + 1) * PAGES]
        if feature_major:
            for p, pr in enumerate(pages):
                rows_ref[p * PAGE_ROWS:(p + 1) * PAGE_ROWS, :] = pr[0].T
            for l in range(CMP_STRIDE):
                xs_ref[l, g * CHUNKS:(g + 1) * CHUNKS, :] = rows_ref[pl.ds(l, CHUNKS, stride=CMP_STRIDE), :]
        else:
            for p, pr in enumerate(pages):
                for l in range(CMP_STRIDE):
                    xs_ref[l, g * CHUNKS + 8 * p:g * CHUNKS + 8 * p + 8, :] = pr[0, pl.ds(l, PAGE_ROWS // CMP_STRIDE, stride=CMP_STRIDE), :]


def _compress_chunks(xs_ref, pos_ref, w1_ref, w2_ref):
    hid2 = 2 * CMP_HIDDEN
    rows = CMP_GROUP * CHUNKS
    acc_a = jnp.zeros((rows, hid2), F32)
    acc_b = jnp.zeros((rows, hid2), F32)
    for lp in range(CMP_STRIDE // 2):
        l0, l1 = 2 * lp, 2 * lp + 1
        x0, x1 = xs_ref[l0], xs_ref[l1]

        def lhs(off):
            return jnp.concatenate([(x0 + pos_ref[off + l0:off + l0 + 1, :]).astype(BF16),
                                    (x1 + pos_ref[off + l1:off + l1 + 1, :]).astype(BF16)], axis=1)

        acc_a = acc_a + jnp.dot(lhs(0), w1_ref[lp, :, 0:hid2], preferred_element_type=F32)
        acc_b = acc_b + jnp.dot(lhs(CMP_STRIDE), w1_ref[lp, :, hid2:2 * hid2], preferred_element_type=F32)
    hid = acc_a + pltpu.roll(acc_b, rows - 1, 0)
    out = jnp.dot(jax.nn.gelu(hid).astype(BF16), w2_ref[...], preferred_element_type=F32)
    row = lax.broadcasted_iota(I32, out.shape, 0) % CHUNKS
    return jnp.where(row < CHUNKS - 1, out, 0.0).reshape(CMP_GROUP, CHUNKS, LANE)


def _compress_kernel(pt_ref, *refs, feature_major):
    n_pages = CMP_GROUP * PAGES
    k_pages, v_pages = refs[:n_pages], refs[n_pages:2 * n_pages]
    posk_ref, w1k_ref, w2k_ref, posv_ref, w1v_ref, w2v_ref, kc_ref, vc_ref, xk_ref, xv_ref, rows_ref = refs[2 * n_pages:]
    _fill_chunks(k_pages, xk_ref, rows_ref, feature_major)
    kc_ref[...] = _compress_chunks(xk_ref, posk_ref, w1k_ref, w2k_ref)
    _fill_chunks(v_pages, xv_ref, rows_ref, feature_major)
    vc_ref[...] = _compress_chunks(xv_ref, posv_ref, w1v_ref, w2v_ref)


def _page_spec(p, half, feature_major):
    if feature_major:
        return pl.BlockSpec((1, LANE, PAGE_ROWS), lambda i, pt: (pt[i * CMP_GROUP * PAGES + p], half, 0))
    return pl.BlockSpec((1, PAGE_ROWS, LANE), lambda i, pt: (pt[i * CMP_GROUP * PAGES + p], 0, half))


def _const_spec(shape):
    nd = len(shape)
    return pl.BlockSpec(shape, lambda i, pt: (0,) * nd)


def _compress(pages, page_ids, cmp_wk, cmp_wv, feature_major=False):
    n_b = page_ids.shape[0] // PAGES
    assert n_b % CMP_GROUP == 0
    n_pages = CMP_GROUP * PAGES
    consts = list(cmp_wk) + list(cmp_wv)
    grid_spec = pltpu.PrefetchScalarGridSpec(
        num_scalar_prefetch=1,
        grid=(n_b // CMP_GROUP,),
        in_specs=[_page_spec(p, h, feature_major) for h in range(2) for p in range(n_pages)] + [_const_spec(c.shape) for c in consts],
        out_specs=[pl.BlockSpec((CMP_GROUP, CHUNKS, LANE), lambda i, pt: (i, 0, 0))] * 2,
        scratch_shapes=[pltpu.VMEM((CMP_STRIDE, CMP_GROUP * CHUNKS, LANE), F32)] * 2 + [pltpu.VMEM((PAGES * PAGE_ROWS, LANE), F32)],
    )
    return pl.pallas_call(
        functools.partial(_compress_kernel, feature_major=feature_major),
        grid_spec=grid_spec,
        out_shape=[jax.ShapeDtypeStruct((n_b, CHUNKS, LANE), F32)] * 2,
        compiler_params=_cparams("parallel"),
        name="compress",
    )(page_ids, *([pages] * (2 * n_pages)), *consts)


def _cover_tables(n_sel):
    c_start = np.arange(CHUNKS) * CMP_STRIDE
    s_start = np.arange(n_sel) * SEL_BLOCK
    cover = ((c_start[:, None] < s_start[None, :] + SEL_BLOCK) & (c_start[:, None] + CMP_LEN > s_start[None, :])).astype(np.float32)
    cover[CHUNKS - 1] = 0.0
    out = np.zeros((NSA_KV_HEADS, LANE, CHUNKS), np.float32)
    for h in range(NSA_KV_HEADS):
        out[h, h * 64:h * 64 + n_sel] = cover.T
    return jnp.asarray(out)


def _softmax_rows(s, valid):
    s = jnp.where(valid, s, NEG)
    m = jnp.max(s, axis=-1, keepdims=True)
    m = jnp.where(m > 0.5 * NEG, m, 0.0)
    p = jnp.where(valid, jnp.exp(s - m), 0.0)
    return p / jnp.maximum(jnp.sum(p, axis=-1, keepdims=True), 1e-30)


def _select_blocks(imp, n_sel, top_n):
    ridx = lax.broadcasted_iota(I32, imp.shape, 0)
    cnt = jnp.zeros(imp.shape, F32)
    for i in range(n_sel):
        vi = imp[i:i + 1, :]
        ahead = (vi > imp) | ((vi == imp) & (ridx > i))
        cnt = cnt + jnp.where(ahead, 1.0, 0.0)
    return jnp.where((cnt < top_n) & (ridx < n_sel), 1.0, 0.0)


def _cmp_attn_kernel(qn_ref, kc_ref, vc_ref, cov_ref, o_ref, sel_ref, *, tq, n_sel):
    qi = pl.program_id(1)
    kc = kc_ref[0].astype(BF16)
    vc = vc_ref[0].astype(BF16)
    qpos = qi * tq + lax.broadcasted_iota(I32, (tq, CHUNKS), 0)
    cidx = lax.broadcasted_iota(I32, (tq, CHUNKS), 1)
    valid = (cidx * CMP_STRIDE + CMP_LEN - 1 <= qpos) & (cidx < CHUNKS - 1)
    psum = [jnp.zeros((tq, CHUNKS), F32) for _ in range(NSA_KV_HEADS)]
    for hh in range(NSA_HEADS):
        q = qn_ref[0, :, hh * LANE:(hh + 1) * LANE].astype(BF16)
        p = _softmax_rows(_dot_nt(q, kc), valid)
        o_ref[0, :, hh * LANE:(hh + 1) * LANE] = jnp.dot(p.astype(BF16), vc, preferred_element_type=F32)
        psum[hh // NSA_GROUP] = psum[hh // NSA_GROUP] + p
    imp = (_dot_nt(cov_ref[0], psum[0], lax.Precision.HIGHEST) + _dot_nt(cov_ref[1], psum[1], lax.Precision.HIGHEST))
    blk = lax.broadcasted_iota(I32, (LANE, tq), 0) & 63
    qpos_t = qi * tq + lax.broadcasted_iota(I32, (LANE, tq), 1)
    cur = qpos_t // SEL_BLOCK
    forced = (blk == 0) | (blk == cur) | (blk == cur - 1)
    imp = jnp.where(forced, FORCE_SCORE, jnp.where(blk * SEL_BLOCK <= qpos_t, imp, -FORCE_SCORE))
    sel_t = jnp.concatenate([_select_blocks(imp[0:64], n_sel, SEL_TOP_N), _select_blocks(imp[64:128], n_sel, SEL_TOP_N)], axis=0)
    sel_ref[0] = sel_t.T


def _cmp_attn(qn, kc, vc, tq):
    b, t, _ = qn.shape
    n_sel = -(-t // SEL_BLOCK)
    cov = _cover_tables(n_sel)
    return pl.pallas_call(
        functools.partial(_cmp_attn_kernel, tq=tq, n_sel=n_sel),
        grid=(b, t // tq),
        in_specs=[pl.BlockSpec((1, tq, NSA_HEADS * LANE), lambda i, j: (i, j, 0)),
                  pl.BlockSpec((1, CHUNKS, LANE), lambda i, j: (i, 0, 0)),
                  pl.BlockSpec((1, CHUNKS, LANE), lambda i, j: (i, 0, 0)),
                  pl.BlockSpec((NSA_KV_HEADS, LANE, CHUNKS), lambda i, j: (0, 0, 0))],
        out_specs=[pl.BlockSpec((1, tq, NSA_HEADS * LANE), lambda i, j: (i, j, 0)),
                   pl.BlockSpec((1, tq, LANE), lambda i, j: (i, j, 0))],
        out_shape=[jax.ShapeDtypeStruct((b, t, NSA_HEADS * LANE), F32), jax.ShapeDtypeStruct((b, t, LANE), F32)],
        compiler_params=_cparams("parallel", "parallel"),
        name="cmp_attn",
    )(qn, kc, vc, cov)


def _key_block_table(t):
    blk = np.arange(t)[:, None] // SEL_BLOCK
    lanes = np.arange(LANE)[None, :] & 63
    return jnp.asarray((blk == lanes).astype(np.float32), dtype=BF16)


def _lane_fold(x, op):
    out = x[:, 0:LANE]
    for c in range(LANE, x.shape[1], LANE):
        out = op(out, x[:, c:c + LANE])
    return out


def _sel_attn_kernel(qn_ref, kv_ref, sel_ref, kb_ref, o_ref, q2_scr, k2_scr, v_scr, s_scr, *, tq, tk):
    qi = pl.program_id(1)
    rows = NSA_HEADS * tq
    t = kv_ref.shape[1]

    @pl.when(qi == 0)
    def _():
        k2_scr[:, 0:LANE] = kv_ref[0, :, 0:LANE].astype(BF16)
        k2_scr[:, LANE:2 * LANE] = kb_ref[...]
        v_scr[...] = kv_ref[0, :, LANE:2 * LANE].astype(BF16)

    not_sel = (1.0 - sel_ref[0]) * NEG
    lane_head = lax.broadcasted_iota(I32, (tq, LANE), 1) // 64
    for hh in range(NSA_HEADS):
        q2_scr[hh * tq:(hh + 1) * tq, 0:LANE] = qn_ref[0, :, hh * LANE:(hh + 1) * LANE].astype(BF16)
        q2_scr[hh * tq:(hh + 1) * tq, LANE:2 * LANE] = jnp.where(lane_head == hh // NSA_GROUP, not_sel, 0.0).astype(BF16)
    q2 = q2_scr[...]
    last = (qi * tq + tq - 1) // tk

    def scores(kt):
        k0 = pl.multiple_of(kt * tk, tk)
        return _dot_nt(q2, k2_scr[pl.ds(k0, tk), :])

    def pass1(kt, m_acc):
        s = scores(kt)
        s_scr[kt] = s
        return jnp.maximum(m_acc, _lane_fold(s, jnp.maximum))

    m_acc = lax.fori_loop(0, last, pass1, jnp.full((rows, LANE), NEG, F32))
    rel = (qi * tq + lax.broadcasted_iota(I32, (tq, tk), 0)) - (last * tk + lax.broadcasted_iota(I32, (tq, tk), 1))
    causal = jnp.where(rel >= 0, 0.0, NEG)
    s_last = scores(last) + jnp.concatenate([causal] * NSA_HEADS, axis=0)
    m = jnp.max(jnp.maximum(m_acc, _lane_fold(s_last, jnp.maximum)), axis=-1, keepdims=True)

    def accumulate(s, v, l_acc, acc):
        p = jnp.exp(s - m)
        return l_acc + _lane_fold(p, jnp.add), acc + jnp.dot(p.astype(BF16), v, preferred_element_type=F32)

    def pass2(kt, carry):
        k0 = pl.multiple_of(kt * tk, tk)
        return accumulate(s_scr[kt], v_scr[pl.ds(k0, tk), :], *carry)

    l_acc, acc = lax.fori_loop(0, last, pass2, (jnp.zeros((rows, LANE), F32), jnp.zeros((rows, LANE), F32)))
    l_acc, acc = accumulate(s_last, v_scr[pl.ds(pl.multiple_of(last * tk, tk), tk), :], l_acc, acc)
    out = acc / jnp.sum(l_acc, axis=-1, keepdims=True)
    for hh in range(NSA_HEADS):
        o_ref[0, :, hh * LANE:(hh + 1) * LANE] = out[hh * tq:(hh + 1) * tq]


def _sel_attn(qn, kv, sel, tq, tk):
    b, t, _ = qn.shape
    rows = NSA_HEADS * tq
    return pl.pallas_call(
        functools.partial(_sel_attn_kernel, tq=tq, tk=tk),
        grid=(b, t // tq),
        in_specs=[pl.BlockSpec((1, tq, NSA_HEADS * LANE), lambda i, j: (i, j, 0)),
                  pl.BlockSpec((1, t, 2 * LANE), lambda i, j: (i, 0, 0)),
                  pl.BlockSpec((1, tq, LANE), lambda i, j: (i, j, 0)),
                  pl.BlockSpec((t, LANE), lambda i, j: (0, 0))],
        out_specs=pl.BlockSpec((1, tq, NSA_HEADS * LANE), lambda i, j: (i, j, 0)),
        out_shape=jax.ShapeDtypeStruct((b, t, NSA_HEADS * LANE), F32),
        scratch_shapes=[pltpu.VMEM((rows, 2 * LANE), BF16), pltpu.VMEM((t, 2 * LANE), BF16), pltpu.VMEM((t, LANE), BF16),
                        pltpu.VMEM((t // tk, rows, tk), F32)],
        compiler_params=_cparams("parallel", "arbitrary"),
        name="sel_attn",
    )(qn, kv, sel, _key_block_table(t))


def _win_attn_kernel(qn_ref, kv_ref, o_ref, q_scr, k_scr, v_scr, *, tq):
    qi = pl.program_id(1)
    span = WINDOW + tq

    @pl.when(qi == 0)
    def _():
        k_scr[...] = kv_ref[0, :, 0:LANE].astype(BF16)
        v_scr[...] = kv_ref[0, :, LANE:2 * LANE].astype(BF16)

    for hh in range(NSA_HEADS):
        q_scr[hh * tq:(hh + 1) * tq, :] = qn_ref[0, :, hh * LANE:(hh + 1) * LANE].astype(BF16)
    k0 = pl.multiple_of(jnp.maximum(qi * tq - WINDOW, 0), tq)
    rel = (qi * tq + lax.broadcasted_iota(I32, (tq, span), 0)) - (k0 + lax.broadcasted_iota(I32, (tq, span), 1))
    bias = jnp.where((rel >= 0) & (rel <= WINDOW), 0.0, NEG)
    s = _dot_nt(q_scr[...], k_scr[pl.ds(k0, span), :]) + jnp.concatenate([bias] * NSA_HEADS, axis=0)
    p = jnp.exp(s - jnp.max(s, axis=-1, keepdims=True))
    out = jnp.dot(p.astype(BF16), v_scr[pl.ds(k0, span), :], preferred_element_type=F32) / jnp.sum(p, axis=-1, keepdims=True)
    for hh in range(NSA_HEADS):
        o_ref[0, :, hh * LANE:(hh + 1) * LANE] = out[hh * tq:(hh + 1) * tq]


def _win_attn(qn, kv, tq):
    b, t, _ = qn.shape
    assert t >= WINDOW + tq
    rows = NSA_HEADS * tq
    return pl.pallas_call(
        functools.partial(_win_attn_kernel, tq=tq),
        grid=(b, t // tq),
        in_specs=[pl.BlockSpec((1, tq, NSA_HEADS * LANE), lambda i, j: (i, j, 0)),
                  pl.BlockSpec((1, t, 2 * LANE), lambda i, j: (i, 0, 0))],
        out_specs=pl.BlockSpec((1, tq, NSA_HEADS * LANE), lambda i, j: (i, j, 0)),
        out_shape=jax.ShapeDtypeStruct((b, t, NSA_HEADS * LANE), F32),
        scratch_shapes=[pltpu.VMEM((rows, LANE), BF16), pltpu.VMEM((t, LANE), BF16), pltpu.VMEM((t, LANE), BF16)],
        compiler_params=_cparams("parallel", "arbitrary"),
        name="win_attn",
    )(qn, kv)


def _dec_softmax(scores, vals):
    m = scores[0].max(axis=-1, keepdims=True)
    for s in scores[1:]:
        m = jnp.maximum(m, s.max(axis=-1, keepdims=True))
    den = jnp.zeros_like(m)
    out = jnp.zeros((m.shape[0], LANE), F32)
    for s, v in zip(scores, vals):
        p = jnp.exp(s - m)
        den = den + p.sum(axis=-1, keepdims=True)
        out = out + (p * v if s.shape[1] == 1 else _dot_nt(p.astype(BF16), v))
    return out / den


DEC_GROUP = 2


def _nsa_decode_kernel(pt_ref, *refs, past_len, n_sel):
    per_seq = refs[DEC_GROUP * PAGES:]
    cov_ref = per_seq[6]
    for bb in range(DEC_GROUP):
        views = [r.at[pl.ds(bb, 1)] for i, r in enumerate(per_seq) if i != 6]
        _nsa_decode_one(refs[bb * PAGES:(bb + 1) * PAGES], *views[:6], cov_ref, *views[6:], past_len=past_len, n_sel=n_sel)


def _nsa_decode_one(pages, qn_ref, kc_ref, vc_ref, kvs_ref, kvw_ref, win_ref, cov_ref,
                    ocmp_ref, osel_ref, owin_ref, wnew_ref, *, past_len, n_sel):
    nh = NSA_HEADS
    q = jnp.concatenate([qn_ref[0, :, hh * LANE:(hh + 1) * LANE] for hh in range(nh)], axis=0)
    qb = q.astype(BF16)
    cidx = lax.broadcasted_iota(I32, (nh, CHUNKS), 1)
    valid = (cidx * CMP_STRIDE + CMP_LEN - 1 <= past_len) & (cidx < CHUNKS - 1)
    p = _softmax_rows(_dot_nt(qb, kc_ref[0].astype(BF16)), valid)
    o_cmp = jnp.dot(p.astype(BF16), vc_ref[0].astype(BF16), preferred_element_type=F32)
    imp = None
    for h in range(NSA_KV_HEADS):
        ps = jnp.sum(p[h * NSA_GROUP:(h + 1) * NSA_GROUP], axis=0, keepdims=True)
        term = _dot_nt(cov_ref[h], jnp.broadcast_to(ps, (LANE, CHUNKS)), lax.Precision.HIGHEST)
        imp = term if imp is None else imp + term
    blk = lax.broadcasted_iota(I32, (LANE, LANE), 0) & 63
    cur = past_len // SEL_BLOCK
    forced = (blk == 0) | (blk == cur) | (blk == cur - 1)
    imp = jnp.where(forced, FORCE_SCORE, jnp.where(blk * SEL_BLOCK <= past_len, imp, -FORCE_SCORE))
    sel_t = jnp.concatenate([_select_blocks(imp[0:64], n_sel, min(SEL_TOP_N, n_sel)),
                             _select_blocks(imp[64:128], n_sel, min(SEL_TOP_N, n_sel))], axis=0)
    sel = sel_t.T[0:1, :]
    head_of_row = lax.broadcasted_iota(I32, (nh, 1), 0) // NSA_GROUP

    def picked(s):
        return jnp.where(head_of_row == 0, sel[:, s:s + 1], sel[:, 64 + s:64 + s + 1])

    first_half = lax.broadcasted_iota(I32, (nh, PAGE_ROWS), 1) < SEL_BLOCK
    scores, vals = [], []
    for pg, pr in enumerate(pages):
        s = jnp.dot(qb, pr[0, 0:LANE, :].astype(BF16), preferred_element_type=F32)
        ok = jnp.where(first_half, picked(2 * pg), picked(2 * pg + 1)) > 0.5
        scores.append(jnp.where(ok, s, NEG))
        vals.append(pr[0, LANE:2 * LANE, :].astype(BF16))
    s_new = jnp.sum(q * kvs_ref[0, :, 0:LANE], axis=-1, keepdims=True)
    scores.append(jnp.where(picked(past_len // SEL_BLOCK) > 0.5, s_new, NEG))
    vals.append(kvs_ref[0, :, LANE:2 * LANE])
    o_sel = _dec_softmax(scores, vals)
    n_win = win_ref.shape[2]
    kpos = past_len - n_win + lax.broadcasted_iota(I32, (nh, n_win), 1)
    rel = past_len - kpos
    s_win = jnp.dot(qb, win_ref[0, 0:LANE, :].astype(BF16), preferred_element_type=F32)
    s_win = jnp.where((kpos >= 0) & (rel >= 0) & (rel <= WINDOW), s_win, NEG)
    s_new = jnp.sum(q * kvw_ref[0, :, 0:LANE], axis=-1, keepdims=True)
    o_win = _dec_softmax([s_win, s_new], [win_ref[0, LANE:2 * LANE, :].astype(BF16), kvw_ref[0, :, LANE:2 * LANE]])
    for hh in range(nh):
        ocmp_ref[0, :, hh * LANE:(hh + 1) * LANE] = o_cmp[hh:hh + 1]
        osel_ref[0, :, hh * LANE:(hh + 1) * LANE] = o_sel[hh:hh + 1]
        owin_ref[0, :, hh * LANE:(hh + 1) * LANE] = o_win[hh:hh + 1]
    for c in range(0, 2 * LANE, LANE):
        new_col = jnp.broadcast_to(kvw_ref[0, :, c:c + LANE], (LANE, LANE)).T
        shifted = pltpu.roll(win_ref[0, c:c + LANE, :], n_win - 1, 1)
        lane = lax.broadcasted_iota(I32, shifted.shape, 1)
        wnew_ref[0, c:c + LANE, :] = jnp.where(lane == n_win - 1, jnp.concatenate([new_col] * (n_win // LANE), axis=1), shifted)


def _nsa_decode(qn, kc, vc, sel_pages, page_ids, kvs_new, kvw_new, win_state):
    b = qn.shape[0]
    n_win = win_state.shape[2]
    past_len = PAGES * PAGE_ROWS
    n_sel = -(-(past_len + 1) // SEL_BLOCK)
    cov = _cover_tables(n_sel)

    assert b % DEC_GROUP == 0

    def per_b(shape):
        nd = len(shape)
        return pl.BlockSpec((DEC_GROUP,) + shape[1:], lambda i, pt: (i,) + (0,) * (nd - 1))

    slots = NSA_HEADS * LANE
    grid_spec = pltpu.PrefetchScalarGridSpec(
        num_scalar_prefetch=1,
        grid=(b // DEC_GROUP,),
        in_specs=[pl.BlockSpec((1, 2 * LANE, PAGE_ROWS), (lambda i, pt, p=p: (pt[i * DEC_GROUP * PAGES + p], 0, 0)))
                  for p in range(DEC_GROUP * PAGES)]
        + [per_b(qn.shape), per_b(kc.shape), per_b(vc.shape), per_b(kvs_new.shape), per_b(kvw_new.shape), per_b(win_state.shape),
           _const_spec(cov.shape)],
        out_specs=[per_b((b, 1, slots))] * 3 + [per_b(win_state.shape)],
    )
    return pl.pallas_call(
        functools.partial(_nsa_decode_kernel, past_len=past_len, n_sel=n_sel),
        grid_spec=grid_spec,
        out_shape=[jax.ShapeDtypeStruct((b, 1, slots), F32)] * 3 + [jax.ShapeDtypeStruct(win_state.shape, F32)],
        compiler_params=_cparams("parallel"),
        name="nsa_decode",
    )(page_ids, *([sel_pages] * (DEC_GROUP * PAGES)), qn, kc, vc, kvs_new, kvw_new, win_state, cov)


def _dot_tn(a, b):
    return lax.dot_general(a, b, (((0,), (0,)), ((), ())), preferred_element_type=F32)


def _cumsum_table():
    r = np.arange(LANE)
    return jnp.asarray(((r[:, None] // GLA_SUB == r[None, :] // GLA_SUB) & (r[None, :] <= r[:, None])).astype(np.float32))


def _gla_kernel(*refs, t, t_valid, has_state):
    if has_state:
        qg_ref, kg_ref, vg_ref, misc_ref, wa_ref, ba_ref, lt_ref, seg_ref, s0_ref, o_ref, s_ref, b_scr, st_scr = refs
    else:
        qg_ref, kg_ref, vg_ref, misc_ref, wa_ref, ba_ref, lt_ref, seg_ref, o_ref, s_ref, b_scr, st_scr = refs
    z = jnp.dot(misc_ref[0], wa_ref[...], preferred_element_type=F32, precision=lax.Precision.HIGHEST) + ba_ref[...]
    la = (jnp.minimum(z, 0.0) - jnp.log1p(jnp.exp(-jnp.abs(z)))) * (1.0 / GLA_TAU)
    if t_valid < t:
        la = jnp.where(lax.broadcasted_iota(I32, la.shape, 0) < t_valid, la, 0.0)
    tile = min(t, LANE)
    for r in range(0, t, tile):
        b_scr[r:r + tile, :] = jnp.dot(lt_ref[0:tile, 0:tile], la[r:r + tile, :], preferred_element_type=F32,
                                       precision=lax.Precision.HIGHEST)
    pairs = GLA_HEADS // 2
    for p in range(pairs):
        if has_state:
            st_scr[p] = s0_ref[0, 2 * p:2 * p + 2].reshape(2 * GLA_DK, GLA_DV).T
        else:
            st_scr[p] = jnp.zeros((GLA_DV, LANE), F32)
    head_a = lax.broadcasted_iota(I32, (GLA_SUB, LANE), 1) < GLA_DK
    row = lax.broadcasted_iota(I32, (GLA_SUB, LANE), 0)

    def pair_chunk(q, k, v, b, st):
        b_last = b[GLA_SUB - 1:GLA_SUB, :]
        st_b = st.astype(BF16)
        qe = q * jnp.exp(b)
        o = jnp.concatenate([_dot_nt(jnp.where(head_a, qe, 0.0).astype(BF16), st_b),
                             _dot_nt(jnp.where(head_a, 0.0, qe).astype(BF16), st_b)], axis=1)
        ws = []
        for j in range(GLA_SUB):
            w = q * k[j:j + 1, :] * jnp.exp(jnp.minimum(b - b[j:j + 1, :], 0.0))
            ws.append(jnp.where(row >= j, w, 0.0))
        a_all = jnp.dot(jnp.concatenate(ws, axis=0).astype(BF16), seg_ref[...], preferred_element_type=F32)
        for j in range(GLA_SUB):
            o = o + a_all[j * GLA_SUB:(j + 1) * GLA_SUB] * v[j:j + 1, :]
        kd = k * jnp.exp(b_last - b)
        upd = (_dot_tn(v[:, 0:GLA_DV].astype(BF16), jnp.where(head_a, kd, 0.0).astype(BF16))
               + _dot_tn(v[:, GLA_DV:2 * GLA_DV].astype(BF16), jnp.where(head_a, 0.0, kd).astype(BF16)))
        return o, jnp.exp(b_last) * st + upd

    def chunk(c, carry):
        r0 = pl.multiple_of(c * GLA_SUB, GLA_SUB)
        for p in range(pairs):
            o, st_new = pair_chunk(qg_ref[0, pl.ds(r0, GLA_SUB), p * LANE:(p + 1) * LANE],
                                          kg_ref[0, pl.ds(r0, GLA_SUB), p * LANE:(p + 1) * LANE],
                                          vg_ref[0, pl.ds(r0, GLA_SUB), 2 * p * GLA_DV:2 * (p + 1) * GLA_DV],
                                          b_scr[pl.ds(r0, GLA_SUB), p * LANE:(p + 1) * LANE], st_scr[p])
            o_ref[0, pl.ds(r0, GLA_SUB), 2 * p * GLA_DV:2 * (p + 1) * GLA_DV] = o
            st_scr[p] = st_new
        return carry

    lax.fori_loop(0, t // GLA_SUB, chunk, 0, unroll=4 if t // GLA_SUB % 4 == 0 else 1)
    for p in range(pairs):
        s_ref[0, 2 * p:2 * p + 2] = st_scr[p].T.reshape(2, GLA_DK, GLA_DV)


def _gla(qg, kg, vg, misc, wa_pad, ba, s0, t_valid):
    b, t, _ = qg.shape
    has_state = s0 is not None
    hk = GLA_HEADS * GLA_DK

    def per_b(shape):
        nd = len(shape)
        return pl.BlockSpec((1,) + shape[1:], lambda i: (i,) + (0,) * (nd - 1))

    def const(shape):
        nd = len(shape)
        return pl.BlockSpec(shape, lambda i: (0,) * nd)

    lanes = np.arange(LANE)[:, None] < GLA_DK
    seg = jnp.asarray((lanes == (np.arange(2 * GLA_DV)[None, :] < GLA_DV)).astype(np.float32), dtype=BF16)
    in_specs = [per_b(qg.shape), per_b(kg.shape), per_b(vg.shape), per_b(misc.shape),
                const((LANE, hk)), const((1, hk)), const((LANE, LANE)), const((LANE, 2 * GLA_DV))]
    args = [qg, kg, vg, misc, wa_pad, ba.reshape(1, -1), _cumsum_table(), seg]
    state_shape = (b, GLA_HEADS, GLA_DK, GLA_DV)
    if has_state:
        in_specs.append(per_b(state_shape))
        args.append(s0)
    return pl.pallas_call(
        functools.partial(_gla_kernel, t=t, t_valid=t_valid, has_state=has_state),
        grid=(b,),
        in_specs=in_specs,
        out_specs=[per_b(vg.shape), per_b(state_shape)],
        out_shape=[jax.ShapeDtypeStruct(vg.shape, F32), jax.ShapeDtypeStruct(state_shape, F32)],
        scratch_shapes=[pltpu.VMEM((t, hk), F32), pltpu.VMEM((GLA_HEADS // 2, GLA_DV, LANE), F32)],
        compiler_params=_cparams("parallel"),
        name="gla",
    )(*args)


def _gate_expand_table():
    out = np.zeros((3, LANE, NSA_HEADS * LANE), np.float32)
    for hh in range(NSA_HEADS):
        for j in range(3):
            out[j, MISC_GN + 3 * hh + j, hh * LANE:(hh + 1) * LANE] = 1.0
    return jnp.asarray(np.concatenate([out, out], axis=1), dtype=BF16)


def _pad_br_a(w_br_a):
    zero = jnp.zeros((HEAD_DIM, w_br_a.shape[1]), w_br_a.dtype)
    parts = []
    for hh in range(NSA_HEADS):
        wh = w_br_a[hh * HEAD_DIM:(hh + 1) * HEAD_DIM]
        parts += [wh, zero] if hh // NSA_GROUP == 0 else [zero, wh]
    return jnp.concatenate(parts, axis=0).astype(BF16)


def _layer_norm(v, g, b):
    mu = jnp.mean(v, axis=-1, keepdims=True)
    var = jnp.mean(jnp.square(v - mu), axis=-1, keepdims=True)
    return (v - mu) * lax.rsqrt(var + LN_EPS) * g + b


def _mixer_tail_kernel(ocmp_ref, osel_ref, owin_ref, misc_ref, ogla_ref, rg_ref, gm_ref, x_ref, gate_ref, scf_ref, shf_ref,
                       ex_ref, ng_ref, wa_ref, wb_ref, wo_ref, lg_ref, lb_ref, x1_ref, xm_ref):
    sig = jax.nn.sigmoid(misc_ref[0])
    sig_hi = sig.astype(BF16)
    sig_lo = (sig - sig_hi.astype(F32)).astype(BF16)
    sig_split = jnp.concatenate([sig_hi, sig_lo], axis=1)
    o_nsa = None
    for j, ref in enumerate((ocmp_ref, osel_ref, owin_ref)):
        g = jnp.dot(sig_split, ex_ref[j], preferred_element_type=F32)
        o_nsa = g * ref[0] if o_nsa is None else o_nsa + g * ref[0]
    br_a = _bdot(o_nsa, wa_ref[...])
    heads = []
    for h in range(GLA_HEADS):
        seg = ogla_ref[0, :, h * GLA_DV:(h + 1) * GLA_DV]
        mu = jnp.mean(seg, axis=-1, keepdims=True)
        var = jnp.mean(jnp.square(seg - mu), axis=-1, keepdims=True)
        r = rg_ref[0, :, h * GLA_DV:(h + 1) * GLA_DV]
        heads.append((seg - mu) * lax.rsqrt(var + LN_EPS) * ng_ref[:, h * GLA_DV:(h + 1) * GLA_DV] * (r * jax.nn.sigmoid(r)))
    br_b = _bdot(jnp.concatenate(heads, axis=1), wb_ref[...])
    gm_a = jax.nn.sigmoid(gm_ref[0, :, 0:D_MODEL])
    gm_b = jax.nn.sigmoid(gm_ref[0, :, D_MODEL:2 * D_MODEL])
    y = _bdot(gm_a * br_a + gm_b * br_b, wo_ref[...])
    x1 = _layer_norm(DN_ALPHA * x_ref[0] + gate_ref[0] * y, lg_ref[...], lb_ref[...])
    x1_ref[0] = x1
    xm_ref[0] = x1 * (1.0 + scf_ref[0]) + shf_ref[0]


def _mixer_tail(ocmp, osel, owin, misc, ogla, rg, gm, x, gate_m, scale_f, shift_f, consts, tm):
    b, t, d = x.shape
    per_tok = gate_m.shape[1] != 1

    def tok(w):
        return pl.BlockSpec((1, tm, w), lambda i, j: (i, j, 0))

    mod_spec = tok(d) if per_tok else pl.BlockSpec((1, 1, d), lambda i, j: (i, 0, 0))

    def const(a):
        nd = a.ndim
        return pl.BlockSpec(a.shape, lambda i, j: (0,) * nd)

    return pl.pallas_call(
        _mixer_tail_kernel,
        grid=(b, t // tm),
        in_specs=[tok(NSA_HEADS * LANE)] * 3 + [tok(LANE), tok(GLA_HEADS * GLA_DV), tok(GLA_HEADS * GLA_DV), tok(2 * d), tok(d),
                                               mod_spec, mod_spec, mod_spec] + [const(c) for c in consts],
        out_specs=[tok(d), tok(d)],
        out_shape=[jax.ShapeDtypeStruct((b, t, d), F32)] * 2,
        compiler_params=_cparams("parallel", "parallel"),
        name="mixer_tail",
    )(ocmp, osel, owin, misc, ogla, rg, gm, x, gate_m, scale_f, shift_f, *consts)


ROUTE_TILE = LANE


def _first_index(hit, iota, size, axis):
    return jnp.min(jnp.where(hit, iota, size), axis=axis, keepdims=True)


def _part_specs(parts, tm):
    n_a = parts[0].shape[0] // tm
    d = parts[0].shape[1]
    return [pl.BlockSpec((tm, d), lambda i, *_: (jnp.minimum(i, n_a - 1), 0)),
            pl.BlockSpec((tm, d), lambda i, *_: (jnp.maximum(i - n_a, 0), 0))], n_a


def _part_tile(i, n_a, a_ref, b_ref):
    return jnp.where(i < n_a, a_ref[...], b_ref[...])


def _router_kernel(xa_ref, xb_ref, wr_ref, bias_ref, tri_ref, eidx_ref, rank_ref, wrow_ref, cnt_ref, carry_ref, *, n_a):
    i = pl.program_id(0)
    tm = ROUTE_TILE
    per = N_EXPERTS // N_GROUPS

    @pl.when(i == 0)
    def _():
        carry_ref[...] = jnp.zeros_like(carry_ref)

    logits = _dot_nt(wr_ref[...], _part_tile(i, n_a, xa_ref, xb_ref), lax.Precision.HIGHEST)
    s = jax.nn.sigmoid(logits)
    sb = s + bias_ref[...]
    sb3 = sb.reshape(N_GROUPS, per, tm)
    in_grp = lax.broadcasted_iota(I32, sb3.shape, 1)
    m1 = jnp.max(sb3, axis=1, keepdims=True)
    first = _first_index(sb3 == m1, in_grp, per, 1)
    m2 = jnp.max(jnp.where(in_grp == first, NEG, sb3), axis=1, keepdims=True)
    gs = (m1 + m2).reshape(N_GROUPS, tm)
    g_iota = lax.broadcasted_iota(I32, gs.shape, 0)
    g_keep = jnp.zeros(gs.shape, jnp.bool_)
    for _ in range(TOPK_GROUPS):
        pick = g_iota == _first_index(gs == jnp.max(gs, axis=0, keepdims=True), g_iota, N_GROUPS, 0)
        g_keep = g_keep | pick
        gs = jnp.where(pick, NEG, gs)
    sbm = jnp.where(g_keep.reshape(N_GROUPS, 1, tm), sb3, NEG).reshape(N_EXPERTS, tm)
    e_iota = lax.broadcasted_iota(I32, sbm.shape, 0)
    idxs, sels = [], []
    onehot = jnp.zeros(sbm.shape, F32)
    for _ in range(TOP_K):
        idx = _first_index(sbm == jnp.max(sbm, axis=0, keepdims=True), e_iota, N_EXPERTS, 0)
        pick = e_iota == idx
        idxs.append(idx)
        sels.append(jnp.sum(jnp.where(pick, s, 0.0), axis=0, keepdims=True))
        sbm = jnp.where(pick, NEG, sbm)
        onehot = onehot + jnp.where(pick, 1.0, 0.0)
    sel = jnp.concatenate(sels, axis=0)
    wts = sel / jnp.sum(sel, axis=0, keepdims=True) * ROUTED_SCALE
    carry = carry_ref[...]
    before = carry + jnp.dot(onehot.astype(BF16), tri_ref[...], preferred_element_type=F32)
    ranks = [jnp.sum(jnp.where(e_iota == idx, before, 0.0), axis=0, keepdims=True) for idx in idxs]
    eidx_ref[...] = jnp.concatenate(idxs, axis=0)
    rank_ref[...] = jnp.concatenate(ranks, axis=0).astype(I32)
    wrow_ref[...] = jnp.concatenate([wts, jnp.zeros((LANE - TOP_K, tm), F32)], axis=0).T
    carry = carry + jnp.sum(onehot, axis=1, keepdims=True)
    carry_ref[...] = carry
    cnt_ref[...] = carry


def _router(xm_parts, w_router, router_bias):
    n = xm_parts[0].shape[0] + xm_parts[1].shape[0]
    d = xm_parts[0].shape[1]
    tm = ROUTE_TILE
    r = np.arange(tm)
    tri = jnp.asarray((r[:, None] < r[None, :]).astype(np.float32), dtype=BF16)
    x_specs, n_a = _part_specs(xm_parts, tm)
    return pl.pallas_call(
        functools.partial(_router_kernel, n_a=n_a),
        grid=(n // tm,),
        in_specs=x_specs + [
                  pl.BlockSpec((N_EXPERTS, d), lambda i: (0, 0)),
                  pl.BlockSpec((N_EXPERTS, 1), lambda i: (0, 0)),
                  pl.BlockSpec((tm, tm), lambda i: (0, 0))],
        out_specs=[pl.BlockSpec((TOP_K, tm), lambda i: (0, i)),
                   pl.BlockSpec((TOP_K, tm), lambda i: (0, i)),
                   pl.BlockSpec((tm, LANE), lambda i: (i, 0)),
                   pl.BlockSpec((N_EXPERTS, LANE), lambda i: (0, 0))],
        out_shape=[jax.ShapeDtypeStruct((TOP_K, n), I32), jax.ShapeDtypeStruct((TOP_K, n), I32),
                   jax.ShapeDtypeStruct((n, LANE), F32), jax.ShapeDtypeStruct((N_EXPERTS, LANE), F32)],
        scratch_shapes=[pltpu.VMEM((N_EXPERTS, LANE), F32)],
        compiler_params=_cparams("arbitrary"),
        name="router",
    )(*xm_parts, w_router.T, router_bias.reshape(N_EXPERTS, 1), tri)


def _dest_kernel(eidx_ref, rank_ref, start_ref, dest_ref):
    tm = ROUTE_TILE
    e_iota = lax.broadcasted_iota(I32, (N_EXPERTS, tm), 0)
    start = start_ref[...]
    for j in range(dest_ref.shape[0]):
        cols = slice(j * tm, (j + 1) * tm)
        rows = [jnp.sum(jnp.where(e_iota == eidx_ref[k:k + 1, cols], start, 0.0), axis=0, keepdims=True) for k in range(TOP_K)]
        dest_ref[j] = jnp.concatenate(rows, axis=0).astype(I32) + rank_ref[:, cols]


def _dest(eidx, rank, pad_start):
    n = eidx.shape[1]
    tm = ROUTE_TILE
    per_step = 3 if (n // tm) % 3 == 0 else 1
    return pl.pallas_call(
        _dest_kernel,
        grid=(n // tm // per_step,),
        in_specs=[pl.BlockSpec((TOP_K, per_step * tm), lambda i: (0, i)),
                  pl.BlockSpec((TOP_K, per_step * tm), lambda i: (0, i)),
                  pl.BlockSpec((N_EXPERTS, 1), lambda i: (0, 0))],
        out_specs=pl.BlockSpec((per_step, TOP_K, tm), lambda i: (i, 0, 0)),
        out_shape=jax.ShapeDtypeStruct((n // tm, TOP_K, tm), I32),
        compiler_params=_cparams("parallel"),
        name="dest",
    )(eidx, rank, pad_start.astype(F32).reshape(N_EXPERTS, 1))


TOKEN_SUB = D_MODEL // LANE


def _to_token_tiles(x, ref):
    m = x.shape[0]
    for s in range(TOKEN_SUB):
        ref[pl.ds(s, m, stride=TOKEN_SUB), :] = x[:, s * LANE:(s + 1) * LANE]


def _from_token_tiles(ref, m, s):
    return ref[pl.ds(s, m, stride=TOKEN_SUB), :]


def _row_copy(src_ref, src_row, dst_ref, dst_row, sem):
    return pltpu.make_async_copy(src_ref.at[pl.ds(src_row * TOKEN_SUB, TOKEN_SUB)],
                                 dst_ref.at[pl.ds(dst_row * TOKEN_SUB, TOKEN_SUB)], sem)


def _dispatch_kernel(dest_ref, pad_ref, nb_ref, xa_ref, xb_ref, xs_ref, tile_ref, zero_ref, sems, sem, *, n_a):
    i = pl.program_id(0)
    tm = ROUTE_TILE
    bm = EXPERT_BLOCK
    last = pl.num_programs(0) - 1
    src = tile_ref.at[i % 2]
    _to_token_tiles(_part_tile(i, n_a, xa_ref, xb_ref), src)

    def start_row(r, c):
        for k in range(TOP_K):
            _row_copy(src, r, xs_ref, dest_ref[0, k, r], sems.at[i % 2]).start(priority=k % 2)
        return c

    def wait_step(slot):
        for _ in range(TOP_K):
            pltpu.make_async_copy(tile_ref.at[0], xs_ref.at[pl.ds(0, tm * TOKEN_SUB)], sems.at[slot]).wait()

    lax.fori_loop(0, tm, start_row, 0)

    @pl.when(i >= 1)
    def _():
        wait_step((i - 1) % 2)

    @pl.when(i == last)
    def _():
        wait_step(i % 2)
        zero_ref[...] = jnp.zeros_like(zero_ref)

        def per_expert(e, c):
            lo, hi = pad_ref[0, e], pad_ref[1, e]
            lax.fori_loop(lo, hi, lambda r, cc: (_row_copy(zero_ref, 0, xs_ref, r, sem).start(), cc)[1], 0)
            lax.fori_loop(lo, hi, lambda r, cc: (_row_copy(zero_ref, 0, xs_ref, r, sem).wait(), cc)[1], 0)
            return c

        lax.fori_loop(0, N_EXPERTS, per_expert, 0)

        def tail_copy(blk):
            return pltpu.make_async_copy(zero_ref, xs_ref.at[pl.ds(blk * bm * TOKEN_SUB, bm * TOKEN_SUB)], sem)

        n_blocks = xs_ref.shape[0] // (bm * TOKEN_SUB)
        lax.fori_loop(nb_ref[0], n_blocks, lambda blk, c: (tail_copy(blk).start(), c)[1], 0)
        lax.fori_loop(nb_ref[0], n_blocks, lambda blk, c: (tail_copy(blk).wait(), c)[1], 0)


def _dispatch(xm_parts, dest_tiles, pad_range, n_used, n_rows):
    tm = ROUTE_TILE
    x_specs, n_a = _part_specs(xm_parts, tm)
    return pl.pallas_call(
        functools.partial(_dispatch_kernel, n_a=n_a),
        grid=(dest_tiles.shape[0],),
        in_specs=[pl.BlockSpec((1, TOP_K, tm), lambda i: (i, 0, 0), memory_space=pltpu.SMEM),
                  pl.BlockSpec(memory_space=pltpu.SMEM),
                  pl.BlockSpec(memory_space=pltpu.SMEM)] + x_specs,
        out_specs=pl.BlockSpec(memory_space=pl.ANY),
        out_shape=jax.ShapeDtypeStruct((n_rows * TOKEN_SUB, LANE), F32),
        scratch_shapes=[pltpu.VMEM((2, tm * TOKEN_SUB, LANE), F32), pltpu.VMEM((EXPERT_BLOCK * TOKEN_SUB, LANE), F32),
                        pltpu.SemaphoreType.DMA((2,)), pltpu.SemaphoreType.DMA(())],
        compiler_params=_cparams("arbitrary"),
        name="dispatch",
    )(dest_tiles, pad_range, n_used, *xm_parts)


def _experts_kernel(first_ref, cnt_ref, nb_ref, xs_ref, wg_ref, wu_ref, wd_ref, ys_ref,
                    xbuf, ybuf, wg_s, wu_s, wd_s, in_sems, out_sems):
    e = pl.program_id(0)
    bm = EXPERT_BLOCK
    brows = bm * TOKEN_SUB
    total = nb_ref[0]
    n_blocks = xs_ref.shape[0] // brows

    def x_copy(g, slot):
        return pltpu.make_async_copy(xs_ref.at[pl.ds(g * brows, brows)], xbuf.at[slot], in_sems.at[slot])

    def y_copy(g, slot):
        return pltpu.make_async_copy(ybuf.at[slot], ys_ref.at[pl.ds(g * brows, brows)], out_sems.at[slot])

    depth = xbuf.shape[0]

    @pl.when(e == 0)
    def _():
        for g0 in range(depth - 1):
            @pl.when(g0 < total)
            def _():
                x_copy(g0, g0).start()

    @pl.when(cnt_ref[e] > 0)
    def _():
        wg_s[...] = wg_ref[0].astype(BF16)
        wu_s[...] = wu_ref[0].astype(BF16)
        wd_s[...] = wd_ref[0].astype(BF16)

    def block(c, carry):
        g = first_ref[e] + c
        slot = g % depth
        x_copy(g, slot).wait()

        @pl.when(g + depth - 1 < total)
        def _():
            x_copy(g + depth - 1, (g + depth - 1) % depth).start()

        x = jnp.concatenate([_from_token_tiles(xbuf.at[slot], bm, s) for s in range(TOKEN_SUB)], axis=1).astype(BF16)
        gate = jnp.dot(x, wg_s[...], preferred_element_type=F32)
        up = jnp.dot(x, wu_s[...], preferred_element_type=F32)
        y = jnp.dot((gate * jax.nn.sigmoid(gate) * up).astype(BF16), wd_s[...], preferred_element_type=F32)

        @pl.when(g >= depth)
        def _():
            y_copy(g - depth, slot).wait()

        _to_token_tiles(y, ybuf.at[slot])
        y_copy(g, slot).start()
        return carry

    lax.fori_loop(0, cnt_ref[e], block, 0)

    @pl.when(e == pl.num_programs(0) - 1)
    def _():
        for back in range(depth, 0, -1):
            @pl.when(total >= back)
            def _():
                y_copy(total - back, (total - back) % depth).wait()

        ybuf[0] = jnp.zeros(ybuf.shape[1:], F32)
        lax.fori_loop(total, n_blocks, lambda g, c: (y_copy(g, 0).start(), c)[1], 0)
        lax.fori_loop(total, n_blocks, lambda g, c: (y_copy(g, 0).wait(), c)[1], 0)


def _experts(xs, first_block, n_block, n_used, w_gate, w_up, w_down):
    brows = EXPERT_BLOCK * TOKEN_SUB
    n_exp, d, f = w_gate.shape
    grid_spec = pltpu.PrefetchScalarGridSpec(
        num_scalar_prefetch=3,
        grid=(n_exp,),
        in_specs=[pl.BlockSpec(memory_space=pl.ANY),
                  pl.BlockSpec((1, d, f), lambda e, *_: (e, 0, 0)),
                  pl.BlockSpec((1, d, f), lambda e, *_: (e, 0, 0)),
                  pl.BlockSpec((1, f, d), lambda e, *_: (e, 0, 0))],
        out_specs=pl.BlockSpec(memory_space=pl.ANY),
        scratch_shapes=[pltpu.VMEM((EXPERT_RING, brows, LANE), F32), pltpu.VMEM((EXPERT_RING, brows, LANE), F32),
                        pltpu.VMEM((d, f), BF16), pltpu.VMEM((d, f), BF16), pltpu.VMEM((f, d), BF16),
                        pltpu.SemaphoreType.DMA((EXPERT_RING,)), pltpu.SemaphoreType.DMA((EXPERT_RING,))],
    )
    return pl.pallas_call(
        _experts_kernel,
        grid_spec=grid_spec,
        out_shape=jax.ShapeDtypeStruct(xs.shape, F32),
        compiler_params=_cparams("arbitrary"),
        name="experts",
    )(first_block, n_block, n_used, xs, w_gate, w_up, w_down)


def _combine_kernel(dest_ref, dnext_ref, ys_ref, wrow_ref, xa_ref, xb_ref, x1a_ref, x1b_ref, ga_ref, gb_ref,
                    sg_ref, su_ref, sd_ref, lg_ref, lb_ref, ya_ref, yb_ref, buf_ref, sems, *, n_a):
    i = pl.program_id(0)
    tm = ROUTE_TILE
    slot = i % 2

    def issue(d_ref, s):
        def start_row(r, c):
            for k in range(TOP_K):
                _row_copy(ys_ref, d_ref[0, k, r], buf_ref.at[s, k], r, sems.at[s]).start(priority=k % 2)
            return c

        lax.fori_loop(0, tm, start_row, 0)

    @pl.when(i == 0)
    def _():
        issue(dest_ref, 0)

    @pl.when(i < pl.num_programs(0) - 1)
    def _():
        issue(dnext_ref, 1 - slot)

    x = _part_tile(i, n_a, xa_ref, xb_ref).astype(BF16)
    g = jnp.dot(x, sg_ref[...], preferred_element_type=F32)
    u = jnp.dot(x, su_ref[...], preferred_element_type=F32)
    shared = jnp.dot((g * jax.nn.sigmoid(g) * u).astype(BF16), sd_ref[...], preferred_element_type=F32)
    for k in range(TOP_K):
        pltpu.make_async_copy(ys_ref.at[pl.ds(0, tm * TOKEN_SUB)], buf_ref.at[slot, k], sems.at[slot]).wait()
    w = wrow_ref[...]
    cols = []
    for s in range(TOKEN_SUB):
        routed = w[:, 0:1] * _from_token_tiles(buf_ref.at[slot, 0], tm, s)
        for k in range(1, TOP_K):
            routed = routed + w[:, k:k + 1] * _from_token_tiles(buf_ref.at[slot, k], tm, s)
        cols.append(routed + shared[:, s * LANE:(s + 1) * LANE])
    moe = jnp.concatenate(cols, axis=1)
    gate = jnp.where(i < n_a, ga_ref[0], gb_ref[...])
    y = _layer_norm(DN_ALPHA * _part_tile(i, n_a, x1a_ref, x1b_ref) + gate * moe, lg_ref[...], lb_ref[...])

    @pl.when(i < n_a)
    def _():
        ya_ref[...] = y

    @pl.when(i >= n_a)
    def _():
        yb_ref[...] = y


def _combine(ys, dest_tiles, wrow, xm_parts, x1_parts, gate_a, gate_b, ws_gate, ws_up, ws_down, ln_g, ln_b):
    n = xm_parts[0].shape[0] + xm_parts[1].shape[0]
    d = xm_parts[0].shape[1]
    tm = ROUTE_TILE
    f = ws_gate.shape[1]
    x_specs, n_a = _part_specs(xm_parts, tm)
    tiles_per_seq = n_a // gate_a.shape[0]
    gate_specs = [pl.BlockSpec((1, 1, d), lambda i: (jnp.minimum(i, n_a - 1) // tiles_per_seq, 0, 0)), x_specs[1]]
    vec = pl.BlockSpec((1, d), lambda i: (0, 0))
    return pl.pallas_call(
        functools.partial(_combine_kernel, n_a=n_a),
        grid=(n // tm,),
        in_specs=[pl.BlockSpec((1, TOP_K, tm), lambda i: (i, 0, 0), memory_space=pltpu.SMEM),
                  pl.BlockSpec((1, TOP_K, tm), lambda i: (jnp.minimum(i + 1, n // tm - 1), 0, 0), memory_space=pltpu.SMEM),
                  pl.BlockSpec(memory_space=pl.ANY),
                  pl.BlockSpec((tm, LANE), lambda i: (i, 0))] + x_specs + x_specs + gate_specs + [
                  pl.BlockSpec((d, f), lambda i: (0, 0)),
                  pl.BlockSpec((d, f), lambda i: (0, 0)),
                  pl.BlockSpec((f, d), lambda i: (0, 0)), vec, vec],
        out_specs=x_specs,
        out_shape=[jax.ShapeDtypeStruct(p.shape, F32) for p in xm_parts],
        scratch_shapes=[pltpu.VMEM((2, TOP_K, tm * TOKEN_SUB, LANE), F32), pltpu.SemaphoreType.DMA((2,))],
        compiler_params=_cparams("arbitrary"),
        name="combine",
    )(dest_tiles, dest_tiles, ys, wrow, *xm_parts, *x1_parts, gate_a, gate_b,
      ws_gate.astype(BF16), ws_up.astype(BF16), ws_down.astype(BF16), ln_g.reshape(1, d), ln_b.reshape(1, d))


def _moe(xm, x1, gate_a, gate_b, ln_g, ln_b, w_router, router_bias, w_e_gate, w_e_up, w_e_down, w_s_gate, w_s_up, w_s_down):
    n = xm[0].shape[0] + xm[1].shape[0]
    eidx, rank, wrow, cnt = _router(xm, w_router, router_bias)
    counts = cnt[:, 0].astype(I32)
    padded = (counts + EXPERT_BLOCK - 1) // EXPERT_BLOCK * EXPERT_BLOCK
    pad_end = jnp.cumsum(padded)
    pad_start = pad_end - padded
    dest_tiles = _dest(eidx, rank, pad_start)
    n_blocks = -(-(n * TOP_K) // EXPERT_BLOCK) + N_EXPERTS
    n_used = (pad_end[-1:] // EXPERT_BLOCK).astype(I32)
    pad_range = jnp.stack([pad_start + counts, pad_end]).astype(I32)
    xs = _dispatch(xm, dest_tiles, pad_range, n_used, n_blocks * EXPERT_BLOCK)
    ys = _experts(xs, (pad_start // EXPERT_BLOCK).astype(I32), (padded // EXPERT_BLOCK).astype(I32), n_used,
                  w_e_gate, w_e_up, w_e_down)
    return _combine(ys, dest_tiles, wrow, xm, x1, gate_a, gate_b, w_s_gate, w_s_up, w_s_down, ln_g, ln_b)


def kernel(x_prompt, x_sample, cache_kv_cmp, cache_kv_sel, state_kv_win, state_gla, page_table, c_prompt, c_sample, w_in, b_in, cmp_k_pos, cmp_k_w1, cmp_k_w2, cmp_v_pos, cmp_v_w1, cmp_v_w2, gla_w_a2, gla_b_a, gla_norm_g, w_br_a, w_br_b, w_out, ln1_g, ln1_b, w_ada, b_ada, w_router, router_bias, w_e_gate, w_e_up, w_e_down, w_s_gate, w_s_up, w_s_down, ln2_g, ln2_b):
    bp, tp, d = x_prompt.shape
    nd, td = x_sample.shape[:2]
    n_pool, page_rows = cache_kv_cmp.shape[:2]
    past_len = page_table.shape[1] * page_rows
    assert td == 1 and d == D_MODEL and page_rows == PAGE_ROWS and page_table.shape[1] == PAGES and tp == PAGES * PAGE_ROWS
    kv_w = 2 * NSA_KV_HEADS * HEAD_DIM

    mod = _adaln(jnp.concatenate([c_prompt, c_sample], axis=0), w_ada, b_ada)
    mod_p = [m.reshape(bp, 1, d) for m in jnp.split(mod[:bp], 6, axis=-1)]
    mod_s = [m.reshape(1, nd, d) for m in jnp.split(mod[bp:], 6, axis=-1)]

    w_pack, b_pack = _pack_in_weights(w_in, b_in)
    cmp_wk = _pack_cmp_weights(cmp_k_pos, cmp_k_w1, cmp_k_w2)
    cmp_wv = _pack_cmp_weights(cmp_v_pos, cmp_v_w1, cmp_v_w2)
    wa_pad = jnp.zeros((LANE, GLA_HEADS * GLA_DK), F32).at[MISC_AG:MISC_AG + GLA_GATE_RANK].set(gla_w_a2)
    tail_consts = (_gate_expand_table(), gla_norm_g.reshape(1, -1), _pad_br_a(w_br_a), w_br_b.astype(BF16), w_out.astype(BF16),
                   ln1_g.reshape(1, d), ln1_b.reshape(1, d))

    gm, qn, vg, rg, kvc, kvs, kvw, qg, kg, misc = _inproj(
        x_prompt, mod_p[0], mod_p[1], w_pack, b_pack, _rope_tables(jnp.arange(tp, dtype=I32)), 256)
    kc, vc = _compress(kvc.reshape(bp * PAGES, PAGE_ROWS, kv_w), jnp.arange(bp * PAGES, dtype=I32), cmp_wk, cmp_wv)
    ocmp, sel = _cmp_attn(qn, kc, vc, 256)
    osel = _sel_attn(qn, kvs, sel, 128, 256)
    owin = _win_attn(qn, kvw, 128)
    ogla, gla_p = _gla(qg, kg, vg, misc, wa_pad, gla_b_a, None, tp)
    x1_p, xm_p = _mixer_tail(ocmp, osel, owin, misc, ogla, rg, gm, x_prompt, mod_p[2], mod_p[4], mod_p[3], tail_consts, 256)
    n_win = min(WINDOW, tp)
    outs_p = (kvc.reshape(bp, tp, 2, NSA_KV_HEADS, HEAD_DIM), kvs.reshape(bp, tp, 2, NSA_KV_HEADS, HEAD_DIM),
              kvw[:, tp - n_win:].reshape(bp, n_win, 2, NSA_KV_HEADS, HEAD_DIM), gla_p)

    gm, qn, vg, rg, kvc, kvs, kvw, qg, kg, misc = _inproj(
        x_sample.reshape(1, nd, d), mod_s[0], mod_s[1], w_pack, b_pack, _rope_tables(jnp.full((nd,), past_len, I32)), nd)
    page_ids = page_table.reshape(-1).astype(I32)
    kc, vc = _compress(cache_kv_cmp.reshape(n_pool, PAGE_ROWS, kv_w).transpose(0, 2, 1), page_ids, cmp_wk, cmp_wv,
                       feature_major=True)
    ocmp, osel, owin, win_new = _nsa_decode(
        qn.reshape(nd, 1, -1), kc, vc, cache_kv_sel.reshape(n_pool, PAGE_ROWS, kv_w).transpose(0, 2, 1), page_ids,
        kvs.reshape(nd, 1, kv_w), kvw.reshape(nd, 1, kv_w), state_kv_win.reshape(nd, -1, kv_w).transpose(0, 2, 1))
    win_new = win_new.transpose(0, 2, 1)

    def pad_rows(a):
        return jnp.pad(a.reshape(nd, 1, -1), ((0, 0), (0, GLA_SUB - 1), (0, 0)))

    ogla, gla_s = _gla(pad_rows(qg), pad_rows(kg), pad_rows(vg), pad_rows(misc), wa_pad, gla_b_a, state_gla, 1)
    x1_s, xm_s = _mixer_tail(ocmp.reshape(1, nd, -1), osel.reshape(1, nd, -1), owin.reshape(1, nd, -1), misc,
                             ogla[:, 0].reshape(1, nd, -1), rg, gm, x_sample.reshape(1, nd, d),
                             mod_s[2], mod_s[4], mod_s[3], tail_consts, nd)
    outs_s = (kvc.reshape(nd, 1, 2, NSA_KV_HEADS, HEAD_DIM), kvs.reshape(nd, 1, 2, NSA_KV_HEADS, HEAD_DIM),
              win_new.reshape(state_kv_win.shape), gla_s)

    n_p = bp * tp
    y_p, y_s = _moe((xm_p.reshape(n_p, d), xm_s.reshape(nd, d)), (x1_p.reshape(n_p, d), x1_s.reshape(nd, d)),
                    mod_p[5], mod_s[5].reshape(nd, d), ln2_g, ln2_b,
                    w_router, router_bias, w_e_gate, w_e_up, w_e_down, w_s_gate, w_s_up, w_s_down)
    return (y_p.reshape(bp, tp, d), y_s.reshape(nd, 1, d)) + outs_p + outs_s
```

```python
import functools

import numpy as np
import jax
import jax.numpy as jnp
from jax import lax
from jax.experimental import pallas as pl
from jax.experimental.pallas import tpu as pltpu

F32 = jnp.float32
BF16 = jnp.bfloat16
I32 = jnp.int32

D_MODEL = 1024
NSA_HEADS = 8
NSA_KV_HEADS = 2
NSA_GROUP = NSA_HEADS // NSA_KV_HEADS
HEAD_DIM = 64
ROT_DIM = HEAD_DIM // 4
ROPE_THETA = 500000.0
CMP_LEN = 32
CMP_STRIDE = 16
CMP_HIDDEN = 256
SEL_BLOCK = 64
SEL_TOP_N = 16
WINDOW = 512
FORCE_SCORE = 1.0e4
GLA_HEADS = 4
GLA_DK = 64
GLA_DV = 128
GLA_GATE_RANK = 16
GLA_TAU = 16.0
GLA_SUB = 16
N_EXPERTS = 256
TOP_K = 8
N_GROUPS = 8
TOPK_GROUPS = 4
EXPERT_DIM = 256
SHARED_DIM = 256
ROUTED_SCALE = 2.5
EXPERT_BLOCK = 128
EXPERT_RING = 4
DN_ALPHA = 2.0 ** 0.25
LN_EPS = 1e-5
LANE = 128
NEG = -1.0e30
VMEM_LIMIT = 56 * 1024 * 1024

SEG_GM = (0, 2 * D_MODEL)
SEG_QN = (SEG_GM[0] + SEG_GM[1], NSA_HEADS * LANE)
SEG_VG = (SEG_QN[0] + SEG_QN[1], GLA_HEADS * GLA_DV)
SEG_RG = (SEG_VG[0] + SEG_VG[1], GLA_HEADS * GLA_DV)
SEG_KVC = (SEG_RG[0] + SEG_RG[1], 2 * LANE)
SEG_KVS = (SEG_KVC[0] + SEG_KVC[1], 2 * LANE)
SEG_KVW = (SEG_KVS[0] + SEG_KVS[1], 2 * LANE)
SEG_QG = (SEG_KVW[0] + SEG_KVW[1], GLA_HEADS * GLA_DK)
SEG_KG = (SEG_QG[0] + SEG_QG[1], GLA_HEADS * GLA_DK)
SEG_MISC = (SEG_KG[0] + SEG_KG[1], LANE)
IN_PACKED = SEG_MISC[0] + SEG_MISC[1]
MISC_GN = 0
MISC_AG = NSA_HEADS * 3


def _cparams(*sem):
    return pltpu.CompilerParams(dimension_semantics=sem, vmem_limit_bytes=VMEM_LIMIT)


def _bdot(a, b):
    return jnp.dot(a.astype(BF16), b.astype(BF16), preferred_element_type=F32)


def _dot_nt(a, b, precision=None):
    return lax.dot_general(a, b, (((1,), (1,)), ((), ())), preferred_element_type=F32, precision=precision)


def _adaln_kernel(c_ref, w_ref, b_ref, o_ref):
    c = c_ref[...]
    o_ref[...] = _bdot(c * jax.nn.sigmoid(c), w_ref[...]) + b_ref[...]


def _adaln(c, w_ada, b_ada):
    n, d = c.shape
    m = w_ada.shape[1]
    tn = 512
    return pl.pallas_call(
        _adaln_kernel,
        grid=(m // tn,),
        in_specs=[pl.BlockSpec((n, d), lambda j: (0, 0)),
                  pl.BlockSpec((d, tn), lambda j: (0, j)),
                  pl.BlockSpec((1, tn), lambda j: (0, j))],
        out_specs=pl.BlockSpec((n, tn), lambda j: (0, j)),
        out_shape=jax.ShapeDtypeStruct((n, m), F32),
        compiler_params=_cparams("parallel"),
        name="adaln",
    )(c, w_ada, b_ada.reshape(1, m))


def _rope_tables(pos):
    half = ROT_DIM // 2
    inv = jnp.power(ROPE_THETA, -jnp.arange(half, dtype=F32) * 2.0 / ROT_DIM)
    ang = pos.astype(F32)[:, None] * inv[None, :]
    cos, sin = jnp.cos(ang), jnp.sin(ang)
    t = pos.shape[0]
    one = jnp.ones((t, HEAD_DIM - ROT_DIM), F32)
    z8 = jnp.zeros((t, half), F32)
    z48 = jnp.zeros((t, HEAD_DIM - ROT_DIM), F32)
    c = jnp.concatenate([cos, cos, one, cos, cos, one], axis=1)
    s1 = jnp.concatenate([-sin, z8, z48, -sin, z8, z48], axis=1)
    s2 = jnp.concatenate([z8, sin, z48, z8, sin, z48], axis=1)
    return c, s1, s2


def _pack_in_weights(w_in, b_in):
    sizes = (512, 128, 128, 128, 128, 128, 128, 24, 256, 256, 512, 512, 16, 2048)
    offs = np.concatenate([[0], np.cumsum(sizes)])

    def pack(w):
        seg = [w[..., offs[i]:offs[i + 1]] for i in range(len(sizes))]
        q_n, k_c, v_c, k_s, v_s, k_w, v_w, g_n, q_g, k_g, v_g, r_g, a_g, g_m = seg
        zero = jnp.zeros_like(q_n[..., :HEAD_DIM])
        q_slots = []
        for hh in range(NSA_HEADS):
            qh = q_n[..., hh * HEAD_DIM:(hh + 1) * HEAD_DIM] * (HEAD_DIM ** -0.5)
            q_slots += [qh, zero] if hh // NSA_GROUP == 0 else [zero, qh]
        misc_pad = jnp.zeros_like(w[..., :LANE - g_n.shape[-1] - a_g.shape[-1]])
        return jnp.concatenate([g_m] + q_slots + [v_g, r_g, k_c, v_c, k_s, v_s, k_w, v_w,
                                                   q_g * (GLA_DK ** -0.5), k_g, g_n, a_g, misc_pad], axis=-1)

    return pack(w_in).astype(BF16), pack(b_in.reshape(1, -1))


def _inproj_kernel(x_ref, sh_ref, sc_ref, w_ref, b_ref, rc_ref, rs1_ref, rs2_ref,
                   gm_ref, qn_ref, vg_ref, rg_ref, kvc_ref, kvs_ref, kvw_ref, qg_ref, kg_ref, misc_ref):
    h = (x_ref[0] * (1.0 + sc_ref[0]) + sh_ref[0]).astype(BF16)
    rc, rs1, rs2 = rc_ref[...], rs1_ref[...], rs2_ref[...]

    def proj(off, width):
        return jnp.dot(h, w_ref[:, off:off + width], preferred_element_type=F32) + b_ref[:, off:off + width]

    def rope(z):
        return z * rc + pltpu.roll(z, LANE - ROT_DIM // 2, 1) * rs1 + pltpu.roll(z, ROT_DIM // 2, 1) * rs2

    def plain(ref, seg):
        off, width = seg
        step = min(width, 512)
        for c in range(0, width, step):
            ref[0, :, c:c + step] = proj(off + c, step)

    plain(gm_ref, SEG_GM)
    for c in range(0, SEG_QN[1], 512):
        z = proj(SEG_QN[0] + c, 512)
        for s in range(0, 512, LANE):
            qn_ref[0, :, c + s:c + s + LANE] = rope(z[:, s:s + LANE])
    plain(vg_ref, SEG_VG)
    plain(rg_ref, SEG_RG)
    for ref, seg in ((kvc_ref, SEG_KVC), (kvs_ref, SEG_KVS), (kvw_ref, SEG_KVW)):
        z = proj(seg[0], seg[1])
        ref[0, :, 0:LANE] = rope(z[:, 0:LANE])
        ref[0, :, LANE:2 * LANE] = z[:, LANE:2 * LANE]
    plain(qg_ref, SEG_QG)
    plain(kg_ref, SEG_KG)
    plain(misc_ref, SEG_MISC)


def _inproj(x, shift, scale, w_pack, b_pack, tables, tm):
    b, t, d = x.shape
    per_tok = shift.shape[1] != 1
    mod_spec = (pl.BlockSpec((1, tm, d), lambda i, j: (i, j, 0)) if per_tok
                else pl.BlockSpec((1, 1, d), lambda i, j: (i, 0, 0)))
    segs = (SEG_GM, SEG_QN, SEG_VG, SEG_RG, SEG_KVC, SEG_KVS, SEG_KVW, SEG_QG, SEG_KG, SEG_MISC)
    tab_spec = pl.BlockSpec((tm, LANE), lambda i, j: (j, 0))
    return pl.pallas_call(
        _inproj_kernel,
        grid=(b, t // tm),
        in_specs=[pl.BlockSpec((1, tm, d), lambda i, j: (i, j, 0)), mod_spec, mod_spec,
                  pl.BlockSpec((d, IN_PACKED), lambda i, j: (0, 0)),
                  pl.BlockSpec((1, IN_PACKED), lambda i, j: (0, 0)),
                  tab_spec, tab_spec, tab_spec],
        out_specs=[pl.BlockSpec((1, tm, w), lambda i, j: (i, j, 0)) for _, w in segs],
        out_shape=[jax.ShapeDtypeStruct((b, t, w), F32) for _, w in segs],
        compiler_params=_cparams("parallel", "parallel"),
        name="inproj",
    )(x, shift, scale, w_pack, b_pack, *tables)


CHUNKS = 128
PAGE_ROWS = 128
PAGES = 16


def _pack_cmp_weights(pos, w1, w2):
    pos2 = jnp.concatenate([pos, pos], axis=1)
    z1 = jnp.zeros_like(w1)
    bd1 = jnp.concatenate([jnp.concatenate([w1, z1], axis=2), jnp.concatenate([z1, w1], axis=2)], axis=1)
    w1p = jnp.concatenate([bd1[:CMP_STRIDE], bd1[CMP_STRIDE:]], axis=2).astype(BF16)
    w1p = w1p.reshape(CMP_STRIDE // 2, 2 * LANE, 4 * CMP_HIDDEN)
    z2 = jnp.zeros_like(w2)
    w2p = jnp.concatenate([jnp.concatenate([w2, z2], axis=1), jnp.concatenate([z2, w2], axis=1)], axis=0).astype(BF16)
    return pos2, w1p, w2p


CMP_GROUP = 2


def _fill_chunks(page_refs, xs_ref, rows_ref, feature_major):
    for g in range(CMP_GROUP):
        pages = page_refs[g * PAGES:(g + 1) * PAGES]
        if feature_major:
            for p, pr in enumerate(pages):
                rows_ref[p * PAGE_ROWS:(p + 1) * PAGE_ROWS, :] = pr[0].T
            for l in range(CMP_STRIDE):
                xs_ref[l, g * CHUNKS:(g + 1) * CHUNKS, :] = rows_ref[pl.ds(l, CHUNKS, stride=CMP_STRIDE), :]
        else:
            for p, pr in enumerate(pages):
                for l in range(CMP_STRIDE):
                    xs_ref[l, g * CHUNKS + 8 * p:g * CHUNKS + 8 * p + 8, :] = pr[0, pl.ds(l, PAGE_ROWS // CMP_STRIDE, stride=CMP_STRIDE), :]


def _compress_chunks(xs_ref, pos_ref, w1_ref, w2_ref):
    hid2 = 2 * CMP_HIDDEN
    rows = CMP_GROUP * CHUNKS
    acc_a = jnp.zeros((rows, hid2), F32)
    acc_b = jnp.zeros((rows, hid2), F32)
    for lp in range(CMP_STRIDE // 2):
        l0, l1 = 2 * lp, 2 * lp + 1
        x0, x1 = xs_ref[l0], xs_ref[l1]

        def lhs(off):
            return jnp.concatenate([(x0 + pos_ref[off + l0:off + l0 + 1, :]).astype(BF16),
                                    (x1 + pos_ref[off + l1:off + l1 + 1, :]).astype(BF16)], axis=1)

        acc_a = acc_a + jnp.dot(lhs(0), w1_ref[lp, :, 0:hid2], preferred_element_type=F32)
        acc_b = acc_b + jnp.dot(lhs(CMP_STRIDE), w1_ref[lp, :, hid2:2 * hid2], preferred_element_type=F32)
    hid = acc_a + pltpu.roll(acc_b, rows - 1, 0)
    out = jnp.dot(jax.nn.gelu(hid).astype(BF16), w2_ref[...], preferred_element_type=F32)
    row = lax.broadcasted_iota(I32, out.shape, 0) % CHUNKS
    return jnp.where(row < CHUNKS - 1, out, 0.0).reshape(CMP_GROUP, CHUNKS, LANE)


def _compress_kernel(pt_ref, *refs, feature_major):
    n_pages = CMP_GROUP * PAGES
    k_pages, v_pages = refs[:n_pages], refs[n_pages:2 * n_pages]
    posk_ref, w1k_ref, w2k_ref, posv_ref, w1v_ref, w2v_ref, kc_ref, vc_ref, xk_ref, xv_ref, rows_ref = refs[2 * n_pages:]
    _fill_chunks(k_pages, xk_ref, rows_ref, feature_major)
    kc_ref[...] = _compress_chunks(xk_ref, posk_ref, w1k_ref, w2k_ref)
    _fill_chunks(v_pages, xv_ref, rows_ref, feature_major)
    vc_ref[...] = _compress_chunks(xv_ref, posv_ref, w1v_ref, w2v_ref)


def _page_spec(p, half, feature_major):
    if feature_major:
        return pl.BlockSpec((1, LANE, PAGE_ROWS), lambda i, pt: (pt[i * CMP_GROUP * PAGES + p], half, 0))
    return pl.BlockSpec((1, PAGE_ROWS, LANE), lambda i, pt: (pt[i * CMP_GROUP * PAGES + p], 0, half))


def _const_spec(shape):
    nd = len(shape)
    return pl.BlockSpec(shape, lambda i, pt: (0,) * nd)


def _compress(pages, page_ids, cmp_wk, cmp_wv, feature_major=False):
    n_b = page_ids.shape[0] // PAGES
    assert n_b % CMP_GROUP == 0
    n_pages = CMP_GROUP * PAGES
    consts = list(cmp_wk) + list(cmp_wv)
    grid_spec = pltpu.PrefetchScalarGridSpec(
        num_scalar_prefetch=1,
        grid=(n_b // CMP_GROUP,),
        in_specs=[_page_spec(p, h, feature_major) for h in range(2) for p in range(n_pages)] + [_const_spec(c.shape) for c in consts],
        out_specs=[pl.BlockSpec((CMP_GROUP, CHUNKS, LANE), lambda i, pt: (i, 0, 0))] * 2,
        scratch_shapes=[pltpu.VMEM((CMP_STRIDE, CMP_GROUP * CHUNKS, LANE), F32)] * 2 + [pltpu.VMEM((PAGES * PAGE_ROWS, LANE), F32)],
    )
    return pl.pallas_call(
        functools.partial(_compress_kernel, feature_major=feature_major),
        grid_spec=grid_spec,
        out_shape=[jax.ShapeDtypeStruct((n_b, CHUNKS, LANE), F32)] * 2,
        compiler_params=_cparams("parallel"),
        name="compress",
    )(page_ids, *([pages] * (2 * n_pages)), *consts)


def _cover_tables(n_sel):
    c_start = np.arange(CHUNKS) * CMP_STRIDE
    s_start = np.arange(n_sel) * SEL_BLOCK
    cover = ((c_start[:, None] < s_start[None, :] + SEL_BLOCK) & (c_start[:, None] + CMP_LEN > s_start[None, :])).astype(np.float32)
    cover[CHUNKS - 1] = 0.0
    out = np.zeros((NSA_KV_HEADS, LANE, CHUNKS), np.float32)
    for h in range(NSA_KV_HEADS):
        out[h, h * 64:h * 64 + n_sel] = cover.T
    return jnp.asarray(out)


def _softmax_rows(s, valid):
    s = jnp.where(valid, s, NEG)
    m = jnp.max(s, axis=-1, keepdims=True)
    m = jnp.where(m > 0.5 * NEG, m, 0.0)
    p = jnp.where(valid, jnp.exp(s - m), 0.0)
    return p / jnp.maximum(jnp.sum(p, axis=-1, keepdims=True), 1e-30)


def _select_blocks(imp, n_sel, top_n):
    ridx = lax.broadcasted_iota(I32, imp.shape, 0)
    cnt = jnp.zeros(imp.shape, F32)
    for i in range(n_sel):
        vi = imp[i:i + 1, :]
        ahead = (vi > imp) | ((vi == imp) & (ridx > i))
        cnt = cnt + jnp.where(ahead, 1.0, 0.0)
    return jnp.where((cnt < top_n) & (ridx < n_sel), 1.0, 0.0)


def _cmp_attn_kernel(qn_ref, kc_ref, vc_ref, cov_ref, o_ref, sel_ref, *, tq, n_sel):
    qi = pl.program_id(1)
    kc = kc_ref[0].astype(BF16)
    vc = vc_ref[0].astype(BF16)
    qpos = qi * tq + lax.broadcasted_iota(I32, (tq, CHUNKS), 0)
    cidx = lax.broadcasted_iota(I32, (tq, CHUNKS), 1)
    valid = (cidx * CMP_STRIDE + CMP_LEN - 1 <= qpos) & (cidx < CHUNKS - 1)
    psum = [jnp.zeros((tq, CHUNKS), F32) for _ in range(NSA_KV_HEADS)]
    for hh in range(NSA_HEADS):
        q = qn_ref[0, :, hh * LANE:(hh + 1) * LANE].astype(BF16)
        p = _softmax_rows(_dot_nt(q, kc), valid)
        o_ref[0, :, hh * LANE:(hh + 1) * LANE] = jnp.dot(p.astype(BF16), vc, preferred_element_type=F32)
        psum[hh // NSA_GROUP] = psum[hh // NSA_GROUP] + p
    imp = (_dot_nt(cov_ref[0], psum[0], lax.Precision.HIGHEST) + _dot_nt(cov_ref[1], psum[1], lax.Precision.HIGHEST))
    blk = lax.broadcasted_iota(I32, (LANE, tq), 0) & 63
    qpos_t = qi * tq + lax.broadcasted_iota(I32, (LANE, tq), 1)
    cur = qpos_t // SEL_BLOCK
    forced = (blk == 0) | (blk == cur) | (blk == cur - 1)
    imp = jnp.where(forced, FORCE_SCORE, jnp.where(blk * SEL_BLOCK <= qpos_t, imp, -FORCE_SCORE))
    sel_t = jnp.concatenate([_select_blocks(imp[0:64], n_sel, SEL_TOP_N), _select_blocks(imp[64:128], n_sel, SEL_TOP_N)], axis=0)
    sel_ref[0] = sel_t.T


def _cmp_attn(qn, kc, vc, tq):
    b, t, _ = qn.shape
    n_sel = -(-t // SEL_BLOCK)
    cov = _cover_tables(n_sel)
    return pl.pallas_call(
        functools.partial(_cmp_attn_kernel, tq=tq, n_sel=n_sel),
        grid=(b, t // tq),
        in_specs=[pl.BlockSpec((1, tq, NSA_HEADS * LANE), lambda i, j: (i, j, 0)),
                  pl.BlockSpec((1, CHUNKS, LANE), lambda i, j: (i, 0, 0)),
                  pl.BlockSpec((1, CHUNKS, LANE), lambda i, j: (i, 0, 0)),
                  pl.BlockSpec((NSA_KV_HEADS, LANE, CHUNKS), lambda i, j: (0, 0, 0))],
        out_specs=[pl.BlockSpec((1, tq, NSA_HEADS * LANE), lambda i, j: (i, j, 0)),
                   pl.BlockSpec((1, tq, LANE), lambda i, j: (i, j, 0))],
        out_shape=[jax.ShapeDtypeStruct((b, t, NSA_HEADS * LANE), F32), jax.ShapeDtypeStruct((b, t, LANE), F32)],
        compiler_params=_cparams("parallel", "parallel"),
        name="cmp_attn",
    )(qn, kc, vc, cov)


def _key_block_table(t):
    blk = np.arange(t)[:, None] // SEL_BLOCK
    lanes = np.arange(LANE)[None, :] & 63
    return jnp.asarray((blk == lanes).astype(np.float32), dtype=BF16)


def _lane_fold(x, op):
    out = x[:, 0:LANE]
    for c in range(LANE, x.shape[1], LANE):
        out = op(out, x[:, c:c + LANE])
    return out


def _sel_attn_kernel(qn_ref, kv_ref, sel_ref, kb_ref, o_ref, q2_scr, k2_scr, v_scr, s_scr, *, tq, tk):
    qi = pl.program_id(1)
    rows = NSA_HEADS * tq
    t = kv_ref.shape[1]

    @pl.when(qi == 0)
    def _():
        k2_scr[:, 0:LANE] = kv_ref[0, :, 0:LANE].astype(BF16)
        k2_scr[:, LANE:2 * LANE] = kb_ref[...]
        v_scr[...] = kv_ref[0, :, LANE:2 * LANE].astype(BF16)

    not_sel = (1.0 - sel_ref[0]) * NEG
    lane_head = lax.broadcasted_iota(I32, (tq, LANE), 1) // 64
    for hh in range(NSA_HEADS):
        q2_scr[hh * tq:(hh + 1) * tq, 0:LANE] = qn_ref[0, :, hh * LANE:(hh + 1) * LANE].astype(BF16)
        q2_scr[hh * tq:(hh + 1) * tq, LANE:2 * LANE] = jnp.where(lane_head == hh // NSA_GROUP, not_sel, 0.0).astype(BF16)
    q2 = q2_scr[...]
    last = (qi * tq + tq - 1) // tk

    def scores(kt):
        k0 = pl.multiple_of(kt * tk, tk)
        return _dot_nt(q2, k2_scr[pl.ds(k0, tk), :])

    def pass1(kt, m_acc):
        s = scores(kt)
        s_scr[kt] = s
        return jnp.maximum(m_acc, _lane_fold(s, jnp.maximum))

    m_acc = lax.fori_loop(0, last, pass1, jnp.full((rows, LANE), NEG, F32))
    rel = (qi * tq + lax.broadcasted_iota(I32, (tq, tk), 0)) - (last * tk + lax.broadcasted_iota(I32, (tq, tk), 1))
    causal = jnp.where(rel >= 0, 0.0, NEG)
    s_last = scores(last) + jnp.concatenate([causal] * NSA_HEADS, axis=0)
    m = jnp.max(jnp.maximum(m_acc, _lane_fold(s_last, jnp.maximum)), axis=-1, keepdims=True)

    def accumulate(s, v, l_acc, acc):
        p = jnp.exp(s - m)
        return l_acc + _lane_fold(p, jnp.add), acc + jnp.dot(p.astype(BF16), v, preferred_element_type=F32)

    def pass2(kt, carry):
        k0 = pl.multiple_of(kt * tk, tk)
        return accumulate(s_scr[kt], v_scr[pl.ds(k0, tk), :], *carry)

    l_acc, acc = lax.fori_loop(0, last, pass2, (jnp.zeros((rows, LANE), F32), jnp.zeros((rows, LANE), F32)))
    l_acc, acc = accumulate(s_last, v_scr[pl.ds(pl.multiple_of(last * tk, tk), tk), :], l_acc, acc)
    out = acc / jnp.sum(l_acc, axis=-1, keepdims=True)
    for hh in range(NSA_HEADS):
        o_ref[0, :, hh * LANE:(hh + 1) * LANE] = out[hh * tq:(hh + 1) * tq]


def _sel_attn(qn, kv, sel, tq, tk):
    b, t, _ = qn.shape
    rows = NSA_HEADS * tq
    return pl.pallas_call(
        functools.partial(_sel_attn_kernel, tq=tq, tk=tk),
        grid=(b, t // tq),
        in_specs=[pl.BlockSpec((1, tq, NSA_HEADS * LANE), lambda i, j: (i, j, 0)),
                  pl.BlockSpec((1, t, 2 * LANE), lambda i, j: (i, 0, 0)),
                  pl.BlockSpec((1, tq, LANE), lambda i, j: (i, j, 0)),
                  pl.BlockSpec((t, LANE), lambda i, j: (0, 0))],
        out_specs=pl.BlockSpec((1, tq, NSA_HEADS * LANE), lambda i, j: (i, j, 0)),
        out_shape=jax.ShapeDtypeStruct((b, t, NSA_HEADS * LANE), F32),
        scratch_shapes=[pltpu.VMEM((rows, 2 * LANE), BF16), pltpu.VMEM((t, 2 * LANE), BF16), pltpu.VMEM((t, LANE), BF16),
                        pltpu.VMEM((t // tk, rows, tk), F32)],
        compiler_params=_cparams("parallel", "arbitrary"),
        name="sel_attn",
    )(qn, kv, sel, _key_block_table(t))


def _win_attn_kernel(qn_ref, kv_ref, o_ref, q_scr, k_scr, v_scr, *, tq):
    qi = pl.program_id(1)
    span = WINDOW + tq

    @pl.when(qi == 0)
    def _():
        k_scr[...] = kv_ref[0, :, 0:LANE].astype(BF16)
        v_scr[...] = kv_ref[0, :, LANE:2 * LANE].astype(BF16)

    for hh in range(NSA_HEADS):
        q_scr[hh * tq:(hh + 1) * tq, :] = qn_ref[0, :, hh * LANE:(hh + 1) * LANE].astype(BF16)
    k0 = pl.multiple_of(jnp.maximum(qi * tq - WINDOW, 0), tq)
    rel = (qi * tq + lax.broadcasted_iota(I32, (tq, span), 0)) - (k0 + lax.broadcasted_iota(I32, (tq, span), 1))
    bias = jnp.where((rel >= 0) & (rel <= WINDOW), 0.0, NEG)
    s = _dot_nt(q_scr[...], k_scr[pl.ds(k0, span), :]) + jnp.concatenate([bias] * NSA_HEADS, axis=0)
    p = jnp.exp(s - jnp.max(s, axis=-1, keepdims=True))
    out = jnp.dot(p.astype(BF16), v_scr[pl.ds(k0, span), :], preferred_element_type=F32) / jnp.sum(p, axis=-1, keepdims=True)
    for hh in range(NSA_HEADS):
        o_ref[0, :, hh * LANE:(hh + 1) * LANE] = out[hh * tq:(hh + 1) * tq]


def _win_attn(qn, kv, tq):
    b, t, _ = qn.shape
    assert t >= WINDOW + tq
    rows = NSA_HEADS * tq
    return pl.pallas_call(
        functools.partial(_win_attn_kernel, tq=tq),
        grid=(b, t // tq),
        in_specs=[pl.BlockSpec((1, tq, NSA_HEADS * LANE), lambda i, j: (i, j, 0)),
                  pl.BlockSpec((1, t, 2 * LANE), lambda i, j: (i, 0, 0))],
        out_specs=pl.BlockSpec((1, tq, NSA_HEADS * LANE), lambda i, j: (i, j, 0)),
        out_shape=jax.ShapeDtypeStruct((b, t, NSA_HEADS * LANE), F32),
        scratch_shapes=[pltpu.VMEM((rows, LANE), BF16), pltpu.VMEM((t, LANE), BF16), pltpu.VMEM((t, LANE), BF16)],
        compiler_params=_cparams("parallel", "arbitrary"),
        name="win_attn",
    )(qn, kv)


def _dec_softmax(scores, vals):
    m = scores[0].max(axis=-1, keepdims=True)
    for s in scores[1:]:
        m = jnp.maximum(m, s.max(axis=-1, keepdims=True))
    den = jnp.zeros_like(m)
    out = jnp.zeros((m.shape[0], LANE), F32)
    for s, v in zip(scores, vals):
        p = jnp.exp(s - m)
        den = den + p.sum(axis=-1, keepdims=True)
        out = out + (p * v if s.shape[1] == 1 else _dot_nt(p.astype(BF16), v))
    return out / den


DEC_GROUP = 2


def _nsa_decode_kernel(pt_ref, *refs, past_len, n_sel):
    per_seq = refs[DEC_GROUP * PAGES:]
    cov_ref = per_seq[6]
    for bb in range(DEC_GROUP):
        views = [r.at[pl.ds(bb, 1)] for i, r in enumerate(per_seq) if i != 6]
        _nsa_decode_one(refs[bb * PAGES:(bb + 1) * PAGES], *views[:6], cov_ref, *views[6:], past_len=past_len, n_sel=n_sel)


def _nsa_decode_one(pages, qn_ref, kc_ref, vc_ref, kvs_ref, kvw_ref, win_ref, cov_ref,
                    ocmp_ref, osel_ref, owin_ref, wnew_ref, *, past_len, n_sel):
    nh = NSA_HEADS
    q = jnp.concatenate([qn_ref[0, :, hh * LANE:(hh + 1) * LANE] for hh in range(nh)], axis=0)
    qb = q.astype(BF16)
    cidx = lax.broadcasted_iota(I32, (nh, CHUNKS), 1)
    valid = (cidx * CMP_STRIDE + CMP_LEN - 1 <= past_len) & (cidx < CHUNKS - 1)
    p = _softmax_rows(_dot_nt(qb, kc_ref[0].astype(BF16)), valid)
    o_cmp = jnp.dot(p.astype(BF16), vc_ref[0].astype(BF16), preferred_element_type=F32)
    imp = None
    for h in range(NSA_KV_HEADS):
        ps = jnp.sum(p[h * NSA_GROUP:(h + 1) * NSA_GROUP], axis=0, keepdims=True)
        term = _dot_nt(cov_ref[h], jnp.broadcast_to(ps, (LANE, CHUNKS)), lax.Precision.HIGHEST)
        imp = term if imp is None else imp + term
    blk = lax.broadcasted_iota(I32, (LANE, LANE), 0) & 63
    cur = past_len // SEL_BLOCK
    forced = (blk == 0) | (blk == cur) | (blk == cur - 1)
    imp = jnp.where(forced, FORCE_SCORE, jnp.where(blk * SEL_BLOCK <= past_len, imp, -FORCE_SCORE))
    sel_t = jnp.concatenate([_select_blocks(imp[0:64], n_sel, min(SEL_TOP_N, n_sel)),
                             _select_blocks(imp[64:128], n_sel, min(SEL_TOP_N, n_sel))], axis=0)
    sel = sel_t.T[0:1, :]
    head_of_row = lax.broadcasted_iota(I32, (nh, 1), 0) // NSA_GROUP

    def picked(s):
        return jnp.where(head_of_row == 0, sel[:, s:s + 1], sel[:, 64 + s:64 + s + 1])

    first_half = lax.broadcasted_iota(I32, (nh, PAGE_ROWS), 1) < SEL_BLOCK
    scores, vals = [], []
    for pg, pr in enumerate(pages):
        s = jnp.dot(qb, pr[0, 0:LANE, :].astype(BF16), preferred_element_type=F32)
        ok = jnp.where(first_half, picked(2 * pg), picked(2 * pg + 1)) > 0.5
        scores.append(jnp.where(ok, s, NEG))
        vals.append(pr[0, LANE:2 * LANE, :].astype(BF16))
    s_new = jnp.sum(q * kvs_ref[0, :, 0:LANE], axis=-1, keepdims=True)
    scores.append(jnp.where(picked(past_len // SEL_BLOCK) > 0.5, s_new, NEG))
    vals.append(kvs_ref[0, :, LANE:2 * LANE])
    o_sel = _dec_softmax(scores, vals)
    n_win = win_ref.shape[2]
    kpos = past_len - n_win + lax.broadcasted_iota(I32, (nh, n_win), 1)
    rel = past_len - kpos
    s_win = jnp.dot(qb, win_ref[0, 0:LANE, :].astype(BF16), preferred_element_type=F32)
    s_win = jnp.where((kpos >= 0) & (rel >= 0) & (rel <= WINDOW), s_win, NEG)
    s_new = jnp.sum(q * kvw_ref[0, :, 0:LANE], axis=-1, keepdims=True)
    o_win = _dec_softmax([s_win, s_new], [win_ref[0, LANE:2 * LANE, :].astype(BF16), kvw_ref[0, :, LANE:2 * LANE]])
    for hh in range(nh):
        ocmp_ref[0, :, hh * LANE:(hh + 1) * LANE] = o_cmp[hh:hh + 1]
        osel_ref[0, :, hh * LANE:(hh + 1) * LANE] = o_sel[hh:hh + 1]
        owin_ref[0, :, hh * LANE:(hh + 1) * LANE] = o_win[hh:hh + 1]
    for c in range(0, 2 * LANE, LANE):
        new_col = jnp.broadcast_to(kvw_ref[0, :, c:c + LANE], (LANE, LANE)).T
        shifted = pltpu.roll(win_ref[0, c:c + LANE, :], n_win - 1, 1)
        lane = lax.broadcasted_iota(I32, shifted.shape, 1)
        wnew_ref[0, c:c + LANE, :] = jnp.where(lane == n_win - 1, jnp.concatenate([new_col] * (n_win // LANE), axis=1), shifted)


def _nsa_decode(qn, kc, vc, sel_pages, page_ids, kvs_new, kvw_new, win_state):
    b = qn.shape[0]
    n_win = win_state.shape[2]
    past_len = PAGES * PAGE_ROWS
    n_sel = -(-(past_len + 1) // SEL_BLOCK)
    cov = _cover_tables(n_sel)

    assert b % DEC_GROUP == 0

    def per_b(shape):
        nd = len(shape)
        return pl.BlockSpec((DEC_GROUP,) + shape[1:], lambda i, pt: (i,) + (0,) * (nd - 1))

    slots = NSA_HEADS * LANE
    grid_spec = pltpu.PrefetchScalarGridSpec(
        num_scalar_prefetch=1,
        grid=(b // DEC_GROUP,),
        in_specs=[pl.BlockSpec((1, 2 * LANE, PAGE_ROWS), (lambda i, pt, p=p: (pt[i * DEC_GROUP * PAGES + p], 0, 0)))
                  for p in range(DEC_GROUP * PAGES)]
        + [per_b(qn.shape), per_b(kc.shape), per_b(vc.shape), per_b(kvs_new.shape), per_b(kvw_new.shape), per_b(win_state.shape),
           _const_spec(cov.shape)],
        out_specs=[per_b((b, 1, slots))] * 3 + [per_b(win_state.shape)],
    )
    return pl.pallas_call(
        functools.partial(_nsa_decode_kernel, past_len=past_len, n_sel=n_sel),
        grid_spec=grid_spec,
        out_shape=[jax.ShapeDtypeStruct((b, 1, slots), F32)] * 3 + [jax.ShapeDtypeStruct(win_state.shape, F32)],
        compiler_params=_cparams("parallel"),
        name="nsa_decode",
    )(page_ids, *([sel_pages] * (DEC_GROUP * PAGES)), qn, kc, vc, kvs_new, kvw_new, win_state, cov)


def _dot_tn(a, b):
    return lax.dot_general(a, b, (((0,), (0,)), ((), ())), preferred_element_type=F32)


def _cumsum_table():
    r = np.arange(LANE)
    return jnp.asarray(((r[:, None] // GLA_SUB == r[None, :] // GLA_SUB) & (r[None, :] <= r[:, None])).astype(np.float32))


def _gla_kernel(*refs, t, t_valid, has_state):
    if has_state:
        qg_ref, kg_ref, vg_ref, misc_ref, wa_ref, ba_ref, lt_ref, seg_ref, s0_ref, o_ref, s_ref, b_scr, st_scr = refs
    else:
        qg_ref, kg_ref, vg_ref, misc_ref, wa_ref, ba_ref, lt_ref, seg_ref, o_ref, s_ref, b_scr, st_scr = refs
    z = jnp.dot(misc_ref[0], wa_ref[...], preferred_element_type=F32, precision=lax.Precision.HIGHEST) + ba_ref[...]
    la = (jnp.minimum(z, 0.0) - jnp.log1p(jnp.exp(-jnp.abs(z)))) * (1.0 / GLA_TAU)
    if t_valid < t:
        la = jnp.where(lax.broadcasted_iota(I32, la.shape, 0) < t_valid, la, 0.0)
    tile = min(t, LANE)
    for r in range(0, t, tile):
        b_scr[r:r + tile, :] = jnp.dot(lt_ref[0:tile, 0:tile], la[r:r + tile, :], preferred_element_type=F32,
                                       precision=lax.Precision.HIGHEST)
    pairs = GLA_HEADS // 2
    for p in range(pairs):
        if has_state:
            st_scr[p] = s0_ref[0, 2 * p:2 * p + 2].reshape(2 * GLA_DK, GLA_DV).T
        else:
            st_scr[p] = jnp.zeros((GLA_DV, LANE), F32)
    head_a = lax.broadcasted_iota(I32, (GLA_SUB, LANE), 1) < GLA_DK
    row = lax.broadcasted_iota(I32, (GLA_SUB, LANE), 0)

    def pair_chunk(q, k, v, b, st):
        b_last = b[GLA_SUB - 1:GLA_SUB, :]
        st_b = st.astype(BF16)
        qe = q * jnp.exp(b)
        o = jnp.concatenate([_dot_nt(jnp.where(head_a, qe, 0.0).astype(BF16), st_b),
                             _dot_nt(jnp.where(head_a, 0.0, qe).astype(BF16), st_b)], axis=1)
        ws = []
        for j in range(GLA_SUB):
            w = q * k[j:j + 1, :] * jnp.exp(jnp.minimum(b - b[j:j + 1, :], 0.0))
            ws.append(jnp.where(row >= j, w, 0.0))
        a_all = jnp.dot(jnp.concatenate(ws, axis=0).astype(BF16), seg_ref[...], preferred_element_type=F32)
        for j in range(GLA_SUB):
            o = o + a_all[j * GLA_SUB:(j + 1) * GLA_SUB] * v[j:j + 1, :]
        kd = k * jnp.exp(b_last - b)
        upd = (_dot_tn(v[:, 0:GLA_DV].astype(BF16), jnp.where(head_a, kd, 0.0).astype(BF16))
               + _dot_tn(v[:, GLA_DV:2 * GLA_DV].astype(BF16), jnp.where(head_a, 0.0, kd).astype(BF16)))
        return o, jnp.exp(b_last) * st + upd

    def chunk(c, carry):
        r0 = pl.multiple_of(c * GLA_SUB, GLA_SUB)
        for p in range(pairs):
            o, st_new = pair_chunk(qg_ref[0, pl.ds(r0, GLA_SUB), p * LANE:(p + 1) * LANE],
                                          kg_ref[0, pl.ds(r0, GLA_SUB), p * LANE:(p + 1) * LANE],
                                          vg_ref[0, pl.ds(r0, GLA_SUB), 2 * p * GLA_DV:2 * (p + 1) * GLA_DV],
                                          b_scr[pl.ds(r0, GLA_SUB), p * LANE:(p + 1) * LANE], st_scr[p])
            o_ref[0, pl.ds(r0, GLA_SUB), 2 * p * GLA_DV:2 * (p + 1) * GLA_DV] = o
            st_scr[p] = st_new
        return carry

    lax.fori_loop(0, t // GLA_SUB, chunk, 0, unroll=4 if t // GLA_SUB % 4 == 0 else 1)
    for p in range(pairs):
        s_ref[0, 2 * p:2 * p + 2] = st_scr[p].T.reshape(2, GLA_DK, GLA_DV)


def _gla(qg, kg, vg, misc, wa_pad, ba, s0, t_valid):
    b, t, _ = qg.shape
    has_state = s0 is not None
    hk = GLA_HEADS * GLA_DK

    def per_b(shape):
        nd = len(shape)
        return pl.BlockSpec((1,) + shape[1:], lambda i: (i,) + (0,) * (nd - 1))

    def const(shape):
        nd = len(shape)
        return pl.BlockSpec(shape, lambda i: (0,) * nd)

    lanes = np.arange(LANE)[:, None] < GLA_DK
    seg = jnp.asarray((lanes == (np.arange(2 * GLA_DV)[None, :] < GLA_DV)).astype(np.float32), dtype=BF16)
    in_specs = [per_b(qg.shape), per_b(kg.shape), per_b(vg.shape), per_b(misc.shape),
                const((LANE, hk)), const((1, hk)), const((LANE, LANE)), const((LANE, 2 * GLA_DV))]
    args = [qg, kg, vg, misc, wa_pad, ba.reshape(1, -1), _cumsum_table(), seg]
    state_shape = (b, GLA_HEADS, GLA_DK, GLA_DV)
    if has_state:
        in_specs.append(per_b(state_shape))
        args.append(s0)
    return pl.pallas_call(
        functools.partial(_gla_kernel, t=t, t_valid=t_valid, has_state=has_state),
        grid=(b,),
        in_specs=in_specs,
        out_specs=[per_b(vg.shape), per_b(state_shape)],
        out_shape=[jax.ShapeDtypeStruct(vg.shape, F32), jax.ShapeDtypeStruct(state_shape, F32)],
        scratch_shapes=[pltpu.VMEM((t, hk), F32), pltpu.VMEM((GLA_HEADS // 2, GLA_DV, LANE), F32)],
        compiler_params=_cparams("parallel"),
        name="gla",
    )(*args)


def _gate_expand_table():
    out = np.zeros((3, LANE, NSA_HEADS * LANE), np.float32)
    for hh in range(NSA_HEADS):
        for j in range(3):
            out[j, MISC_GN + 3 * hh + j, hh * LANE:(hh + 1) * LANE] = 1.0
    return jnp.asarray(np.concatenate([out, out], axis=1), dtype=BF16)


def _pad_br_a(w_br_a):
    zero = jnp.zeros((HEAD_DIM, w_br_a.shape[1]), w_br_a.dtype)
    parts = []
    for hh in range(NSA_HEADS):
        wh = w_br_a[hh * HEAD_DIM:(hh + 1) * HEAD_DIM]
        parts += [wh, zero] if hh // NSA_GROUP == 0 else [zero, wh]
    return jnp.concatenate(parts, axis=0).astype(BF16)


def _layer_norm(v, g, b):
    mu = jnp.mean(v, axis=-1, keepdims=True)
    var = jnp.mean(jnp.square(v - mu), axis=-1, keepdims=True)
    return (v - mu) * lax.rsqrt(var + LN_EPS) * g + b


def _mixer_tail_kernel(ocmp_ref, osel_ref, owin_ref, misc_ref, ogla_ref, rg_ref, gm_ref, x_ref, gate_ref, scf_ref, shf_ref,
                       ex_ref, ng_ref, wa_ref, wb_ref, wo_ref, lg_ref, lb_ref, x1_ref, xm_ref):
    sig = jax.nn.sigmoid(misc_ref[0])
    sig_hi = sig.astype(BF16)
    sig_lo = (sig - sig_hi.astype(F32)).astype(BF16)
    sig_split = jnp.concatenate([sig_hi, sig_lo], axis=1)
    o_nsa = None
    for j, ref in enumerate((ocmp_ref, osel_ref, owin_ref)):
        g = jnp.dot(sig_split, ex_ref[j], preferred_element_type=F32)
        o_nsa = g * ref[0] if o_nsa is None else o_nsa + g * ref[0]
    br_a = _bdot(o_nsa, wa_ref[...])
    heads = []
    for h in range(GLA_HEADS):
        seg = ogla_ref[0, :, h * GLA_DV:(h + 1) * GLA_DV]
        mu = jnp.mean(seg, axis=-1, keepdims=True)
        var = jnp.mean(jnp.square(seg - mu), axis=-1, keepdims=True)
        r = rg_ref[0, :, h * GLA_DV:(h + 1) * GLA_DV]
        heads.append((seg - mu) * lax.rsqrt(var + LN_EPS) * ng_ref[:, h * GLA_DV:(h + 1) * GLA_DV] * (r * jax.nn.sigmoid(r)))
    br_b = _bdot(jnp.concatenate(heads, axis=1), wb_ref[...])
    gm_a = jax.nn.sigmoid(gm_ref[0, :, 0:D_MODEL])
    gm_b = jax.nn.sigmoid(gm_ref[0, :, D_MODEL:2 * D_MODEL])
    y = _bdot(gm_a * br_a + gm_b * br_b, wo_ref[...])
    x1 = _layer_norm(DN_ALPHA * x_ref[0] + gate_ref[0] * y, lg_ref[...], lb_ref[...])
    x1_ref[0] = x1
    xm_ref[0] = x1 * (1.0 + scf_ref[0]) + shf_ref[0]


def _mixer_tail(ocmp, osel, owin, misc, ogla, rg, gm, x, gate_m, scale_f, shift_f, consts, tm):
    b, t, d = x.shape
    per_tok = gate_m.shape[1] != 1

    def tok(w):
        return pl.BlockSpec((1, tm, w), lambda i, j: (i, j, 0))

    mod_spec = tok(d) if per_tok else pl.BlockSpec((1, 1, d), lambda i, j: (i, 0, 0))

    def const(a):
        nd = a.ndim
        return pl.BlockSpec(a.shape, lambda i, j: (0,) * nd)

    return pl.pallas_call(
        _mixer_tail_kernel,
        grid=(b, t // tm),
        in_specs=[tok(NSA_HEADS * LANE)] * 3 + [tok(LANE), tok(GLA_HEADS * GLA_DV), tok(GLA_HEADS * GLA_DV), tok(2 * d), tok(d),
                                               mod_spec, mod_spec, mod_spec] + [const(c) for c in consts],
        out_specs=[tok(d), tok(d)],
        out_shape=[jax.ShapeDtypeStruct((b, t, d), F32)] * 2,
        compiler_params=_cparams("parallel", "parallel"),
        name="mixer_tail",
    )(ocmp, osel, owin, misc, ogla, rg, gm, x, gate_m, scale_f, shift_f, *consts)


ROUTE_TILE = LANE


def _first_index(hit, iota, size, axis):
    return jnp.min(jnp.where(hit, iota, size), axis=axis, keepdims=True)


def _part_specs(parts, tm):
    n_a = parts[0].shape[0] // tm
    d = parts[0].shape[1]
    return [pl.BlockSpec((tm, d), lambda i, *_: (jnp.minimum(i, n_a - 1), 0)),
            pl.BlockSpec((tm, d), lambda i, *_: (jnp.maximum(i - n_a, 0), 0))], n_a


def _part_tile(i, n_a, a_ref, b_ref):
    return jnp.where(i < n_a, a_ref[...], b_ref[...])


def _router_kernel(xa_ref, xb_ref, wr_ref, bias_ref, tri_ref, eidx_ref, rank_ref, wrow_ref, cnt_ref, carry_ref, *, n_a):
    i = pl.program_id(0)
    tm = ROUTE_TILE
    per = N_EXPERTS // N_GROUPS

    @pl.when(i == 0)
    def _():
        carry_ref[...] = jnp.zeros_like(carry_ref)

    logits = _dot_nt(wr_ref[...], _part_tile(i, n_a, xa_ref, xb_ref), lax.Precision.HIGHEST)
    s = jax.nn.sigmoid(logits)
    sb = s + bias_ref[...]
    sb3 = sb.reshape(N_GROUPS, per, tm)
    in_grp = lax.broadcasted_iota(I32, sb3.shape, 1)
    m1 = jnp.max(sb3, axis=1, keepdims=True)
    first = _first_index(sb3 == m1, in_grp, per, 1)
    m2 = jnp.max(jnp.where(in_grp == first, NEG, sb3), axis=1, keepdims=True)
    gs = (m1 + m2).reshape(N_GROUPS, tm)
    g_iota = lax.broadcasted_iota(I32, gs.shape, 0)
    g_keep = jnp.zeros(gs.shape, jnp.bool_)
    for _ in range(TOPK_GROUPS):
        pick = g_iota == _first_index(gs == jnp.max(gs, axis=0, keepdims=True), g_iota, N_GROUPS, 0)
        g_keep = g_keep | pick
        gs = jnp.where(pick, NEG, gs)
    sbm = jnp.where(g_keep.reshape(N_GROUPS, 1, tm), sb3, NEG).reshape(N_EXPERTS, tm)
    e_iota = lax.broadcasted_iota(I32, sbm.shape, 0)
    idxs, sels = [], []
    onehot = jnp.zeros(sbm.shape, F32)
    for _ in range(TOP_K):
        idx = _first_index(sbm == jnp.max(sbm, axis=0, keepdims=True), e_iota, N_EXPERTS, 0)
        pick = e_iota == idx
        idxs.append(idx)
        sels.append(jnp.sum(jnp.where(pick, s, 0.0), axis=0, keepdims=True))
        sbm = jnp.where(pick, NEG, sbm)
        onehot = onehot + jnp.where(pick, 1.0, 0.0)
    sel = jnp.concatenate(sels, axis=0)
    wts = sel / jnp.sum(sel, axis=0, keepdims=True) * ROUTED_SCALE
    carry = carry_ref[...]
    before = carry + jnp.dot(onehot.astype(BF16), tri_ref[...], preferred_element_type=F32)
    ranks = [jnp.sum(jnp.where(e_iota == idx, before, 0.0), axis=0, keepdims=True) for idx in idxs]
    eidx_ref[...] = jnp.concatenate(idxs, axis=0)
    rank_ref[...] = jnp.concatenate(ranks, axis=0).astype(I32)
    wrow_ref[...] = jnp.concatenate([wts, jnp.zeros((LANE - TOP_K, tm), F32)], axis=0).T
    carry = carry + jnp.sum(onehot, axis=1, keepdims=True)
    carry_ref[...] = carry
    cnt_ref[...] = carry


def _router(xm_parts, w_router, router_bias):
    n = xm_parts[0].shape[0] + xm_parts[1].shape[0]
    d = xm_parts[0].shape[1]
    tm = ROUTE_TILE
    r = np.arange(tm)
    tri = jnp.asarray((r[:, None] < r[None, :]).astype(np.float32), dtype=BF16)
    x_specs, n_a = _part_specs(xm_parts, tm)
    return pl.pallas_call(
        functools.partial(_router_kernel, n_a=n_a),
        grid=(n // tm,),
        in_specs=x_specs + [
                  pl.BlockSpec((N_EXPERTS, d), lambda i: (0, 0)),
                  pl.BlockSpec((N_EXPERTS, 1), lambda i: (0, 0)),
                  pl.BlockSpec((tm, tm), lambda i: (0, 0))],
        out_specs=[pl.BlockSpec((TOP_K, tm), lambda i: (0, i)),
                   pl.BlockSpec((TOP_K, tm), lambda i: (0, i)),
                   pl.BlockSpec((tm, LANE), lambda i: (i, 0)),
                   pl.BlockSpec((N_EXPERTS, LANE), lambda i: (0, 0))],
        out_shape=[jax.ShapeDtypeStruct((TOP_K, n), I32), jax.ShapeDtypeStruct((TOP_K, n), I32),
                   jax.ShapeDtypeStruct((n, LANE), F32), jax.ShapeDtypeStruct((N_EXPERTS, LANE), F32)],
        scratch_shapes=[pltpu.VMEM((N_EXPERTS, LANE), F32)],
        compiler_params=_cparams("arbitrary"),
        name="router",
    )(*xm_parts, w_router.T, router_bias.reshape(N_EXPERTS, 1), tri)


def _dest_kernel(eidx_ref, rank_ref, start_ref, dest_ref):
    tm = ROUTE_TILE
    e_iota = lax.broadcasted_iota(I32, (N_EXPERTS, tm), 0)
    start = start_ref[...]
    for j in range(dest_ref.shape[0]):
        cols = slice(j * tm, (j + 1) * tm)
        rows = [jnp.sum(jnp.where(e_iota == eidx_ref[k:k + 1, cols], start, 0.0), axis=0, keepdims=True) for k in range(TOP_K)]
        dest_ref[j] = jnp.concatenate(rows, axis=0).astype(I32) + rank_ref[:, cols]


def _dest(eidx, rank, pad_start):
    n = eidx.shape[1]
    tm = ROUTE_TILE
    per_step = 3 if (n // tm) % 3 == 0 else 1
    return pl.pallas_call(
        _dest_kernel,
        grid=(n // tm // per_step,),
        in_specs=[pl.BlockSpec((TOP_K, per_step * tm), lambda i: (0, i)),
                  pl.BlockSpec((TOP_K, per_step * tm), lambda i: (0, i)),
                  pl.BlockSpec((N_EXPERTS, 1), lambda i: (0, 0))],
        out_specs=pl.BlockSpec((per_step, TOP_K, tm), lambda i: (i, 0, 0)),
        out_shape=jax.ShapeDtypeStruct((n // tm, TOP_K, tm), I32),
        compiler_params=_cparams("parallel"),
        name="dest",
    )(eidx, rank, pad_start.astype(F32).reshape(N_EXPERTS, 1))


TOKEN_SUB = D_MODEL // LANE


def _to_token_tiles(x, ref):
    m = x.shape[0]
    for s in range(TOKEN_SUB):
        ref[pl.ds(s, m, stride=TOKEN_SUB), :] = x[:, s * LANE:(s + 1) * LANE]


def _from_token_tiles(ref, m, s):
    return ref[pl.ds(s, m, stride=TOKEN_SUB), :]


def _row_copy(src_ref, src_row, dst_ref, dst_row, sem):
    return pltpu.make_async_copy(src_ref.at[pl.ds(src_row * TOKEN_SUB, TOKEN_SUB)],
                                 dst_ref.at[pl.ds(dst_row * TOKEN_SUB, TOKEN_SUB)], sem)


def _dispatch_kernel(dest_ref, pad_ref, nb_ref, xa_ref, xb_ref, xs_ref, tile_ref, zero_ref, sems, sem, *, n_a):
    i = pl.program_id(0)
    tm = ROUTE_TILE
    bm = EXPERT_BLOCK
    last = pl.num_programs(0) - 1
    src = tile_ref.at[i % 2]
    _to_token_tiles(_part_tile(i, n_a, xa_ref, xb_ref), src)

    def start_row(r, c):
        for k in range(TOP_K):
            _row_copy(src, r, xs_ref, dest_ref[0, k, r], sems.at[i % 2]).start(priority=k % 2)
        return c

    def wait_step(slot):
        for _ in range(TOP_K):
            pltpu.make_async_copy(tile_ref.at[0], xs_ref.at[pl.ds(0, tm * TOKEN_SUB)], sems.at[slot]).wait()

    lax.fori_loop(0, tm, start_row, 0)

    @pl.when(i >= 1)
    def _():
        wait_step((i - 1) % 2)

    @pl.when(i == last)
    def _():
        wait_step(i % 2)
        zero_ref[...] = jnp.zeros_like(zero_ref)

        def pad_rows(e, c, op):
            lax.fori_loop(pad_ref[0, e], pad_ref[1, e], lambda r, cc: (op(_row_copy(zero_ref, 0, xs_ref, r, sem)), cc)[1], 0)
            return c

        lax.fori_loop(0, N_EXPERTS, functools.partial(pad_rows, op=lambda cp: cp.start()), 0)
        lax.fori_loop(0, N_EXPERTS, functools.partial(pad_rows, op=lambda cp: cp.wait()), 0)

        def tail_copy(blk):
            return pltpu.make_async_copy(zero_ref, xs_ref.at[pl.ds(blk * bm * TOKEN_SUB, bm * TOKEN_SUB)], sem)

        n_blocks = xs_ref.shape[0] // (bm * TOKEN_SUB)
        lax.fori_loop(nb_ref[0], n_blocks, lambda blk, c: (tail_copy(blk).start(), c)[1], 0)
        lax.fori_loop(nb_ref[0], n_blocks, lambda blk, c: (tail_copy(blk).wait(), c)[1], 0)


def _dispatch(xm_parts, dest_tiles, pad_range, n_used, n_rows):
    tm = ROUTE_TILE
    x_specs, n_a = _part_specs(xm_parts, tm)
    return pl.pallas_call(
        functools.partial(_dispatch_kernel, n_a=n_a),
        grid=(dest_tiles.shape[0],),
        in_specs=[pl.BlockSpec((1, TOP_K, tm), lambda i: (i, 0, 0), memory_space=pltpu.SMEM),
                  pl.BlockSpec(memory_space=pltpu.SMEM),
                  pl.BlockSpec(memory_space=pltpu.SMEM)] + x_specs,
        out_specs=pl.BlockSpec(memory_space=pl.ANY),
        out_shape=jax.ShapeDtypeStruct((n_rows * TOKEN_SUB, LANE), F32),
        scratch_shapes=[pltpu.VMEM((2, tm * TOKEN_SUB, LANE), F32), pltpu.VMEM((EXPERT_BLOCK * TOKEN_SUB, LANE), F32),
                        pltpu.SemaphoreType.DMA((2,)), pltpu.SemaphoreType.DMA(())],
        compiler_params=_cparams("arbitrary"),
        name="dispatch",
    )(dest_tiles, pad_range, n_used, *xm_parts)


def _experts_kernel(first_ref, cnt_ref, nb_ref, xs_ref, wg_ref, wu_ref, wd_ref, ys_ref,
                    xbuf, ybuf, wg_s, wu_s, wd_s, in_sems, out_sems):
    e = pl.program_id(0)
    bm = EXPERT_BLOCK
    brows = bm * TOKEN_SUB
    total = nb_ref[0]
    n_blocks = xs_ref.shape[0] // brows

    def x_copy(g, slot):
        return pltpu.make_async_copy(xs_ref.at[pl.ds(g * brows, brows)], xbuf.at[slot], in_sems.at[slot])

    def y_copy(g, slot):
        return pltpu.make_async_copy(ybuf.at[slot], ys_ref.at[pl.ds(g * brows, brows)], out_sems.at[slot])

    depth = xbuf.shape[0]

    @pl.when(e == 0)
    def _():
        for g0 in range(depth - 1):
            @pl.when(g0 < total)
            def _():
                x_copy(g0, g0).start()

    @pl.when(cnt_ref[e] > 0)
    def _():
        wg_s[...] = wg_ref[0].astype(BF16)
        wu_s[...] = wu_ref[0].astype(BF16)
        wd_s[...] = wd_ref[0].astype(BF16)

    def block(c, carry):
        g = first_ref[e] + c
        slot = g % depth
        x_copy(g, slot).wait()

        @pl.when(g + depth - 1 < total)
        def _():
            x_copy(g + depth - 1, (g + depth - 1) % depth).start()

        x = jnp.concatenate([_from_token_tiles(xbuf.at[slot], bm, s) for s in range(TOKEN_SUB)], axis=1).astype(BF16)
        gate = jnp.dot(x, wg_s[...], preferred_element_type=F32)
        up = jnp.dot(x, wu_s[...], preferred_element_type=F32)
        y = jnp.dot((gate * jax.nn.sigmoid(gate) * up).astype(BF16), wd_s[...], preferred_element_type=F32)

        @pl.when(g >= depth)
        def _():
            y_copy(g - depth, slot).wait()

        _to_token_tiles(y, ybuf.at[slot])
        y_copy(g, slot).start()
        return carry

    lax.fori_loop(0, cnt_ref[e], block, 0)

    @pl.when(e == pl.num_programs(0) - 1)
    def _():
        for back in range(depth, 0, -1):
            @pl.when(total >= back)
            def _():
                y_copy(total - back, (total - back) % depth).wait()

        ybuf[0] = jnp.zeros(ybuf.shape[1:], F32)
        lax.fori_loop(total, n_blocks, lambda g, c: (y_copy(g, 0).start(), c)[1], 0)
        lax.fori_loop(total, n_blocks, lambda g, c: (y_copy(g, 0).wait(), c)[1], 0)


def _experts(xs, first_block, n_block, n_used, w_gate, w_up, w_down):
    brows = EXPERT_BLOCK * TOKEN_SUB
    n_exp, d, f = w_gate.shape
    grid_spec = pltpu.PrefetchScalarGridSpec(
        num_scalar_prefetch=3,
        grid=(n_exp,),
        in_specs=[pl.BlockSpec(memory_space=pl.ANY),
                  pl.BlockSpec((1, d, f), lambda e, *_: (e, 0, 0)),
                  pl.BlockSpec((1, d, f), lambda e, *_: (e, 0, 0)),
                  pl.BlockSpec((1, f, d), lambda e, *_: (e, 0, 0))],
        out_specs=pl.BlockSpec(memory_space=pl.ANY),
        scratch_shapes=[pltpu.VMEM((EXPERT_RING, brows, LANE), F32), pltpu.VMEM((EXPERT_RING, brows, LANE), F32),
                        pltpu.VMEM((d, f), BF16), pltpu.VMEM((d, f), BF16), pltpu.VMEM((f, d), BF16),
                        pltpu.SemaphoreType.DMA((EXPERT_RING,)), pltpu.SemaphoreType.DMA((EXPERT_RING,))],
    )
    return pl.pallas_call(
        _experts_kernel,
        grid_spec=grid_spec,
        out_shape=jax.ShapeDtypeStruct(xs.shape, F32),
        compiler_params=_cparams("arbitrary"),
        name="experts",
    )(first_block, n_block, n_used, xs, w_gate, w_up, w_down)


def _combine_kernel(dest_ref, dnext_ref, ys_ref, wrow_ref, xa_ref, xb_ref, x1a_ref, x1b_ref, ga_ref, gb_ref,
                    sg_ref, su_ref, sd_ref, lg_ref, lb_ref, ya_ref, yb_ref, buf_ref, sems, *, n_a):
    i = pl.program_id(0)
    tm = ROUTE_TILE
    slot = i % 2

    def issue(d_ref, s):
        def start_row(r, c):
            for k in range(TOP_K):
                _row_copy(ys_ref, d_ref[0, k, r], buf_ref.at[s, k], r, sems.at[s]).start(priority=k % 2)
            return c

        lax.fori_loop(0, tm, start_row, 0)

    @pl.when(i == 0)
    def _():
        issue(dest_ref, 0)

    @pl.when(i < pl.num_programs(0) - 1)
    def _():
        issue(dnext_ref, 1 - slot)

    x = _part_tile(i, n_a, xa_ref, xb_ref).astype(BF16)
    g = jnp.dot(x, sg_ref[...], preferred_element_type=F32)
    u = jnp.dot(x, su_ref[...], preferred_element_type=F32)
    shared = jnp.dot((g * jax.nn.sigmoid(g) * u).astype(BF16), sd_ref[...], preferred_element_type=F32)
    for k in range(TOP_K):
        pltpu.make_async_copy(ys_ref.at[pl.ds(0, tm * TOKEN_SUB)], buf_ref.at[slot, k], sems.at[slot]).wait()
    w = wrow_ref[...]
    cols = []
    for s in range(TOKEN_SUB):
        routed = w[:, 0:1] * _from_token_tiles(buf_ref.at[slot, 0], tm, s)
        for k in range(1, TOP_K):
            routed = routed + w[:, k:k + 1] * _from_token_tiles(buf_ref.at[slot, k], tm, s)
        cols.append(routed + shared[:, s * LANE:(s + 1) * LANE])
    moe = jnp.concatenate(cols, axis=1)
    gate = jnp.where(i < n_a, ga_ref[0], gb_ref[...])
    y = _layer_norm(DN_ALPHA * _part_tile(i, n_a, x1a_ref, x1b_ref) + gate * moe, lg_ref[...], lb_ref[...])

    @pl.when(i < n_a)
    def _():
        ya_ref[...] = y

    @pl.when(i >= n_a)
    def _():
        yb_ref[...] = y


def _combine(ys, dest_tiles, wrow, xm_parts, x1_parts, gate_a, gate_b, ws_gate, ws_up, ws_down, ln_g, ln_b):
    n = xm_parts[0].shape[0] + xm_parts[1].shape[0]
    d = xm_parts[0].shape[1]
    tm = ROUTE_TILE
    f = ws_gate.shape[1]
    x_specs, n_a = _part_specs(xm_parts, tm)
    tiles_per_seq = n_a // gate_a.shape[0]
    gate_specs = [pl.BlockSpec((1, 1, d), lambda i: (jnp.minimum(i, n_a - 1) // tiles_per_seq, 0, 0)), x_specs[1]]
    vec = pl.BlockSpec((1, d), lambda i: (0, 0))
    return pl.pallas_call(
        functools.partial(_combine_kernel, n_a=n_a),
        grid=(n // tm,),
        in_specs=[pl.BlockSpec((1, TOP_K, tm), lambda i: (i, 0, 0), memory_space=pltpu.SMEM),
                  pl.BlockSpec((1, TOP_K, tm), lambda i: (jnp.minimum(i + 1, n // tm - 1), 0, 0), memory_space=pltpu.SMEM),
                  pl.BlockSpec(memory_space=pl.ANY),
                  pl.BlockSpec((tm, LANE), lambda i: (i, 0))] + x_specs + x_specs + gate_specs + [
                  pl.BlockSpec((d, f), lambda i: (0, 0)),
                  pl.BlockSpec((d, f), lambda i: (0, 0)),
                  pl.BlockSpec((f, d), lambda i: (0, 0)), vec, vec],
        out_specs=x_specs,
        out_shape=[jax.ShapeDtypeStruct(p.shape, F32) for p in xm_parts],
        scratch_shapes=[pltpu.VMEM((2, TOP_K, tm * TOKEN_SUB, LANE), F32), pltpu.SemaphoreType.DMA((2,))],
        compiler_params=_cparams("arbitrary"),
        name="combine",
    )(dest_tiles, dest_tiles, ys, wrow, *xm_parts, *x1_parts, gate_a, gate_b,
      ws_gate.astype(BF16), ws_up.astype(BF16), ws_down.astype(BF16), ln_g.reshape(1, d), ln_b.reshape(1, d))


def _moe(xm, x1, gate_a, gate_b, ln_g, ln_b, w_router, router_bias, w_e_gate, w_e_up, w_e_down, w_s_gate, w_s_up, w_s_down):
    n = xm[0].shape[0] + xm[1].shape[0]
    eidx, rank, wrow, cnt = _router(xm, w_router, router_bias)
    counts = cnt[:, 0].astype(I32)
    padded = (counts + EXPERT_BLOCK - 1) // EXPERT_BLOCK * EXPERT_BLOCK
    pad_end = jnp.cumsum(padded)
    pad_start = pad_end - padded
    dest_tiles = _dest(eidx, rank, pad_start)
    n_blocks = -(-(n * TOP_K) // EXPERT_BLOCK) + N_EXPERTS
    n_used = (pad_end[-1:] // EXPERT_BLOCK).astype(I32)
    pad_range = jnp.stack([pad_start + counts, pad_end]).astype(I32)
    xs = _dispatch(xm, dest_tiles, pad_range, n_used, n_blocks * EXPERT_BLOCK)
    ys = _experts(xs, (pad_start // EXPERT_BLOCK).astype(I32), (padded // EXPERT_BLOCK).astype(I32), n_used,
                  w_e_gate, w_e_up, w_e_down)
    return _combine(ys, dest_tiles, wrow, xm, x1, gate_a, gate_b, w_s_gate, w_s_up, w_s_down, ln_g, ln_b)


def kernel(x_prompt, x_sample, cache_kv_cmp, cache_kv_sel, state_kv_win, state_gla, page_table, c_prompt, c_sample, w_in, b_in, cmp_k_pos, cmp_k_w1, cmp_k_w2, cmp_v_pos, cmp_v_w1, cmp_v_w2, gla_w_a2, gla_b_a, gla_norm_g, w_br_a, w_br_b, w_out, ln1_g, ln1_b, w_ada, b_ada, w_router, router_bias, w_e_gate, w_e_up, w_e_down, w_s_gate, w_s_up, w_s_down, ln2_g, ln2_b):
    bp, tp, d = x_prompt.shape
    nd, td = x_sample.shape[:2]
    n_pool, page_rows = cache_kv_cmp.shape[:2]
    past_len = page_table.shape[1] * page_rows
    assert td == 1 and d == D_MODEL and page_rows == PAGE_ROWS and page_table.shape[1] == PAGES and tp == PAGES * PAGE_ROWS
    kv_w = 2 * NSA_KV_HEADS * HEAD_DIM

    mod = _adaln(jnp.concatenate([c_prompt, c_sample], axis=0), w_ada, b_ada)
    mod_p = [m.reshape(bp, 1, d) for m in jnp.split(mod[:bp], 6, axis=-1)]
    mod_s = [m.reshape(1, nd, d) for m in jnp.split(mod[bp:], 6, axis=-1)]

    w_pack, b_pack = _pack_in_weights(w_in, b_in)
    cmp_wk = _pack_cmp_weights(cmp_k_pos, cmp_k_w1, cmp_k_w2)
    cmp_wv = _pack_cmp_weights(cmp_v_pos, cmp_v_w1, cmp_v_w2)
    wa_pad = jnp.zeros((LANE, GLA_HEADS * GLA_DK), F32).at[MISC_AG:MISC_AG + GLA_GATE_RANK].set(gla_w_a2)
    tail_consts = (_gate_expand_table(), gla_norm_g.reshape(1, -1), _pad_br_a(w_br_a), w_br_b.astype(BF16), w_out.astype(BF16),
                   ln1_g.reshape(1, d), ln1_b.reshape(1, d))

    gm, qn, vg, rg, kvc, kvs, kvw, qg, kg, misc = _inproj(
        x_prompt, mod_p[0], mod_p[1], w_pack, b_pack, _rope_tables(jnp.arange(tp, dtype=I32)), 256)
    kc, vc = _compress(kvc.reshape(bp * PAGES, PAGE_ROWS, kv_w), jnp.arange(bp * PAGES, dtype=I32), cmp_wk, cmp_wv)
    ocmp, sel = _cmp_attn(qn, kc, vc, 256)
    osel = _sel_attn(qn, kvs, sel, 128, 256)
    owin = _win_attn(qn, kvw, 128)
    ogla, gla_p = _gla(qg, kg, vg, misc, wa_pad, gla_b_a, None, tp)
    x1_p, xm_p = _mixer_tail(ocmp, osel, owin, misc, ogla, rg, gm, x_prompt, mod_p[2], mod_p[4], mod_p[3], tail_consts, 256)
    n_win = min(WINDOW, tp)
    outs_p = (kvc.reshape(bp, tp, 2, NSA_KV_HEADS, HEAD_DIM), kvs.reshape(bp, tp, 2, NSA_KV_HEADS, HEAD_DIM),
              kvw[:, tp - n_win:].reshape(bp, n_win, 2, NSA_KV_HEADS, HEAD_DIM), gla_p)

    gm, qn, vg, rg, kvc, kvs, kvw, qg, kg, misc = _inproj(
        x_sample.reshape(1, nd, d), mod_s[0], mod_s[1], w_pack, b_pack, _rope_tables(jnp.full((nd,), past_len, I32)), nd)
    page_ids = page_table.reshape(-1).astype(I32)
    kc, vc = _compress(cache_kv_cmp.reshape(n_pool, PAGE_ROWS, kv_w).transpose(0, 2, 1), page_ids, cmp_wk, cmp_wv,
                       feature_major=True)
    ocmp, osel, owin, win_new = _nsa_decode(
        qn.reshape(nd, 1, -1), kc, vc, cache_kv_sel.reshape(n_pool, PAGE_ROWS, kv_w).transpose(0, 2, 1), page_ids,
        kvs.reshape(nd, 1, kv_w), kvw.reshape(nd, 1, kv_w), state_kv_win.reshape(nd, -1, kv_w).transpose(0, 2, 1))
    win_new = win_new.transpose(0, 2, 1)

    def pad_rows(a):
        return jnp.pad(a.reshape(nd, 1, -1), ((0, 0), (0, GLA_SUB - 1), (0, 0)))

    ogla, gla_s = _gla(pad_rows(qg), pad_rows(kg), pad_rows(vg), pad_rows(misc), wa_pad, gla_b_a, state_gla, 1)
    x1_s, xm_s = _mixer_tail(ocmp.reshape(1, nd, -1), osel.reshape(1, nd, -1), owin.reshape(1, nd, -1), misc,
                             ogla[:, 0].reshape(1, nd, -1), rg, gm, x_sample.reshape(1, nd, d),
                             mod_s[2], mod_s[4], mod_s[3], tail_consts, nd)
    outs_s = (kvc.reshape(nd, 1, 2, NSA_KV_HEADS, HEAD_DIM), kvs.reshape(nd, 1, 2, NSA_KV_HEADS, HEAD_DIM),
              win_new.reshape(state_kv_win.shape), gla_s)

    n_p = bp * tp
    y_p, y_s = _moe((xm_p.reshape(n_p, d), xm_s.reshape(nd, d)), (x1_p.reshape(n_p, d), x1_s.reshape(nd, d)),
                    mod_p[5], mod_s[5].reshape(nd, d), ln2_g, ln2_b,
                    w_router, router_bias, w_e_gate, w_e_up, w_e_down, w_s_gate, w_s_up, w_s_down)
    return (y_p.reshape(bp, tp, d), y_s.reshape(nd, 1, d)) + outs_p + outs_s
```

```python
import functools

import numpy as np
import jax
import jax.numpy as jnp
from jax import lax
from jax.experimental import pallas as pl
from jax.experimental.pallas import tpu as pltpu

F32 = jnp.float32
BF16 = jnp.bfloat16
I32 = jnp.int32

D_MODEL = 1024
NSA_HEADS = 8
NSA_KV_HEADS = 2
NSA_GROUP = NSA_HEADS // NSA_KV_HEADS
HEAD_DIM = 64
ROT_DIM = HEAD_DIM // 4
ROPE_THETA = 500000.0
CMP_LEN = 32
CMP_STRIDE = 16
CMP_HIDDEN = 256
SEL_BLOCK = 64
SEL_TOP_N = 16
WINDOW = 512
FORCE_SCORE = 1.0e4
GLA_HEADS = 4
GLA_DK = 64
GLA_DV = 128
GLA_GATE_RANK = 16
GLA_TAU = 16.0
GLA_SUB = 16
N_EXPERTS = 256
TOP_K = 8
N_GROUPS = 8
TOPK_GROUPS = 4
EXPERT_DIM = 256
SHARED_DIM = 256
ROUTED_SCALE = 2.5
EXPERT_BLOCK = 128
EXPERT_RING = 4
PAD_GROUP = 32
DN_ALPHA = 2.0 ** 0.25
LN_EPS = 1e-5
LANE = 128
NEG = -1.0e30
VMEM_LIMIT = 56 * 1024 * 1024

SEG_GM = (0, 2 * D_MODEL)
SEG_QN = (SEG_GM[0] + SEG_GM[1], NSA_HEADS * LANE)
SEG_VG = (SEG_QN[0] + SEG_QN[1], GLA_HEADS * GLA_DV)
SEG_RG = (SEG_VG[0] + SEG_VG[1], GLA_HEADS * GLA_DV)
SEG_KVC = (SEG_RG[0] + SEG_RG[1], 2 * LANE)
SEG_KVS = (SEG_KVC[0] + SEG_KVC[1], 2 * LANE)
SEG_KVW = (SEG_KVS[0] + SEG_KVS[1], 2 * LANE)
SEG_QG = (SEG_KVW[0] + SEG_KVW[1], GLA_HEADS * GLA_DK)
SEG_KG = (SEG_QG[0] + SEG_QG[1], GLA_HEADS * GLA_DK)
SEG_MISC = (SEG_KG[0] + SEG_KG[1], LANE)
IN_PACKED = SEG_MISC[0] + SEG_MISC[1]
MISC_GN = 0
MISC_AG = NSA_HEADS * 3


def _cparams(*sem):
    return pltpu.CompilerParams(dimension_semantics=sem, vmem_limit_bytes=VMEM_LIMIT)


def _bdot(a, b):
    return jnp.dot(a.astype(BF16), b.astype(BF16), preferred_element_type=F32)


def _dot_nt(a, b, precision=None):
    return lax.dot_general(a, b, (((1,), (1,)), ((), ())), preferred_element_type=F32, precision=precision)


def _adaln_kernel(c_ref, w_ref, b_ref, o_ref):
    c = c_ref[...]
    o_ref[...] = _bdot(c * jax.nn.sigmoid(c), w_ref[...]) + b_ref[...]


def _adaln(c, w_ada, b_ada):
    n, d = c.shape
    m = w_ada.shape[1]
    tn = 512
    return pl.pallas_call(
        _adaln_kernel,
        grid=(m // tn,),
        in_specs=[pl.BlockSpec((n, d), lambda j: (0, 0)),
                  pl.BlockSpec((d, tn), lambda j: (0, j)),
                  pl.BlockSpec((1, tn), lambda j: (0, j))],
        out_specs=pl.BlockSpec((n, tn), lambda j: (0, j)),
        out_shape=jax.ShapeDtypeStruct((n, m), F32),
        compiler_params=_cparams("parallel"),
        name="adaln",
    )(c, w_ada, b_ada.reshape(1, m))


def _rope_tables(pos):
    half = ROT_DIM // 2
    inv = jnp.power(ROPE_THETA, -jnp.arange(half, dtype=F32) * 2.0 / ROT_DIM)
    ang = pos.astype(F32)[:, None] * inv[None, :]
    cos, sin = jnp.cos(ang), jnp.sin(ang)
    t = pos.shape[0]
    one = jnp.ones((t, HEAD_DIM - ROT_DIM), F32)
    z8 = jnp.zeros((t, half), F32)
    z48 = jnp.zeros((t, HEAD_DIM - ROT_DIM), F32)
    c = jnp.concatenate([cos, cos, one, cos, cos, one], axis=1)
    s1 = jnp.concatenate([-sin, z8, z48, -sin, z8, z48], axis=1)
    s2 = jnp.concatenate([z8, sin, z48, z8, sin, z48], axis=1)
    return c, s1, s2


def _pack_in_weights(w_in, b_in):
    sizes = (512, 128, 128, 128, 128, 128, 128, 24, 256, 256, 512, 512, 16, 2048)
    offs = np.concatenate([[0], np.cumsum(sizes)])

    def pack(w):
        seg = [w[..., offs[i]:offs[i + 1]] for i in range(len(sizes))]
        q_n, k_c, v_c, k_s, v_s, k_w, v_w, g_n, q_g, k_g, v_g, r_g, a_g, g_m = seg
        zero = jnp.zeros_like(q_n[..., :HEAD_DIM])
        q_slots = []
        for hh in range(NSA_HEADS):
            qh = q_n[..., hh * HEAD_DIM:(hh + 1) * HEAD_DIM] * (HEAD_DIM ** -0.5)
            q_slots += [qh, zero] if hh // NSA_GROUP == 0 else [zero, qh]
        misc_pad = jnp.zeros_like(w[..., :LANE - g_n.shape[-1] - a_g.shape[-1]])
        return jnp.concatenate([g_m] + q_slots + [v_g, r_g, k_c, v_c, k_s, v_s, k_w, v_w,
                                                   q_g * (GLA_DK ** -0.5), k_g, g_n, a_g, misc_pad], axis=-1)

    return pack(w_in).astype(BF16), pack(b_in.reshape(1, -1))


def _inproj_kernel(x_ref, sh_ref, sc_ref, w_ref, b_ref, rc_ref, rs1_ref, rs2_ref,
                   gm_ref, qn_ref, vg_ref, rg_ref, kvc_ref, kvs_ref, kvw_ref, qg_ref, kg_ref, misc_ref):
    h = (x_ref[0] * (1.0 + sc_ref[0]) + sh_ref[0]).astype(BF16)
    rc, rs1, rs2 = rc_ref[...], rs1_ref[...], rs2_ref[...]

    def proj(off, width):
        return jnp.dot(h, w_ref[:, off:off + width], preferred_element_type=F32) + b_ref[:, off:off + width]

    def rope(z):
        return z * rc + pltpu.roll(z, LANE - ROT_DIM // 2, 1) * rs1 + pltpu.roll(z, ROT_DIM // 2, 1) * rs2

    def plain(ref, seg):
        off, width = seg
        step = min(width, 512)
        for c in range(0, width, step):
            ref[0, :, c:c + step] = proj(off + c, step)

    plain(gm_ref, SEG_GM)
    for c in range(0, SEG_QN[1], 512):
        z = proj(SEG_QN[0] + c, 512)
        for s in range(0, 512, LANE):
            qn_ref[0, :, c + s:c + s + LANE] = rope(z[:, s:s + LANE])
    plain(vg_ref, SEG_VG)
    plain(rg_ref, SEG_RG)
    for ref, seg in ((kvc_ref, SEG_KVC), (kvs_ref, SEG_KVS), (kvw_ref, SEG_KVW)):
        z = proj(seg[0], seg[1])
        ref[0, :, 0:LANE] = rope(z[:, 0:LANE])
        ref[0, :, LANE:2 * LANE] = z[:, LANE:2 * LANE]
    plain(qg_ref, SEG_QG)
    plain(kg_ref, SEG_KG)
    plain(misc_ref, SEG_MISC)


def _inproj(x, shift, scale, w_pack, b_pack, tables, tm):
    b, t, d = x.shape
    per_tok = shift.shape[1] != 1
    mod_spec = (pl.BlockSpec((1, tm, d), lambda i, j: (i, j, 0)) if per_tok
                else pl.BlockSpec((1, 1, d), lambda i, j: (i, 0, 0)))
    segs = (SEG_GM, SEG_QN, SEG_VG, SEG_RG, SEG_KVC, SEG_KVS, SEG_KVW, SEG_QG, SEG_KG, SEG_MISC)
    tab_spec = pl.BlockSpec((tm, LANE), lambda i, j: (j, 0))
    return pl.pallas_call(
        _inproj_kernel,
        grid=(b, t // tm),
        in_specs=[pl.BlockSpec((1, tm, d), lambda i, j: (i, j, 0)), mod_spec, mod_spec,
                  pl.BlockSpec((d, IN_PACKED), lambda i, j: (0, 0)),
                  pl.BlockSpec((1, IN_PACKED), lambda i, j: (0, 0)),
                  tab_spec, tab_spec, tab_spec],
        out_specs=[pl.BlockSpec((1, tm, w), lambda i, j: (i, j, 0)) for _, w in segs],
        out_shape=[jax.ShapeDtypeStruct((b, t, w), F32) for _, w in segs],
        compiler_params=_cparams("parallel", "parallel"),
        name="inproj",
    )(x, shift, scale, w_pack, b_pack, *tables)


CHUNKS = 128
PAGE_ROWS = 128
PAGES = 16


def _pack_cmp_weights(pos, w1, w2):
    pos2 = jnp.concatenate([pos, pos], axis=1)
    z1 = jnp.zeros_like(w1)
    bd1 = jnp.concatenate([jnp.concatenate([w1, z1], axis=2), jnp.concatenate([z1, w1], axis=2)], axis=1)
    w1p = jnp.concatenate([bd1[:CMP_STRIDE], bd1[CMP_STRIDE:]], axis=2).astype(BF16)
    w1p = w1p.reshape(CMP_STRIDE // 2, 2 * LANE, 4 * CMP_HIDDEN)
    z2 = jnp.zeros_like(w2)
    w2p = jnp.concatenate([jnp.concatenate([w2, z2], axis=1), jnp.concatenate([z2, w2], axis=1)], axis=0).astype(BF16)
    return pos2, w1p, w2p


CMP_GROUP = 2


def _fill_chunks(page_refs, xs_ref, rows_ref, feature_major):
    for g in range(CMP_GROUP):
        pages = page_refs[g * PAGES:(g + 1) * PAGES]
        if feature_major:
            for p, pr in enumerate(pages):
                rows_ref[p * PAGE_ROWS:(p + 1) * PAGE_ROWS, :] = pr[0].T
            for l in range(CMP_STRIDE):
                xs_ref[l, g * CHUNKS:(g + 1) * CHUNKS, :] = rows_ref[pl.ds(l, CHUNKS, stride=CMP_STRIDE), :]
        else:
            for p, pr in enumerate(pages):
                for l in range(CMP_STRIDE):
                    xs_ref[l, g * CHUNKS + 8 * p:g * CHUNKS + 8 * p + 8, :] = pr[0, pl.ds(l, PAGE_ROWS // CMP_STRIDE, stride=CMP_STRIDE), :]


def _compress_chunks(xs_ref, pos_ref, w1_ref, w2_ref):
    hid2 = 2 * CMP_HIDDEN
    rows = CMP_GROUP * CHUNKS
    acc_a = jnp.zeros((rows, hid2), F32)
    acc_b = jnp.zeros((rows, hid2), F32)
    for lp in range(CMP_STRIDE // 2):
        l0, l1 = 2 * lp, 2 * lp + 1
        x0, x1 = xs_ref[l0], xs_ref[l1]

        def lhs(off):
            return jnp.concatenate([(x0 + pos_ref[off + l0:off + l0 + 1, :]).astype(BF16),
                                    (x1 + pos_ref[off + l1:off + l1 + 1, :]).astype(BF16)], axis=1)

        acc_a = acc_a + jnp.dot(lhs(0), w1_ref[lp, :, 0:hid2], preferred_element_type=F32)
        acc_b = acc_b + jnp.dot(lhs(CMP_STRIDE), w1_ref[lp, :, hid2:2 * hid2], preferred_element_type=F32)
    hid = acc_a + pltpu.roll(acc_b, rows - 1, 0)
    out = jnp.dot(jax.nn.gelu(hid).astype(BF16), w2_ref[...], preferred_element_type=F32)
    row = lax.broadcasted_iota(I32, out.shape, 0) % CHUNKS
    return jnp.where(row < CHUNKS - 1, out, 0.0).reshape(CMP_GROUP, CHUNKS, LANE)


def _compress_kernel(pt_ref, *refs, feature_major):
    n_pages = CMP_GROUP * PAGES
    k_pages, v_pages = refs[:n_pages], refs[n_pages:2 * n_pages]
    posk_ref, w1k_ref, w2k_ref, posv_ref, w1v_ref, w2v_ref, kc_ref, vc_ref, xk_ref, xv_ref, rows_ref = refs[2 * n_pages:]
    _fill_chunks(k_pages, xk_ref, rows_ref, feature_major)
    kc_ref[...] = _compress_chunks(xk_ref, posk_ref, w1k_ref, w2k_ref)
    _fill_chunks(v_pages, xv_ref, rows_ref, feature_major)
    vc_ref[...] = _compress_chunks(xv_ref, posv_ref, w1v_ref, w2v_ref)


def _page_spec(p, half, feature_major):
    if feature_major:
        return pl.BlockSpec((1, LANE, PAGE_ROWS), lambda i, pt: (pt[i * CMP_GROUP * PAGES + p], half, 0))
    return pl.BlockSpec((1, PAGE_ROWS, LANE), lambda i, pt: (pt[i * CMP_GROUP * PAGES + p], 0, half))


def _const_spec(shape):
    nd = len(shape)
    return pl.BlockSpec(shape, lambda i, pt: (0,) * nd)


def _compress(pages, page_ids, cmp_wk, cmp_wv, feature_major=False):
    n_b = page_ids.shape[0] // PAGES
    assert n_b % CMP_GROUP == 0
    n_pages = CMP_GROUP * PAGES
    consts = list(cmp_wk) + list(cmp_wv)
    grid_spec = pltpu.PrefetchScalarGridSpec(
        num_scalar_prefetch=1,
        grid=(n_b // CMP_GROUP,),
        in_specs=[_page_spec(p, h, feature_major) for h in range(2) for p in range(n_pages)] + [_const_spec(c.shape) for c in consts],
        out_specs=[pl.BlockSpec((CMP_GROUP, CHUNKS, LANE), lambda i, pt: (i, 0, 0))] * 2,
        scratch_shapes=[pltpu.VMEM((CMP_STRIDE, CMP_GROUP * CHUNKS, LANE), F32)] * 2 + [pltpu.VMEM((PAGES * PAGE_ROWS, LANE), F32)],
    )
    return pl.pallas_call(
        functools.partial(_compress_kernel, feature_major=feature_major),
        grid_spec=grid_spec,
        out_shape=[jax.ShapeDtypeStruct((n_b, CHUNKS, LANE), F32)] * 2,
        compiler_params=_cparams("parallel"),
        name="compress",
    )(page_ids, *([pages] * (2 * n_pages)), *consts)


def _cover_tables(n_sel):
    c_start = np.arange(CHUNKS) * CMP_STRIDE
    s_start = np.arange(n_sel) * SEL_BLOCK
    cover = ((c_start[:, None] < s_start[None, :] + SEL_BLOCK) & (c_start[:, None] + CMP_LEN > s_start[None, :])).astype(np.float32)
    cover[CHUNKS - 1] = 0.0
    out = np.zeros((NSA_KV_HEADS, LANE, CHUNKS), np.float32)
    for h in range(NSA_KV_HEADS):
        out[h, h * 64:h * 64 + n_sel] = cover.T
    return jnp.asarray(out)


def _softmax_rows(s, valid):
    s = jnp.where(valid, s, NEG)
    m = jnp.max(s, axis=-1, keepdims=True)
    m = jnp.where(m > 0.5 * NEG, m, 0.0)
    p = jnp.where(valid, jnp.exp(s - m), 0.0)
    return p / jnp.maximum(jnp.sum(p, axis=-1, keepdims=True), 1e-30)


def _select_blocks(imp, n_sel, top_n):
    ridx = lax.broadcasted_iota(I32, imp.shape, 0)
    cnt = jnp.zeros(imp.shape, F32)
    for i in range(n_sel):
        vi = imp[i:i + 1, :]
        ahead = (vi > imp) | ((vi == imp) & (ridx > i))
        cnt = cnt + jnp.where(ahead, 1.0, 0.0)
    return jnp.where((cnt < top_n) & (ridx < n_sel), 1.0, 0.0)


def _cmp_attn_kernel(qn_ref, kc_ref, vc_ref, cov_ref, o_ref, sel_ref, *, tq, n_sel):
    qi = pl.program_id(1)
    kc = kc_ref[0].astype(BF16)
    vc = vc_ref[0].astype(BF16)
    qpos = qi * tq + lax.broadcasted_iota(I32, (tq, CHUNKS), 0)
    cidx = lax.broadcasted_iota(I32, (tq, CHUNKS), 1)
    valid = (cidx * CMP_STRIDE + CMP_LEN - 1 <= qpos) & (cidx < CHUNKS - 1)
    psum = [jnp.zeros((tq, CHUNKS), F32) for _ in range(NSA_KV_HEADS)]
    for hh in range(NSA_HEADS):
        q = qn_ref[0, :, hh * LANE:(hh + 1) * LANE].astype(BF16)
        p = _softmax_rows(_dot_nt(q, kc), valid)
        o_ref[0, :, hh * LANE:(hh + 1) * LANE] = jnp.dot(p.astype(BF16), vc, preferred_element_type=F32)
        psum[hh // NSA_GROUP] = psum[hh // NSA_GROUP] + p
    imp = (_dot_nt(cov_ref[0], psum[0], lax.Precision.HIGHEST) + _dot_nt(cov_ref[1], psum[1], lax.Precision.HIGHEST))
    blk = lax.broadcasted_iota(I32, (LANE, tq), 0) & 63
    qpos_t = qi * tq + lax.broadcasted_iota(I32, (LANE, tq), 1)
    cur = qpos_t // SEL_BLOCK
    forced = (blk == 0) | (blk == cur) | (blk == cur - 1)
    imp = jnp.where(forced, FORCE_SCORE, jnp.where(blk * SEL_BLOCK <= qpos_t, imp, -FORCE_SCORE))
    sel_t = jnp.concatenate([_select_blocks(imp[0:64], n_sel, SEL_TOP_N), _select_blocks(imp[64:128], n_sel, SEL_TOP_N)], axis=0)
    sel_ref[0] = sel_t.T


def _cmp_attn(qn, kc, vc, tq):
    b, t, _ = qn.shape
    n_sel = -(-t // SEL_BLOCK)
    cov = _cover_tables(n_sel)
    return pl.pallas_call(
        functools.partial(_cmp_attn_kernel, tq=tq, n_sel=n_sel),
        grid=(b, t // tq),
        in_specs=[pl.BlockSpec((1, tq, NSA_HEADS * LANE), lambda i, j: (i, j, 0)),
                  pl.BlockSpec((1, CHUNKS, LANE), lambda i, j: (i, 0, 0)),
                  pl.BlockSpec((1, CHUNKS, LANE), lambda i, j: (i, 0, 0)),
                  pl.BlockSpec((NSA_KV_HEADS, LANE, CHUNKS), lambda i, j: (0, 0, 0))],
        out_specs=[pl.BlockSpec((1, tq, NSA_HEADS * LANE), lambda i, j: (i, j, 0)),
                   pl.BlockSpec((1, tq, LANE), lambda i, j: (i, j, 0))],
        out_shape=[jax.ShapeDtypeStruct((b, t, NSA_HEADS * LANE), F32), jax.ShapeDtypeStruct((b, t, LANE), F32)],
        compiler_params=_cparams("parallel", "parallel"),
        name="cmp_attn",
    )(qn, kc, vc, cov)


def _key_block_table(t):
    blk = np.arange(t)[:, None] // SEL_BLOCK
    lanes = np.arange(LANE)[None, :] & 63
    return jnp.asarray((blk == lanes).astype(np.float32), dtype=BF16)


def _lane_fold(x, op):
    out = x[:, 0:LANE]
    for c in range(LANE, x.shape[1], LANE):
        out = op(out, x[:, c:c + LANE])
    return out


def _sel_attn_kernel(qn_ref, kv_ref, sel_ref, kb_ref, o_ref, q2_scr, k2_scr, v_scr, s_scr, *, tq, tk):
    qi = pl.program_id(1)
    rows = NSA_HEADS * tq
    t = kv_ref.shape[1]

    @pl.when(qi == 0)
    def _():
        k2_scr[:, 0:LANE] = kv_ref[0, :, 0:LANE].astype(BF16)
        k2_scr[:, LANE:2 * LANE] = kb_ref[...]
        v_scr[...] = kv_ref[0, :, LANE:2 * LANE].astype(BF16)

    not_sel = (1.0 - sel_ref[0]) * NEG
    lane_head = lax.broadcasted_iota(I32, (tq, LANE), 1) // 64
    for hh in range(NSA_HEADS):
        q2_scr[hh * tq:(hh + 1) * tq, 0:LANE] = qn_ref[0, :, hh * LANE:(hh + 1) * LANE].astype(BF16)
        q2_scr[hh * tq:(hh + 1) * tq, LANE:2 * LANE] = jnp.where(lane_head == hh // NSA_GROUP, not_sel, 0.0).astype(BF16)
    q2 = q2_scr[...]
    last = (qi * tq + tq - 1) // tk

    def scores(kt):
        k0 = pl.multiple_of(kt * tk, tk)
        return _dot_nt(q2, k2_scr[pl.ds(k0, tk), :])

    def pass1(kt, m_acc):
        s = scores(kt)
        s_scr[kt] = s
        return jnp.maximum(m_acc, _lane_fold(s, jnp.maximum))

    m_acc = lax.fori_loop(0, last, pass1, jnp.full((rows, LANE), NEG, F32))
    rel = (qi * tq + lax.broadcasted_iota(I32, (tq, tk), 0)) - (last * tk + lax.broadcasted_iota(I32, (tq, tk), 1))
    causal = jnp.where(rel >= 0, 0.0, NEG)
    s_last = scores(last) + jnp.concatenate([causal] * NSA_HEADS, axis=0)
    m = jnp.max(jnp.maximum(m_acc, _lane_fold(s_last, jnp.maximum)), axis=-1, keepdims=True)

    def accumulate(s, v, l_acc, acc):
        p = jnp.exp(s - m)
        return l_acc + _lane_fold(p, jnp.add), acc + jnp.dot(p.astype(BF16), v, preferred_element_type=F32)

    def pass2(kt, carry):
        k0 = pl.multiple_of(kt * tk, tk)
        return accumulate(s_scr[kt], v_scr[pl.ds(k0, tk), :], *carry)

    l_acc, acc = lax.fori_loop(0, last, pass2, (jnp.zeros((rows, LANE), F32), jnp.zeros((rows, LANE), F32)))
    l_acc, acc = accumulate(s_last, v_scr[pl.ds(pl.multiple_of(last * tk, tk), tk), :], l_acc, acc)
    out = acc / jnp.sum(l_acc, axis=-1, keepdims=True)
    for hh in range(NSA_HEADS):
        o_ref[0, :, hh * LANE:(hh + 1) * LANE] = out[hh * tq:(hh + 1) * tq]


def _sel_attn(qn, kv, sel, tq, tk):
    b, t, _ = qn.shape
    rows = NSA_HEADS * tq
    return pl.pallas_call(
        functools.partial(_sel_attn_kernel, tq=tq, tk=tk),
        grid=(b, t // tq),
        in_specs=[pl.BlockSpec((1, tq, NSA_HEADS * LANE), lambda i, j: (i, j, 0)),
                  pl.BlockSpec((1, t, 2 * LANE), lambda i, j: (i, 0, 0)),
                  pl.BlockSpec((1, tq, LANE), lambda i, j: (i, j, 0)),
                  pl.BlockSpec((t, LANE), lambda i, j: (0, 0))],
        out_specs=pl.BlockSpec((1, tq, NSA_HEADS * LANE), lambda i, j: (i, j, 0)),
        out_shape=jax.ShapeDtypeStruct((b, t, NSA_HEADS * LANE), F32),
        scratch_shapes=[pltpu.VMEM((rows, 2 * LANE), BF16), pltpu.VMEM((t, 2 * LANE), BF16), pltpu.VMEM((t, LANE), BF16),
                        pltpu.VMEM((t // tk, rows, tk), F32)],
        compiler_params=_cparams("parallel", "arbitrary"),
        name="sel_attn",
    )(qn, kv, sel, _key_block_table(t))


def _win_attn_kernel(qn_ref, kv_ref, o_ref, q_scr, k_scr, v_scr, *, tq):
    qi = pl.program_id(1)
    span = WINDOW + tq

    @pl.when(qi == 0)
    def _():
        k_scr[...] = kv_ref[0, :, 0:LANE].astype(BF16)
        v_scr[...] = kv_ref[0, :, LANE:2 * LANE].astype(BF16)

    for hh in range(NSA_HEADS):
        q_scr[hh * tq:(hh + 1) * tq, :] = qn_ref[0, :, hh * LANE:(hh + 1) * LANE].astype(BF16)
    k0 = pl.multiple_of(jnp.maximum(qi * tq - WINDOW, 0), tq)
    rel = (qi * tq + lax.broadcasted_iota(I32, (tq, span), 0)) - (k0 + lax.broadcasted_iota(I32, (tq, span), 1))
    bias = jnp.where((rel >= 0) & (rel <= WINDOW), 0.0, NEG)
    s = _dot_nt(q_scr[...], k_scr[pl.ds(k0, span), :]) + jnp.concatenate([bias] * NSA_HEADS, axis=0)
    p = jnp.exp(s - jnp.max(s, axis=-1, keepdims=True))
    out = jnp.dot(p.astype(BF16), v_scr[pl.ds(k0, span), :], preferred_element_type=F32) / jnp.sum(p, axis=-1, keepdims=True)
    for hh in range(NSA_HEADS):
        o_ref[0, :, hh * LANE:(hh + 1) * LANE] = out[hh * tq:(hh + 1) * tq]


def _win_attn(qn, kv, tq):
    b, t, _ = qn.shape
    assert t >= WINDOW + tq
    rows = NSA_HEADS * tq
    return pl.pallas_call(
        functools.partial(_win_attn_kernel, tq=tq),
        grid=(b, t // tq),
        in_specs=[pl.BlockSpec((1, tq, NSA_HEADS * LANE), lambda i, j: (i, j, 0)),
                  pl.BlockSpec((1, t, 2 * LANE), lambda i, j: (i, 0, 0))],
        out_specs=pl.BlockSpec((1, tq, NSA_HEADS * LANE), lambda i, j: (i, j, 0)),
        out_shape=jax.ShapeDtypeStruct((b, t, NSA_HEADS * LANE), F32),
        scratch_shapes=[pltpu.VMEM((rows, LANE), BF16), pltpu.VMEM((t, LANE), BF16), pltpu.VMEM((t, LANE), BF16)],
        compiler_params=_cparams("parallel", "arbitrary"),
        name="win_attn",
    )(qn, kv)


def _dec_softmax(scores, vals):
    m = scores[0].max(axis=-1, keepdims=True)
    for s in scores[1:]:
        m = jnp.maximum(m, s.max(axis=-1, keepdims=True))
    den = jnp.zeros_like(m)
    out = jnp.zeros((m.shape[0], LANE), F32)
    for s, v in zip(scores, vals):
        p = jnp.exp(s - m)
        den = den + p.sum(axis=-1, keepdims=True)
        out = out + (p * v if s.shape[1] == 1 else _dot_nt(p.astype(BF16), v))
    return out / den


DEC_GROUP = 2


def _nsa_decode_kernel(pt_ref, *refs, past_len, n_sel):
    per_seq = refs[DEC_GROUP * PAGES:]
    cov_ref = per_seq[6]
    for bb in range(DEC_GROUP):
        views = [r.at[pl.ds(bb, 1)] for i, r in enumerate(per_seq) if i != 6]
        _nsa_decode_one(refs[bb * PAGES:(bb + 1) * PAGES], *views[:6], cov_ref, *views[6:], past_len=past_len, n_sel=n_sel)


def _nsa_decode_one(pages, qn_ref, kc_ref, vc_ref, kvs_ref, kvw_ref, win_ref, cov_ref,
                    ocmp_ref, osel_ref, owin_ref, wnew_ref, *, past_len, n_sel):
    nh = NSA_HEADS
    q = jnp.concatenate([qn_ref[0, :, hh * LANE:(hh + 1) * LANE] for hh in range(nh)], axis=0)
    qb = q.astype(BF16)
    cidx = lax.broadcasted_iota(I32, (nh, CHUNKS), 1)
    valid = (cidx * CMP_STRIDE + CMP_LEN - 1 <= past_len) & (cidx < CHUNKS - 1)
    p = _softmax_rows(_dot_nt(qb, kc_ref[0].astype(BF16)), valid)
    o_cmp = jnp.dot(p.astype(BF16), vc_ref[0].astype(BF16), preferred_element_type=F32)
    imp = None
    for h in range(NSA_KV_HEADS):
        ps = jnp.sum(p[h * NSA_GROUP:(h + 1) * NSA_GROUP], axis=0, keepdims=True)
        term = _dot_nt(cov_ref[h], jnp.broadcast_to(ps, (LANE, CHUNKS)), lax.Precision.HIGHEST)
        imp = term if imp is None else imp + term
    blk = lax.broadcasted_iota(I32, (LANE, LANE), 0) & 63
    cur = past_len // SEL_BLOCK
    forced = (blk == 0) | (blk == cur) | (blk == cur - 1)
    imp = jnp.where(forced, FORCE_SCORE, jnp.where(blk * SEL_BLOCK <= past_len, imp, -FORCE_SCORE))
    sel_t = jnp.concatenate([_select_blocks(imp[0:64], n_sel, min(SEL_TOP_N, n_sel)),
                             _select_blocks(imp[64:128], n_sel, min(SEL_TOP_N, n_sel))], axis=0)
    sel = sel_t.T[0:1, :]
    head_of_row = lax.broadcasted_iota(I32, (nh, 1), 0) // NSA_GROUP

    def picked(s):
        return jnp.where(head_of_row == 0, sel[:, s:s + 1], sel[:, 64 + s:64 + s + 1])

    first_half = lax.broadcasted_iota(I32, (nh, PAGE_ROWS), 1) < SEL_BLOCK
    scores, vals = [], []
    for pg, pr in enumerate(pages):
        s = jnp.dot(qb, pr[0, 0:LANE, :].astype(BF16), preferred_element_type=F32)
        ok = jnp.where(first_half, picked(2 * pg), picked(2 * pg + 1)) > 0.5
        scores.append(jnp.where(ok, s, NEG))
        vals.append(pr[0, LANE:2 * LANE, :].astype(BF16))
    s_new = jnp.sum(q * kvs_ref[0, :, 0:LANE], axis=-1, keepdims=True)
    scores.append(jnp.where(picked(past_len // SEL_BLOCK) > 0.5, s_new, NEG))
    vals.append(kvs_ref[0, :, LANE:2 * LANE])
    o_sel = _dec_softmax(scores, vals)
    n_win = win_ref.shape[2]
    kpos = past_len - n_win + lax.broadcasted_iota(I32, (nh, n_win), 1)
    rel = past_len - kpos
    s_win = jnp.dot(qb, win_ref[0, 0:LANE, :].astype(BF16), preferred_element_type=F32)
    s_win = jnp.where((kpos >= 0) & (rel >= 0) & (rel <= WINDOW), s_win, NEG)
    s_new = jnp.sum(q * kvw_ref[0, :, 0:LANE], axis=-1, keepdims=True)
    o_win = _dec_softmax([s_win, s_new], [win_ref[0, LANE:2 * LANE, :].astype(BF16), kvw_ref[0, :, LANE:2 * LANE]])
    for hh in range(nh):
        ocmp_ref[0, :, hh * LANE:(hh + 1) * LANE] = o_cmp[hh:hh + 1]
        osel_ref[0, :, hh * LANE:(hh + 1) * LANE] = o_sel[hh:hh + 1]
        owin_ref[0, :, hh * LANE:(hh + 1) * LANE] = o_win[hh:hh + 1]
    for c in range(0, 2 * LANE, LANE):
        new_col = jnp.broadcast_to(kvw_ref[0, :, c:c + LANE], (LANE, LANE)).T
        shifted = pltpu.roll(win_ref[0, c:c + LANE, :], n_win - 1, 1)
        lane = lax.broadcasted_iota(I32, shifted.shape, 1)
        wnew_ref[0, c:c + LANE, :] = jnp.where(lane == n_win - 1, jnp.concatenate([new_col] * (n_win // LANE), axis=1), shifted)


def _nsa_decode(qn, kc, vc, sel_pages, page_ids, kvs_new, kvw_new, win_state):
    b = qn.shape[0]
    n_win = win_state.shape[2]
    past_len = PAGES * PAGE_ROWS
    n_sel = -(-(past_len + 1) // SEL_BLOCK)
    cov = _cover_tables(n_sel)

    assert b % DEC_GROUP == 0

    def per_b(shape):
        nd = len(shape)
        return pl.BlockSpec((DEC_GROUP,) + shape[1:], lambda i, pt: (i,) + (0,) * (nd - 1))

    slots = NSA_HEADS * LANE
    grid_spec = pltpu.PrefetchScalarGridSpec(
        num_scalar_prefetch=1,
        grid=(b // DEC_GROUP,),
        in_specs=[pl.BlockSpec((1, 2 * LANE, PAGE_ROWS), (lambda i, pt, p=p: (pt[i * DEC_GROUP * PAGES + p], 0, 0)))
                  for p in range(DEC_GROUP * PAGES)]
        + [per_b(qn.shape), per_b(kc.shape), per_b(vc.shape), per_b(kvs_new.shape), per_b(kvw_new.shape), per_b(win_state.shape),
           _const_spec(cov.shape)],
        out_specs=[per_b((b, 1, slots))] * 3 + [per_b(win_state.shape)],
    )
    return pl.pallas_call(
        functools.partial(_nsa_decode_kernel, past_len=past_len, n_sel=n_sel),
        grid_spec=grid_spec,
        out_shape=[jax.ShapeDtypeStruct((b, 1, slots), F32)] * 3 + [jax.ShapeDtypeStruct(win_state.shape, F32)],
        compiler_params=_cparams("parallel"),
        name="nsa_decode",
    )(page_ids, *([sel_pages] * (DEC_GROUP * PAGES)), qn, kc, vc, kvs_new, kvw_new, win_state, cov)


def _dot_tn(a, b):
    return lax.dot_general(a, b, (((0,), (0,)), ((), ())), preferred_element_type=F32)


def _cumsum_table():
    r = np.arange(LANE)
    return jnp.asarray(((r[:, None] // GLA_SUB == r[None, :] // GLA_SUB) & (r[None, :] <= r[:, None])).astype(np.float32))


def _gla_kernel(*refs, t, t_valid, has_state):
    if has_state:
        qg_ref, kg_ref, vg_ref, misc_ref, wa_ref, ba_ref, lt_ref, seg_ref, s0_ref, o_ref, s_ref, b_scr, st_scr = refs
    else:
        qg_ref, kg_ref, vg_ref, misc_ref, wa_ref, ba_ref, lt_ref, seg_ref, o_ref, s_ref, b_scr, st_scr = refs
    z = jnp.dot(misc_ref[0], wa_ref[...], preferred_element_type=F32, precision=lax.Precision.HIGHEST) + ba_ref[...]
    la = (jnp.minimum(z, 0.0) - jnp.log1p(jnp.exp(-jnp.abs(z)))) * (1.0 / GLA_TAU)
    if t_valid < t:
        la = jnp.where(lax.broadcasted_iota(I32, la.shape, 0) < t_valid, la, 0.0)
    tile = min(t, LANE)
    for r in range(0, t, tile):
        b_scr[r:r + tile, :] = jnp.dot(lt_ref[0:tile, 0:tile], la[r:r + tile, :], preferred_element_type=F32,
                                       precision=lax.Precision.HIGHEST)
    pairs = GLA_HEADS // 2
    for p in range(pairs):
        if has_state:
            st_scr[p] = s0_ref[0, 2 * p:2 * p + 2].reshape(2 * GLA_DK, GLA_DV).T
        else:
            st_scr[p] = jnp.zeros((GLA_DV, LANE), F32)
    head_a = lax.broadcasted_iota(I32, (GLA_SUB, LANE), 1) < GLA_DK
    row = lax.broadcasted_iota(I32, (GLA_SUB, LANE), 0)

    def pair_chunk(q, k, v, b, st):
        b_last = b[GLA_SUB - 1:GLA_SUB, :]
        st_b = st.astype(BF16)
        qe = q * jnp.exp(b)
        o = jnp.concatenate([_dot_nt(jnp.where(head_a, qe, 0.0).astype(BF16), st_b),
                             _dot_nt(jnp.where(head_a, 0.0, qe).astype(BF16), st_b)], axis=1)
        ws = []
        for j in range(GLA_SUB):
            w = q * k[j:j + 1, :] * jnp.exp(jnp.minimum(b - b[j:j + 1, :], 0.0))
            ws.append(jnp.where(row >= j, w, 0.0))
        a_all = jnp.dot(jnp.concatenate(ws, axis=0).astype(BF16), seg_ref[...], preferred_element_type=F32)
        for j in range(GLA_SUB):
            o = o + a_all[j * GLA_SUB:(j + 1) * GLA_SUB] * v[j:j + 1, :]
        kd = k * jnp.exp(b_last - b)
        upd = (_dot_tn(v[:, 0:GLA_DV].astype(BF16), jnp.where(head_a, kd, 0.0).astype(BF16))
               + _dot_tn(v[:, GLA_DV:2 * GLA_DV].astype(BF16), jnp.where(head_a, 0.0, kd).astype(BF16)))
        return o, jnp.exp(b_last) * st + upd

    def chunk(c, carry):
        r0 = pl.multiple_of(c * GLA_SUB, GLA_SUB)
        for p in range(pairs):
            o, st_new = pair_chunk(qg_ref[0, pl.ds(r0, GLA_SUB), p * LANE:(p + 1) * LANE],
                                          kg_ref[0, pl.ds(r0, GLA_SUB), p * LANE:(p + 1) * LANE],
                                          vg_ref[0, pl.ds(r0, GLA_SUB), 2 * p * GLA_DV:2 * (p + 1) * GLA_DV],
                                          b_scr[pl.ds(r0, GLA_SUB), p * LANE:(p + 1) * LANE], st_scr[p])
            o_ref[0, pl.ds(r0, GLA_SUB), 2 * p * GLA_DV:2 * (p + 1) * GLA_DV] = o
            st_scr[p] = st_new
        return carry

    lax.fori_loop(0, t // GLA_SUB, chunk, 0, unroll=4 if t // GLA_SUB % 4 == 0 else 1)
    for p in range(pairs):
        s_ref[0, 2 * p:2 * p + 2] = st_scr[p].T.reshape(2, GLA_DK, GLA_DV)


def _gla(qg, kg, vg, misc, wa_pad, ba, s0, t_valid):
    b, t, _ = qg.shape
    has_state = s0 is not None
    hk = GLA_HEADS * GLA_DK

    def per_b(shape):
        nd = len(shape)
        return pl.BlockSpec((1,) + shape[1:], lambda i: (i,) + (0,) * (nd - 1))

    def const(shape):
        nd = len(shape)
        return pl.BlockSpec(shape, lambda i: (0,) * nd)

    lanes = np.arange(LANE)[:, None] < GLA_DK
    seg = jnp.asarray((lanes == (np.arange(2 * GLA_DV)[None, :] < GLA_DV)).astype(np.float32), dtype=BF16)
    in_specs = [per_b(qg.shape), per_b(kg.shape), per_b(vg.shape), per_b(misc.shape),
                const((LANE, hk)), const((1, hk)), const((LANE, LANE)), const((LANE, 2 * GLA_DV))]
    args = [qg, kg, vg, misc, wa_pad, ba.reshape(1, -1), _cumsum_table(), seg]
    state_shape = (b, GLA_HEADS, GLA_DK, GLA_DV)
    if has_state:
        in_specs.append(per_b(state_shape))
        args.append(s0)
    return pl.pallas_call(
        functools.partial(_gla_kernel, t=t, t_valid=t_valid, has_state=has_state),
        grid=(b,),
        in_specs=in_specs,
        out_specs=[per_b(vg.shape), per_b(state_shape)],
        out_shape=[jax.ShapeDtypeStruct(vg.shape, F32), jax.ShapeDtypeStruct(state_shape, F32)],
        scratch_shapes=[pltpu.VMEM((t, hk), F32), pltpu.VMEM((GLA_HEADS // 2, GLA_DV, LANE), F32)],
        compiler_params=_cparams("parallel"),
        name="gla",
    )(*args)


def _gate_expand_table():
    out = np.zeros((3, LANE, NSA_HEADS * LANE), np.float32)
    for hh in range(NSA_HEADS):
        for j in range(3):
            out[j, MISC_GN + 3 * hh + j, hh * LANE:(hh + 1) * LANE] = 1.0
    return jnp.asarray(np.concatenate([out, out], axis=1), dtype=BF16)


def _pad_br_a(w_br_a):
    zero = jnp.zeros((HEAD_DIM, w_br_a.shape[1]), w_br_a.dtype)
    parts = []
    for hh in range(NSA_HEADS):
        wh = w_br_a[hh * HEAD_DIM:(hh + 1) * HEAD_DIM]
        parts += [wh, zero] if hh // NSA_GROUP == 0 else [zero, wh]
    return jnp.concatenate(parts, axis=0).astype(BF16)


def _layer_norm(v, g, b):
    mu = jnp.mean(v, axis=-1, keepdims=True)
    var = jnp.mean(jnp.square(v - mu), axis=-1, keepdims=True)
    return (v - mu) * lax.rsqrt(var + LN_EPS) * g + b


def _mixer_tail_kernel(ocmp_ref, osel_ref, owin_ref, misc_ref, ogla_ref, rg_ref, gm_ref, x_ref, gate_ref, scf_ref, shf_ref,
                       ex_ref, ng_ref, wa_ref, wb_ref, wo_ref, lg_ref, lb_ref, x1_ref, xm_ref):
    sig = jax.nn.sigmoid(misc_ref[0])
    sig_hi = sig.astype(BF16)
    sig_lo = (sig - sig_hi.astype(F32)).astype(BF16)
    sig_split = jnp.concatenate([sig_hi, sig_lo], axis=1)
    o_nsa = None
    for j, ref in enumerate((ocmp_ref, osel_ref, owin_ref)):
        g = jnp.dot(sig_split, ex_ref[j], preferred_element_type=F32)
        o_nsa = g * ref[0] if o_nsa is None else o_nsa + g * ref[0]
    br_a = _bdot(o_nsa, wa_ref[...])
    heads = []
    for h in range(GLA_HEADS):
        seg = ogla_ref[0, :, h * GLA_DV:(h + 1) * GLA_DV]
        mu = jnp.mean(seg, axis=-1, keepdims=True)
        var = jnp.mean(jnp.square(seg - mu), axis=-1, keepdims=True)
        r = rg_ref[0, :, h * GLA_DV:(h + 1) * GLA_DV]
        heads.append((seg - mu) * lax.rsqrt(var + LN_EPS) * ng_ref[:, h * GLA_DV:(h + 1) * GLA_DV] * (r * jax.nn.sigmoid(r)))
    br_b = _bdot(jnp.concatenate(heads, axis=1), wb_ref[...])
    gm_a = jax.nn.sigmoid(gm_ref[0, :, 0:D_MODEL])
    gm_b = jax.nn.sigmoid(gm_ref[0, :, D_MODEL:2 * D_MODEL])
    y = _bdot(gm_a * br_a + gm_b * br_b, wo_ref[...])
    x1 = _layer_norm(DN_ALPHA * x_ref[0] + gate_ref[0] * y, lg_ref[...], lb_ref[...])
    x1_ref[0] = x1
    xm_ref[0] = x1 * (1.0 + scf_ref[0]) + shf_ref[0]


def _mixer_tail(ocmp, osel, owin, misc, ogla, rg, gm, x, gate_m, scale_f, shift_f, consts, tm):
    b, t, d = x.shape
    per_tok = gate_m.shape[1] != 1

    def tok(w):
        return pl.BlockSpec((1, tm, w), lambda i, j: (i, j, 0))

    mod_spec = tok(d) if per_tok else pl.BlockSpec((1, 1, d), lambda i, j: (i, 0, 0))

    def const(a):
        nd = a.ndim
        return pl.BlockSpec(a.shape, lambda i, j: (0,) * nd)

    return pl.pallas_call(
        _mixer_tail_kernel,
        grid=(b, t // tm),
        in_specs=[tok(NSA_HEADS * LANE)] * 3 + [tok(LANE), tok(GLA_HEADS * GLA_DV), tok(GLA_HEADS * GLA_DV), tok(2 * d), tok(d),
                                               mod_spec, mod_spec, mod_spec] + [const(c) for c in consts],
        out_specs=[tok(d), tok(d)],
        out_shape=[jax.ShapeDtypeStruct((b, t, d), F32)] * 2,
        compiler_params=_cparams("parallel", "parallel"),
        name="mixer_tail",
    )(ocmp, osel, owin, misc, ogla, rg, gm, x, gate_m, scale_f, shift_f, *consts)


ROUTE_TILE = LANE


def _first_index(hit, iota, size, axis):
    return jnp.min(jnp.where(hit, iota, size), axis=axis, keepdims=True)


def _part_specs(parts, tm):
    n_a = parts[0].shape[0] // tm
    d = parts[0].shape[1]
    return [pl.BlockSpec((tm, d), lambda i, *_: (jnp.minimum(i, n_a - 1), 0)),
            pl.BlockSpec((tm, d), lambda i, *_: (jnp.maximum(i - n_a, 0), 0))], n_a


def _part_tile(i, n_a, a_ref, b_ref):
    return jnp.where(i < n_a, a_ref[...], b_ref[...])


def _router_kernel(xa_ref, xb_ref, wr_ref, bias_ref, tri_ref, eidx_ref, rank_ref, wrow_ref, cnt_ref, carry_ref, *, n_a):
    i = pl.program_id(0)
    tm = ROUTE_TILE
    per = N_EXPERTS // N_GROUPS

    @pl.when(i == 0)
    def _():
        carry_ref[...] = jnp.zeros_like(carry_ref)

    logits = _dot_nt(wr_ref[...], _part_tile(i, n_a, xa_ref, xb_ref), lax.Precision.HIGHEST)
    s = jax.nn.sigmoid(logits)
    sb = s + bias_ref[...]
    sb3 = sb.reshape(N_GROUPS, per, tm)
    in_grp = lax.broadcasted_iota(I32, sb3.shape, 1)
    m1 = jnp.max(sb3, axis=1, keepdims=True)
    first = _first_index(sb3 == m1, in_grp, per, 1)
    m2 = jnp.max(jnp.where(in_grp == first, NEG, sb3), axis=1, keepdims=True)
    gs = (m1 + m2).reshape(N_GROUPS, tm)
    g_iota = lax.broadcasted_iota(I32, gs.shape, 0)
    g_keep = jnp.zeros(gs.shape, jnp.bool_)
    for _ in range(TOPK_GROUPS):
        pick = g_iota == _first_index(gs == jnp.max(gs, axis=0, keepdims=True), g_iota, N_GROUPS, 0)
        g_keep = g_keep | pick
        gs = jnp.where(pick, NEG, gs)
    sbm = jnp.where(g_keep.reshape(N_GROUPS, 1, tm), sb3, NEG).reshape(N_EXPERTS, tm)
    e_iota = lax.broadcasted_iota(I32, sbm.shape, 0)
    idxs, sels = [], []
    onehot = jnp.zeros(sbm.shape, F32)
    for _ in range(TOP_K):
        idx = _first_index(sbm == jnp.max(sbm, axis=0, keepdims=True), e_iota, N_EXPERTS, 0)
        pick = e_iota == idx
        idxs.append(idx)
        sels.append(jnp.sum(jnp.where(pick, s, 0.0), axis=0, keepdims=True))
        sbm = jnp.where(pick, NEG, sbm)
        onehot = onehot + jnp.where(pick, 1.0, 0.0)
    sel = jnp.concatenate(sels, axis=0)
    wts = sel / jnp.sum(sel, axis=0, keepdims=True) * ROUTED_SCALE
    carry = carry_ref[...]
    before = carry + jnp.dot(onehot.astype(BF16), tri_ref[...], preferred_element_type=F32)
    ranks = [jnp.sum(jnp.where(e_iota == idx, before, 0.0), axis=0, keepdims=True) for idx in idxs]
    eidx_ref[...] = jnp.concatenate(idxs, axis=0)
    rank_ref[...] = jnp.concatenate(ranks, axis=0).astype(I32)
    wrow_ref[...] = jnp.concatenate([wts, jnp.zeros((LANE - TOP_K, tm), F32)], axis=0).T
    carry = carry + jnp.sum(onehot, axis=1, keepdims=True)
    carry_ref[...] = carry
    cnt_ref[...] = carry


def _router(xm_parts, w_router, router_bias):
    n = xm_parts[0].shape[0] + xm_parts[1].shape[0]
    d = xm_parts[0].shape[1]
    tm = ROUTE_TILE
    r = np.arange(tm)
    tri = jnp.asarray((r[:, None] < r[None, :]).astype(np.float32), dtype=BF16)
    x_specs, n_a = _part_specs(xm_parts, tm)
    return pl.pallas_call(
        functools.partial(_router_kernel, n_a=n_a),
        grid=(n // tm,),
        in_specs=x_specs + [
                  pl.BlockSpec((N_EXPERTS, d), lambda i: (0, 0)),
                  pl.BlockSpec((N_EXPERTS, 1), lambda i: (0, 0)),
                  pl.BlockSpec((tm, tm), lambda i: (0, 0))],
        out_specs=[pl.BlockSpec((TOP_K, tm), lambda i: (0, i)),
                   pl.BlockSpec((TOP_K, tm), lambda i: (0, i)),
                   pl.BlockSpec((tm, LANE), lambda i: (i, 0)),
                   pl.BlockSpec((N_EXPERTS, LANE), lambda i: (0, 0))],
        out_shape=[jax.ShapeDtypeStruct((TOP_K, n), I32), jax.ShapeDtypeStruct((TOP_K, n), I32),
                   jax.ShapeDtypeStruct((n, LANE), F32), jax.ShapeDtypeStruct((N_EXPERTS, LANE), F32)],
        scratch_shapes=[pltpu.VMEM((N_EXPERTS, LANE), F32)],
        compiler_params=_cparams("arbitrary"),
        name="router",
    )(*xm_parts, w_router.T, router_bias.reshape(N_EXPERTS, 1), tri)


def _dest_kernel(eidx_ref, rank_ref, start_ref, dest_ref):
    tm = ROUTE_TILE
    e_iota = lax.broadcasted_iota(I32, (N_EXPERTS, tm), 0)
    start = start_ref[...]
    for j in range(dest_ref.shape[0]):
        cols = slice(j * tm, (j + 1) * tm)
        rows = [jnp.sum(jnp.where(e_iota == eidx_ref[k:k + 1, cols], start, 0.0), axis=0, keepdims=True) for k in range(TOP_K)]
        dest_ref[j] = jnp.concatenate(rows, axis=0).astype(I32) + rank_ref[:, cols]


def _dest(eidx, rank, pad_start):
    n = eidx.shape[1]
    tm = ROUTE_TILE
    per_step = 3 if (n // tm) % 3 == 0 else 1
    return pl.pallas_call(
        _dest_kernel,
        grid=(n // tm // per_step,),
        in_specs=[pl.BlockSpec((TOP_K, per_step * tm), lambda i: (0, i)),
                  pl.BlockSpec((TOP_K, per_step * tm), lambda i: (0, i)),
                  pl.BlockSpec((N_EXPERTS, 1), lambda i: (0, 0))],
        out_specs=pl.BlockSpec((per_step, TOP_K, tm), lambda i: (i, 0, 0)),
        out_shape=jax.ShapeDtypeStruct((n // tm, TOP_K, tm), I32),
        compiler_params=_cparams("parallel"),
        name="dest",
    )(eidx, rank, pad_start.astype(F32).reshape(N_EXPERTS, 1))


TOKEN_SUB = D_MODEL // LANE


def _to_token_tiles(x, ref):
    m = x.shape[0]
    for s in range(TOKEN_SUB):
        ref[pl.ds(s, m, stride=TOKEN_SUB), :] = x[:, s * LANE:(s + 1) * LANE]


def _from_token_tiles(ref, m, s):
    return ref[pl.ds(s, m, stride=TOKEN_SUB), :]


def _row_copy(src_ref, src_row, dst_ref, dst_row, sem):
    return pltpu.make_async_copy(src_ref.at[pl.ds(src_row * TOKEN_SUB, TOKEN_SUB)],
                                 dst_ref.at[pl.ds(dst_row * TOKEN_SUB, TOKEN_SUB)], sem)


def _dispatch_kernel(dest_ref, pad_ref, nb_ref, xa_ref, xb_ref, xs_ref, tile_ref, zero_ref, sems, sem, *, n_a):
    i = pl.program_id(0)
    tm = ROUTE_TILE
    bm = EXPERT_BLOCK
    last = pl.num_programs(0) - 1
    src = tile_ref.at[i % 2]
    _to_token_tiles(_part_tile(i, n_a, xa_ref, xb_ref), src)

    def start_row(r, c):
        for k in range(TOP_K):
            _row_copy(src, r, xs_ref, dest_ref[0, k, r], sems.at[i % 2]).start(priority=k % 2)
        return c

    def wait_step(slot):
        for _ in range(TOP_K):
            pltpu.make_async_copy(tile_ref.at[0], xs_ref.at[pl.ds(0, tm * TOKEN_SUB)], sems.at[slot]).wait()

    lax.fori_loop(0, tm, start_row, 0)

    @pl.when(i >= 1)
    def _():
        wait_step((i - 1) % 2)

    @pl.when(i == last)
    def _():
        wait_step(i % 2)
        zero_ref[...] = jnp.zeros_like(zero_ref)

        def pad_rows(e, c, op):
            lax.fori_loop(pad_ref[0, e], pad_ref[1, e], lambda r, cc: (op(_row_copy(zero_ref, 0, xs_ref, r, sem)), cc)[1], 0)
            return c

        for e0 in range(0, N_EXPERTS, PAD_GROUP):
            lax.fori_loop(e0, e0 + PAD_GROUP, functools.partial(pad_rows, op=lambda cp: cp.start()), 0)
            lax.fori_loop(e0, e0 + PAD_GROUP, functools.partial(pad_rows, op=lambda cp: cp.wait()), 0)

        def tail_copy(blk):
            return pltpu.make_async_copy(zero_ref, xs_ref.at[pl.ds(blk * bm * TOKEN_SUB, bm * TOKEN_SUB)], sem)

        n_blocks = xs_ref.shape[0] // (bm * TOKEN_SUB)
        lax.fori_loop(nb_ref[0], n_blocks, lambda blk, c: (tail_copy(blk).start(), c)[1], 0)
        lax.fori_loop(nb_ref[0], n_blocks, lambda blk, c: (tail_copy(blk).wait(), c)[1], 0)


def _dispatch(xm_parts, dest_tiles, pad_range, n_used, n_rows):
    tm = ROUTE_TILE
    x_specs, n_a = _part_specs(xm_parts, tm)
    return pl.pallas_call(
        functools.partial(_dispatch_kernel, n_a=n_a),
        grid=(dest_tiles.shape[0],),
        in_specs=[pl.BlockSpec((1, TOP_K, tm), lambda i: (i, 0, 0), memory_space=pltpu.SMEM),
                  pl.BlockSpec(memory_space=pltpu.SMEM),
                  pl.BlockSpec(memory_space=pltpu.SMEM)] + x_specs,
        out_specs=pl.BlockSpec(memory_space=pl.ANY),
        out_shape=jax.ShapeDtypeStruct((n_rows * TOKEN_SUB, LANE), F32),
        scratch_shapes=[pltpu.VMEM((2, tm * TOKEN_SUB, LANE), F32), pltpu.VMEM((EXPERT_BLOCK * TOKEN_SUB, LANE), F32),
                        pltpu.SemaphoreType.DMA((2,)), pltpu.SemaphoreType.DMA(())],
        compiler_params=_cparams("arbitrary"),
        name="dispatch",
    )(dest_tiles, pad_range, n_used, *xm_parts)


def _experts_kernel(first_ref, cnt_ref, nb_ref, xs_ref, wg_ref, wu_ref, wd_ref, ys_ref,
                    xbuf, ybuf, wg_s, wu_s, wd_s, in_sems, out_sems):
    e = pl.program_id(0)
    bm = EXPERT_BLOCK
    brows = bm * TOKEN_SUB
    total = nb_ref[0]
    n_blocks = xs_ref.shape[0] // brows

    def x_copy(g, slot):
        return pltpu.make_async_copy(xs_ref.at[pl.ds(g * brows, brows)], xbuf.at[slot], in_sems.at[slot])

    def y_copy(g, slot):
        return pltpu.make_async_copy(ybuf.at[slot], ys_ref.at[pl.ds(g * brows, brows)], out_sems.at[slot])

    depth = xbuf.shape[0]

    @pl.when(e == 0)
    def _():
        for g0 in range(depth - 1):
            @pl.when(g0 < total)
            def _():
                x_copy(g0, g0).start()

    @pl.when(cnt_ref[e] > 0)
    def _():
        wg_s[...] = wg_ref[0].astype(BF16)
        wu_s[...] = wu_ref[0].astype(BF16)
        wd_s[...] = wd_ref[0].astype(BF16)

    def block(c, carry):
        g = first_ref[e] + c
        slot = g % depth
        x_copy(g, slot).wait()

        @pl.when(g + depth - 1 < total)
        def _():
            x_copy(g + depth - 1, (g + depth - 1) % depth).start()

        x = jnp.concatenate([_from_token_tiles(xbuf.at[slot], bm, s) for s in range(TOKEN_SUB)], axis=1).astype(BF16)
        gate = jnp.dot(x, wg_s[...], preferred_element_type=F32)
        up = jnp.dot(x, wu_s[...], preferred_element_type=F32)
        y = jnp.dot((gate * jax.nn.sigmoid(gate) * up).astype(BF16), wd_s[...], preferred_element_type=F32)

        @pl.when(g >= depth)
        def _():
            y_copy(g - depth, slot).wait()

        _to_token_tiles(y, ybuf.at[slot])
        y_copy(g, slot).start()
        return carry

    lax.fori_loop(0, cnt_ref[e], block, 0)

    @pl.when(e == pl.num_programs(0) - 1)
    def _():
        for back in range(depth, 0, -1):
            @pl.when(total >= back)
            def _():
                y_copy(total - back, (total - back) % depth).wait()

        ybuf[0] = jnp.zeros(ybuf.shape[1:], F32)
        lax.fori_loop(total, n_blocks, lambda g, c: (y_copy(g, 0).start(), c)[1], 0)
        lax.fori_loop(total, n_blocks, lambda g, c: (y_copy(g, 0).wait(), c)[1], 0)


def _experts(xs, first_block, n_block, n_used, w_gate, w_up, w_down):
    brows = EXPERT_BLOCK * TOKEN_SUB
    n_exp, d, f = w_gate.shape
    grid_spec = pltpu.PrefetchScalarGridSpec(
        num_scalar_prefetch=3,
        grid=(n_exp,),
        in_specs=[pl.BlockSpec(memory_space=pl.ANY),
                  pl.BlockSpec((1, d, f), lambda e, *_: (e, 0, 0)),
                  pl.BlockSpec((1, d, f), lambda e, *_: (e, 0, 0)),
                  pl.BlockSpec((1, f, d), lambda e, *_: (e, 0, 0))],
        out_specs=pl.BlockSpec(memory_space=pl.ANY),
        scratch_shapes=[pltpu.VMEM((EXPERT_RING, brows, LANE), F32), pltpu.VMEM((EXPERT_RING, brows, LANE), F32),
                        pltpu.VMEM((d, f), BF16), pltpu.VMEM((d, f), BF16), pltpu.VMEM((f, d), BF16),
                        pltpu.SemaphoreType.DMA((EXPERT_RING,)), pltpu.SemaphoreType.DMA((EXPERT_RING,))],
    )
    return pl.pallas_call(
        _experts_kernel,
        grid_spec=grid_spec,
        out_shape=jax.ShapeDtypeStruct(xs.shape, F32),
        compiler_params=_cparams("arbitrary"),
        name="experts",
    )(first_block, n_block, n_used, xs, w_gate, w_up, w_down)


def _combine_kernel(dest_ref, dnext_ref, ys_ref, wrow_ref, xa_ref, xb_ref, x1a_ref, x1b_ref, ga_ref, gb_ref,
                    sg_ref, su_ref, sd_ref, lg_ref, lb_ref, ya_ref, yb_ref, buf_ref, sems, *, n_a):
    i = pl.program_id(0)
    tm = ROUTE_TILE
    slot = i % 2

    def issue(d_ref, s):
        def start_row(r, c):
            for k in range(TOP_K):
                _row_copy(ys_ref, d_ref[0, k, r], buf_ref.at[s, k], r, sems.at[s]).start(priority=k % 2)
            return c

        lax.fori_loop(0, tm, start_row, 0)

    @pl.when(i == 0)
    def _():
        issue(dest_ref, 0)

    @pl.when(i < pl.num_programs(0) - 1)
    def _():
        issue(dnext_ref, 1 - slot)

    x = _part_tile(i, n_a, xa_ref, xb_ref).astype(BF16)
    g = jnp.dot(x, sg_ref[...], preferred_element_type=F32)
    u = jnp.dot(x, su_ref[...], preferred_element_type=F32)
    shared = jnp.dot((g * jax.nn.sigmoid(g) * u).astype(BF16), sd_ref[...], preferred_element_type=F32)
    for k in range(TOP_K):
        pltpu.make_async_copy(ys_ref.at[pl.ds(0, tm * TOKEN_SUB)], buf_ref.at[slot, k], sems.at[slot]).wait()
    w = wrow_ref[...]
    cols = []
    for s in range(TOKEN_SUB):
        routed = w[:, 0:1] * _from_token_tiles(buf_ref.at[slot, 0], tm, s)
        for k in range(1, TOP_K):
            routed = routed + w[:, k:k + 1] * _from_token_tiles(buf_ref.at[slot, k], tm, s)
        cols.append(routed + shared[:, s * LANE:(s + 1) * LANE])
    moe = jnp.concatenate(cols, axis=1)
    gate = jnp.where(i < n_a, ga_ref[0], gb_ref[...])
    y = _layer_norm(DN_ALPHA * _part_tile(i, n_a, x1a_ref, x1b_ref) + gate * moe, lg_ref[...], lb_ref[...])

    @pl.when(i < n_a)
    def _():
        ya_ref[...] = y

    @pl.when(i >= n_a)
    def _():
        yb_ref[...] = y


def _combine(ys, dest_tiles, wrow, xm_parts, x1_parts, gate_a, gate_b, ws_gate, ws_up, ws_down, ln_g, ln_b):
    n = xm_parts[0].shape[0] + xm_parts[1].shape[0]
    d = xm_parts[0].shape[1]
    tm = ROUTE_TILE
    f = ws_gate.shape[1]
    x_specs, n_a = _part_specs(xm_parts, tm)
    tiles_per_seq = n_a // gate_a.shape[0]
    gate_specs = [pl.BlockSpec((1, 1, d), lambda i: (jnp.minimum(i, n_a - 1) // tiles_per_seq, 0, 0)), x_specs[1]]
    vec = pl.BlockSpec((1, d), lambda i: (0, 0))
    return pl.pallas_call(
        functools.partial(_combine_kernel, n_a=n_a),
        grid=(n // tm,),
        in_specs=[pl.BlockSpec((1, TOP_K, tm), lambda i: (i, 0, 0), memory_space=pltpu.SMEM),
                  pl.BlockSpec((1, TOP_K, tm), lambda i: (jnp.minimum(i + 1, n // tm - 1), 0, 0), memory_space=pltpu.SMEM),
                  pl.BlockSpec(memory_space=pl.ANY),
                  pl.BlockSpec((tm, LANE), lambda i: (i, 0))] + x_specs + x_specs + gate_specs + [
                  pl.BlockSpec((d, f), lambda i: (0, 0)),
                  pl.BlockSpec((d, f), lambda i: (0, 0)),
                  pl.BlockSpec((f, d), lambda i: (0, 0)), vec, vec],
        out_specs=x_specs,
        out_shape=[jax.ShapeDtypeStruct(p.shape, F32) for p in xm_parts],
        scratch_shapes=[pltpu.VMEM((2, TOP_K, tm * TOKEN_SUB, LANE), F32), pltpu.SemaphoreType.DMA((2,))],
        compiler_params=_cparams("arbitrary"),
        name="combine",
    )(dest_tiles, dest_tiles, ys, wrow, *xm_parts, *x1_parts, gate_a, gate_b,
      ws_gate.astype(BF16), ws_up.astype(BF16), ws_down.astype(BF16), ln_g.reshape(1, d), ln_b.reshape(1, d))


def _moe(xm, x1, gate_a, gate_b, ln_g, ln_b, w_router, router_bias, w_e_gate, w_e_up, w_e_down, w_s_gate, w_s_up, w_s_down):
    n = xm[0].shape[0] + xm[1].shape[0]
    eidx, rank, wrow, cnt = _router(xm, w_router, router_bias)
    counts = cnt[:, 0].astype(I32)
    padded = (counts + EXPERT_BLOCK - 1) // EXPERT_BLOCK * EXPERT_BLOCK
    pad_end = jnp.cumsum(padded)
    pad_start = pad_end - padded
    dest_tiles = _dest(eidx, rank, pad_start)
    n_blocks = -(-(n * TOP_K) // EXPERT_BLOCK) + N_EXPERTS
    n_used = (pad_end[-1:] // EXPERT_BLOCK).astype(I32)
    pad_range = jnp.stack([pad_start + counts, pad_end]).astype(I32)
    xs = _dispatch(xm, dest_tiles, pad_range, n_used, n_blocks * EXPERT_BLOCK)
    ys = _experts(xs, (pad_start // EXPERT_BLOCK).astype(I32), (padded // EXPERT_BLOCK).astype(I32), n_used,
                  w_e_gate, w_e_up, w_e_down)
    return _combine(ys, dest_tiles, wrow, xm, x1, gate_a, gate_b, w_s_gate, w_s_up, w_s_down, ln_g, ln_b)


def kernel(x_prompt, x_sample, cache_kv_cmp, cache_kv_sel, state_kv_win, state_gla, page_table, c_prompt, c_sample, w_in, b_in, cmp_k_pos, cmp_k_w1, cmp_k_w2, cmp_v_pos, cmp_v_w1, cmp_v_w2, gla_w_a2, gla_b_a, gla_norm_g, w_br_a, w_br_b, w_out, ln1_g, ln1_b, w_ada, b_ada, w_router, router_bias, w_e_gate, w_e_up, w_e_down, w_s_gate, w_s_up, w_s_down, ln2_g, ln2_b):
    bp, tp, d = x_prompt.shape
    nd, td = x_sample.shape[:2]
    n_pool, page_rows = cache_kv_cmp.shape[:2]
    past_len = page_table.shape[1] * page_rows
    assert td == 1 and d == D_MODEL and page_rows == PAGE_ROWS and page_table.shape[1] == PAGES and tp == PAGES * PAGE_ROWS
    kv_w = 2 * NSA_KV_HEADS * HEAD_DIM

    mod = _adaln(jnp.concatenate([c_prompt, c_sample], axis=0), w_ada, b_ada)
    mod_p = [m.reshape(bp, 1, d) for m in jnp.split(mod[:bp], 6, axis=-1)]
    mod_s = [m.reshape(1, nd, d) for m in jnp.split(mod[bp:], 6, axis=-1)]

    w_pack, b_pack = _pack_in_weights(w_in, b_in)
    cmp_wk = _pack_cmp_weights(cmp_k_pos, cmp_k_w1, cmp_k_w2)
    cmp_wv = _pack_cmp_weights(cmp_v_pos, cmp_v_w1, cmp_v_w2)
    wa_pad = jnp.zeros((LANE, GLA_HEADS * GLA_DK), F32).at[MISC_AG:MISC_AG + GLA_GATE_RANK].set(gla_w_a2)
    tail_consts = (_gate_expand_table(), gla_norm_g.reshape(1, -1), _pad_br_a(w_br_a), w_br_b.astype(BF16), w_out.astype(BF16),
                   ln1_g.reshape(1, d), ln1_b.reshape(1, d))

    gm, qn, vg, rg, kvc, kvs, kvw, qg, kg, misc = _inproj(
        x_prompt, mod_p[0], mod_p[1], w_pack, b_pack, _rope_tables(jnp.arange(tp, dtype=I32)), 256)
    kc, vc = _compress(kvc.reshape(bp * PAGES, PAGE_ROWS, kv_w), jnp.arange(bp * PAGES, dtype=I32), cmp_wk, cmp_wv)
    ocmp, sel = _cmp_attn(qn, kc, vc, 256)
    osel = _sel_attn(qn, kvs, sel, 128, 256)
    owin = _win_attn(qn, kvw, 128)
    ogla, gla_p = _gla(qg, kg, vg, misc, wa_pad, gla_b_a, None, tp)
    x1_p, xm_p = _mixer_tail(ocmp, osel, owin, misc, ogla, rg, gm, x_prompt, mod_p[2], mod_p[4], mod_p[3], tail_consts, 256)
    n_win = min(WINDOW, tp)
    outs_p = (kvc.reshape(bp, tp, 2, NSA_KV_HEADS, HEAD_DIM), kvs.reshape(bp, tp, 2, NSA_KV_HEADS, HEAD_DIM),
              kvw[:, tp - n_win:].reshape(bp, n_win, 2, NSA_KV_HEADS, HEAD_DIM), gla_p)

    gm, qn, vg, rg, kvc, kvs, kvw, qg, kg, misc = _inproj(
        x_sample.reshape(1, nd, d), mod_s[0], mod_s[1], w_pack, b_pack, _rope_tables(jnp.full((nd,), past_len, I32)), nd)
    page_ids = page_table.reshape(-1).astype(I32)
    kc, vc = _compress(cache_kv_cmp.reshape(n_pool, PAGE_ROWS, kv_w).transpose(0, 2, 1), page_ids, cmp_wk, cmp_wv,
                       feature_major=True)
    ocmp, osel, owin, win_new = _nsa_decode(
        qn.reshape(nd, 1, -1), kc, vc, cache_kv_sel.reshape(n_pool, PAGE_ROWS, kv_w).transpose(0, 2, 1), page_ids,
        kvs.reshape(nd, 1, kv_w), kvw.reshape(nd, 1, kv_w), state_kv_win.reshape(nd, -1, kv_w).transpose(0, 2, 1))
    win_new = win_new.transpose(0, 2, 1)

    def pad_rows(a):
        return jnp.pad(a.reshape(nd, 1, -1), ((0, 0), (0, GLA_SUB - 1), (0, 0)))

    ogla, gla_s = _gla(pad_rows(qg), pad_rows(kg), pad_rows(vg), pad_rows(misc), wa_pad, gla_b_a, state_gla, 1)
    x1_s, xm_s = _mixer_tail(ocmp.reshape(1, nd, -1), osel.reshape(1, nd, -1), owin.reshape(1, nd, -1), misc,
                             ogla[:, 0].reshape(1, nd, -1), rg, gm, x_sample.reshape(1, nd, d),
                             mod_s[2], mod_s[4], mod_s[3], tail_consts, nd)
    outs_s = (kvc.reshape(nd, 1, 2, NSA_KV_HEADS, HEAD_DIM), kvs.reshape(nd, 1, 2, NSA_KV_HEADS, HEAD_DIM),
              win_new.reshape(state_kv_win.shape), gla_s)

    n_p = bp * tp
    y_p, y_s = _moe((xm_p.reshape(n_p, d), xm_s.reshape(nd, d)), (x1_p.reshape(n_p, d), x1_s.reshape(nd, d)),
                    mod_p[5], mod_s[5].reshape(nd, d), ln2_g, ln2_b,
                    w_router, router_bias, w_e_gate, w_e_up, w_e_down, w_s_gate, w_s_up, w_s_down)
    return (y_p.reshape(bp, tp, d), y_s.reshape(nd, 1, d)) + outs_p + outs_s
```
